```python
import math
import jax, jax.numpy as jnp
from jax import lax
import numpy as np

D_MODEL = 2048
BATCH = 16
SEQ = 2048
DEPTH = 1
DEC_BATCH = 8
DEC_SEQ = 16
PAST_LEN = 1024

CHUNK = 64
Q_BLOCK = 128
H_A = 8
DH_A = 64
DK_A = 2 * DH_A
DV_A = 128
H_B = 8
DH_B = 128
N_PREV_CHUNKS = 8
BAND_PAST = N_PREV_CHUNKS * CHUNK
BAND_LEN = BAND_PAST + CHUNK
MAX_REL = 128
H_C = 4
DH_C = 256
N_MEM = 256
N_BRANCH = 3
W_A = H_A * DV_A
W_B = H_B * DH_B
W_C = H_C * DH_C
D_FF = ((8 * D_MODEL + 3 * 256 - 1) // (3 * 256)) * 256
IN_SIZES = [H_A * DK_A, H_A * DK_A, W_A, W_B, W_B, W_B, W_C, N_BRANCH * D_MODEL]
N_IN = sum(IN_SIZES)
SPLIT_IDX = [int(i) for i in np.cumsum(IN_SIZES)[:-1]]
EPS = 1e-6
NEG_INF = -1e30

kernel_name = "gated_hybrid_streaming_encoder_step"


def rms_norm(x, g):
    xf = x.astype(jnp.float32)
    y = xf * lax.rsqrt(jnp.mean(xf * xf, axis=-1, keepdims=True) + EPS) * g.astype(jnp.float32)
    return y.astype(x.dtype)


def head_norm(o, g, lam_init):
    of = o.astype(jnp.float32)
    y = of * lax.rsqrt(jnp.mean(of * of, axis=-1, keepdims=True) + EPS) * g.astype(jnp.float32) * (1.0 - lam_init)
    return y.astype(o.dtype)


def alibi_slopes():
    return jnp.asarray([2.0 ** (-8.0 * (h + 1) / H_A) for h in range(H_A)], dtype=jnp.float32)


def diff_lambda(lq1, lk1, lq2, lk2, lam_init):
    f = lambda a: a.astype(jnp.float32)
    return jnp.exp(jnp.sum(f(lq1) * f(lk1))) - jnp.exp(jnp.sum(f(lq2) * f(lk2))) + lam_init


def diff_attend(q, k, v, qpos, kpos, lam):
    q1, q2 = jnp.split(q, 2, axis=-1)
    k1, k2 = jnp.split(k, 2, axis=-1)
    dist = jnp.abs(qpos[:, None] - kpos[None, :]).astype(jnp.float32)
    bias = -alibi_slopes()[:, None, None] * dist[None]
    visible = (kpos[None, :] // CHUNK) <= (qpos[:, None] // CHUNK)
    scale = DH_A ** -0.5

    def probs(qa, ka):
        s = jnp.einsum("bqhd,bkhd->bhqk", qa, ka).astype(jnp.float32) * scale + bias[None]
        return jax.nn.softmax(jnp.where(visible[None, None], s, NEG_INF), axis=-1)

    a = probs(q1, k1) - lam * probs(q2, k2)
    return jnp.einsum("bhqk,bkhd->bqhd", a.astype(v.dtype), v)


def diff_attention_prompt(q, k, v, lam):
    B, S = q.shape[0], q.shape[1]
    nb = S // Q_BLOCK
    kpos = jnp.arange(S)
    qb = q.reshape(B, nb, Q_BLOCK, H_A, DK_A).transpose(1, 0, 2, 3, 4)

    def step(args):
        qblk, i = args
        qpos = i * Q_BLOCK + jnp.arange(Q_BLOCK)
        return diff_attend(qblk, k, v, qpos, kpos, lam)

    o = lax.map(step, (qb, jnp.arange(nb)))
    return o.transpose(1, 0, 2, 3, 4).reshape(B, S, H_A, DV_A)


def band_attend(q, k, v, qpos, kpos, rel_bias):
    s = jnp.einsum("bqhd,bkhd->bhqk", q, k).astype(jnp.float32) * (DH_B ** -0.5)
    rel = jnp.clip(kpos[None, :] - qpos[:, None], -MAX_REL, MAX_REL) + MAX_REL
    s = s + jnp.take(rel_bias, rel, axis=1).astype(jnp.float32)[None]
    qch = qpos[:, None] // CHUNK
    kch = kpos[None, :] // CHUNK
    valid = (kpos[None, :] >= 0) & (kch <= qch) & (kch >= qch - N_PREV_CHUNKS)
    p = jax.nn.softmax(jnp.where(valid[None, None], s, NEG_INF), axis=-1)
    return jnp.einsum("bhqk,bkhd->bqhd", p.astype(v.dtype), v)


def band_attention_prompt(q, k, v, rel_bias):
    B, S = q.shape[0], q.shape[1]
    nc = S // CHUNK
    pad = ((0, 0), (BAND_PAST, 0), (0, 0), (0, 0))
    kpad = jnp.pad(k, pad)
    vpad = jnp.pad(v, pad)
    qc = q.reshape(B, nc, CHUNK, H_B, DH_B).transpose(1, 0, 2, 3, 4)

    def step(args):
        qchunk, c = args
        start = c * CHUNK
        kb = lax.dynamic_slice_in_dim(kpad, start, BAND_LEN, axis=1)
        vb = lax.dynamic_slice_in_dim(vpad, start, BAND_LEN, axis=1)
        qpos = start + jnp.arange(CHUNK)
        kpos = start - BAND_PAST + jnp.arange(BAND_LEN)
        return band_attend(qchunk, kb, vb, qpos, kpos, rel_bias)

    o = lax.map(step, (qc, jnp.arange(nc)))
    return o.transpose(1, 0, 2, 3, 4).reshape(B, S, H_B, DH_B)


def memory_kv(mem, g, w_mem_kv):
    B = mem.shape[0]
    m = rms_norm(mem, g) @ w_mem_kv
    mk, mv = jnp.split(m, 2, axis=-1)
    return mk.reshape(B, N_MEM, H_C, DH_C), mv.reshape(B, N_MEM, H_C, DH_C)


def cross_attend(q, mk, mv):
    s = jnp.einsum("bqhd,bkhd->bhqk", q, mk).astype(jnp.float32) * (DH_C ** -0.5)
    p = jax.nn.softmax(s, axis=-1)
    return jnp.einsum("bhqk,bkhd->bqhd", p.astype(mv.dtype), mv)


def swiglu(h, w_ffn_in, w_ffn_out):
    a, b = jnp.split(h @ w_ffn_in, 2, axis=-1)
    return (jax.nn.silu(a) * b) @ w_ffn_out


def layer_forward(x, mk, mv, attend_a, attend_b, lam_init, norm_mix_pre, norm_mix_post, w_in, b_gate,
                  subln_a, w_br_a, w_br_b, w_br_c, w_out, norm_ffn_pre, norm_ffn_post, w_ffn_in, w_ffn_out):
    B, T, _ = x.shape
    h = rms_norm(x, norm_mix_pre)
    q_a, k_a, v_a, q_b, k_b, v_b, q_c, g = jnp.split(h @ w_in, SPLIT_IDX, axis=-1)
    q_a = q_a.reshape(B, T, H_A, DK_A)
    k_a = k_a.reshape(B, T, H_A, DK_A)
    v_a = v_a.reshape(B, T, H_A, DV_A)
    q_b = q_b.reshape(B, T, H_B, DH_B)
    k_b = k_b.reshape(B, T, H_B, DH_B)
    v_b = v_b.reshape(B, T, H_B, DH_B)
    q_c = q_c.reshape(B, T, H_C, DH_C)
    o_a = head_norm(attend_a(q_a, k_a, v_a), subln_a, lam_init).reshape(B, T, W_A)
    o_b = attend_b(q_b, k_b, v_b).reshape(B, T, W_B)
    o_c = cross_attend(q_c, mk, mv).reshape(B, T, W_C)
    gates = jax.nn.sigmoid(g + b_gate).reshape(B, T, N_BRANCH, D_MODEL)
    merged = (gates[..., 0, :] * (o_a @ w_br_a) + gates[..., 1, :] * (o_b @ w_br_b)
              + gates[..., 2, :] * (o_c @ w_br_c))
    x = x + rms_norm(merged @ w_out, norm_mix_post)
    x = x + rms_norm(swiglu(rms_norm(x, norm_ffn_pre), w_ffn_in, w_ffn_out), norm_ffn_post)
    return x, k_a, v_a, k_b, v_b


def setup_inputs(seed: int = 0) -> dict:
    key = jax.random.key(seed)
    ks = iter(jax.random.split(key, 48))
    nrm = lambda shape, scale: scale * jax.random.normal(next(ks), shape, jnp.float32)
    gain = lambda n: 1.0 + nrm((DEPTH, n), 0.05)
    lb = min(BAND_PAST, PAST_LEN)
    return {
        "x_prompt": nrm((BATCH, SEQ, D_MODEL), 1.0),
        "x_sample": nrm((DEC_BATCH, DEC_SEQ, D_MODEL), 1.0),
        "cache_a_k": nrm((DEPTH, DEC_BATCH, PAST_LEN, H_A, DK_A), 1.0),
        "cache_a_v": nrm((DEPTH, DEC_BATCH, PAST_LEN, H_A, DV_A), 1.0),
        "cache_b_k": nrm((DEPTH, DEC_BATCH, lb, H_B, DH_B), 1.0),
        "cache_b_v": nrm((DEPTH, DEC_BATCH, lb, H_B, DH_B), 1.0),
        "cache_mem_k": nrm((DEPTH, DEC_BATCH, N_MEM, H_C, DH_C), 1.0),
        "cache_mem_v": nrm((DEPTH, DEC_BATCH, N_MEM, H_C, DH_C), 1.0),
        "mem_prompt": nrm((BATCH, N_MEM, D_MODEL), 1.0),
        "norm_mix_pre": gain(D_MODEL),
        "norm_mix_post": gain(D_MODEL),
        "norm_mem": gain(D_MODEL),
        "w_in": nrm((DEPTH, D_MODEL, N_IN), D_MODEL ** -0.5),
        "b_gate": nrm((DEPTH, N_BRANCH * D_MODEL), 0.1),
        "lambda_q1": nrm((DEPTH, DH_A), 0.1),
        "lambda_k1": nrm((DEPTH, DH_A), 0.1),
        "lambda_q2": nrm((DEPTH, DH_A), 0.1),
        "lambda_k2": nrm((DEPTH, DH_A), 0.1),
        "subln_a": gain(DV_A),
        "rel_bias_b": nrm((DEPTH, H_B, 2 * MAX_REL + 1), 0.1),
        "w_mem_kv": nrm((DEPTH, D_MODEL, 2 * W_C), D_MODEL ** -0.5),
        "w_br_a": nrm((DEPTH, W_A, D_MODEL), W_A ** -0.5),
        "w_br_b": nrm((DEPTH, W_B, D_MODEL), W_B ** -0.5),
        "w_br_c": nrm((DEPTH, W_C, D_MODEL), W_C ** -0.5),
        "w_out": nrm((DEPTH, D_MODEL, D_MODEL), D_MODEL ** -0.5),
        "norm_ffn_pre": gain(D_MODEL),
        "norm_ffn_post": gain(D_MODEL),
        "w_ffn_in": nrm((DEPTH, D_MODEL, 2 * D_FF), D_MODEL ** -0.5),
        "w_ffn_out": nrm((DEPTH, D_FF, D_MODEL), D_FF ** -0.5),
    }


def reference(x_prompt, x_sample, cache_a_k, cache_a_v, cache_b_k, cache_b_v, cache_mem_k, cache_mem_v,
              mem_prompt, norm_mix_pre, norm_mix_post, norm_mem, w_in, b_gate, lambda_q1, lambda_k1,
              lambda_q2, lambda_k2, subln_a, rel_bias_b, w_mem_kv, w_br_a, w_br_b, w_br_c, w_out,
              norm_ffn_pre, norm_ffn_post, w_ffn_in, w_ffn_out):
    S = x_prompt.shape[1]
    T = x_sample.shape[1]
    P = cache_a_k.shape[2]
    Lb = cache_b_k.shape[2]
    Lb_prompt = min(BAND_PAST, S)
    yp, ys = x_prompt, x_sample
    akp, avp, bkp, bvp, mkp, mvp, aks, avs, bks, bvs = ([] for _ in range(10))
    for l in range(DEPTH):
        lam_init = 0.8 - 0.6 * math.exp(-0.3 * l)
        lam = diff_lambda(lambda_q1[l], lambda_k1[l], lambda_q2[l], lambda_k2[l], lam_init)
        shared = (lam_init, norm_mix_pre[l], norm_mix_post[l], w_in[l], b_gate[l], subln_a[l], w_br_a[l],
                  w_br_b[l], w_br_c[l], w_out[l], norm_ffn_pre[l], norm_ffn_post[l], w_ffn_in[l], w_ffn_out[l])
        rb = rel_bias_b[l]

        mk_p, mv_p = memory_kv(mem_prompt, norm_mem[l], w_mem_kv[l])
        attend_a_p = lambda q, k, v, lam=lam: diff_attention_prompt(q, k, v, lam)
        attend_b_p = lambda q, k, v, rb=rb: band_attention_prompt(q, k, v, rb)
        yp, ka, va, kb, vb = layer_forward(yp, mk_p, mv_p, attend_a_p, attend_b_p, *shared)
        akp.append(ka)
        avp.append(va)
        bkp.append(kb[:, S - Lb_prompt:])
        bvp.append(vb[:, S - Lb_prompt:])
        mkp.append(mk_p)
        mvp.append(mv_p)

        ca_k, ca_v, cb_k, cb_v = cache_a_k[l], cache_a_v[l], cache_b_k[l], cache_b_v[l]
        qpos = P + jnp.arange(T)

        def attend_a_s(q, k, v, lam=lam, ca_k=ca_k, ca_v=ca_v, qpos=qpos):
            k_all = jnp.concatenate([ca_k, k], axis=1)
            v_all = jnp.concatenate([ca_v, v], axis=1)
            return diff_attend(q, k_all, v_all, qpos, jnp.arange(P + T), lam)

        def attend_b_s(q, k, v, rb=rb, cb_k=cb_k, cb_v=cb_v, qpos=qpos):
            k_all = jnp.concatenate([cb_k, k], axis=1)
            v_all = jnp.concatenate([cb_v, v], axis=1)
            return band_attend(q, k_all, v_all, qpos, P - Lb + jnp.arange(Lb + T), rb)

        ys, ka, va, kb, vb = layer_forward(ys, cache_mem_k[l], cache_mem_v[l], attend_a_s, attend_b_s, *shared)
        aks.append(ka)
        avs.append(va)
        bks.append(kb)
        bvs.append(vb)

    return (yp, ys, jnp.stack(akp), jnp.stack(avp), jnp.stack(bkp), jnp.stack(bvp), jnp.stack(mkp),
            jnp.stack(mvp), jnp.stack(aks), jnp.stack(avs), jnp.stack(bks), jnp.stack(bvs))
```

```python
import functools
import math

import jax
import jax.numpy as jnp
from jax import lax
from jax.experimental import pallas as pl
from jax.experimental.pallas import tpu as pltpu

F32 = jnp.float32
BF16 = jnp.bfloat16

CHUNK = 64
H_A = 8
DH_A = 64
DK_A = 2 * DH_A
DV_A = 128
H_B = 8
DH_B = 128
N_PREV_CHUNKS = 8
BAND_PAST = N_PREV_CHUNKS * CHUNK
MAX_REL = 128
H_C = 4
DH_C = 256
N_BRANCH = 3
EPS = 1e-6
NEG_INF = -1e30

LANES = 128
VMEM_LIMIT = 56 * 1024 * 1024
TOEPLITZ_W = 768
BAND_TILE_W = BAND_PAST + 2 * CHUNK


def _params(*sem):
    return pltpu.CompilerParams(dimension_semantics=sem, vmem_limit_bytes=VMEM_LIMIT)


def _block(n, target):
    if n <= target:
        return n
    b = target
    while n % b:
        b //= 2
    return b


def _rms_kernel(x_ref, g_ref, o_ref):
    x = x_ref[...]
    ms = jnp.mean(x * x, axis=-1, keepdims=True)
    o_ref[...] = (x * lax.rsqrt(ms + EPS) * g_ref[...]).astype(o_ref.dtype)


def rms_norm_bf16(x, g):
    m, d = x.shape
    bm = _block(m, 512)
    return pl.pallas_call(
        _rms_kernel,
        grid=(m // bm,),
        in_specs=[pl.BlockSpec((bm, d), lambda i: (i, 0)),
                  pl.BlockSpec((1, d), lambda i: (0, 0))],
        out_specs=pl.BlockSpec((bm, d), lambda i: (i, 0)),
        out_shape=jax.ShapeDtypeStruct((m, d), BF16),
        compiler_params=_params("parallel"),
    )(x, g.reshape(1, d))


def _mm_kernel(a_ref, w_ref, o_ref):
    o_ref[...] = jnp.dot(a_ref[...], w_ref[...], preferred_element_type=F32).astype(o_ref.dtype)


def matmul_cols(a, w, col0, n, out_dtype):
    m, k = a.shape
    bm = _block(m, 1024)
    bn = _block(n, 1024)
    assert col0 % bn == 0
    off = col0 // bn
    return pl.pallas_call(
        _mm_kernel,
        grid=(m // bm, n // bn),
        in_specs=[pl.BlockSpec((bm, k), lambda i, j: (i, 0)),
                  pl.BlockSpec((k, bn), lambda i, j: (0, j + off))],
        out_specs=pl.BlockSpec((bm, bn), lambda i, j: (i, j)),
        out_shape=jax.ShapeDtypeStruct((m, n), out_dtype),
        compiler_params=_params("parallel", "arbitrary"),
    )(a, w)


def _attn_a_kernel(q_ref, k_ref, v_ref, slope_ref, lq1_ref, lk1_ref, lq2_ref, lk2_ref, sub_ref, o_ref,
                   *, past, tq, lam_init):
    t = q_ref.shape[1]
    tk = k_ref.shape[1]
    slope = slope_ref[0][:, :1]
    lam =(jnp.exp(jnp.sum(lq1_ref[...] * lk1_ref[...], keepdims=True))
           - jnp.exp(jnp.sum(lq2_ref[...] * lk2_ref[...], keepdims=True)) + lam_init)
    scale = DH_A ** -0.5
    k = k_ref[0].astype(BF16)
    v = v_ref[0].astype(BF16)
    lane = lax.broadcasted_iota(jnp.int32, (1, DK_A), 1)
    first_half = lane < DH_A
    contract_last = (((1,), (1,)), ((), ()))

    for i in range(t // tq):
        q = q_ref[0, i * tq:(i + 1) * tq, :]
        q1 = jnp.where(first_half, q, jnp.zeros_like(q))
        q2 = jnp.where(first_half, jnp.zeros_like(q), q)
        q_last = past + (i + 1) * tq - 1
        ext = min(tk, (q_last // CHUNK + 1) * CHUNK)
        kk = k[:ext]
        vv = v[:ext]
        qpos = lax.broadcasted_iota(jnp.int32, (tq, ext), 0) + (past + i * tq)
        kpos = lax.broadcasted_iota(jnp.int32, (tq, ext), 1)
        bias = -slope * jnp.abs(qpos - kpos).astype(F32)
        visible = (kpos // CHUNK) <= (qpos // CHUNK)

        def softmax_pv(qh):
            s = lax.dot_general(qh, kk, contract_last, preferred_element_type=F32) * scale + bias
            s = jnp.where(visible, s, NEG_INF)
            m = jnp.max(s, axis=-1, keepdims=True)
            p = jnp.exp(s - m)
            l = jnp.sum(p, axis=-1, keepdims=True)
            return jnp.dot(p.astype(BF16), vv, preferred_element_type=F32) / l

        o = softmax_pv(q1) - lam * softmax_pv(q2)
        ms = jnp.mean(o * o, axis=-1, keepdims=True)
        y = o * lax.rsqrt(ms + EPS) * sub_ref[...] * (1.0 - lam_init)
        o_ref[0, i * tq:(i + 1) * tq, :] = y.astype(o_ref.dtype)


def attention_a(q, k, v, lq1, lk1, lq2, lk2, subln, lam_init):
    b, t, _ = q.shape
    tk = k.shape[1]
    past = tk - t
    assert past % CHUNK == 0
    tq = _block(t, 256)
    assert tq % CHUNK == 0 or tq == t
    vec = lambda a: a.reshape(1, -1).astype(F32)
    small = lambda n: pl.BlockSpec((1, n), lambda bi, hi: (0, 0))
    slopes = jnp.asarray([[[2.0 ** (-8.0 * (hh + 1) / H_A)] * LANES] for hh in range(H_A)], dtype=F32)
    return pl.pallas_call(
        functools.partial(_attn_a_kernel, past=past, tq=tq, lam_init=lam_init),
        grid=(b, H_A),
        in_specs=[pl.BlockSpec((1, t, DK_A), lambda bi, hi: (bi, 0, hi)),
                  pl.BlockSpec((1, tk, DK_A), lambda bi, hi: (bi, 0, hi)),
                  pl.BlockSpec((1, tk, DV_A), lambda bi, hi: (bi, 0, hi)),
                  pl.BlockSpec((1, 1, LANES), lambda bi, hi: (hi, 0, 0)),
                  small(DH_A), small(DH_A), small(DH_A), small(DH_A), small(DV_A)],
        out_specs=pl.BlockSpec((1, t, DV_A), lambda bi, hi: (bi, 0, hi)),
        out_shape=jax.ShapeDtypeStruct((b, t, H_A * DV_A), BF16),
        compiler_params=_params("parallel", "arbitrary"),
    )(q, k, v, slopes, vec(lq1), vec(lk1), vec(lq2), vec(lk2), vec(subln))


def _attn_b_kernel(q_ref, k_ref, v_ref, g_ref, o_ref, *, q0, k0, tq):
    t = q_ref.shape[1]
    tk = k_ref.shape[1]
    scale = DH_B ** -0.5
    k = k_ref[0].astype(BF16)
    v = v_ref[0].astype(BF16)
    contract_last = (((1,), (1,)), ((), ()))

    grow = jnp.broadcast_to(g_ref[0], (tq, TOEPLITZ_W))
    rolled = pltpu.roll(grow, TOEPLITZ_W - LANES, 1, stride=1, stride_axis=0)
    bias = rolled[:, :BAND_TILE_W]
    r = lax.broadcasted_iota(jnp.int32, (tq, BAND_TILE_W), 0)
    c = lax.broadcasted_iota(jnp.int32, (tq, BAND_TILE_W), 1)
    qch = r // CHUNK
    kch = c // CHUNK - N_PREV_CHUNKS
    valid = (kch <= qch) & (kch >= qch - N_PREV_CHUNKS)
    bias_mask = jnp.where(valid, bias, NEG_INF)

    for gi in range(t // tq):
        qa = q0 + gi * tq
        lo = max((qa // CHUNK - N_PREV_CHUNKS) * CHUNK, k0, 0)
        hi = min(((qa + tq - 1) // CHUNK + 1) * CHUNK, k0 + tk)
        off = lo - qa + BAND_PAST
        w = hi - lo
        assert 0 <= off and off + w <= BAND_TILE_W and off % LANES == 0
        q = q_ref[0, gi * tq:(gi + 1) * tq, :]
        kk = k[lo - k0:hi - k0]
        vv = v[lo - k0:hi - k0]
        s = lax.dot_general(q, kk, contract_last, preferred_element_type=F32) * scale
        s = s + bias_mask[:, off:off + w]
        m = jnp.max(s, axis=-1, keepdims=True)
        p = jnp.exp(s - m)
        l = jnp.sum(p, axis=-1, keepdims=True)
        o = jnp.dot(p.astype(BF16), vv, preferred_element_type=F32) / l
        o_ref[0, gi * tq:(gi + 1) * tq, :] = o.astype(o_ref.dtype)


def _toeplitz_rows(rel_bias):
    n_lo = LANES + BAND_PAST - MAX_REL
    n_hi = TOEPLITZ_W - n_lo - (2 * MAX_REL + 1)
    assert n_hi <= 0
    lo = jnp.broadcast_to(rel_bias[:, :1], (H_B, n_lo))
    g = jnp.concatenate([lo, rel_bias[:, :TOEPLITZ_W - n_lo]], axis=1)
    return g.reshape(H_B, 1, TOEPLITZ_W).astype(F32)


def attention_b(q, k, v, rel_bias, q0, k0):
    b, t, _ = q.shape
    tk = k.shape[1]
    assert q0 % CHUNK == 0 and k0 % CHUNK == 0 and MAX_REL == LANES
    tq = _block(t, 2 * CHUNK)
    assert tq % CHUNK == 0 or t <= CHUNK
    return pl.pallas_call(
        functools.partial(_attn_b_kernel, q0=q0, k0=k0, tq=tq),
        grid=(b, H_B),
        in_specs=[pl.BlockSpec((1, t, DH_B), lambda bi, hi: (bi, 0, hi)),
                  pl.BlockSpec((1, tk, DH_B), lambda bi, hi: (bi, 0, hi)),
                  pl.BlockSpec((1, tk, DH_B), lambda bi, hi: (bi, 0, hi)),
                  pl.BlockSpec((1, 1, TOEPLITZ_W), lambda bi, hi: (hi, 0, 0))],
        out_specs=pl.BlockSpec((1, t, DH_B), lambda bi, hi: (bi, 0, hi)),
        out_shape=jax.ShapeDtypeStruct((b, t, H_B * DH_B), BF16),
        compiler_params=_params("parallel", "arbitrary"),
    )(q, k, v, _toeplitz_rows(rel_bias))


def _attn_c_kernel(q_ref, k_ref, v_ref, o_ref, *, tq):
    t = q_ref.shape[1]
    scale = DH_C ** -0.5
    k = k_ref[0].astype(BF16)
    v = v_ref[0].astype(BF16)
    contract_last = (((1,), (1,)), ((), ()))
    for i in range(t // tq):
        q = q_ref[0, i * tq:(i + 1) * tq, :]
        s = lax.dot_general(q, k, contract_last, preferred_element_type=F32) * scale
        m = jnp.max(s, axis=-1, keepdims=True)
        p = jnp.exp(s - m)
        l = jnp.sum(p, axis=-1, keepdims=True)
        o = jnp.dot(p.astype(BF16), v, preferred_element_type=F32) / l
        o_ref[0, i * tq:(i + 1) * tq, :] = o.astype(o_ref.dtype)


def attention_c(q, mk, mv):
    b, t, _ = q.shape
    n_mem = mk.shape[1]
    tq = _block(t, 512)
    return pl.pallas_call(
        functools.partial(_attn_c_kernel, tq=tq),
        grid=(b, H_C),
        in_specs=[pl.BlockSpec((1, t, DH_C), lambda bi, hi: (bi, 0, hi)),
                  pl.BlockSpec((1, n_mem, DH_C), lambda bi, hi: (bi, 0, hi)),
                  pl.BlockSpec((1, n_mem, DH_C), lambda bi, hi: (bi, 0, hi))],
        out_specs=pl.BlockSpec((1, t, DH_C), lambda bi, hi: (bi, 0, hi)),
        out_shape=jax.ShapeDtypeStruct((b, t, H_C * DH_C), BF16),
        compiler_params=_params("parallel", "arbitrary"),
    )(q, mk, mv)


def _merge_kernel(oa_ref, ob_ref, oc_ref, ga_ref, gb_ref, gc_ref, ba_ref, bb_ref, bc_ref,
                  wa_ref, wb_ref, wc_ref, o_ref):
    def branch(o_r, g_r, b_r, w_r):
        gate = jax.nn.sigmoid(g_r[...].astype(F32) + b_r[...])
        return gate * jnp.dot(o_r[...], w_r[...], preferred_element_type=F32)

    merged = (branch(oa_ref, ga_ref, ba_ref, wa_ref) + branch(ob_ref, gb_ref, bb_ref, wb_ref)
              + branch(oc_ref, gc_ref, bc_ref, wc_ref))
    o_ref[...] = merged.astype(o_ref.dtype)


def merge_branches(oa, ob, oc, gate_logits, b_gate, wa, wb, wc):
    m, w_in = oa.shape
    d = wa.shape[1]
    bm = _block(m, 256)
    row = lambda width: pl.BlockSpec((bm, width), lambda i: (i, 0))
    gate = lambda j: pl.BlockSpec((bm, d), lambda i: (i, j))
    bias = lambda j: pl.BlockSpec((1, d), lambda i: (0, j))
    weight = pl.BlockSpec((w_in, d), lambda i: (0, 0))
    bg = b_gate.reshape(1, N_BRANCH * d).astype(F32)
    return pl.pallas_call(
        _merge_kernel,
        grid=(m // bm,),
        in_specs=[row(w_in), row(w_in), row(w_in), gate(0), gate(1), gate(2),
                  bias(0), bias(1), bias(2), weight, weight, weight],
        out_specs=pl.BlockSpec((bm, d), lambda i: (i, 0)),
        out_shape=jax.ShapeDtypeStruct((m, d), BF16),
        compiler_params=_params("parallel"),
    )(oa, ob, oc, gate_logits, gate_logits, gate_logits, bg, bg, bg, wa, wb, wc)


def _proj_norm_res_kernel(a_ref, w_ref, x_ref, g_ref, o_ref, acc_ref):
    kk = pl.program_id(1)

    @pl.when(kk == 0)
    def _():
        acc_ref[...] = jnp.zeros_like(acc_ref)

    acc_ref[...] += jnp.dot(a_ref[...], w_ref[...], preferred_element_type=F32)

    @pl.when(kk == pl.num_programs(1) - 1)
    def _():
        y = acc_ref[...]
        ms = jnp.mean(y * y, axis=-1, keepdims=True)
        o_ref[...] = x_ref[...] + y * lax.rsqrt(ms + EPS) * g_ref[...]


def proj_norm_residual(a, w, x, g, bk):
    m, k = a.shape
    d = w.shape[1]
    bm = _block(m, 512)
    assert k % bk == 0
    return pl.pallas_call(
        _proj_norm_res_kernel,
        grid=(m // bm, k // bk),
        in_specs=[pl.BlockSpec((bm, bk), lambda i, j: (i, j)),
                  pl.BlockSpec((bk, d), lambda i, j: (j, 0)),
                  pl.BlockSpec((bm, d), lambda i, j: (i, 0)),
                  pl.BlockSpec((1, d), lambda i, j: (0, 0))],
        out_specs=pl.BlockSpec((bm, d), lambda i, j: (i, 0)),
        out_shape=jax.ShapeDtypeStruct((m, d), F32),
        scratch_shapes=[pltpu.VMEM((bm, d), F32)],
        compiler_params=_params("parallel", "arbitrary"),
    )(a, w, x, g.reshape(1, d).astype(F32))


def _ffn_in_kernel(h_ref, wa_ref, wb_ref, o_ref):
    h = h_ref[...]
    a = jnp.dot(h, wa_ref[...], preferred_element_type=F32)
    b = jnp.dot(h, wb_ref[...], preferred_element_type=F32)
    o_ref[...] = (jax.nn.silu(a) * b).astype(o_ref.dtype)


def ffn_in(h, w):
    m, k = h.shape
    f = w.shape[1] // 2
    bm = _block(m, 1024)
    bf = 512
    assert f % bf == 0
    nf = f // bf
    return pl.pallas_call(
        _ffn_in_kernel,
        grid=(m // bm, nf),
        in_specs=[pl.BlockSpec((bm, k), lambda i, j: (i, 0)),
                  pl.BlockSpec((k, bf), lambda i, j: (0, j)),
                  pl.BlockSpec((k, bf), lambda i, j: (0, j + nf))],
        out_specs=pl.BlockSpec((bm, bf), lambda i, j: (i, j)),
        out_shape=jax.ShapeDtypeStruct((m, f), BF16),
        compiler_params=_params("parallel", "arbitrary"),
    )(h, w, w)


def _layer(x, mk, mv, past_a, past_b, lam_init, norm_mix_pre, norm_mix_post, w_in, b_gate,
           lq1, lk1, lq2, lk2, subln_a, rel_bias, w_br_a, w_br_b, w_br_c, w_out,
           norm_ffn_pre, norm_ffn_post, w_ffn_in, w_ffn_out):
    b, t, d = x.shape
    m = b * t
    xf = x.reshape(m, d)
    w_head = H_A * DK_A
    h = rms_norm_bf16(xf, norm_mix_pre)
    proj = lambda idx, dt: matmul_cols(h, w_in, idx * w_head, w_head, dt).reshape(b, t, w_head)
    q_a, k_a, v_a = proj(0, BF16), proj(1, F32), proj(2, F32)
    q_b, k_b, v_b = proj(3, BF16), proj(4, F32), proj(5, F32)
    q_c = proj(6, BF16)
    gate_logits = matmul_cols(h, w_in, 7 * w_head, N_BRANCH * d, BF16)

    if past_a is None:
        ka_all, va_all, kb_all, vb_all = k_a, v_a, k_b, v_b
        q0 = k0 = 0
    else:
        ka_all = jnp.concatenate([past_a[0], k_a], axis=1)
        va_all = jnp.concatenate([past_a[1], v_a], axis=1)
        kb_all = jnp.concatenate([past_b[0], k_b], axis=1)
        vb_all = jnp.concatenate([past_b[1], v_b], axis=1)
        q0 = past_a[0].shape[1]
        k0 = q0 - past_b[0].shape[1]
    o_a = attention_a(q_a, ka_all, va_all, lq1, lk1, lq2, lk2, subln_a, lam_init)
    o_b = attention_b(q_b, kb_all, vb_all, rel_bias, q0, k0)
    o_c = attention_c(q_c, mk, mv)

    merged = merge_branches(o_a.reshape(m, -1), o_b.reshape(m, -1), o_c.reshape(m, -1),
                            gate_logits, b_gate, w_br_a, w_br_b, w_br_c)
    x1 = proj_norm_residual(merged, w_out, xf, norm_mix_post, bk=d)
    h2 = rms_norm_bf16(x1, norm_ffn_pre)
    act = ffn_in(h2, w_ffn_in)
    f = act.shape[1]
    y = proj_norm_residual(act, w_ffn_out, x1, norm_ffn_post, bk=f // 4)
    return y.reshape(b, t, d), k_a, v_a, k_b, v_b


def kernel(x_prompt, x_sample, cache_a_k, cache_a_v, cache_b_k, cache_b_v, cache_mem_k, cache_mem_v, mem_prompt, norm_mix_pre, norm_mix_post, norm_mem, w_in, b_gate, lambda_q1, lambda_k1, lambda_q2, lambda_k2, subln_a, rel_bias_b, w_mem_kv, w_br_a, w_br_b, w_br_c, w_out, norm_ffn_pre, norm_ffn_post, w_ffn_in, w_ffn_out):
    depth = w_in.shape[0]
    bsz, s, d = x_prompt.shape
    dec_b, t, _ = x_sample.shape
    n_mem = mem_prompt.shape[1]
    lb_prompt = min(BAND_PAST, s)
    yp, ys = x_prompt, x_sample
    outs = [[] for _ in range(10)]
    heads = lambda a, nh: a.reshape(a.shape[0], a.shape[1], nh, a.shape[2] // nh)
    flat = lambda a: a.reshape(a.shape[0], a.shape[1], -1)
    for l in range(depth):
        lam_init = 0.8 - 0.6 * math.exp(-0.3 * l)
        bf = lambda w: w[l].astype(BF16)
        shared = (lam_init, norm_mix_pre[l], norm_mix_post[l], bf(w_in), b_gate[l],
                  lambda_q1[l], lambda_k1[l], lambda_q2[l], lambda_k2[l], subln_a[l], rel_bias_b[l],
                  bf(w_br_a), bf(w_br_b), bf(w_br_c), bf(w_out),
                  norm_ffn_pre[l], norm_ffn_post[l], bf(w_ffn_in), bf(w_ffn_out))

        mem_n = rms_norm_bf16(mem_prompt.reshape(bsz * n_mem, d), norm_mem[l])
        w_mem = bf(w_mem_kv)
        w_c = H_C * DH_C
        mk_p = matmul_cols(mem_n, w_mem, 0, w_c, F32).reshape(bsz, n_mem, w_c)
        mv_p = matmul_cols(mem_n, w_mem, w_c, w_c, F32).reshape(bsz, n_mem, w_c)
        yp, ka, va, kb, vb = _layer(yp, mk_p, mv_p, None, None, *shared)
        new = [heads(ka, H_A), heads(va, H_A), heads(kb[:, s - lb_prompt:], H_B),
               heads(vb[:, s - lb_prompt:], H_B), heads(mk_p, H_C), heads(mv_p, H_C)]

        ys, ka, va, kb, vb = _layer(ys, flat(cache_mem_k[l]), flat(cache_mem_v[l]),
                                    (flat(cache_a_k[l]), flat(cache_a_v[l])),
                                    (flat(cache_b_k[l]), flat(cache_b_v[l])), *shared)
        new += [heads(ka, H_A), heads(va, H_A), heads(kb, H_B), heads(vb, H_B)]
        for o, a in zip(outs, new):
            o.append(a)
    return (yp, ys) + tuple(jnp.stack(o) for o in outs)
```

```python
import functools
import math

import jax
import jax.numpy as jnp
from jax import lax
from jax.experimental import pallas as pl
from jax.experimental.pallas import tpu as pltpu

F32 = jnp.float32
BF16 = jnp.bfloat16

CHUNK = 64
H_A = 8
DH_A = 64
DK_A = 2 * DH_A
DV_A = 128
H_B = 8
DH_B = 128
N_PREV_CHUNKS = 8
BAND_PAST = N_PREV_CHUNKS * CHUNK
MAX_REL = 128
H_C = 4
DH_C = 256
N_BRANCH = 3
EPS = 1e-6
NEG_INF = -1e30

LANES = 128
VMEM_LIMIT = 56 * 1024 * 1024
BAND_GROUP = 256
A_BLOCK = 256

CONTRACT_LAST = (((1,), (1,)), ((), ()))


def _params(*sem):
    return pltpu.CompilerParams(dimension_semantics=sem, vmem_limit_bytes=VMEM_LIMIT)


def _block(n, target):
    if n <= target:
        return n
    b = target
    while n % b:
        b //= 2
    return b


def _rms_kernel(x_ref, g_ref, o_ref):
    x = x_ref[...]
    ms = jnp.mean(x * x, axis=-1, keepdims=True)
    o_ref[...] = (x * lax.rsqrt(ms + EPS) * g_ref[...]).astype(o_ref.dtype)


def rms_norm_bf16(x, g):
    m, d = x.shape
    bm = _block(m, 512)
    return pl.pallas_call(
        _rms_kernel,
        grid=(m // bm,),
        in_specs=[pl.BlockSpec((bm, d), lambda i: (i, 0)),
                  pl.BlockSpec((1, d), lambda i: (0, 0))],
        out_specs=pl.BlockSpec((bm, d), lambda i: (i, 0)),
        out_shape=jax.ShapeDtypeStruct((m, d), BF16),
        name="rms_norm",
        compiler_params=_params("parallel"),
    )(x, g.reshape(1, d))


def _mm_kernel(a_ref, w_ref, o_ref, wb_ref):
    @pl.when(pl.program_id(1) == 0)
    def _():
        wb_ref[...] = w_ref[...].astype(BF16)

    o_ref[...] = jnp.dot(a_ref[...], wb_ref[...], preferred_element_type=F32).astype(o_ref.dtype)


def matmul_cols(a, w, col0, n, out_dtype):
    m, k = a.shape
    bm = _block(m, 1024)
    bn = _block(n, 1024)
    assert col0 % bn == 0
    off = col0 // bn
    return pl.pallas_call(
        _mm_kernel,
        grid=(n // bn, m // bm),
        in_specs=[pl.BlockSpec((bm, k), lambda j, i: (i, 0)),
                  pl.BlockSpec((k, bn), lambda j, i: (0, j + off))],
        out_specs=pl.BlockSpec((bm, bn), lambda j, i: (i, j)),
        out_shape=jax.ShapeDtypeStruct((m, n), out_dtype),
        scratch_shapes=[pltpu.VMEM((k, bn), BF16)],
        name="matmul_cols",
        compiler_params=_params("arbitrary", "arbitrary"),
    )(a, w)


def _diff_lambda(lq1_ref, lk1_ref, lq2_ref, lk2_ref, lam_init):
    return (jnp.exp(jnp.sum(lq1_ref[...] * lk1_ref[...], keepdims=True))
            - jnp.exp(jnp.sum(lq2_ref[...] * lk2_ref[...], keepdims=True)) + lam_init)


def _head_norm(o, sub_ref, lam_init):
    ms = jnp.mean(o * o, axis=-1, keepdims=True)
    return o * lax.rsqrt(ms + EPS) * sub_ref[...] * (1.0 - lam_init)


def _own_alibi(n, slope):
    r = lax.broadcasted_iota(jnp.int32, (n, n), 0)
    c = lax.broadcasted_iota(jnp.int32, (n, n), 1)
    bias = slope * (r - jnp.abs(r - c)).astype(F32)
    return jnp.where(c // CHUNK <= r // CHUNK, bias, NEG_INF)


def _attn_a_kernel(q_ref, k_ref, v_ref, slope_ref, lq1_ref, lk1_ref, lq2_ref, lk2_ref, sub_ref, o_ref,
                   *, tq, lam_init):
    t = q_ref.shape[1]
    n_blocks = t // tq
    slope = slope_ref[0][:, :1]
    lam = _diff_lambda(lq1_ref, lk1_ref, lq2_ref, lk2_ref, lam_init)
    k = k_ref[0].astype(BF16)
    v = v_ref[0].astype(BF16)
    first_half = lax.broadcasted_iota(jnp.int32, (1, DK_A), 1) < DH_A
    own_bias = _own_alibi(tq, slope)
    n_before_max = (n_blocks - 1) * tq
    if n_before_max:
        j = lax.broadcasted_iota(jnp.int32, (1, n_before_max), 1)
        before_bias = slope * (j - n_before_max).astype(F32)

    for i in range(n_blocks):
        nb = i * tq
        q = q_ref[0, nb:nb + tq, :] * (DH_A ** -0.5)
        k_own, v_own = k[nb:nb + tq], v[nb:nb + tq]

        def softmax_pv(qh):
            s_own = lax.dot_general(qh, k_own, CONTRACT_LAST, preferred_element_type=F32) + own_bias
            m = jnp.max(s_own, axis=-1, keepdims=True)
            if nb:
                s_bef = lax.dot_general(qh, k[:nb], CONTRACT_LAST, preferred_element_type=F32)
                s_bef = s_bef + before_bias[:, n_before_max - nb:]
                m = jnp.maximum(m, jnp.max(s_bef, axis=-1, keepdims=True))
            p_own = jnp.exp(s_own - m)
            l = jnp.sum(p_own, axis=-1, keepdims=True)
            pv = jnp.dot(p_own.astype(BF16), v_own, preferred_element_type=F32)
            if nb:
                p_bef = jnp.exp(s_bef - m)
                l = l + jnp.sum(p_bef, axis=-1, keepdims=True)
                pv = pv + jnp.dot(p_bef.astype(BF16), v[:nb], preferred_element_type=F32)
            return pv / l

        o = (softmax_pv(jnp.where(first_half, q, jnp.zeros_like(q)))
             - lam * softmax_pv(jnp.where(first_half, jnp.zeros_like(q), q)))
        o_ref[0, nb:nb + tq, :] = _head_norm(o, sub_ref, lam_init).astype(o_ref.dtype)


def _row(a):
    return a.reshape(1, -1).astype(F32)


def _const_spec(shape):
    return pl.BlockSpec(shape, lambda *_: (0,) * len(shape))


def attention_a(q, k, v, lq1, lk1, lq2, lk2, subln, lam_init):
    b, t, _ = q.shape
    tq = _block(t, A_BLOCK)
    assert tq % CHUNK == 0
    slopes = jnp.asarray([[[2.0 ** (-8.0 * (hh + 1) / H_A)] * LANES] for hh in range(H_A)], dtype=F32)
    head = lambda width: pl.BlockSpec((1, t, width), lambda bi, hi: (bi, 0, hi))
    return pl.pallas_call(
        functools.partial(_attn_a_kernel, tq=tq, lam_init=lam_init),
        grid=(b, H_A),
        in_specs=[head(DK_A), head(DK_A), head(DV_A),
                  pl.BlockSpec((1, 1, LANES), lambda bi, hi: (hi, 0, 0)),
                  _const_spec((1, DH_A)), _const_spec((1, DH_A)), _const_spec((1, DH_A)),
                  _const_spec((1, DH_A)), _const_spec((1, DV_A))],
        out_specs=head(DV_A),
        out_shape=jax.ShapeDtypeStruct((b, t, H_A * DV_A), BF16),
        name="attention_a",
        compiler_params=_params("parallel", "arbitrary"),
    )(q, k, v, slopes, _row(lq1), _row(lk1), _row(lq2), _row(lk2), _row(subln))


def _attn_a_decode_kernel(q_ref, kn_ref, vn_ref, ck_ref, cv_ref, lq1_ref, lk1_ref, lq2_ref, lk2_ref,
                          sub_ref, o_ref, *, lam_init):
    t = q_ref.shape[1]
    p_len = ck_ref.shape[1] // H_A
    lam = _diff_lambda(lq1_ref, lk1_ref, lq2_ref, lk2_ref, lam_init)
    first_half = lax.broadcasted_iota(jnp.int32, (1, DK_A), 1) < DH_A
    j = lax.broadcasted_iota(jnp.int32, (1, p_len), 1)
    before_dist = (j - p_len).astype(F32)
    for h in range(H_A):
        slope = 2.0 ** (-8.0 * (h + 1) / H_A)
        cols = slice(h * DK_A, (h + 1) * DK_A)
        q = q_ref[0, :, cols] * (DH_A ** -0.5)
        kc = ck_ref[0, pl.ds(h, p_len, stride=H_A), :].astype(BF16)
        vc = cv_ref[0, pl.ds(h, p_len, stride=H_A), :].astype(BF16)
        kn = kn_ref[0, :, cols].astype(BF16)
        vn = vn_ref[0, :, cols].astype(BF16)
        own_bias = _own_alibi(t, slope)
        before_bias = slope * before_dist

        def softmax_pv(qh):
            s_new = lax.dot_general(qh, kn, CONTRACT_LAST, preferred_element_type=F32) + own_bias
            s_old = lax.dot_general(qh, kc, CONTRACT_LAST, preferred_element_type=F32) + before_bias
            m = jnp.maximum(jnp.max(s_new, axis=-1, keepdims=True), jnp.max(s_old, axis=-1, keepdims=True))
            p_new = jnp.exp(s_new - m)
            p_old = jnp.exp(s_old - m)
            l = jnp.sum(p_new, axis=-1, keepdims=True) + jnp.sum(p_old, axis=-1, keepdims=True)
            pv = (jnp.dot(p_new.astype(BF16), vn, preferred_element_type=F32)
                  + jnp.dot(p_old.astype(BF16), vc, preferred_element_type=F32))
            return pv / l

        o = (softmax_pv(jnp.where(first_half, q, jnp.zeros_like(q)))
             - lam * softmax_pv(jnp.where(first_half, jnp.zeros_like(q), q)))
        o_ref[0, :, cols] = _head_norm(o, sub_ref, lam_init).astype(o_ref.dtype)


def attention_a_decode(q, k_new, v_new, cache_k, cache_v, lq1, lk1, lq2, lk2, subln, lam_init):
    b, t, w = q.shape
    p_len = cache_k.shape[1]
    assert p_len % CHUNK == 0 and t <= CHUNK
    rows = lambda a: a.reshape(b, p_len * H_A, a.shape[-1])
    new = pl.BlockSpec((1, t, w), lambda bi: (bi, 0, 0))
    old = pl.BlockSpec((1, p_len * H_A, DK_A), lambda bi: (bi, 0, 0))
    return pl.pallas_call(
        functools.partial(_attn_a_decode_kernel, lam_init=lam_init),
        grid=(b,),
        in_specs=[new, new, new, old, old,
                  _const_spec((1, DH_A)), _const_spec((1, DH_A)), _const_spec((1, DH_A)),
                  _const_spec((1, DH_A)), _const_spec((1, DV_A))],
        out_specs=new,
        out_shape=jax.ShapeDtypeStruct((b, t, w), BF16),
        name="attention_a_decode",
        compiler_params=_params("parallel"),
    )(q, k_new, v_new, rows(cache_k), rows(cache_v), _row(lq1), _row(lk1), _row(lq2), _row(lk2), _row(subln))


def _band_tile(tq):
    pad = -(-tq // LANES) * LANES
    return BAND_PAST + pad, pad


def _band_bias_mask(g_row, tq):
    tile_w, pad = _band_tile(tq)
    width = g_row.shape[1]
    assert width == tile_w + pad
    rolled = pltpu.roll(jnp.broadcast_to(g_row, (tq, width)), width - pad, 1, stride=1, stride_axis=0)
    r = lax.broadcasted_iota(jnp.int32, (tq, tile_w), 0)
    c = lax.broadcasted_iota(jnp.int32, (tq, tile_w), 1)
    qch = r // CHUNK
    kch = c // CHUNK - N_PREV_CHUNKS
    valid = (kch <= qch) & (kch >= qch - N_PREV_CHUNKS)
    return jnp.where(valid, rolled[:, :tile_w], NEG_INF)


def _toeplitz_rows(rel_bias, tq):
    tile_w, pad = _band_tile(tq)
    width = tile_w + pad
    n_lo = pad + BAND_PAST - MAX_REL
    n_hi = max(width - n_lo - (2 * MAX_REL + 1), 0)
    lo = jnp.broadcast_to(rel_bias[:, :1], (H_B, n_lo))
    hi = jnp.broadcast_to(rel_bias[:, -1:], (H_B, n_hi))
    g = jnp.concatenate([lo, rel_bias, hi], axis=1)[:, :width]
    return g.reshape(H_B, 1, width).astype(F32)


def _attn_b_kernel(q_ref, k_ref, v_ref, g_ref, o_ref, *, tq):
    t = q_ref.shape[1]
    tile_w, _ = _band_tile(tq)
    scale = DH_B ** -0.5
    k = k_ref[0].astype(BF16)
    v = v_ref[0].astype(BF16)
    bias_mask = _band_bias_mask(g_ref[0], tq)

    for gi in range(t // tq):
        qa = gi * tq
        lo = max(qa - BAND_PAST, 0)
        hi = qa + tq
        off = lo - qa + BAND_PAST
        assert off % LANES == 0 and off + hi - lo == tile_w
        q = q_ref[0, qa:qa + tq, :]
        s = lax.dot_general(q, k[lo:hi], CONTRACT_LAST, preferred_element_type=F32) * scale
        s = s + bias_mask[:, off:]
        m = jnp.max(s, axis=-1, keepdims=True)
        p = jnp.exp(s - m)
        l = jnp.sum(p, axis=-1, keepdims=True)
        o = jnp.dot(p.astype(BF16), v[lo:hi], preferred_element_type=F32) / l
        o_ref[0, qa:qa + tq, :] = o.astype(o_ref.dtype)


def attention_b(q, k, v, rel_bias):
    b, t, _ = q.shape
    tq = _block(t, BAND_GROUP)
    assert tq % LANES == 0
    g = _toeplitz_rows(rel_bias, tq)
    head = pl.BlockSpec((1, t, DH_B), lambda bi, hi: (bi, 0, hi))
    return pl.pallas_call(
        functools.partial(_attn_b_kernel, tq=tq),
        grid=(b, H_B),
        in_specs=[head, head, head, pl.BlockSpec((1, 1, g.shape[2]), lambda bi, hi: (hi, 0, 0))],
        out_specs=head,
        out_shape=jax.ShapeDtypeStruct((b, t, H_B * DH_B), BF16),
        name="attention_b",
        compiler_params=_params("parallel", "arbitrary"),
    )(q, k, v, g)


def _attn_b_decode_kernel(q_ref, kn_ref, vn_ref, ck_ref, cv_ref, g_ref, o_ref):
    t = q_ref.shape[1]
    scale = DH_B ** -0.5
    for h in range(H_B):
        cols = slice(h * DH_B, (h + 1) * DH_B)
        q = q_ref[0, :, cols]
        kc = ck_ref[0, pl.ds(h, BAND_PAST, stride=H_B), :].astype(BF16)
        vc = cv_ref[0, pl.ds(h, BAND_PAST, stride=H_B), :].astype(BF16)
        kn = kn_ref[0, :, cols].astype(BF16)
        vn = vn_ref[0, :, cols].astype(BF16)
        bias_mask = _band_bias_mask(g_ref[h], t)
        s_old = lax.dot_general(q, kc, CONTRACT_LAST, preferred_element_type=F32) * scale
        s_old = s_old + bias_mask[:, :BAND_PAST]
        s_new = lax.dot_general(q, kn, CONTRACT_LAST, preferred_element_type=F32) * scale
        s_new = s_new + bias_mask[:, BAND_PAST:BAND_PAST + t]
        m = jnp.maximum(jnp.max(s_new, axis=-1, keepdims=True), jnp.max(s_old, axis=-1, keepdims=True))
        p_new = jnp.exp(s_new - m)
        p_old = jnp.exp(s_old - m)
        l = jnp.sum(p_new, axis=-1, keepdims=True) + jnp.sum(p_old, axis=-1, keepdims=True)
        pv = (jnp.dot(p_new.astype(BF16), vn, preferred_element_type=F32)
              + jnp.dot(p_old.astype(BF16), vc, preferred_element_type=F32))
        o_ref[0, :, cols] = (pv / l).astype(o_ref.dtype)


def attention_b_decode(q, k_new, v_new, cache_k, cache_v, rel_bias, p_len):
    b, t, w = q.shape
    assert cache_k.shape[1] == BAND_PAST and p_len % CHUNK == 0 and p_len >= BAND_PAST and t <= CHUNK
    g = _toeplitz_rows(rel_bias, t)
    rows = lambda a: a.reshape(b, BAND_PAST * H_B, a.shape[-1])
    new = pl.BlockSpec((1, t, w), lambda bi: (bi, 0, 0))
    old = pl.BlockSpec((1, BAND_PAST * H_B, DH_B), lambda bi: (bi, 0, 0))
    return pl.pallas_call(
        _attn_b_decode_kernel,
        grid=(b,),
        in_specs=[new, new, new, old, old, _const_spec(g.shape)],
        out_specs=new,
        out_shape=jax.ShapeDtypeStruct((b, t, w), BF16),
        name="attention_b_decode",
        compiler_params=_params("parallel"),
    )(q, k_new, v_new, rows(cache_k), rows(cache_v), g)


def _attn_c_kernel(q_ref, k_ref, v_ref, o_ref, *, tq):
    t = q_ref.shape[1]
    scale = DH_C ** -0.5
    k = k_ref[0].astype(BF16)
    v = v_ref[0].astype(BF16)
    for i in range(t // tq):
        q = q_ref[0, i * tq:(i + 1) * tq, :]
        s = lax.dot_general(q, k, CONTRACT_LAST, preferred_element_type=F32) * scale
        m = jnp.max(s, axis=-1, keepdims=True)
        p = jnp.exp(s - m)
        l = jnp.sum(p, axis=-1, keepdims=True)
        o = jnp.dot(p.astype(BF16), v, preferred_element_type=F32) / l
        o_ref[0, i * tq:(i + 1) * tq, :] = o.astype(o_ref.dtype)


def attention_c(q, mk, mv):
    b, t, _ = q.shape
    n_mem = mk.shape[1]
    tq = _block(t, 512)
    return pl.pallas_call(
        functools.partial(_attn_c_kernel, tq=tq),
        grid=(b, H_C),
        in_specs=[pl.BlockSpec((1, t, DH_C), lambda bi, hi: (bi, 0, hi)),
                  pl.BlockSpec((1, n_mem, DH_C), lambda bi, hi: (bi, 0, hi)),
                  pl.BlockSpec((1, n_mem, DH_C), lambda bi, hi: (bi, 0, hi))],
        out_specs=pl.BlockSpec((1, t, DH_C), lambda bi, hi: (bi, 0, hi)),
        out_shape=jax.ShapeDtypeStruct((b, t, H_C * DH_C), BF16),
        name="attention_c",
        compiler_params=_params("parallel", "arbitrary"),
    )(q, mk, mv)


def _merge_kernel(oa_ref, ob_ref, oc_ref, ga_ref, gb_ref, gc_ref, ba_ref, bb_ref, bc_ref,
                  wa_ref, wb_ref, wc_ref, o_ref):
    def branch(o_r, g_r, b_r, w_r):
        gate = jax.nn.sigmoid(g_r[...].astype(F32) + b_r[...])
        return gate * jnp.dot(o_r[...], w_r[...], preferred_element_type=F32)

    merged = (branch(oa_ref, ga_ref, ba_ref, wa_ref) + branch(ob_ref, gb_ref, bb_ref, wb_ref)
              + branch(oc_ref, gc_ref, bc_ref, wc_ref))
    o_ref[...] = merged.astype(o_ref.dtype)


def merge_branches(oa, ob, oc, gate_logits, b_gate, wa, wb, wc):
    m, w_in = oa.shape
    d = wa.shape[1]
    bm = _block(m, 256)
    row = lambda width: pl.BlockSpec((bm, width), lambda i: (i, 0))
    gate = lambda j: pl.BlockSpec((bm, d), lambda i: (i, j))
    bias = lambda j: pl.BlockSpec((1, d), lambda i: (0, j))
    weight = pl.BlockSpec((w_in, d), lambda i: (0, 0), pipeline_mode=pl.Buffered(1))
    bg = b_gate.reshape(1, N_BRANCH * d).astype(F32)
    return pl.pallas_call(
        _merge_kernel,
        grid=(m // bm,),
        in_specs=[row(w_in), row(w_in), row(w_in), gate(0), gate(1), gate(2),
                  bias(0), bias(1), bias(2), weight, weight, weight],
        out_specs=pl.BlockSpec((bm, d), lambda i: (i, 0)),
        out_shape=jax.ShapeDtypeStruct((m, d), BF16),
        name="merge_branches",
        compiler_params=_params("parallel"),
    )(oa, ob, oc, gate_logits, gate_logits, gate_logits, bg, bg, bg, wa, wb, wc)


def _proj_norm_res_kernel(*refs, next_norm):
    if next_norm:
        a_ref, w_ref, x_ref, g_ref, g2_ref, o_ref, h_ref = refs
    else:
        a_ref, w_ref, x_ref, g_ref, o_ref = refs
    y = jnp.dot(a_ref[...], w_ref[...], preferred_element_type=F32)
    ms = jnp.mean(y * y, axis=-1, keepdims=True)
    o = x_ref[...] + y * lax.rsqrt(ms + EPS) * g_ref[...]
    o_ref[...] = o
    if next_norm:
        ms2 = jnp.mean(o * o, axis=-1, keepdims=True)
        h_ref[...] = (o * lax.rsqrt(ms2 + EPS) * g2_ref[...]).astype(h_ref.dtype)


def proj_norm_residual(a, w, x, g, bm, next_g=None):
    m, k = a.shape
    d = w.shape[1]
    bm = _block(m, bm)
    vec = pl.BlockSpec((1, d), lambda i: (0, 0))
    row = pl.BlockSpec((bm, d), lambda i: (i, 0))
    in_specs = [pl.BlockSpec((bm, k), lambda i: (i, 0)),
                pl.BlockSpec((k, d), lambda i: (0, 0), pipeline_mode=pl.Buffered(1)), row, vec]
    args = [a, w, x, g.reshape(1, d).astype(F32)]
    out_specs, out_shape = row, jax.ShapeDtypeStruct((m, d), F32)
    if next_g is not None:
        in_specs.append(vec)
        args.append(next_g.reshape(1, d).astype(F32))
        out_specs = (row, row)
        out_shape = (out_shape, jax.ShapeDtypeStruct((m, d), BF16))
    return pl.pallas_call(
        functools.partial(_proj_norm_res_kernel, next_norm=next_g is not None),
        grid=(m // bm,),
        in_specs=in_specs,
        out_specs=out_specs,
        out_shape=out_shape,
        name="proj_norm_residual",
        compiler_params=_params("parallel"),
    )(*args)


def _ffn_in_kernel(h_ref, wa_ref, wb_ref, o_ref, wa_bf, wb_bf):
    @pl.when(pl.program_id(1) == 0)
    def _():
        wa_bf[...] = wa_ref[...].astype(BF16)
        wb_bf[...] = wb_ref[...].astype(BF16)

    h = h_ref[...]
    a = jnp.dot(h, wa_bf[...], preferred_element_type=F32)
    b = jnp.dot(h, wb_bf[...], preferred_element_type=F32)
    o_ref[...] = (jax.nn.silu(a) * b).astype(o_ref.dtype)


def ffn_in(h, w):
    m, k = h.shape
    f = w.shape[1] // 2
    bm = _block(m, 1024)
    bf = 512
    assert f % bf == 0
    nf = f // bf
    return pl.pallas_call(
        _ffn_in_kernel,
        grid=(nf, m // bm),
        in_specs=[pl.BlockSpec((bm, k), lambda j, i: (i, 0)),
                  pl.BlockSpec((k, bf), lambda j, i: (0, j)),
                  pl.BlockSpec((k, bf), lambda j, i: (0, j + nf))],
        out_specs=pl.BlockSpec((bm, bf), lambda j, i: (i, j)),
        out_shape=jax.ShapeDtypeStruct((m, f), BF16),
        scratch_shapes=[pltpu.VMEM((k, bf), BF16), pltpu.VMEM((k, bf), BF16)],
        name="ffn_in",
        compiler_params=_params("arbitrary", "arbitrary"),
    )(h, w, w)


def _layer(x, mk, mv, caches, lam_init, norm_mix_pre, norm_mix_post, w_in, b_gate,
           lq1, lk1, lq2, lk2, subln_a, rel_bias, w_br_a, w_br_b, w_br_c, w_out,
           norm_ffn_pre, norm_ffn_post, w_ffn_in, w_ffn_out):
    b, t, d = x.shape
    m = b * t
    xf = x.reshape(m, d)
    w_head = H_A * DK_A
    h = rms_norm_bf16(xf, norm_mix_pre)
    proj = lambda idx, dt: matmul_cols(h, w_in, idx * w_head, w_head, dt).reshape(b, t, w_head)
    q_a, k_a, v_a = proj(0, BF16), proj(1, F32), proj(2, F32)
    q_b, k_b, v_b = proj(3, BF16), proj(4, F32), proj(5, F32)
    q_c = proj(6, BF16)
    gate_logits = matmul_cols(h, w_in, 7 * w_head, N_BRANCH * d, BF16)

    if caches is None:
        o_a = attention_a(q_a, k_a, v_a, lq1, lk1, lq2, lk2, subln_a, lam_init)
        o_b = attention_b(q_b, k_b, v_b, rel_bias)
    else:
        ca_k, ca_v, cb_k, cb_v = caches
        o_a = attention_a_decode(q_a, k_a, v_a, ca_k, ca_v, lq1, lk1, lq2, lk2, subln_a, lam_init)
        o_b = attention_b_decode(q_b, k_b, v_b, cb_k, cb_v, rel_bias, ca_k.shape[1])
    o_c = attention_c(q_c, mk, mv)

    merged = merge_branches(o_a.reshape(m, -1), o_b.reshape(m, -1), o_c.reshape(m, -1),
                            gate_logits, b_gate, w_br_a, w_br_b, w_br_c)
    x1, h2 = proj_norm_residual(merged, w_out, xf, norm_mix_post, bm=512, next_g=norm_ffn_pre)
    act = ffn_in(h2, w_ffn_in)
    y = proj_norm_residual(act, w_ffn_out, x1, norm_ffn_post, bm=256)
    return y.reshape(b, t, d), k_a, v_a, k_b, v_b


def kernel(x_prompt, x_sample, cache_a_k, cache_a_v, cache_b_k, cache_b_v, cache_mem_k, cache_mem_v, mem_prompt, norm_mix_pre, norm_mix_post, norm_mem, w_in, b_gate, lambda_q1, lambda_k1, lambda_q2, lambda_k2, subln_a, rel_bias_b, w_mem_kv, w_br_a, w_br_b, w_br_c, w_out, norm_ffn_pre, norm_ffn_post, w_ffn_in, w_ffn_out):
    depth = w_in.shape[0]
    bsz, s, d = x_prompt.shape
    n_mem = mem_prompt.shape[1]
    lb_prompt = min(BAND_PAST, s)
    yp, ys = x_prompt, x_sample
    outs = [[] for _ in range(10)]
    heads = lambda a, nh: a.reshape(a.shape[0], a.shape[1], nh, a.shape[2] // nh)
    flat = lambda a: a.reshape(a.shape[0], a.shape[1], -1)
    for l in range(depth):
        lam_init = 0.8 - 0.6 * math.exp(-0.3 * l)
        bf = lambda w: w[l].astype(BF16)
        shared = (lam_init, norm_mix_pre[l], norm_mix_post[l], w_in[l], b_gate[l],
                  lambda_q1[l], lambda_k1[l], lambda_q2[l], lambda_k2[l], subln_a[l], rel_bias_b[l],
                  bf(w_br_a), bf(w_br_b), bf(w_br_c), bf(w_out),
                  norm_ffn_pre[l], norm_ffn_post[l], w_ffn_in[l], bf(w_ffn_out))

        mem_n = rms_norm_bf16(mem_prompt.reshape(bsz * n_mem, d), norm_mem[l])
        w_c = H_C * DH_C
        mk_p = matmul_cols(mem_n, w_mem_kv[l], 0, w_c, F32).reshape(bsz, n_mem, w_c)
        mv_p = matmul_cols(mem_n, w_mem_kv[l], w_c, w_c, F32).reshape(bsz, n_mem, w_c)
        yp, ka, va, kb, vb = _layer(yp, mk_p, mv_p, None, *shared)
        new = [heads(ka, H_A), heads(va, H_A), heads(kb[:, s - lb_prompt:], H_B),
               heads(vb[:, s - lb_prompt:], H_B), heads(mk_p, H_C), heads(mv_p, H_C)]

        caches = (cache_a_k[l], cache_a_v[l], cache_b_k[l], cache_b_v[l])
        ys, ka, va, kb, vb = _layer(ys, flat(cache_mem_k[l]), flat(cache_mem_v[l]), caches, *shared)
        new += [heads(ka, H_A), heads(va, H_A), heads(kb, H_B), heads(vb, H_B)]
        for o, a in zip(outs, new):
            o.append(a)
    return (yp, ys) + tuple(jnp.stack(o) for o in outs)
```

```python
import functools
import math

import jax
import jax.numpy as jnp
from jax import lax
from jax.experimental import pallas as pl
from jax.experimental.pallas import tpu as pltpu

F32 = jnp.float32
BF16 = jnp.bfloat16

CHUNK = 64
H_A = 8
DH_A = 64
DK_A = 2 * DH_A
DV_A = 128
H_B = 8
DH_B = 128
N_PREV_CHUNKS = 8
BAND_PAST = N_PREV_CHUNKS * CHUNK
MAX_REL = 128
H_C = 4
DH_C = 256
N_BRANCH = 3
EPS = 1e-6
NEG_INF = -1e30

LANES = 128
VMEM_LIMIT = 56 * 1024 * 1024
BAND_GROUP = 256
A_BLOCK = 256
SAFE_LOGIT = 40.0

CONTRACT_LAST = (((1,), (1,)), ((), ()))


def _params(*sem):
    return pltpu.CompilerParams(dimension_semantics=sem, vmem_limit_bytes=VMEM_LIMIT)


def _block(n, target):
    if n <= target:
        return n
    b = target
    while n % b:
        b //= 2
    return b


def _rms_kernel(x_ref, g_ref, o_ref):
    x = x_ref[...]
    ms = jnp.mean(x * x, axis=-1, keepdims=True)
    o_ref[...] = (x * lax.rsqrt(ms + EPS) * g_ref[...]).astype(o_ref.dtype)


def rms_norm_bf16(x, g):
    m, d = x.shape
    bm = _block(m, 512)
    return pl.pallas_call(
        _rms_kernel,
        grid=(m // bm,),
        in_specs=[pl.BlockSpec((bm, d), lambda i: (i, 0)),
                  pl.BlockSpec((1, d), lambda i: (0, 0))],
        out_specs=pl.BlockSpec((bm, d), lambda i: (i, 0)),
        out_shape=jax.ShapeDtypeStruct((m, d), BF16),
        name="rms_norm",
        compiler_params=_params("parallel"),
    )(x, g.reshape(1, d))


def _mm_kernel(a_ref, w_ref, o_ref, wb_ref):
    @pl.when(pl.program_id(1) == 0)
    def _():
        wb_ref[...] = w_ref[...].astype(BF16)

    o_ref[...] = jnp.dot(a_ref[...], wb_ref[...], preferred_element_type=F32).astype(o_ref.dtype)


def matmul_cols(a, w, col0, n, out_dtype):
    m, k = a.shape
    bm = _block(m, 1024)
    bn = _block(n, 1024)
    assert col0 % bn == 0
    off = col0 // bn
    return pl.pallas_call(
        _mm_kernel,
        grid=(n // bn, m // bm),
        in_specs=[pl.BlockSpec((bm, k), lambda j, i: (i, 0)),
                  pl.BlockSpec((k, bn), lambda j, i: (0, j + off))],
        out_specs=pl.BlockSpec((bm, bn), lambda j, i: (i, j)),
        out_shape=jax.ShapeDtypeStruct((m, n), out_dtype),
        scratch_shapes=[pltpu.VMEM((k, bn), BF16)],
        name="matmul_cols",
        compiler_params=_params("arbitrary", "arbitrary"),
    )(a, w)


def _diff_lambda(lq1_ref, lk1_ref, lq2_ref, lk2_ref, lam_init):
    return (jnp.exp(jnp.sum(lq1_ref[...] * lk1_ref[...], keepdims=True))
            - jnp.exp(jnp.sum(lq2_ref[...] * lk2_ref[...], keepdims=True)) + lam_init)


def _head_norm(o, sub_ref, lam_init):
    ms = jnp.mean(o * o, axis=-1, keepdims=True)
    return o * lax.rsqrt(ms + EPS) * sub_ref[...] * (1.0 - lam_init)


def _own_alibi(n, slope):
    r = lax.broadcasted_iota(jnp.int32, (n, n), 0)
    c = lax.broadcasted_iota(jnp.int32, (n, n), 1)
    bias = slope * (r - jnp.abs(r - c)).astype(F32)
    return jnp.where(c // CHUNK <= r // CHUNK, bias, NEG_INF)


def _max_sq_norms(x, n_groups):
    w = x.shape[1]
    lane = lax.broadcasted_iota(jnp.int32, (w, LANES), 0)
    group = lax.broadcasted_iota(jnp.int32, (w, LANES), 1)
    indicator = jnp.where(lane // (w // n_groups) == group, 1.0, 0.0).astype(BF16)
    sq = jnp.square(x.astype(F32)).astype(BF16)
    return jnp.max(jnp.dot(sq, indicator, preferred_element_type=F32), axis=0, keepdims=True)


def _attn_a_kernel(q_ref, k_ref, v_ref, slope_ref, kf_ref, qf_ref, lq1_ref, lk1_ref, lq2_ref, lk2_ref,
                   sub_ref, o_ref, ke_scr, ve_scr, *, tq, lam_init):
    q_sq = _max_sq_norms(q_ref[0] * (DH_A ** -0.5), 2)
    k_sq = _max_sq_norms(k_ref[0].astype(BF16), 2)
    safe = jnp.max((q_sq * k_sq)[:, :2]) <= SAFE_LOGIT ** 2 / 1.03
    args = (q_ref, k_ref, v_ref, slope_ref, lq1_ref, lk1_ref, lq2_ref, lk2_ref, sub_ref, o_ref)

    @pl.when(safe)
    def _():
        _attn_a_bounded(*args, kf_ref, qf_ref, ke_scr, ve_scr, tq=tq, lam_init=lam_init)

    @pl.when(jnp.logical_not(safe))
    def _():
        _attn_a_general(*args, tq=tq, lam_init=lam_init)


def _position_features(t):
    pos = jnp.arange(t, dtype=jnp.int32)[:, None]
    lo = pos & 7
    hi = (pos - lo).astype(F32)
    lo = lo.astype(F32)
    one = jnp.ones((t, 1), F32)
    pad = jnp.zeros((t, LANES - 4), F32)
    k_side = jnp.concatenate([hi, lo, one, one, pad], axis=1).astype(BF16)
    q_side = jnp.concatenate([one, one, -hi, -lo, pad], axis=1)
    return k_side, q_side


def _attn_a_bounded(q_ref, k_ref, v_ref, slope_ref, lq1_ref, lk1_ref, lq2_ref, lk2_ref, sub_ref, o_ref,
                    kf_ref, qf_ref, ke_scr, ve_scr, *, tq, lam_init):
    t = q_ref.shape[1]
    slope = slope_ref[0][:, :1]
    lam = _diff_lambda(lq1_ref, lk1_ref, lq2_ref, lk2_ref, lam_init)
    first_half = lax.broadcasted_iota(jnp.int32, (1, DK_A), 1) < DH_A
    ke_scr[:, :DK_A] = k_ref[0].astype(BF16)
    ke_scr[:, DK_A:] = kf_ref[...]
    ve_scr[:, :DV_A] = v_ref[0].astype(BF16)
    ve_scr[:, DV_A:] = jnp.ones((t, DV_A), BF16)

    r = lax.broadcasted_iota(jnp.int32, (tq, tq), 0)
    c = lax.broadcasted_iota(jnp.int32, (tq, tq), 1)
    own_fix = jnp.where(c // CHUNK <= r // CHUNK, -2.0 * slope * jnp.maximum(c - r, 0).astype(F32), NEG_INF)
    own_fix = jnp.concatenate([own_fix, own_fix], axis=0)

    for i in range(t // tq):
        nb = i * tq
        q = q_ref[0, nb:nb + tq, :] * (DH_A ** -0.5)
        q_feat = (qf_ref[nb:nb + tq, :] * slope).astype(BF16)
        zero = jnp.zeros_like(q)
        qe = jnp.concatenate([jnp.concatenate([jnp.where(first_half, q, zero), q_feat], axis=1),
                              jnp.concatenate([jnp.where(first_half, zero, q), q_feat], axis=1)], axis=0)
        s_own = lax.dot_general(qe, ke_scr[nb:nb + tq, :], CONTRACT_LAST, preferred_element_type=F32)
        acc = jnp.dot(jnp.exp(s_own + own_fix).astype(BF16), ve_scr[nb:nb + tq, :],
                      preferred_element_type=F32)
        if nb:
            s_bef = lax.dot_general(qe, ke_scr[:nb, :], CONTRACT_LAST, preferred_element_type=F32)
            acc = acc + jnp.dot(jnp.exp(s_bef).astype(BF16), ve_scr[:nb, :], preferred_element_type=F32)
        o = acc[:, :DV_A] / acc[:, DV_A:]
        o = o[:tq] - lam * o[tq:]
        o_ref[0, nb:nb + tq, :] = _head_norm(o, sub_ref, lam_init).astype(o_ref.dtype)


def _attn_a_general(q_ref, k_ref, v_ref, slope_ref, lq1_ref, lk1_ref, lq2_ref, lk2_ref, sub_ref, o_ref,
                    *, tq, lam_init):
    t = q_ref.shape[1]
    n_blocks = t // tq
    slope = slope_ref[0][:, :1]
    lam = _diff_lambda(lq1_ref, lk1_ref, lq2_ref, lk2_ref, lam_init)
    k = k_ref[0].astype(BF16)
    v = v_ref[0].astype(BF16)
    first_half = lax.broadcasted_iota(jnp.int32, (1, DK_A), 1) < DH_A
    own_bias = _own_alibi(tq, slope)
    n_before_max = (n_blocks - 1) * tq
    if n_before_max:
        j = lax.broadcasted_iota(jnp.int32, (1, n_before_max), 1)
        before_bias = slope * (j - n_before_max).astype(F32)

    for i in range(n_blocks):
        nb = i * tq
        q = q_ref[0, nb:nb + tq, :] * (DH_A ** -0.5)
        k_own, v_own = k[nb:nb + tq], v[nb:nb + tq]

        def softmax_pv(qh):
            s_own = lax.dot_general(qh, k_own, CONTRACT_LAST, preferred_element_type=F32) + own_bias
            m = jnp.max(s_own, axis=-1, keepdims=True)
            if nb:
                s_bef = lax.dot_general(qh, k[:nb], CONTRACT_LAST, preferred_element_type=F32)
                s_bef = s_bef + before_bias[:, n_before_max - nb:]
                m = jnp.maximum(m, jnp.max(s_bef, axis=-1, keepdims=True))
            p_own = jnp.exp(s_own - m)
            l = jnp.sum(p_own, axis=-1, keepdims=True)
            pv = jnp.dot(p_own.astype(BF16), v_own, preferred_element_type=F32)
            if nb:
                p_bef = jnp.exp(s_bef - m)
                l = l + jnp.sum(p_bef, axis=-1, keepdims=True)
                pv = pv + jnp.dot(p_bef.astype(BF16), v[:nb], preferred_element_type=F32)
            return pv / l

        o = (softmax_pv(jnp.where(first_half, q, jnp.zeros_like(q)))
             - lam * softmax_pv(jnp.where(first_half, jnp.zeros_like(q), q)))
        o_ref[0, nb:nb + tq, :] = _head_norm(o, sub_ref, lam_init).astype(o_ref.dtype)


def _row(a):
    return a.reshape(1, -1).astype(F32)


def _const_spec(shape):
    return pl.BlockSpec(shape, lambda *_: (0,) * len(shape))


def attention_a(q, k, v, lq1, lk1, lq2, lk2, subln, lam_init):
    b, t, _ = q.shape
    tq = _block(t, A_BLOCK)
    assert tq % CHUNK == 0
    slopes = jnp.asarray([[[2.0 ** (-8.0 * (hh + 1) / H_A)] * LANES] for hh in range(H_A)], dtype=F32)
    head = lambda width: pl.BlockSpec((1, t, width), lambda bi, hi: (bi, 0, hi))
    k_feat, q_feat = _position_features(t)
    return pl.pallas_call(
        functools.partial(_attn_a_kernel, tq=tq, lam_init=lam_init),
        grid=(b, H_A),
        in_specs=[head(DK_A), head(DK_A), head(DV_A),
                  pl.BlockSpec((1, 1, LANES), lambda bi, hi: (hi, 0, 0)),
                  _const_spec((t, LANES)), _const_spec((t, LANES)),
                  _const_spec((1, DH_A)), _const_spec((1, DH_A)), _const_spec((1, DH_A)),
                  _const_spec((1, DH_A)), _const_spec((1, DV_A))],
        out_specs=head(DV_A),
        out_shape=jax.ShapeDtypeStruct((b, t, H_A * DV_A), BF16),
        scratch_shapes=[pltpu.VMEM((t, DK_A + LANES), BF16), pltpu.VMEM((t, 2 * DV_A), BF16)],
        name="attention_a",
        compiler_params=_params("parallel", "arbitrary"),
    )(q, k, v, slopes, k_feat, q_feat, _row(lq1), _row(lk1), _row(lq2), _row(lk2), _row(subln))


def _attn_a_decode_kernel(q_ref, kn_ref, vn_ref, ck_ref, cv_ref, lq1_ref, lk1_ref, lq2_ref, lk2_ref,
                          sub_ref, o_ref, *, lam_init):
    t = q_ref.shape[1]
    p_len = ck_ref.shape[1] // H_A
    lam = _diff_lambda(lq1_ref, lk1_ref, lq2_ref, lk2_ref, lam_init)
    first_half = lax.broadcasted_iota(jnp.int32, (1, DK_A), 1) < DH_A
    j = lax.broadcasted_iota(jnp.int32, (1, p_len), 1)
    before_dist = (j - p_len).astype(F32)
    for h in range(H_A):
        slope = 2.0 ** (-8.0 * (h + 1) / H_A)
        cols = slice(h * DK_A, (h + 1) * DK_A)
        q = q_ref[0, :, cols] * (DH_A ** -0.5)
        kc = ck_ref[0, pl.ds(h, p_len, stride=H_A), :].astype(BF16)
        vc = cv_ref[0, pl.ds(h, p_len, stride=H_A), :].astype(BF16)
        kn = kn_ref[0, :, cols].astype(BF16)
        vn = vn_ref[0, :, cols].astype(BF16)
        own_bias = _own_alibi(t, slope)
        before_bias = slope * before_dist

        def softmax_pv(qh):
            s_new = lax.dot_general(qh, kn, CONTRACT_LAST, preferred_element_type=F32) + own_bias
            s_old = lax.dot_general(qh, kc, CONTRACT_LAST, preferred_element_type=F32) + before_bias
            m = jnp.maximum(jnp.max(s_new, axis=-1, keepdims=True), jnp.max(s_old, axis=-1, keepdims=True))
            p_new = jnp.exp(s_new - m)
            p_old = jnp.exp(s_old - m)
            l = jnp.sum(p_new, axis=-1, keepdims=True) + jnp.sum(p_old, axis=-1, keepdims=True)
            pv = (jnp.dot(p_new.astype(BF16), vn, preferred_element_type=F32)
                  + jnp.dot(p_old.astype(BF16), vc, preferred_element_type=F32))
            return pv / l

        o = (softmax_pv(jnp.where(first_half, q, jnp.zeros_like(q)))
             - lam * softmax_pv(jnp.where(first_half, jnp.zeros_like(q), q)))
        o_ref[0, :, cols] = _head_norm(o, sub_ref, lam_init).astype(o_ref.dtype)


def attention_a_decode(q, k_new, v_new, cache_k, cache_v, lq1, lk1, lq2, lk2, subln, lam_init):
    b, t, w = q.shape
    p_len = cache_k.shape[1]
    assert p_len % CHUNK == 0 and t <= CHUNK
    rows = lambda a: a.reshape(b, p_len * H_A, a.shape[-1])
    new = pl.BlockSpec((1, t, w), lambda bi: (bi, 0, 0))
    old = pl.BlockSpec((1, p_len * H_A, DK_A), lambda bi: (bi, 0, 0))
    return pl.pallas_call(
        functools.partial(_attn_a_decode_kernel, lam_init=lam_init),
        grid=(b,),
        in_specs=[new, new, new, old, old,
                  _const_spec((1, DH_A)), _const_spec((1, DH_A)), _const_spec((1, DH_A)),
                  _const_spec((1, DH_A)), _const_spec((1, DV_A))],
        out_specs=new,
        out_shape=jax.ShapeDtypeStruct((b, t, w), BF16),
        name="attention_a_decode",
        compiler_params=_params("parallel"),
    )(q, k_new, v_new, rows(cache_k), rows(cache_v), _row(lq1), _row(lk1), _row(lq2), _row(lk2), _row(subln))


def _band_tile(tq):
    pad = -(-tq // LANES) * LANES
    return BAND_PAST + pad, pad


def _band_bias_mask(g_row, tq):
    tile_w, pad = _band_tile(tq)
    width = g_row.shape[1]
    assert width == tile_w + pad
    rolled = pltpu.roll(jnp.broadcast_to(g_row, (tq, width)), width - pad, 1, stride=1, stride_axis=0)
    r = lax.broadcasted_iota(jnp.int32, (tq, tile_w), 0)
    c = lax.broadcasted_iota(jnp.int32, (tq, tile_w), 1)
    qch = r // CHUNK
    kch = c // CHUNK - N_PREV_CHUNKS
    valid = (kch <= qch) & (kch >= qch - N_PREV_CHUNKS)
    return jnp.where(valid, rolled[:, :tile_w], NEG_INF)


def _toeplitz_rows(rel_bias, tq):
    tile_w, pad = _band_tile(tq)
    width = tile_w + pad
    n_lo = pad + BAND_PAST - MAX_REL
    n_hi = max(width - n_lo - (2 * MAX_REL + 1), 0)
    lo = jnp.broadcast_to(rel_bias[:, :1], (H_B, n_lo))
    hi = jnp.broadcast_to(rel_bias[:, -1:], (H_B, n_hi))
    g = jnp.concatenate([lo, rel_bias, hi], axis=1)[:, :width]
    return g.reshape(H_B, 1, width).astype(F32)


def _attn_b_kernel(q_ref, k_ref, v_ref, g_ref, o_ref, ve_scr, *, tq):
    t = q_ref.shape[1]
    tile_w, _ = _band_tile(tq)
    scale = DH_B ** -0.5
    bias_mask = _band_bias_mask(g_ref[0], tq)
    qk_sq = _max_sq_norms(q_ref[0], 1) * _max_sq_norms(k_ref[0].astype(BF16), 1)
    room = SAFE_LOGIT - jnp.max(jnp.abs(g_ref[0]))
    safe = jnp.logical_and(room > 0.0, jnp.max(qk_sq[:, :1]) * (scale * scale * 1.03) <= room * room)

    def groups():
        for gi in range(t // tq):
            qa = gi * tq
            lo = max(qa - BAND_PAST, 0)
            hi = qa + tq
            off = lo - qa + BAND_PAST
            assert off % LANES == 0 and off + hi - lo == tile_w
            yield slice(qa, hi), slice(lo, hi), off

    @pl.when(safe)
    def _():
        log2e = math.log2(math.e)
        k = k_ref[0].astype(BF16)
        ve_scr[:, :DH_B] = v_ref[0].astype(BF16)
        ve_scr[:, DH_B:] = jnp.ones((t, DH_B), BF16)
        bias2 = bias_mask * log2e
        for rows, keys, off in groups():
            s = lax.dot_general(q_ref[0, rows, :], k[keys], CONTRACT_LAST, preferred_element_type=F32)
            p = jnp.exp2(s * (scale * log2e) + bias2[:, off:])
            acc = jnp.dot(p.astype(BF16), ve_scr[keys, :], preferred_element_type=F32)
            o_ref[0, rows, :] = (acc[:, :DH_B] / acc[:, DH_B:]).astype(o_ref.dtype)

    @pl.when(jnp.logical_not(safe))
    def _():
        k = k_ref[0].astype(BF16)
        v = v_ref[0].astype(BF16)
        for rows, keys, off in groups():
            s = lax.dot_general(q_ref[0, rows, :], k[keys], CONTRACT_LAST, preferred_element_type=F32) * scale
            s = s + bias_mask[:, off:]
            m = jnp.max(s, axis=-1, keepdims=True)
            p = jnp.exp(s - m)
            l = jnp.sum(p, axis=-1, keepdims=True)
            o = jnp.dot(p.astype(BF16), v[keys], preferred_element_type=F32) / l
            o_ref[0, rows, :] = o.astype(o_ref.dtype)


def attention_b(q, k, v, rel_bias):
    b, t, _ = q.shape
    tq = _block(t, BAND_GROUP)
    assert tq % LANES == 0
    g = _toeplitz_rows(rel_bias, tq)
    head = pl.BlockSpec((1, t, DH_B), lambda bi, hi: (bi, 0, hi))
    return pl.pallas_call(
        functools.partial(_attn_b_kernel, tq=tq),
        grid=(b, H_B),
        in_specs=[head, head, head, pl.BlockSpec((1, 1, g.shape[2]), lambda bi, hi: (hi, 0, 0))],
        out_specs=head,
        out_shape=jax.ShapeDtypeStruct((b, t, H_B * DH_B), BF16),
        scratch_shapes=[pltpu.VMEM((t, 2 * DH_B), BF16)],
        name="attention_b",
        compiler_params=_params("parallel", "arbitrary"),
    )(q, k, v, g)


def _attn_b_decode_kernel(q_ref, kn_ref, vn_ref, ck_ref, cv_ref, g_ref, o_ref):
    t = q_ref.shape[1]
    scale = DH_B ** -0.5
    for h in range(H_B):
        cols = slice(h * DH_B, (h + 1) * DH_B)
        q = q_ref[0, :, cols]
        kc = ck_ref[0, pl.ds(h, BAND_PAST, stride=H_B), :].astype(BF16)
        vc = cv_ref[0, pl.ds(h, BAND_PAST, stride=H_B), :].astype(BF16)
        kn = kn_ref[0, :, cols].astype(BF16)
        vn = vn_ref[0, :, cols].astype(BF16)
        bias_mask = _band_bias_mask(g_ref[h], t)
        s_old = lax.dot_general(q, kc, CONTRACT_LAST, preferred_element_type=F32) * scale
        s_old = s_old + bias_mask[:, :BAND_PAST]
        s_new = lax.dot_general(q, kn, CONTRACT_LAST, preferred_element_type=F32) * scale
        s_new = s_new + bias_mask[:, BAND_PAST:BAND_PAST + t]
        m = jnp.maximum(jnp.max(s_new, axis=-1, keepdims=True), jnp.max(s_old, axis=-1, keepdims=True))
        p_new = jnp.exp(s_new - m)
        p_old = jnp.exp(s_old - m)
        l = jnp.sum(p_new, axis=-1, keepdims=True) + jnp.sum(p_old, axis=-1, keepdims=True)
        pv = (jnp.dot(p_new.astype(BF16), vn, preferred_element_type=F32)
              + jnp.dot(p_old.astype(BF16), vc, preferred_element_type=F32))
        o_ref[0, :, cols] = (pv / l).astype(o_ref.dtype)


def attention_b_decode(q, k_new, v_new, cache_k, cache_v, rel_bias, p_len):
    b, t, w = q.shape
    assert cache_k.shape[1] == BAND_PAST and p_len % CHUNK == 0 and p_len >= BAND_PAST and t <= CHUNK
    g = _toeplitz_rows(rel_bias, t)
    rows = lambda a: a.reshape(b, BAND_PAST * H_B, a.shape[-1])
    new = pl.BlockSpec((1, t, w), lambda bi: (bi, 0, 0))
    old = pl.BlockSpec((1, BAND_PAST * H_B, DH_B), lambda bi: (bi, 0, 0))
    return pl.pallas_call(
        _attn_b_decode_kernel,
        grid=(b,),
        in_specs=[new, new, new, old, old, _const_spec(g.shape)],
        out_specs=new,
        out_shape=jax.ShapeDtypeStruct((b, t, w), BF16),
        name="attention_b_decode",
        compiler_params=_params("parallel"),
    )(q, k_new, v_new, rows(cache_k), rows(cache_v), g)


def _attn_c_kernel(q_ref, k_ref, v_ref, o_ref, *, tq):
    t = q_ref.shape[1]
    scale = DH_C ** -0.5
    k = k_ref[0].astype(BF16)
    v = v_ref[0].astype(BF16)
    qk_sq = _max_sq_norms(q_ref[0], 1) * _max_sq_norms(k, 1)
    safe = jnp.max(qk_sq[:, :1]) * (scale * scale * 1.03) <= SAFE_LOGIT ** 2

    def finish(i, p):
        l = jnp.sum(p, axis=-1, keepdims=True)
        o = jnp.dot(p.astype(BF16), v, preferred_element_type=F32) / l
        o_ref[0, i * tq:(i + 1) * tq, :] = o.astype(o_ref.dtype)

    @pl.when(safe)
    def _():
        for i in range(t // tq):
            s = lax.dot_general(q_ref[0, i * tq:(i + 1) * tq, :], k, CONTRACT_LAST, preferred_element_type=F32)
            finish(i, jnp.exp2(s * (scale * math.log2(math.e))))

    @pl.when(jnp.logical_not(safe))
    def _():
        for i in range(t // tq):
            s = lax.dot_general(q_ref[0, i * tq:(i + 1) * tq, :], k, CONTRACT_LAST, preferred_element_type=F32)
            s = s * scale
            finish(i, jnp.exp(s - jnp.max(s, axis=-1, keepdims=True)))


def attention_c(q, mk, mv):
    b, t, _ = q.shape
    n_mem = mk.shape[1]
    tq = _block(t, 512)
    return pl.pallas_call(
        functools.partial(_attn_c_kernel, tq=tq),
        grid=(b, H_C),
        in_specs=[pl.BlockSpec((1, t, DH_C), lambda bi, hi: (bi, 0, hi)),
                  pl.BlockSpec((1, n_mem, DH_C), lambda bi, hi: (bi, 0, hi)),
                  pl.BlockSpec((1, n_mem, DH_C), lambda bi, hi: (bi, 0, hi))],
        out_specs=pl.BlockSpec((1, t, DH_C), lambda bi, hi: (bi, 0, hi)),
        out_shape=jax.ShapeDtypeStruct((b, t, H_C * DH_C), BF16),
        name="attention_c",
        compiler_params=_params("parallel", "arbitrary"),
    )(q, mk, mv)


def _merge_kernel(oa_ref, ob_ref, oc_ref, ga_ref, gb_ref, gc_ref, ba_ref, bb_ref, bc_ref,
                  wa_ref, wb_ref, wc_ref, o_ref):
    def branch(o_r, g_r, b_r, w_r):
        gate = jax.nn.sigmoid(g_r[...].astype(F32) + b_r[...])
        return gate * jnp.dot(o_r[...], w_r[...], preferred_element_type=F32)

    merged = (branch(oa_ref, ga_ref, ba_ref, wa_ref) + branch(ob_ref, gb_ref, bb_ref, wb_ref)
              + branch(oc_ref, gc_ref, bc_ref, wc_ref))
    o_ref[...] = merged.astype(o_ref.dtype)


def merge_branches(oa, ob, oc, gate_logits, b_gate, wa, wb, wc):
    m, w_in = oa.shape
    d = wa.shape[1]
    bm = _block(m, 256)
    row = lambda width: pl.BlockSpec((bm, width), lambda i: (i, 0))
    gate = lambda j: pl.BlockSpec((bm, d), lambda i: (i, j))
    bias = lambda j: pl.BlockSpec((1, d), lambda i: (0, j))
    weight = pl.BlockSpec((w_in, d), lambda i: (0, 0), pipeline_mode=pl.Buffered(1))
    bg = b_gate.reshape(1, N_BRANCH * d).astype(F32)
    return pl.pallas_call(
        _merge_kernel,
        grid=(m // bm,),
        in_specs=[row(w_in), row(w_in), row(w_in), gate(0), gate(1), gate(2),
                  bias(0), bias(1), bias(2), weight, weight, weight],
        out_specs=pl.BlockSpec((bm, d), lambda i: (i, 0)),
        out_shape=jax.ShapeDtypeStruct((m, d), BF16),
        name="merge_branches",
        compiler_params=_params("parallel"),
    )(oa, ob, oc, gate_logits, gate_logits, gate_logits, bg, bg, bg, wa, wb, wc)


def _proj_norm_res_kernel(*refs, next_norm):
    if next_norm:
        a_ref, w_ref, x_ref, g_ref, g2_ref, o_ref, h_ref = refs
    else:
        a_ref, w_ref, x_ref, g_ref, o_ref = refs
    y = jnp.dot(a_ref[...], w_ref[...], preferred_element_type=F32)
    ms = jnp.mean(y * y, axis=-1, keepdims=True)
    o = x_ref[...] + y * lax.rsqrt(ms + EPS) * g_ref[...]
    o_ref[...] = o
    if next_norm:
        ms2 = jnp.mean(o * o, axis=-1, keepdims=True)
        h_ref[...] = (o * lax.rsqrt(ms2 + EPS) * g2_ref[...]).astype(h_ref.dtype)


def proj_norm_residual(a, w, x, g, bm, next_g=None):
    m, k = a.shape
    d = w.shape[1]
    bm = _block(m, bm)
    vec = pl.BlockSpec((1, d), lambda i: (0, 0))
    row = pl.BlockSpec((bm, d), lambda i: (i, 0))
    in_specs = [pl.BlockSpec((bm, k), lambda i: (i, 0)),
                pl.BlockSpec((k, d), lambda i: (0, 0), pipeline_mode=pl.Buffered(1)), row, vec]
    args = [a, w, x, g.reshape(1, d).astype(F32)]
    out_specs, out_shape = row, jax.ShapeDtypeStruct((m, d), F32)
    if next_g is not None:
        in_specs.append(vec)
        args.append(next_g.reshape(1, d).astype(F32))
        out_specs = (row, row)
        out_shape = (out_shape, jax.ShapeDtypeStruct((m, d), BF16))
    return pl.pallas_call(
        functools.partial(_proj_norm_res_kernel, next_norm=next_g is not None),
        grid=(m // bm,),
        in_specs=in_specs,
        out_specs=out_specs,
        out_shape=out_shape,
        name="proj_norm_residual",
        compiler_params=_params("parallel"),
    )(*args)


def _ffn_in_kernel(h_ref, wa_ref, wb_ref, o_ref, wa_bf, wb_bf):
    @pl.when(pl.program_id(1) == 0)
    def _():
        wa_bf[...] = wa_ref[...].astype(BF16)
        wb_bf[...] = wb_ref[...].astype(BF16)

    h = h_ref[...]
    a = jnp.dot(h, wa_bf[...], preferred_element_type=F32)
    b = jnp.dot(h, wb_bf[...], preferred_element_type=F32)
    o_ref[...] = (jax.nn.silu(a) * b).astype(o_ref.dtype)


def ffn_in(h, w):
    m, k = h.shape
    f = w.shape[1] // 2
    bm = _block(m, 1024)
    bf = 512
    assert f % bf == 0
    nf = f // bf
    return pl.pallas_call(
        _ffn_in_kernel,
        grid=(nf, m // bm),
        in_specs=[pl.BlockSpec((bm, k), lambda j, i: (i, 0)),
                  pl.BlockSpec((k, bf), lambda j, i: (0, j)),
                  pl.BlockSpec((k, bf), lambda j, i: (0, j + nf))],
        out_specs=pl.BlockSpec((bm, bf), lambda j, i: (i, j)),
        out_shape=jax.ShapeDtypeStruct((m, f), BF16),
        scratch_shapes=[pltpu.VMEM((k, bf), BF16), pltpu.VMEM((k, bf), BF16)],
        name="ffn_in",
        compiler_params=_params("arbitrary", "arbitrary"),
    )(h, w, w)


def _layer(x, mk, mv, caches, lam_init, norm_mix_pre, norm_mix_post, w_in, b_gate,
           lq1, lk1, lq2, lk2, subln_a, rel_bias, w_br_a, w_br_b, w_br_c, w_out,
           norm_ffn_pre, norm_ffn_post, w_ffn_in, w_ffn_out):
    b, t, d = x.shape
    m = b * t
    xf = x.reshape(m, d)
    w_head = H_A * DK_A
    h = rms_norm_bf16(xf, norm_mix_pre)
    proj = lambda idx, dt: matmul_cols(h, w_in, idx * w_head, w_head, dt).reshape(b, t, w_head)
    q_a, k_a, v_a = proj(0, BF16), proj(1, F32), proj(2, F32)
    q_b, k_b, v_b = proj(3, BF16), proj(4, F32), proj(5, F32)
    q_c = proj(6, BF16)
    gate_logits = matmul_cols(h, w_in, 7 * w_head, N_BRANCH * d, BF16)

    if caches is None:
        o_a = attention_a(q_a, k_a, v_a, lq1, lk1, lq2, lk2, subln_a, lam_init)
        o_b = attention_b(q_b, k_b, v_b, rel_bias)
    else:
        ca_k, ca_v, cb_k, cb_v = caches
        o_a = attention_a_decode(q_a, k_a, v_a, ca_k, ca_v, lq1, lk1, lq2, lk2, subln_a, lam_init)
        o_b = attention_b_decode(q_b, k_b, v_b, cb_k, cb_v, rel_bias, ca_k.shape[1])
    o_c = attention_c(q_c, mk, mv)

    merged = merge_branches(o_a.reshape(m, -1), o_b.reshape(m, -1), o_c.reshape(m, -1),
                            gate_logits, b_gate, w_br_a, w_br_b, w_br_c)
    x1, h2 = proj_norm_residual(merged, w_out, xf, norm_mix_post, bm=512, next_g=norm_ffn_pre)
    act = ffn_in(h2, w_ffn_in)
    y = proj_norm_residual(act, w_ffn_out, x1, norm_ffn_post, bm=256)
    return y.reshape(b, t, d), k_a, v_a, k_b, v_b


def kernel(x_prompt, x_sample, cache_a_k, cache_a_v, cache_b_k, cache_b_v, cache_mem_k, cache_mem_v, mem_prompt, norm_mix_pre, norm_mix_post, norm_mem, w_in, b_gate, lambda_q1, lambda_k1, lambda_q2, lambda_k2, subln_a, rel_bias_b, w_mem_kv, w_br_a, w_br_b, w_br_c, w_out, norm_ffn_pre, norm_ffn_post, w_ffn_in, w_ffn_out):
    depth = w_in.shape[0]
    bsz, s, d = x_prompt.shape
    n_mem = mem_prompt.shape[1]
    lb_prompt = min(BAND_PAST, s)
    yp, ys = x_prompt, x_sample
    outs = [[] for _ in range(10)]
    heads = lambda a, nh: a.reshape(a.shape[0], a.shape[1], nh, a.shape[2] // nh)
    flat = lambda a: a.reshape(a.shape[0], a.shape[1], -1)
    for l in range(depth):
        lam_init = 0.8 - 0.6 * math.exp(-0.3 * l)
        bf = lambda w: w[l].astype(BF16)
        shared = (lam_init, norm_mix_pre[l], norm_mix_post[l], w_in[l], b_gate[l],
                  lambda_q1[l], lambda_k1[l], lambda_q2[l], lambda_k2[l], subln_a[l], rel_bias_b[l],
                  bf(w_br_a), bf(w_br_b), bf(w_br_c), bf(w_out),
                  norm_ffn_pre[l], norm_ffn_post[l], w_ffn_in[l], bf(w_ffn_out))

        mem_n = rms_norm_bf16(mem_prompt.reshape(bsz * n_mem, d), norm_mem[l])
        w_c = H_C * DH_C
        mk_p = matmul_cols(mem_n, w_mem_kv[l], 0, w_c, F32).reshape(bsz, n_mem, w_c)
        mv_p = matmul_cols(mem_n, w_mem_kv[l], w_c, w_c, F32).reshape(bsz, n_mem, w_c)
        yp, ka, va, kb, vb = _layer(yp, mk_p, mv_p, None, *shared)
        new = [heads(ka, H_A), heads(va, H_A), heads(kb[:, s - lb_prompt:], H_B),
               heads(vb[:, s - lb_prompt:], H_B), heads(mk_p, H_C), heads(mv_p, H_C)]

        caches = (cache_a_k[l], cache_a_v[l], cache_b_k[l], cache_b_v[l])
        ys, ka, va, kb, vb = _layer(ys, flat(cache_mem_k[l]), flat(cache_mem_v[l]), caches, *shared)
        new += [heads(ka, H_A), heads(va, H_A), heads(kb, H_B), heads(vb, H_B)]
        for o, a in zip(outs, new):
            o.append(a)
    return (yp, ys) + tuple(jnp.stack(o) for o in outs)
```

```python
import functools
import math

import jax
import jax.numpy as jnp
from jax import lax
from jax.experimental import pallas as pl
from jax.experimental.pallas import tpu as pltpu

F32 = jnp.float32
BF16 = jnp.bfloat16

CHUNK = 64
H_A = 8
DH_A = 64
DK_A = 2 * DH_A
DV_A = 128
H_B = 8
DH_B = 128
N_PREV_CHUNKS = 8
BAND_PAST = N_PREV_CHUNKS * CHUNK
MAX_REL = 128
H_C = 4
DH_C = 256
N_BRANCH = 3
EPS = 1e-6
NEG_INF = -1e30

LANES = 128
VMEM_LIMIT = 56 * 1024 * 1024
BAND_GROUP = 256
A_BLOCK = 256
MM_ROWS = 2048
PROJ_ROWS = 128
SAFE_LOGIT = 40.0

CONTRACT_LAST = (((1,), (1,)), ((), ()))


def _params(*sem):
    return pltpu.CompilerParams(dimension_semantics=sem, vmem_limit_bytes=VMEM_LIMIT)


def _block(n, target):
    if n <= target:
        return n
    b = target
    while n % b:
        b //= 2
    return b


def _rms_kernel(x_ref, g_ref, o_ref):
    x = x_ref[...]
    ms = jnp.mean(x * x, axis=-1, keepdims=True)
    o_ref[...] = (x * lax.rsqrt(ms + EPS) * g_ref[...]).astype(o_ref.dtype)


def rms_norm_bf16(x, g):
    m, d = x.shape
    bm = _block(m, 512)
    return pl.pallas_call(
        _rms_kernel,
        grid=(m // bm,),
        in_specs=[pl.BlockSpec((bm, d), lambda i: (i, 0)),
                  pl.BlockSpec((1, d), lambda i: (0, 0))],
        out_specs=pl.BlockSpec((bm, d), lambda i: (i, 0)),
        out_shape=jax.ShapeDtypeStruct((m, d), BF16),
        name="rms_norm",
        compiler_params=_params("parallel"),
    )(x, g.reshape(1, d))


def _mm_kernel(a_ref, w_ref, o_ref, wb_ref):
    @pl.when(pl.program_id(1) == 0)
    def _():
        wb_ref[...] = w_ref[...].astype(BF16)

    o_ref[...] = jnp.dot(a_ref[...], wb_ref[...], preferred_element_type=F32).astype(o_ref.dtype)


def matmul_cols(a, w, col0, n, out_dtype):
    m, k = a.shape
    bm = _block(m, MM_ROWS)
    bn = _block(n, 1024)
    assert col0 % bn == 0
    off = col0 // bn
    return pl.pallas_call(
        _mm_kernel,
        grid=(n // bn, m // bm),
        in_specs=[pl.BlockSpec((bm, k), lambda j, i: (i, 0)),
                  pl.BlockSpec((k, bn), lambda j, i: (0, j + off))],
        out_specs=pl.BlockSpec((bm, bn), lambda j, i: (i, j)),
        out_shape=jax.ShapeDtypeStruct((m, n), out_dtype),
        scratch_shapes=[pltpu.VMEM((k, bn), BF16)],
        name="matmul_cols",
        compiler_params=_params("arbitrary", "arbitrary"),
    )(a, w)


def _diff_lambda(lq1_ref, lk1_ref, lq2_ref, lk2_ref, lam_init):
    return (jnp.exp(jnp.sum(lq1_ref[...] * lk1_ref[...], keepdims=True))
            - jnp.exp(jnp.sum(lq2_ref[...] * lk2_ref[...], keepdims=True)) + lam_init)


def _head_norm(o, sub_ref, lam_init):
    ms = jnp.mean(o * o, axis=-1, keepdims=True)
    return o * lax.rsqrt(ms + EPS) * sub_ref[...] * (1.0 - lam_init)


def _own_alibi(n, slope):
    r = lax.broadcasted_iota(jnp.int32, (n, n), 0)
    c = lax.broadcasted_iota(jnp.int32, (n, n), 1)
    bias = slope * (r - jnp.abs(r - c)).astype(F32)
    return jnp.where(c // CHUNK <= r // CHUNK, bias, NEG_INF)


def _max_sq_norms(x, n_groups):
    w = x.shape[1]
    lane = lax.broadcasted_iota(jnp.int32, (w, LANES), 0)
    group = lax.broadcasted_iota(jnp.int32, (w, LANES), 1)
    indicator = jnp.where(lane // (w // n_groups) == group, 1.0, 0.0).astype(BF16)
    sq = jnp.square(x.astype(F32)).astype(BF16)
    return jnp.max(jnp.dot(sq, indicator, preferred_element_type=F32), axis=0, keepdims=True)


def _attn_a_kernel(q_ref, k_ref, v_ref, slope_ref, kf_ref, qf_ref, lq1_ref, lk1_ref, lq2_ref, lk2_ref,
                   sub_ref, o_ref, ke_scr, ve_scr, *, tq, lam_init):
    q_sq = _max_sq_norms(q_ref[0] * (DH_A ** -0.5), 2)
    k_sq = _max_sq_norms(k_ref[0].astype(BF16), 2)
    safe = jnp.max((q_sq * k_sq)[:, :2]) <= SAFE_LOGIT ** 2 / 1.03
    args = (q_ref, k_ref, v_ref, slope_ref, lq1_ref, lk1_ref, lq2_ref, lk2_ref, sub_ref, o_ref)

    @pl.when(safe)
    def _():
        _attn_a_bounded(*args, kf_ref, qf_ref, ke_scr, ve_scr, tq=tq, lam_init=lam_init)

    @pl.when(jnp.logical_not(safe))
    def _():
        _attn_a_general(*args, tq=tq, lam_init=lam_init)


def _position_features(t):
    pos = jnp.arange(t, dtype=jnp.int32)[:, None]
    lo = pos & 7
    hi = (pos - lo).astype(F32)
    lo = lo.astype(F32)
    one = jnp.ones((t, 1), F32)
    pad = jnp.zeros((t, LANES - 4), F32)
    k_side = jnp.concatenate([hi, lo, one, one, pad], axis=1).astype(BF16)
    q_side = jnp.concatenate([one, one, -hi, -lo, pad], axis=1)
    return k_side, q_side


def _attn_a_bounded(q_ref, k_ref, v_ref, slope_ref, lq1_ref, lk1_ref, lq2_ref, lk2_ref, sub_ref, o_ref,
                    kf_ref, qf_ref, ke_scr, ve_scr, *, tq, lam_init):
    t = q_ref.shape[1]
    slope = slope_ref[0][:, :1]
    lam = _diff_lambda(lq1_ref, lk1_ref, lq2_ref, lk2_ref, lam_init)
    first_half = lax.broadcasted_iota(jnp.int32, (1, DK_A), 1) < DH_A
    ke_scr[:, :DK_A] = k_ref[0].astype(BF16)
    ke_scr[:, DK_A:] = kf_ref[...]
    ve_scr[:, :DV_A] = v_ref[0].astype(BF16)
    ve_scr[:, DV_A:] = jnp.ones((t, DV_A), BF16)

    r = lax.broadcasted_iota(jnp.int32, (tq, tq), 0)
    c = lax.broadcasted_iota(jnp.int32, (tq, tq), 1)
    own_fix = jnp.where(c // CHUNK <= r // CHUNK, -2.0 * slope * jnp.maximum(c - r, 0).astype(F32), NEG_INF)
    own_fix = jnp.concatenate([own_fix, own_fix], axis=0)

    for i in range(t // tq):
        nb = i * tq
        q = q_ref[0, nb:nb + tq, :] * (DH_A ** -0.5)
        q_feat = (qf_ref[nb:nb + tq, :] * slope).astype(BF16)
        zero = jnp.zeros_like(q)
        qe = jnp.concatenate([jnp.concatenate([jnp.where(first_half, q, zero), q_feat], axis=1),
                              jnp.concatenate([jnp.where(first_half, zero, q), q_feat], axis=1)], axis=0)
        s_own = lax.dot_general(qe, ke_scr[nb:nb + tq, :], CONTRACT_LAST, preferred_element_type=F32)
        acc = jnp.dot(jnp.exp(s_own + own_fix).astype(BF16), ve_scr[nb:nb + tq, :],
                      preferred_element_type=F32)
        if nb:
            s_bef = lax.dot_general(qe, ke_scr[:nb, :], CONTRACT_LAST, preferred_element_type=F32)
            acc = acc + jnp.dot(jnp.exp(s_bef).astype(BF16), ve_scr[:nb, :], preferred_element_type=F32)
        o = acc[:, :DV_A] / acc[:, DV_A:]
        o = o[:tq] - lam * o[tq:]
        o_ref[0, nb:nb + tq, :] = _head_norm(o, sub_ref, lam_init).astype(o_ref.dtype)


def _attn_a_general(q_ref, k_ref, v_ref, slope_ref, lq1_ref, lk1_ref, lq2_ref, lk2_ref, sub_ref, o_ref,
                    *, tq, lam_init):
    t = q_ref.shape[1]
    n_blocks = t // tq
    slope = slope_ref[0][:, :1]
    lam = _diff_lambda(lq1_ref, lk1_ref, lq2_ref, lk2_ref, lam_init)
    k = k_ref[0].astype(BF16)
    v = v_ref[0].astype(BF16)
    first_half = lax.broadcasted_iota(jnp.int32, (1, DK_A), 1) < DH_A
    own_bias = _own_alibi(tq, slope)
    n_before_max = (n_blocks - 1) * tq
    if n_before_max:
        j = lax.broadcasted_iota(jnp.int32, (1, n_before_max), 1)
        before_bias = slope * (j - n_before_max).astype(F32)

    for i in range(n_blocks):
        nb = i * tq
        q = q_ref[0, nb:nb + tq, :] * (DH_A ** -0.5)
        k_own, v_own = k[nb:nb + tq], v[nb:nb + tq]

        def softmax_pv(qh):
            s_own = lax.dot_general(qh, k_own, CONTRACT_LAST, preferred_element_type=F32) + own_bias
            m = jnp.max(s_own, axis=-1, keepdims=True)
            if nb:
                s_bef = lax.dot_general(qh, k[:nb], CONTRACT_LAST, preferred_element_type=F32)
                s_bef = s_bef + before_bias[:, n_before_max - nb:]
                m = jnp.maximum(m, jnp.max(s_bef, axis=-1, keepdims=True))
            p_own = jnp.exp(s_own - m)
            l = jnp.sum(p_own, axis=-1, keepdims=True)
            pv = jnp.dot(p_own.astype(BF16), v_own, preferred_element_type=F32)
            if nb:
                p_bef = jnp.exp(s_bef - m)
                l = l + jnp.sum(p_bef, axis=-1, keepdims=True)
                pv = pv + jnp.dot(p_bef.astype(BF16), v[:nb], preferred_element_type=F32)
            return pv / l

        o = (softmax_pv(jnp.where(first_half, q, jnp.zeros_like(q)))
             - lam * softmax_pv(jnp.where(first_half, jnp.zeros_like(q), q)))
        o_ref[0, nb:nb + tq, :] = _head_norm(o, sub_ref, lam_init).astype(o_ref.dtype)


def _row(a):
    return a.reshape(1, -1).astype(F32)


def _const_spec(shape):
    return pl.BlockSpec(shape, lambda *_: (0,) * len(shape))


def attention_a(q, k, v, lq1, lk1, lq2, lk2, subln, lam_init):
    b, t, _ = q.shape
    tq = _block(t, A_BLOCK)
    assert tq % CHUNK == 0
    slopes = jnp.asarray([[[2.0 ** (-8.0 * (hh + 1) / H_A)] * LANES] for hh in range(H_A)], dtype=F32)
    head = lambda width: pl.BlockSpec((1, t, width), lambda bi, hi: (bi, 0, hi))
    k_feat, q_feat = _position_features(t)
    return pl.pallas_call(
        functools.partial(_attn_a_kernel, tq=tq, lam_init=lam_init),
        grid=(b, H_A),
        in_specs=[head(DK_A), head(DK_A), head(DV_A),
                  pl.BlockSpec((1, 1, LANES), lambda bi, hi: (hi, 0, 0)),
                  _const_spec((t, LANES)), _const_spec((t, LANES)),
                  _const_spec((1, DH_A)), _const_spec((1, DH_A)), _const_spec((1, DH_A)),
                  _const_spec((1, DH_A)), _const_spec((1, DV_A))],
        out_specs=head(DV_A),
        out_shape=jax.ShapeDtypeStruct((b, t, H_A * DV_A), BF16),
        scratch_shapes=[pltpu.VMEM((t, DK_A + LANES), BF16), pltpu.VMEM((t, 2 * DV_A), BF16)],
        name="attention_a",
        compiler_params=_params("parallel", "arbitrary"),
    )(q, k, v, slopes, k_feat, q_feat, _row(lq1), _row(lk1), _row(lq2), _row(lk2), _row(subln))


def _attn_a_decode_kernel(q_ref, kn_ref, vn_ref, ck_ref, cv_ref, lq1_ref, lk1_ref, lq2_ref, lk2_ref,
                          sub_ref, o_ref, *, lam_init):
    t = q_ref.shape[1]
    p_len = ck_ref.shape[1] // H_A
    lam = _diff_lambda(lq1_ref, lk1_ref, lq2_ref, lk2_ref, lam_init)
    first_half = lax.broadcasted_iota(jnp.int32, (1, DK_A), 1) < DH_A
    j = lax.broadcasted_iota(jnp.int32, (1, p_len), 1)
    before_dist = (j - p_len).astype(F32)
    for h in range(H_A):
        slope = 2.0 ** (-8.0 * (h + 1) / H_A)
        cols = slice(h * DK_A, (h + 1) * DK_A)
        q = q_ref[0, :, cols] * (DH_A ** -0.5)
        kc = ck_ref[0, pl.ds(h, p_len, stride=H_A), :].astype(BF16)
        vc = cv_ref[0, pl.ds(h, p_len, stride=H_A), :].astype(BF16)
        kn = kn_ref[0, :, cols].astype(BF16)
        vn = vn_ref[0, :, cols].astype(BF16)
        own_bias = _own_alibi(t, slope)
        before_bias = slope * before_dist

        def softmax_pv(qh):
            s_new = lax.dot_general(qh, kn, CONTRACT_LAST, preferred_element_type=F32) + own_bias
            s_old = lax.dot_general(qh, kc, CONTRACT_LAST, preferred_element_type=F32) + before_bias
            m = jnp.maximum(jnp.max(s_new, axis=-1, keepdims=True), jnp.max(s_old, axis=-1, keepdims=True))
            p_new = jnp.exp(s_new - m)
            p_old = jnp.exp(s_old - m)
            l = jnp.sum(p_new, axis=-1, keepdims=True) + jnp.sum(p_old, axis=-1, keepdims=True)
            pv = (jnp.dot(p_new.astype(BF16), vn, preferred_element_type=F32)
                  + jnp.dot(p_old.astype(BF16), vc, preferred_element_type=F32))
            return pv / l

        o = (softmax_pv(jnp.where(first_half, q, jnp.zeros_like(q)))
             - lam * softmax_pv(jnp.where(first_half, jnp.zeros_like(q), q)))
        o_ref[0, :, cols] = _head_norm(o, sub_ref, lam_init).astype(o_ref.dtype)


def attention_a_decode(q, k_new, v_new, cache_k, cache_v, lq1, lk1, lq2, lk2, subln, lam_init):
    b, t, w = q.shape
    p_len = cache_k.shape[1]
    assert p_len % CHUNK == 0 and t <= CHUNK
    rows = lambda a: a.reshape(b, p_len * H_A, a.shape[-1])
    new = pl.BlockSpec((1, t, w), lambda bi: (bi, 0, 0))
    old = pl.BlockSpec((1, p_len * H_A, DK_A), lambda bi: (bi, 0, 0))
    return pl.pallas_call(
        functools.partial(_attn_a_decode_kernel, lam_init=lam_init),
        grid=(b,),
        in_specs=[new, new, new, old, old,
                  _const_spec((1, DH_A)), _const_spec((1, DH_A)), _const_spec((1, DH_A)),
                  _const_spec((1, DH_A)), _const_spec((1, DV_A))],
        out_specs=new,
        out_shape=jax.ShapeDtypeStruct((b, t, w), BF16),
        name="attention_a_decode",
        compiler_params=_params("parallel"),
    )(q, k_new, v_new, rows(cache_k), rows(cache_v), _row(lq1), _row(lk1), _row(lq2), _row(lk2), _row(subln))


def _band_tile(tq):
    pad = -(-tq // LANES) * LANES
    return BAND_PAST + pad, pad


def _band_bias_mask(g_row, tq):
    tile_w, pad = _band_tile(tq)
    width = g_row.shape[1]
    assert width == tile_w + pad
    rolled = pltpu.roll(jnp.broadcast_to(g_row, (tq, width)), width - pad, 1, stride=1, stride_axis=0)
    r = lax.broadcasted_iota(jnp.int32, (tq, tile_w), 0)
    c = lax.broadcasted_iota(jnp.int32, (tq, tile_w), 1)
    qch = r // CHUNK
    kch = c // CHUNK - N_PREV_CHUNKS
    valid = (kch <= qch) & (kch >= qch - N_PREV_CHUNKS)
    return jnp.where(valid, rolled[:, :tile_w], NEG_INF)


def _toeplitz_rows(rel_bias, tq):
    tile_w, pad = _band_tile(tq)
    width = tile_w + pad
    n_lo = pad + BAND_PAST - MAX_REL
    n_hi = max(width - n_lo - (2 * MAX_REL + 1), 0)
    lo = jnp.broadcast_to(rel_bias[:, :1], (H_B, n_lo))
    hi = jnp.broadcast_to(rel_bias[:, -1:], (H_B, n_hi))
    g = jnp.concatenate([lo, rel_bias, hi], axis=1)[:, :width]
    return g.reshape(H_B, 1, width).astype(F32)


def _attn_b_kernel(q_ref, k_ref, v_ref, g_ref, o_ref, ve_scr, *, tq):
    t = q_ref.shape[1]
    tile_w, _ = _band_tile(tq)
    scale = DH_B ** -0.5
    bias_mask = _band_bias_mask(g_ref[0], tq)
    qk_sq = _max_sq_norms(q_ref[0], 1) * _max_sq_norms(k_ref[0].astype(BF16), 1)
    room = SAFE_LOGIT - jnp.max(jnp.abs(g_ref[0]))
    safe = jnp.logical_and(room > 0.0, jnp.max(qk_sq[:, :1]) * (scale * scale * 1.03) <= room * room)

    def groups():
        for gi in range(t // tq):
            qa = gi * tq
            lo = max(qa - BAND_PAST, 0)
            hi = qa + tq
            off = lo - qa + BAND_PAST
            assert off % LANES == 0 and off + hi - lo == tile_w
            yield slice(qa, hi), slice(lo, hi), off

    @pl.when(safe)
    def _():
        log2e = math.log2(math.e)
        k = k_ref[0].astype(BF16)
        ve_scr[:, :DH_B] = v_ref[0].astype(BF16)
        ve_scr[:, DH_B:] = jnp.ones((t, DH_B), BF16)
        bias2 = bias_mask * log2e
        for rows, keys, off in groups():
            s = lax.dot_general(q_ref[0, rows, :], k[keys], CONTRACT_LAST, preferred_element_type=F32)
            p = jnp.exp2(s * (scale * log2e) + bias2[:, off:])
            acc = jnp.dot(p.astype(BF16), ve_scr[keys, :], preferred_element_type=F32)
            o_ref[0, rows, :] = (acc[:, :DH_B] / acc[:, DH_B:]).astype(o_ref.dtype)

    @pl.when(jnp.logical_not(safe))
    def _():
        k = k_ref[0].astype(BF16)
        v = v_ref[0].astype(BF16)
        for rows, keys, off in groups():
            s = lax.dot_general(q_ref[0, rows, :], k[keys], CONTRACT_LAST, preferred_element_type=F32) * scale
            s = s + bias_mask[:, off:]
            m = jnp.max(s, axis=-1, keepdims=True)
            p = jnp.exp(s - m)
            l = jnp.sum(p, axis=-1, keepdims=True)
            o = jnp.dot(p.astype(BF16), v[keys], preferred_element_type=F32) / l
            o_ref[0, rows, :] = o.astype(o_ref.dtype)


def attention_b(q, k, v, rel_bias):
    b, t, _ = q.shape
    tq = _block(t, BAND_GROUP)
    assert tq % LANES == 0
    g = _toeplitz_rows(rel_bias, tq)
    head = pl.BlockSpec((1, t, DH_B), lambda bi, hi: (bi, 0, hi))
    return pl.pallas_call(
        functools.partial(_attn_b_kernel, tq=tq),
        grid=(b, H_B),
        in_specs=[head, head, head, pl.BlockSpec((1, 1, g.shape[2]), lambda bi, hi: (hi, 0, 0))],
        out_specs=head,
        out_shape=jax.ShapeDtypeStruct((b, t, H_B * DH_B), BF16),
        scratch_shapes=[pltpu.VMEM((t, 2 * DH_B), BF16)],
        name="attention_b",
        compiler_params=_params("parallel", "arbitrary"),
    )(q, k, v, g)


def _attn_b_decode_kernel(q_ref, kn_ref, vn_ref, ck_ref, cv_ref, g_ref, o_ref):
    t = q_ref.shape[1]
    scale = DH_B ** -0.5
    for h in range(H_B):
        cols = slice(h * DH_B, (h + 1) * DH_B)
        q = q_ref[0, :, cols]
        kc = ck_ref[0, pl.ds(h, BAND_PAST, stride=H_B), :].astype(BF16)
        vc = cv_ref[0, pl.ds(h, BAND_PAST, stride=H_B), :].astype(BF16)
        kn = kn_ref[0, :, cols].astype(BF16)
        vn = vn_ref[0, :, cols].astype(BF16)
        bias_mask = _band_bias_mask(g_ref[h], t)
        s_old = lax.dot_general(q, kc, CONTRACT_LAST, preferred_element_type=F32) * scale
        s_old = s_old + bias_mask[:, :BAND_PAST]
        s_new = lax.dot_general(q, kn, CONTRACT_LAST, preferred_element_type=F32) * scale
        s_new = s_new + bias_mask[:, BAND_PAST:BAND_PAST + t]
        m = jnp.maximum(jnp.max(s_new, axis=-1, keepdims=True), jnp.max(s_old, axis=-1, keepdims=True))
        p_new = jnp.exp(s_new - m)
        p_old = jnp.exp(s_old - m)
        l = jnp.sum(p_new, axis=-1, keepdims=True) + jnp.sum(p_old, axis=-1, keepdims=True)
        pv = (jnp.dot(p_new.astype(BF16), vn, preferred_element_type=F32)
              + jnp.dot(p_old.astype(BF16), vc, preferred_element_type=F32))
        o_ref[0, :, cols] = (pv / l).astype(o_ref.dtype)


def attention_b_decode(q, k_new, v_new, cache_k, cache_v, rel_bias, p_len):
    b, t, w = q.shape
    assert cache_k.shape[1] == BAND_PAST and p_len % CHUNK == 0 and p_len >= BAND_PAST and t <= CHUNK
    g = _toeplitz_rows(rel_bias, t)
    rows = lambda a: a.reshape(b, BAND_PAST * H_B, a.shape[-1])
    new = pl.BlockSpec((1, t, w), lambda bi: (bi, 0, 0))
    old = pl.BlockSpec((1, BAND_PAST * H_B, DH_B), lambda bi: (bi, 0, 0))
    return pl.pallas_call(
        _attn_b_decode_kernel,
        grid=(b,),
        in_specs=[new, new, new, old, old, _const_spec(g.shape)],
        out_specs=new,
        out_shape=jax.ShapeDtypeStruct((b, t, w), BF16),
        name="attention_b_decode",
        compiler_params=_params("parallel"),
    )(q, k_new, v_new, rows(cache_k), rows(cache_v), g)


def _attn_c_kernel(q_ref, k_ref, v_ref, o_ref, *, tq):
    t = q_ref.shape[1]
    scale = DH_C ** -0.5
    k = k_ref[0].astype(BF16)
    v = v_ref[0].astype(BF16)
    qk_sq = _max_sq_norms(q_ref[0], 1) * _max_sq_norms(k, 1)
    safe = jnp.max(qk_sq[:, :1]) * (scale * scale * 1.03) <= SAFE_LOGIT ** 2

    def finish(i, p):
        l = jnp.sum(p, axis=-1, keepdims=True)
        o = jnp.dot(p.astype(BF16), v, preferred_element_type=F32) / l
        o_ref[0, i * tq:(i + 1) * tq, :] = o.astype(o_ref.dtype)

    @pl.when(safe)
    def _():
        for i in range(t // tq):
            s = lax.dot_general(q_ref[0, i * tq:(i + 1) * tq, :], k, CONTRACT_LAST, preferred_element_type=F32)
            finish(i, jnp.exp2(s * (scale * math.log2(math.e))))

    @pl.when(jnp.logical_not(safe))
    def _():
        for i in range(t // tq):
            s = lax.dot_general(q_ref[0, i * tq:(i + 1) * tq, :], k, CONTRACT_LAST, preferred_element_type=F32)
            s = s * scale
            finish(i, jnp.exp(s - jnp.max(s, axis=-1, keepdims=True)))


def attention_c(q, mk, mv):
    b, t, _ = q.shape
    n_mem = mk.shape[1]
    tq = _block(t, 512)
    return pl.pallas_call(
        functools.partial(_attn_c_kernel, tq=tq),
        grid=(b, H_C),
        in_specs=[pl.BlockSpec((1, t, DH_C), lambda bi, hi: (bi, 0, hi)),
                  pl.BlockSpec((1, n_mem, DH_C), lambda bi, hi: (bi, 0, hi)),
                  pl.BlockSpec((1, n_mem, DH_C), lambda bi, hi: (bi, 0, hi))],
        out_specs=pl.BlockSpec((1, t, DH_C), lambda bi, hi: (bi, 0, hi)),
        out_shape=jax.ShapeDtypeStruct((b, t, H_C * DH_C), BF16),
        name="attention_c",
        compiler_params=_params("parallel", "arbitrary"),
    )(q, mk, mv)


def _merge_kernel(oa_ref, ob_ref, oc_ref, ga_ref, gb_ref, gc_ref, ba_ref, bb_ref, bc_ref,
                  wa_ref, wb_ref, wc_ref, o_ref):
    def branch(o_r, g_r, b_r, w_r):
        gate = jax.nn.sigmoid(g_r[...].astype(F32) + b_r[...])
        return gate * jnp.dot(o_r[...], w_r[...], preferred_element_type=F32)

    merged = (branch(oa_ref, ga_ref, ba_ref, wa_ref) + branch(ob_ref, gb_ref, bb_ref, wb_ref)
              + branch(oc_ref, gc_ref, bc_ref, wc_ref))
    o_ref[...] = merged.astype(o_ref.dtype)


def merge_branches(oa, ob, oc, gate_logits, b_gate, wa, wb, wc):
    m, w_in = oa.shape
    d = wa.shape[1]
    bm = _block(m, 256)
    row = lambda width: pl.BlockSpec((bm, width), lambda i: (i, 0))
    gate = lambda j: pl.BlockSpec((bm, d), lambda i: (i, j))
    bias = lambda j: pl.BlockSpec((1, d), lambda i: (0, j))
    weight = pl.BlockSpec((w_in, d), lambda i: (0, 0), pipeline_mode=pl.Buffered(1))
    bg = b_gate.reshape(1, N_BRANCH * d).astype(F32)
    return pl.pallas_call(
        _merge_kernel,
        grid=(m // bm,),
        in_specs=[row(w_in), row(w_in), row(w_in), gate(0), gate(1), gate(2),
                  bias(0), bias(1), bias(2), weight, weight, weight],
        out_specs=pl.BlockSpec((bm, d), lambda i: (i, 0)),
        out_shape=jax.ShapeDtypeStruct((m, d), BF16),
        name="merge_branches",
        compiler_params=_params("parallel"),
    )(oa, ob, oc, gate_logits, gate_logits, gate_logits, bg, bg, bg, wa, wb, wc)


def _proj_norm_res_kernel(*refs, next_norm):
    if next_norm:
        a_ref, w_ref, x_ref, g_ref, g2_ref, o_ref, h_ref = refs
    else:
        a_ref, w_ref, x_ref, g_ref, o_ref = refs
    bm = a_ref.shape[0]
    rows = _block(bm, PROJ_ROWS)
    for r0 in range(0, bm, rows):
        sl = slice(r0, r0 + rows)
        y = jnp.dot(a_ref[sl, :], w_ref[...], preferred_element_type=F32)
        ms = jnp.mean(y * y, axis=-1, keepdims=True)
        o = x_ref[sl, :] + y * lax.rsqrt(ms + EPS) * g_ref[...]
        o_ref[sl, :] = o
        if next_norm:
            ms2 = jnp.mean(o * o, axis=-1, keepdims=True)
            h_ref[sl, :] = (o * lax.rsqrt(ms2 + EPS) * g2_ref[...]).astype(h_ref.dtype)


def proj_norm_residual(a, w, x, g, bm, next_g=None):
    m, k = a.shape
    d = w.shape[1]
    bm = _block(m, bm)
    vec = pl.BlockSpec((1, d), lambda i: (0, 0))
    row = pl.BlockSpec((bm, d), lambda i: (i, 0))
    in_specs = [pl.BlockSpec((bm, k), lambda i: (i, 0)),
                pl.BlockSpec((k, d), lambda i: (0, 0), pipeline_mode=pl.Buffered(1)), row, vec]
    args = [a, w, x, g.reshape(1, d).astype(F32)]
    out_specs, out_shape = row, jax.ShapeDtypeStruct((m, d), F32)
    if next_g is not None:
        in_specs.append(vec)
        args.append(next_g.reshape(1, d).astype(F32))
        out_specs = (row, row)
        out_shape = (out_shape, jax.ShapeDtypeStruct((m, d), BF16))
    return pl.pallas_call(
        functools.partial(_proj_norm_res_kernel, next_norm=next_g is not None),
        grid=(m // bm,),
        in_specs=in_specs,
        out_specs=out_specs,
        out_shape=out_shape,
        name="proj_norm_residual",
        compiler_params=_params("parallel"),
    )(*args)


def _ffn_in_kernel(h_ref, wa_ref, wb_ref, o_ref, wa_bf, wb_bf):
    @pl.when(pl.program_id(1) == 0)
    def _():
        wa_bf[...] = wa_ref[...].astype(BF16)
        wb_bf[...] = wb_ref[...].astype(BF16)

    h = h_ref[...]
    a = jnp.dot(h, wa_bf[...], preferred_element_type=F32)
    b = jnp.dot(h, wb_bf[...], preferred_element_type=F32)
    o_ref[...] = (jax.nn.silu(a) * b).astype(o_ref.dtype)


def ffn_in(h, w):
    m, k = h.shape
    f = w.shape[1] // 2
    bm = _block(m, MM_ROWS)
    bf = 512
    assert f % bf == 0
    nf = f // bf
    return pl.pallas_call(
        _ffn_in_kernel,
        grid=(nf, m // bm),
        in_specs=[pl.BlockSpec((bm, k), lambda j, i: (i, 0)),
                  pl.BlockSpec((k, bf), lambda j, i: (0, j)),
                  pl.BlockSpec((k, bf), lambda j, i: (0, j + nf))],
        out_specs=pl.BlockSpec((bm, bf), lambda j, i: (i, j)),
        out_shape=jax.ShapeDtypeStruct((m, f), BF16),
        scratch_shapes=[pltpu.VMEM((k, bf), BF16), pltpu.VMEM((k, bf), BF16)],
        name="ffn_in",
        compiler_params=_params("arbitrary", "arbitrary"),
    )(h, w, w)


def _layer(x, mk, mv, caches, lam_init, norm_mix_pre, norm_mix_post, w_in, b_gate,
           lq1, lk1, lq2, lk2, subln_a, rel_bias, w_br_a, w_br_b, w_br_c, w_out,
           norm_ffn_pre, norm_ffn_post, w_ffn_in, w_ffn_out):
    b, t, d = x.shape
    m = b * t
    xf = x.reshape(m, d)
    w_head = H_A * DK_A
    h = rms_norm_bf16(xf, norm_mix_pre)
    proj = lambda idx, dt: matmul_cols(h, w_in, idx * w_head, w_head, dt).reshape(b, t, w_head)
    q_a, k_a, v_a = proj(0, BF16), proj(1, F32), proj(2, F32)
    q_b, k_b, v_b = proj(3, BF16), proj(4, F32), proj(5, F32)
    q_c = proj(6, BF16)
    gate_logits = matmul_cols(h, w_in, 7 * w_head, N_BRANCH * d, BF16)

    if caches is None:
        o_a = attention_a(q_a, k_a, v_a, lq1, lk1, lq2, lk2, subln_a, lam_init)
        o_b = attention_b(q_b, k_b, v_b, rel_bias)
    else:
        ca_k, ca_v, cb_k, cb_v = caches
        o_a = attention_a_decode(q_a, k_a, v_a, ca_k, ca_v, lq1, lk1, lq2, lk2, subln_a, lam_init)
        o_b = attention_b_decode(q_b, k_b, v_b, cb_k, cb_v, rel_bias, ca_k.shape[1])
    o_c = attention_c(q_c, mk, mv)

    merged = merge_branches(o_a.reshape(m, -1), o_b.reshape(m, -1), o_c.reshape(m, -1),
                            gate_logits, b_gate, w_br_a, w_br_b, w_br_c)
    x1, h2 = proj_norm_residual(merged, w_out, xf, norm_mix_post, bm=512, next_g=norm_ffn_pre)
    act = ffn_in(h2, w_ffn_in)
    y = proj_norm_residual(act, w_ffn_out, x1, norm_ffn_post, bm=256)
    return y.reshape(b, t, d), k_a, v_a, k_b, v_b


def kernel(x_prompt, x_sample, cache_a_k, cache_a_v, cache_b_k, cache_b_v, cache_mem_k, cache_mem_v, mem_prompt, norm_mix_pre, norm_mix_post, norm_mem, w_in, b_gate, lambda_q1, lambda_k1, lambda_q2, lambda_k2, subln_a, rel_bias_b, w_mem_kv, w_br_a, w_br_b, w_br_c, w_out, norm_ffn_pre, norm_ffn_post, w_ffn_in, w_ffn_out):
    depth = w_in.shape[0]
    bsz, s, d = x_prompt.shape
    n_mem = mem_prompt.shape[1]
    lb_prompt = min(BAND_PAST, s)
    yp, ys = x_prompt, x_sample
    outs = [[] for _ in range(10)]
    heads = lambda a, nh: a.reshape(a.shape[0], a.shape[1], nh, a.shape[2] // nh)
    flat = lambda a: a.reshape(a.shape[0], a.shape[1], -1)
    for l in range(depth):
        lam_init = 0.8 - 0.6 * math.exp(-0.3 * l)
        bf = lambda w: w[l].astype(BF16)
        shared = (lam_init, norm_mix_pre[l], norm_mix_post[l], w_in[l], b_gate[l],
                  lambda_q1[l], lambda_k1[l], lambda_q2[l], lambda_k2[l], subln_a[l], rel_bias_b[l],
                  bf(w_br_a), bf(w_br_b), bf(w_br_c), bf(w_out),
                  norm_ffn_pre[l], norm_ffn_post[l], w_ffn_in[l], bf(w_ffn_out))

        mem_n = rms_norm_bf16(mem_prompt.reshape(bsz * n_mem, d), norm_mem[l])
        w_c = H_C * DH_C
        mk_p = matmul_cols(mem_n, w_mem_kv[l], 0, w_c, F32).reshape(bsz, n_mem, w_c)
        mv_p = matmul_cols(mem_n, w_mem_kv[l], w_c, w_c, F32).reshape(bsz, n_mem, w_c)
        yp, ka, va, kb, vb = _layer(yp, mk_p, mv_p, None, *shared)
        new = [heads(ka, H_A), heads(va, H_A), heads(kb[:, s - lb_prompt:], H_B),
               heads(vb[:, s - lb_prompt:], H_B), heads(mk_p, H_C), heads(mv_p, H_C)]

        caches = (cache_a_k[l], cache_a_v[l], cache_b_k[l], cache_b_v[l])
        ys, ka, va, kb, vb = _layer(ys, flat(cache_mem_k[l]), flat(cache_mem_v[l]), caches, *shared)
        new += [heads(ka, H_A), heads(va, H_A), heads(kb, H_B), heads(vb, H_B)]
        for o, a in zip(outs, new):
            o.append(a)
    return (yp, ys) + tuple(jnp.stack(o) for o in outs)
```

```python
import functools
import math

import jax
import jax.numpy as jnp
from jax import lax
from jax.experimental import pallas as pl
from jax.experimental.pallas import tpu as pltpu

F32 = jnp.float32
BF16 = jnp.bfloat16

CHUNK = 64
H_A = 8
DH_A = 64
DK_A = 2 * DH_A
DV_A = 128
H_B = 8
DH_B = 128
N_PREV_CHUNKS = 8
BAND_PAST = N_PREV_CHUNKS * CHUNK
MAX_REL = 128
H_C = 4
DH_C = 256
N_BRANCH = 3
EPS = 1e-6
NEG_INF = -1e30

LANES = 128
VMEM_LIMIT = 56 * 1024 * 1024
BAND_GROUP = 256
A_BLOCK = 256
MM_ROWS = 2048
PROJ_ROWS = 128
SAFE_LOGIT = 40.0

CONTRACT_LAST = (((1,), (1,)), ((), ()))


def _params(*sem):
    return pltpu.CompilerParams(dimension_semantics=sem, vmem_limit_bytes=VMEM_LIMIT)


def _block(n, target):
    if n <= target:
        return n
    b = target
    while n % b:
        b //= 2
    return b


def _rms_kernel(x_ref, g_ref, o_ref):
    x = x_ref[...]
    ms = jnp.mean(x * x, axis=-1, keepdims=True)
    o_ref[...] = (x * lax.rsqrt(ms + EPS) * g_ref[...]).astype(o_ref.dtype)


def rms_norm_bf16(x, g):
    m, d = x.shape
    bm = _block(m, 512)
    return pl.pallas_call(
        _rms_kernel,
        grid=(m // bm,),
        in_specs=[pl.BlockSpec((bm, d), lambda i: (i, 0)),
                  pl.BlockSpec((1, d), lambda i: (0, 0))],
        out_specs=pl.BlockSpec((bm, d), lambda i: (i, 0)),
        out_shape=jax.ShapeDtypeStruct((m, d), BF16),
        name="rms_norm",
        compiler_params=_params("parallel"),
    )(x, g.reshape(1, d))


def _mm_kernel(*refs, has_extra, has_bias):
    refs = list(refs)
    a_ref = refs.pop(0)
    a2_ref = refs.pop(0) if has_extra else None
    w_ref = refs.pop(0)
    b_ref = refs.pop(0) if has_bias else None
    o_ref = refs.pop(0)
    o2_ref = refs.pop(0) if has_extra else None
    wb_ref, = refs

    def product(x_ref, y_ref, rows):
        for r0 in range(0, x_ref.shape[0], rows):
            acc = jnp.dot(x_ref[r0:r0 + rows, :], wb_ref[...], preferred_element_type=F32)
            if has_bias:
                acc = jax.nn.sigmoid(acc + b_ref[...])
            y_ref[r0:r0 + rows, :] = acc.astype(y_ref.dtype)

    @pl.when(pl.program_id(1) == 0)
    def _():
        wb_ref[...] = w_ref[...].astype(BF16)
        if has_extra:
            product(a2_ref, o2_ref, a2_ref.shape[0])

    product(a_ref, o_ref, _block(a_ref.shape[0], 512) if has_bias else a_ref.shape[0])


def matmul_cols(a, w, col0, n, out_dtype, extra=None, bias=None):
    m, k = a.shape
    bm = _block(m, MM_ROWS * 2 // jnp.dtype(out_dtype).itemsize)
    bn = _block(n, 1024)
    assert col0 % bn == 0
    off = col0 // bn
    in_specs = [pl.BlockSpec((bm, k), lambda j, i: (i, 0))]
    args = [a]
    out_specs = [pl.BlockSpec((bm, bn), lambda j, i: (i, j))]
    out_shape = [jax.ShapeDtypeStruct((m, n), out_dtype)]
    if extra is not None:
        m2 = extra.shape[0]
        in_specs.append(pl.BlockSpec((m2, k), lambda j, i: (0, 0)))
        args.append(extra)
        out_specs.append(pl.BlockSpec((m2, bn), lambda j, i: (0, j)))
        out_shape.append(jax.ShapeDtypeStruct((m2, n), out_dtype))
    in_specs.append(pl.BlockSpec((k, bn), lambda j, i: (0, j + off)))
    args.append(w)
    if bias is not None:
        in_specs.append(pl.BlockSpec((1, bn), lambda j, i: (0, j)))
        args.append(bias.reshape(1, n).astype(F32))
    out = pl.pallas_call(
        functools.partial(_mm_kernel, has_extra=extra is not None, has_bias=bias is not None),
        grid=(n // bn, m // bm),
        in_specs=in_specs,
        out_specs=out_specs,
        out_shape=out_shape,
        scratch_shapes=[pltpu.VMEM((k, bn), BF16)],
        name="matmul_cols",
        compiler_params=_params("arbitrary", "arbitrary"),
    )(*args)
    return out if extra is not None else out[0]


def _diff_lambda(lq1_ref, lk1_ref, lq2_ref, lk2_ref, lam_init):
    return (jnp.exp(jnp.sum(lq1_ref[...] * lk1_ref[...], keepdims=True))
            - jnp.exp(jnp.sum(lq2_ref[...] * lk2_ref[...], keepdims=True)) + lam_init)


def _head_norm(o, sub_ref, lam_init):
    ms = jnp.mean(o * o, axis=-1, keepdims=True)
    return o * lax.rsqrt(ms + EPS) * sub_ref[...] * (1.0 - lam_init)


def _own_alibi(n, slope):
    r = lax.broadcasted_iota(jnp.int32, (n, n), 0)
    c = lax.broadcasted_iota(jnp.int32, (n, n), 1)
    bias = slope * (r - jnp.abs(r - c)).astype(F32)
    return jnp.where(c // CHUNK <= r // CHUNK, bias, NEG_INF)


def _max_sq_norms(x, n_groups):
    w = x.shape[1]
    lane = lax.broadcasted_iota(jnp.int32, (w, LANES), 0)
    group = lax.broadcasted_iota(jnp.int32, (w, LANES), 1)
    indicator = jnp.where(lane // (w // n_groups) == group, 1.0, 0.0).astype(BF16)
    sq = jnp.square(x.astype(F32)).astype(BF16)
    return jnp.max(jnp.dot(sq, indicator, preferred_element_type=F32), axis=0, keepdims=True)


def _attn_a_kernel(q_ref, k_ref, v_ref, slope_ref, kf_ref, qf_ref, lq1_ref, lk1_ref, lq2_ref, lk2_ref,
                   sub_ref, o_ref, ke_scr, ve_scr, *, tq, lam_init):
    q_sq = _max_sq_norms(q_ref[0] * (DH_A ** -0.5), 2)
    k_sq = _max_sq_norms(k_ref[0].astype(BF16), 2)
    safe = jnp.max((q_sq * k_sq)[:, :2]) <= SAFE_LOGIT ** 2 / 1.03
    args = (q_ref, k_ref, v_ref, slope_ref, lq1_ref, lk1_ref, lq2_ref, lk2_ref, sub_ref, o_ref)

    @pl.when(safe)
    def _():
        _attn_a_bounded(*args, kf_ref, qf_ref, ke_scr, ve_scr, tq=tq, lam_init=lam_init)

    @pl.when(jnp.logical_not(safe))
    def _():
        _attn_a_general(*args, tq=tq, lam_init=lam_init)


def _position_features(t):
    pos = jnp.arange(t, dtype=jnp.int32)[:, None]
    lo = pos & 7
    hi = (pos - lo).astype(F32)
    lo = lo.astype(F32)
    one = jnp.ones((t, 1), F32)
    pad = jnp.zeros((t, LANES - 4), F32)
    k_side = jnp.concatenate([hi, lo, one, one, pad], axis=1).astype(BF16)
    q_side = jnp.concatenate([one, one, -hi, -lo, pad], axis=1)
    return k_side, q_side


def _attn_a_bounded(q_ref, k_ref, v_ref, slope_ref, lq1_ref, lk1_ref, lq2_ref, lk2_ref, sub_ref, o_ref,
                    kf_ref, qf_ref, ke_scr, ve_scr, *, tq, lam_init):
    t = q_ref.shape[1]
    slope = slope_ref[0][:, :1]
    lam = _diff_lambda(lq1_ref, lk1_ref, lq2_ref, lk2_ref, lam_init)
    first_half = lax.broadcasted_iota(jnp.int32, (1, DK_A), 1) < DH_A
    ke_scr[:, :DK_A] = k_ref[0].astype(BF16)
    ke_scr[:, DK_A:] = kf_ref[...]
    ve_scr[:, :DV_A] = v_ref[0].astype(BF16)
    ve_scr[:, DV_A:] = jnp.ones((t, DV_A), BF16)

    r = lax.broadcasted_iota(jnp.int32, (tq, tq), 0)
    c = lax.broadcasted_iota(jnp.int32, (tq, tq), 1)
    own_fix = jnp.where(c // CHUNK <= r // CHUNK, -2.0 * slope * jnp.maximum(c - r, 0).astype(F32), NEG_INF)
    own_fix = jnp.concatenate([own_fix, own_fix], axis=0)

    for i in range(t // tq):
        nb = i * tq
        q = q_ref[0, nb:nb + tq, :] * (DH_A ** -0.5)
        q_feat = (qf_ref[nb:nb + tq, :] * slope).astype(BF16)
        zero = jnp.zeros_like(q)
        qe = jnp.concatenate([jnp.concatenate([jnp.where(first_half, q, zero), q_feat], axis=1),
                              jnp.concatenate([jnp.where(first_half, zero, q), q_feat], axis=1)], axis=0)
        s_own = lax.dot_general(qe, ke_scr[nb:nb + tq, :], CONTRACT_LAST, preferred_element_type=F32)
        acc = jnp.dot(jnp.exp(s_own + own_fix).astype(BF16), ve_scr[nb:nb + tq, :],
                      preferred_element_type=F32)
        if nb:
            s_bef = lax.dot_general(qe, ke_scr[:nb, :], CONTRACT_LAST, preferred_element_type=F32)
            acc = acc + jnp.dot(jnp.exp(s_bef).astype(BF16), ve_scr[:nb, :], preferred_element_type=F32)
        o = acc[:, :DV_A] / acc[:, DV_A:]
        o = o[:tq] - lam * o[tq:]
        o_ref[0, nb:nb + tq, :] = _head_norm(o, sub_ref, lam_init).astype(o_ref.dtype)


def _attn_a_general(q_ref, k_ref, v_ref, slope_ref, lq1_ref, lk1_ref, lq2_ref, lk2_ref, sub_ref, o_ref,
                    *, tq, lam_init):
    t = q_ref.shape[1]
    n_blocks = t // tq
    slope = slope_ref[0][:, :1]
    lam = _diff_lambda(lq1_ref, lk1_ref, lq2_ref, lk2_ref, lam_init)
    k = k_ref[0].astype(BF16)
    v = v_ref[0].astype(BF16)
    first_half = lax.broadcasted_iota(jnp.int32, (1, DK_A), 1) < DH_A
    own_bias = _own_alibi(tq, slope)
    n_before_max = (n_blocks - 1) * tq
    if n_before_max:
        j = lax.broadcasted_iota(jnp.int32, (1, n_before_max), 1)
        before_bias = slope * (j - n_before_max).astype(F32)

    for i in range(n_blocks):
        nb = i * tq
        q = q_ref[0, nb:nb + tq, :] * (DH_A ** -0.5)
        k_own, v_own = k[nb:nb + tq], v[nb:nb + tq]

        def softmax_pv(qh):
            s_own = lax.dot_general(qh, k_own, CONTRACT_LAST, preferred_element_type=F32) + own_bias
            m = jnp.max(s_own, axis=-1, keepdims=True)
            if nb:
                s_bef = lax.dot_general(qh, k[:nb], CONTRACT_LAST, preferred_element_type=F32)
                s_bef = s_bef + before_bias[:, n_before_max - nb:]
                m = jnp.maximum(m, jnp.max(s_bef, axis=-1, keepdims=True))
            p_own = jnp.exp(s_own - m)
            l = jnp.sum(p_own, axis=-1, keepdims=True)
            pv = jnp.dot(p_own.astype(BF16), v_own, preferred_element_type=F32)
            if nb:
                p_bef = jnp.exp(s_bef - m)
                l = l + jnp.sum(p_bef, axis=-1, keepdims=True)
                pv = pv + jnp.dot(p_bef.astype(BF16), v[:nb], preferred_element_type=F32)
            return pv / l

        o = (softmax_pv(jnp.where(first_half, q, jnp.zeros_like(q)))
             - lam * softmax_pv(jnp.where(first_half, jnp.zeros_like(q), q)))
        o_ref[0, nb:nb + tq, :] = _head_norm(o, sub_ref, lam_init).astype(o_ref.dtype)


def _row(a):
    return a.reshape(1, -1).astype(F32)


def _const_spec(shape):
    return pl.BlockSpec(shape, lambda *_: (0,) * len(shape))


def attention_a(q, k, v, lq1, lk1, lq2, lk2, subln, lam_init):
    b, t, _ = q.shape
    tq = _block(t, A_BLOCK)
    assert tq % CHUNK == 0
    slopes = jnp.asarray([[[2.0 ** (-8.0 * (hh + 1) / H_A)] * LANES] for hh in range(H_A)], dtype=F32)
    head = lambda width: pl.BlockSpec((1, t, width), lambda bi, hi: (bi, 0, hi))
    k_feat, q_feat = _position_features(t)
    return pl.pallas_call(
        functools.partial(_attn_a_kernel, tq=tq, lam_init=lam_init),
        grid=(b, H_A),
        in_specs=[head(DK_A), head(DK_A), head(DV_A),
                  pl.BlockSpec((1, 1, LANES), lambda bi, hi: (hi, 0, 0)),
                  _const_spec((t, LANES)), _const_spec((t, LANES)),
                  _const_spec((1, DH_A)), _const_spec((1, DH_A)), _const_spec((1, DH_A)),
                  _const_spec((1, DH_A)), _const_spec((1, DV_A))],
        out_specs=head(DV_A),
        out_shape=jax.ShapeDtypeStruct((b, t, H_A * DV_A), BF16),
        scratch_shapes=[pltpu.VMEM((t, DK_A + LANES), BF16), pltpu.VMEM((t, 2 * DV_A), BF16)],
        name="attention_a",
        compiler_params=_params("parallel", "arbitrary"),
    )(q, k, v, slopes, k_feat, q_feat, _row(lq1), _row(lk1), _row(lq2), _row(lk2), _row(subln))


def _attn_a_decode_kernel(q_ref, kn_ref, vn_ref, ck_ref, cv_ref, lq1_ref, lk1_ref, lq2_ref, lk2_ref,
                          sub_ref, o_ref, *, lam_init):
    t = q_ref.shape[1]
    p_len = ck_ref.shape[1] // H_A
    lam = _diff_lambda(lq1_ref, lk1_ref, lq2_ref, lk2_ref, lam_init)
    first_half = lax.broadcasted_iota(jnp.int32, (1, DK_A), 1) < DH_A
    j = lax.broadcasted_iota(jnp.int32, (1, p_len), 1)
    before_dist = (j - p_len).astype(F32)
    for h in range(H_A):
        slope = 2.0 ** (-8.0 * (h + 1) / H_A)
        cols = slice(h * DK_A, (h + 1) * DK_A)
        q = q_ref[0, :, cols] * (DH_A ** -0.5)
        kc = ck_ref[0, pl.ds(h, p_len, stride=H_A), :].astype(BF16)
        vc = cv_ref[0, pl.ds(h, p_len, stride=H_A), :].astype(BF16)
        kn = kn_ref[0, :, cols].astype(BF16)
        vn = vn_ref[0, :, cols].astype(BF16)
        own_bias = _own_alibi(t, slope)
        before_bias = slope * before_dist

        def softmax_pv(qh):
            s_new = lax.dot_general(qh, kn, CONTRACT_LAST, preferred_element_type=F32) + own_bias
            s_old = lax.dot_general(qh, kc, CONTRACT_LAST, preferred_element_type=F32) + before_bias
            m = jnp.maximum(jnp.max(s_new, axis=-1, keepdims=True), jnp.max(s_old, axis=-1, keepdims=True))
            p_new = jnp.exp(s_new - m)
            p_old = jnp.exp(s_old - m)
            l = jnp.sum(p_new, axis=-1, keepdims=True) + jnp.sum(p_old, axis=-1, keepdims=True)
            pv = (jnp.dot(p_new.astype(BF16), vn, preferred_element_type=F32)
                  + jnp.dot(p_old.astype(BF16), vc, preferred_element_type=F32))
            return pv / l

        o = (softmax_pv(jnp.where(first_half, q, jnp.zeros_like(q)))
             - lam * softmax_pv(jnp.where(first_half, jnp.zeros_like(q), q)))
        o_ref[0, :, cols] = _head_norm(o, sub_ref, lam_init).astype(o_ref.dtype)


def attention_a_decode(q, k_new, v_new, cache_k, cache_v, lq1, lk1, lq2, lk2, subln, lam_init):
    b, t, w = q.shape
    p_len = cache_k.shape[1]
    assert p_len % CHUNK == 0 and t <= CHUNK
    rows = lambda a: a.reshape(b, p_len * H_A, a.shape[-1])
    new = pl.BlockSpec((1, t, w), lambda bi: (bi, 0, 0))
    old = pl.BlockSpec((1, p_len * H_A, DK_A), lambda bi: (bi, 0, 0))
    return pl.pallas_call(
        functools.partial(_attn_a_decode_kernel, lam_init=lam_init),
        grid=(b,),
        in_specs=[new, new, new, old, old,
                  _const_spec((1, DH_A)), _const_spec((1, DH_A)), _const_spec((1, DH_A)),
                  _const_spec((1, DH_A)), _const_spec((1, DV_A))],
        out_specs=new,
        out_shape=jax.ShapeDtypeStruct((b, t, w), BF16),
        name="attention_a_decode",
        compiler_params=_params("parallel"),
    )(q, k_new, v_new, rows(cache_k), rows(cache_v), _row(lq1), _row(lk1), _row(lq2), _row(lk2), _row(subln))


def _band_tile(tq):
    pad = -(-tq // LANES) * LANES
    return BAND_PAST + pad, pad


def _band_bias_mask(g_row, tq):
    tile_w, pad = _band_tile(tq)
    width = g_row.shape[1]
    assert width == tile_w + pad
    rolled = pltpu.roll(jnp.broadcast_to(g_row, (tq, width)), width - pad, 1, stride=1, stride_axis=0)
    r = lax.broadcasted_iota(jnp.int32, (tq, tile_w), 0)
    c = lax.broadcasted_iota(jnp.int32, (tq, tile_w), 1)
    qch = r // CHUNK
    kch = c // CHUNK - N_PREV_CHUNKS
    valid = (kch <= qch) & (kch >= qch - N_PREV_CHUNKS)
    return jnp.where(valid, rolled[:, :tile_w], NEG_INF)


def _toeplitz_rows(rel_bias, tq):
    tile_w, pad = _band_tile(tq)
    width = tile_w + pad
    n_lo = pad + BAND_PAST - MAX_REL
    n_hi = max(width - n_lo - (2 * MAX_REL + 1), 0)
    lo = jnp.broadcast_to(rel_bias[:, :1], (H_B, n_lo))
    hi = jnp.broadcast_to(rel_bias[:, -1:], (H_B, n_hi))
    g = jnp.concatenate([lo, rel_bias, hi], axis=1)[:, :width]
    return g.reshape(H_B, 1, width).astype(F32)


def _attn_b_kernel(q_ref, k_ref, v_ref, g_ref, o_ref, ve_scr, *, tq):
    t = q_ref.shape[1]
    tile_w, _ = _band_tile(tq)
    scale = DH_B ** -0.5
    bias_mask = _band_bias_mask(g_ref[0], tq)
    qk_sq = _max_sq_norms(q_ref[0], 1) * _max_sq_norms(k_ref[0].astype(BF16), 1)
    room = SAFE_LOGIT - jnp.max(jnp.abs(g_ref[0]))
    safe = jnp.logical_and(room > 0.0, jnp.max(qk_sq[:, :1]) * (scale * scale * 1.03) <= room * room)

    def groups():
        for gi in range(t // tq):
            qa = gi * tq
            lo = max(qa - BAND_PAST, 0)
            hi = qa + tq
            off = lo - qa + BAND_PAST
            assert off % LANES == 0 and off + hi - lo == tile_w
            yield slice(qa, hi), slice(lo, hi), off

    @pl.when(safe)
    def _():
        log2e = math.log2(math.e)
        k = k_ref[0].astype(BF16)
        ve_scr[:, :DH_B] = v_ref[0].astype(BF16)
        ve_scr[:, DH_B:] = jnp.ones((t, DH_B), BF16)
        bias2 = bias_mask * log2e
        for rows, keys, off in groups():
            s = lax.dot_general(q_ref[0, rows, :], k[keys], CONTRACT_LAST, preferred_element_type=F32)
            p = jnp.exp2(s * (scale * log2e) + bias2[:, off:])
            acc = jnp.dot(p.astype(BF16), ve_scr[keys, :], preferred_element_type=F32)
            o_ref[0, rows, :] = (acc[:, :DH_B] / acc[:, DH_B:]).astype(o_ref.dtype)

    @pl.when(jnp.logical_not(safe))
    def _():
        k = k_ref[0].astype(BF16)
        v = v_ref[0].astype(BF16)
        for rows, keys, off in groups():
            s = lax.dot_general(q_ref[0, rows, :], k[keys], CONTRACT_LAST, preferred_element_type=F32) * scale
            s = s + bias_mask[:, off:]
            m = jnp.max(s, axis=-1, keepdims=True)
            p = jnp.exp(s - m)
            l = jnp.sum(p, axis=-1, keepdims=True)
            o = jnp.dot(p.astype(BF16), v[keys], preferred_element_type=F32) / l
            o_ref[0, rows, :] = o.astype(o_ref.dtype)


def attention_b(q, k, v, rel_bias):
    b, t, _ = q.shape
    tq = _block(t, BAND_GROUP)
    assert tq % LANES == 0
    g = _toeplitz_rows(rel_bias, tq)
    head = pl.BlockSpec((1, t, DH_B), lambda bi, hi: (bi, 0, hi))
    return pl.pallas_call(
        functools.partial(_attn_b_kernel, tq=tq),
        grid=(b, H_B),
        in_specs=[head, head, head, pl.BlockSpec((1, 1, g.shape[2]), lambda bi, hi: (hi, 0, 0))],
        out_specs=head,
        out_shape=jax.ShapeDtypeStruct((b, t, H_B * DH_B), BF16),
        scratch_shapes=[pltpu.VMEM((t, 2 * DH_B), BF16)],
        name="attention_b",
        compiler_params=_params("parallel", "arbitrary"),
    )(q, k, v, g)


def _attn_b_decode_kernel(q_ref, kn_ref, vn_ref, ck_ref, cv_ref, g_ref, o_ref):
    t = q_ref.shape[1]
    scale = DH_B ** -0.5
    for h in range(H_B):
        cols = slice(h * DH_B, (h + 1) * DH_B)
        q = q_ref[0, :, cols]
        kc = ck_ref[0, pl.ds(h, BAND_PAST, stride=H_B), :].astype(BF16)
        vc = cv_ref[0, pl.ds(h, BAND_PAST, stride=H_B), :].astype(BF16)
        kn = kn_ref[0, :, cols].astype(BF16)
        vn = vn_ref[0, :, cols].astype(BF16)
        bias_mask = _band_bias_mask(g_ref[h], t)
        s_old = lax.dot_general(q, kc, CONTRACT_LAST, preferred_element_type=F32) * scale
        s_old = s_old + bias_mask[:, :BAND_PAST]
        s_new = lax.dot_general(q, kn, CONTRACT_LAST, preferred_element_type=F32) * scale
        s_new = s_new + bias_mask[:, BAND_PAST:BAND_PAST + t]
        m = jnp.maximum(jnp.max(s_new, axis=-1, keepdims=True), jnp.max(s_old, axis=-1, keepdims=True))
        p_new = jnp.exp(s_new - m)
        p_old = jnp.exp(s_old - m)
        l = jnp.sum(p_new, axis=-1, keepdims=True) + jnp.sum(p_old, axis=-1, keepdims=True)
        pv = (jnp.dot(p_new.astype(BF16), vn, preferred_element_type=F32)
              + jnp.dot(p_old.astype(BF16), vc, preferred_element_type=F32))
        o_ref[0, :, cols] = (pv / l).astype(o_ref.dtype)


def attention_b_decode(q, k_new, v_new, cache_k, cache_v, rel_bias, p_len):
    b, t, w = q.shape
    assert cache_k.shape[1] == BAND_PAST and p_len % CHUNK == 0 and p_len >= BAND_PAST and t <= CHUNK
    g = _toeplitz_rows(rel_bias, t)
    rows = lambda a: a.reshape(b, BAND_PAST * H_B, a.shape[-1])
    new = pl.BlockSpec((1, t, w), lambda bi: (bi, 0, 0))
    old = pl.BlockSpec((1, BAND_PAST * H_B, DH_B), lambda bi: (bi, 0, 0))
    return pl.pallas_call(
        _attn_b_decode_kernel,
        grid=(b,),
        in_specs=[new, new, new, old, old, _const_spec(g.shape)],
        out_specs=new,
        out_shape=jax.ShapeDtypeStruct((b, t, w), BF16),
        name="attention_b_decode",
        compiler_params=_params("parallel"),
    )(q, k_new, v_new, rows(cache_k), rows(cache_v), g)


def _attn_c_kernel(q_ref, k_ref, v_ref, o_ref, *, tq):
    t = q_ref.shape[1]
    scale = DH_C ** -0.5
    k = k_ref[0].astype(BF16)
    v = v_ref[0].astype(BF16)
    for i in range(t // tq):
        q = q_ref[0, i * tq:(i + 1) * tq, :]
        s = lax.dot_general(q, k, CONTRACT_LAST, preferred_element_type=F32) * scale
        m = jnp.max(s, axis=-1, keepdims=True)
        p = jnp.exp(s - m)
        l = jnp.sum(p, axis=-1, keepdims=True)
        o = jnp.dot(p.astype(BF16), v, preferred_element_type=F32) / l
        o_ref[0, i * tq:(i + 1) * tq, :] = o.astype(o_ref.dtype)


def attention_c(q, mk, mv):
    b, t, _ = q.shape
    n_mem = mk.shape[1]
    tq = _block(t, 512)
    return pl.pallas_call(
        functools.partial(_attn_c_kernel, tq=tq),
        grid=(b, H_C),
        in_specs=[pl.BlockSpec((1, t, DH_C), lambda bi, hi: (bi, 0, hi)),
                  pl.BlockSpec((1, n_mem, DH_C), lambda bi, hi: (bi, 0, hi)),
                  pl.BlockSpec((1, n_mem, DH_C), lambda bi, hi: (bi, 0, hi))],
        out_specs=pl.BlockSpec((1, t, DH_C), lambda bi, hi: (bi, 0, hi)),
        out_shape=jax.ShapeDtypeStruct((b, t, H_C * DH_C), BF16),
        name="attention_c",
        compiler_params=_params("parallel", "arbitrary"),
    )(q, mk, mv)


def _merge_kernel(oa_ref, ob_ref, oc_ref, ga_ref, gb_ref, gc_ref, wa_ref, wb_ref, wc_ref, o_ref):
    def branch(o_r, g_r, w_r):
        return g_r[...].astype(F32) * jnp.dot(o_r[...], w_r[...], preferred_element_type=F32)

    merged = branch(oa_ref, ga_ref, wa_ref) + branch(ob_ref, gb_ref, wb_ref) + branch(oc_ref, gc_ref, wc_ref)
    o_ref[...] = merged.astype(o_ref.dtype)


def merge_branches(oa, ob, oc, gates, wa, wb, wc):
    m, w_in = oa.shape
    d = wa.shape[1]
    bm = _block(m, 256)
    row = lambda width: pl.BlockSpec((bm, width), lambda i: (i, 0))
    gate = lambda j: pl.BlockSpec((bm, d), lambda i: (i, j))
    weight = pl.BlockSpec((w_in, d), lambda i: (0, 0), pipeline_mode=pl.Buffered(1))
    return pl.pallas_call(
        _merge_kernel,
        grid=(m // bm,),
        in_specs=[row(w_in), row(w_in), row(w_in), gate(0), gate(1), gate(2), weight, weight, weight],
        out_specs=pl.BlockSpec((bm, d), lambda i: (i, 0)),
        out_shape=jax.ShapeDtypeStruct((m, d), BF16),
        name="merge_branches",
        compiler_params=_params("parallel"),
    )(oa, ob, oc, gates, gates, gates, wa, wb, wc)


def _proj_norm_res_kernel(*refs, next_norm, rows):
    if next_norm:
        a_ref, w_ref, x_ref, g_ref, g2_ref, o_ref, h_ref = refs
    else:
        a_ref, w_ref, x_ref, g_ref, o_ref = refs
    bm = a_ref.shape[0]
    for r0 in range(0, bm, rows):
        sl = slice(r0, r0 + rows)
        y = jnp.dot(a_ref[sl, :], w_ref[...], preferred_element_type=F32)
        ms = jnp.mean(y * y, axis=-1, keepdims=True)
        o = x_ref[sl, :] + y * lax.rsqrt(ms + EPS) * g_ref[...]
        o_ref[sl, :] = o
        if next_norm:
            ms2 = jnp.mean(o * o, axis=-1, keepdims=True)
            h_ref[sl, :] = (o * lax.rsqrt(ms2 + EPS) * g2_ref[...]).astype(h_ref.dtype)


def proj_norm_residual(a, w, x, g, bm, rows, next_g=None):
    m, k = a.shape
    d = w.shape[1]
    bm = _block(m, bm)
    rows = _block(bm, rows)
    vec = pl.BlockSpec((1, d), lambda i: (0, 0))
    row = pl.BlockSpec((bm, d), lambda i: (i, 0))
    in_specs = [pl.BlockSpec((bm, k), lambda i: (i, 0)),
                pl.BlockSpec((k, d), lambda i: (0, 0), pipeline_mode=pl.Buffered(1)), row, vec]
    args = [a, w, x, g.reshape(1, d).astype(F32)]
    out_specs, out_shape = row, jax.ShapeDtypeStruct((m, d), F32)
    if next_g is not None:
        in_specs.append(vec)
        args.append(next_g.reshape(1, d).astype(F32))
        out_specs = (row, row)
        out_shape = (out_shape, jax.ShapeDtypeStruct((m, d), BF16))
    return pl.pallas_call(
        functools.partial(_proj_norm_res_kernel, next_norm=next_g is not None, rows=rows),
        grid=(m // bm,),
        in_specs=in_specs,
        out_specs=out_specs,
        out_shape=out_shape,
        name="proj_norm_residual",
        compiler_params=_params("parallel"),
    )(*args)


def _ffn_in_kernel(h_ref, h2_ref, wa_ref, wb_ref, o_ref, o2_ref, wa_bf, wb_bf):
    def swiglu(h):
        a = jnp.dot(h, wa_bf[...], preferred_element_type=F32)
        b = jnp.dot(h, wb_bf[...], preferred_element_type=F32)
        return (jax.nn.silu(a) * b).astype(o_ref.dtype)

    @pl.when(pl.program_id(1) == 0)
    def _():
        wa_bf[...] = wa_ref[...].astype(BF16)
        wb_bf[...] = wb_ref[...].astype(BF16)
        o2_ref[...] = swiglu(h2_ref[...])

    o_ref[...] = swiglu(h_ref[...])


def ffn_in(h, extra, w):
    m, k = h.shape
    m2 = extra.shape[0]
    f = w.shape[1] // 2
    bm = _block(m, 1024)
    bf = 512
    assert f % bf == 0
    nf = f // bf
    return pl.pallas_call(
        _ffn_in_kernel,
        grid=(nf, m // bm),
        in_specs=[pl.BlockSpec((bm, k), lambda j, i: (i, 0)),
                  pl.BlockSpec((m2, k), lambda j, i: (0, 0)),
                  pl.BlockSpec((k, bf), lambda j, i: (0, j)),
                  pl.BlockSpec((k, bf), lambda j, i: (0, j + nf))],
        out_specs=[pl.BlockSpec((bm, bf), lambda j, i: (i, j)),
                   pl.BlockSpec((m2, bf), lambda j, i: (0, j))],
        out_shape=[jax.ShapeDtypeStruct((m, f), BF16), jax.ShapeDtypeStruct((m2, f), BF16)],
        scratch_shapes=[pltpu.VMEM((k, bf), BF16), pltpu.VMEM((k, bf), BF16)],
        name="ffn_in",
        compiler_params=_params("arbitrary", "arbitrary"),
    )(h, extra, w, w)


def _layer(xp, xd, mk_p, mv_p, mk_d, mv_d, caches, lam_init, norm_mix_pre, norm_mix_post, w_in, b_gate,
           lq1, lk1, lq2, lk2, subln_a, rel_bias, w_br_a, w_br_b, w_br_c, w_out,
           norm_ffn_pre, norm_ffn_post, w_ffn_in, w_ffn_out):
    d = xp.shape[-1]
    w_head = H_A * DK_A
    shapes = [xp.shape[:2], xd.shape[:2]]
    xs = [xp.reshape(-1, d), xd.reshape(-1, d)]
    hs = [rms_norm_bf16(x, norm_mix_pre) for x in xs]

    def proj(idx, dt):
        outs = matmul_cols(hs[0], w_in, idx * w_head, w_head, dt, extra=hs[1])
        return [o.reshape(*shp, w_head) for o, shp in zip(outs, shapes)]

    q_a, k_a, v_a = proj(0, BF16), proj(1, F32), proj(2, F32)
    q_b, k_b, v_b = proj(3, BF16), proj(4, F32), proj(5, F32)
    q_c = proj(6, BF16)
    gates = matmul_cols(hs[0], w_in, 7 * w_head, N_BRANCH * d, BF16, extra=hs[1], bias=b_gate)

    ca_k, ca_v, cb_k, cb_v = caches
    o_a = [attention_a(q_a[0], k_a[0], v_a[0], lq1, lk1, lq2, lk2, subln_a, lam_init),
           attention_a_decode(q_a[1], k_a[1], v_a[1], ca_k, ca_v, lq1, lk1, lq2, lk2, subln_a, lam_init)]
    o_b = [attention_b(q_b[0], k_b[0], v_b[0], rel_bias),
           attention_b_decode(q_b[1], k_b[1], v_b[1], cb_k, cb_v, rel_bias, ca_k.shape[1])]
    o_c = [attention_c(q_c[0], mk_p, mv_p), attention_c(q_c[1], mk_d, mv_d)]

    x1, h2 = [], []
    for i in range(2):
        m = xs[i].shape[0]
        merged = merge_branches(o_a[i].reshape(m, -1), o_b[i].reshape(m, -1), o_c[i].reshape(m, -1),
                                gates[i], w_br_a, w_br_b, w_br_c)
        a, b = proj_norm_residual(merged, w_out, xs[i], norm_mix_post, bm=512, rows=PROJ_ROWS,
                                  next_g=norm_ffn_pre)
        x1.append(a)
        h2.append(b)
    acts = ffn_in(h2[0], h2[1], w_ffn_in)
    ys = [proj_norm_residual(acts[i], w_ffn_out, x1[i], norm_ffn_post, bm=256, rows=256).reshape(*shapes[i], d)
          for i in range(2)]
    return [(ys[i], k_a[i], v_a[i], k_b[i], v_b[i]) for i in range(2)]


def kernel(x_prompt, x_sample, cache_a_k, cache_a_v, cache_b_k, cache_b_v, cache_mem_k, cache_mem_v, mem_prompt, norm_mix_pre, norm_mix_post, norm_mem, w_in, b_gate, lambda_q1, lambda_k1, lambda_q2, lambda_k2, subln_a, rel_bias_b, w_mem_kv, w_br_a, w_br_b, w_br_c, w_out, norm_ffn_pre, norm_ffn_post, w_ffn_in, w_ffn_out):
    depth = w_in.shape[0]
    bsz, s, d = x_prompt.shape
    n_mem = mem_prompt.shape[1]
    lb_prompt = min(BAND_PAST, s)
    yp, ys = x_prompt, x_sample
    outs = [[] for _ in range(10)]
    heads = lambda a, nh: a.reshape(a.shape[0], a.shape[1], nh, a.shape[2] // nh)
    flat = lambda a: a.reshape(a.shape[0], a.shape[1], -1)
    for l in range(depth):
        lam_init = 0.8 - 0.6 * math.exp(-0.3 * l)
        bf = lambda w: w[l].astype(BF16)
        shared = (lam_init, norm_mix_pre[l], norm_mix_post[l], w_in[l], b_gate[l],
                  lambda_q1[l], lambda_k1[l], lambda_q2[l], lambda_k2[l], subln_a[l], rel_bias_b[l],
                  bf(w_br_a), bf(w_br_b), bf(w_br_c), bf(w_out),
                  norm_ffn_pre[l], norm_ffn_post[l], w_ffn_in[l], bf(w_ffn_out))

        mem_n = rms_norm_bf16(mem_prompt.reshape(bsz * n_mem, d), norm_mem[l])
        w_c = H_C * DH_C
        mk_p = matmul_cols(mem_n, w_mem_kv[l], 0, w_c, F32).reshape(bsz, n_mem, w_c)
        mv_p = matmul_cols(mem_n, w_mem_kv[l], w_c, w_c, F32).reshape(bsz, n_mem, w_c)
        caches = (cache_a_k[l], cache_a_v[l], cache_b_k[l], cache_b_v[l])
        (yp, ka, va, kb, vb), (ys, ka_d, va_d, kb_d, vb_d) = _layer(
            yp, ys, mk_p, mv_p, flat(cache_mem_k[l]), flat(cache_mem_v[l]), caches, *shared)
        new = [heads(ka, H_A), heads(va, H_A), heads(kb[:, s - lb_prompt:], H_B),
               heads(vb[:, s - lb_prompt:], H_B), heads(mk_p, H_C), heads(mv_p, H_C),
               heads(ka_d, H_A), heads(va_d, H_A), heads(kb_d, H_B), heads(vb_d, H_B)]
        for o, a in zip(outs, new):
            o.append(a)
    return (yp, ys) + tuple(jnp.stack(o) for o in outs)
```

```python
import functools
import math

import jax
import jax.numpy as jnp
from jax import lax
from jax.experimental import pallas as pl
from jax.experimental.pallas import tpu as pltpu

F32 = jnp.float32
BF16 = jnp.bfloat16

CHUNK = 64
H_A = 8
DH_A = 64
DK_A = 2 * DH_A
DV_A = 128
H_B = 8
DH_B = 128
N_PREV_CHUNKS = 8
BAND_PAST = N_PREV_CHUNKS * CHUNK
MAX_REL = 128
H_C = 4
DH_C = 256
N_BRANCH = 3
EPS = 1e-6
NEG_INF = -1e30

LANES = 128
VMEM_LIMIT = 56 * 1024 * 1024
BAND_GROUP = 256
A_BLOCK = 256
MM_ROWS = 2048
MERGE_ROWS = 512
PROJ_ROWS = 128
SAFE_LOGIT = 40.0

CONTRACT_LAST = (((1,), (1,)), ((), ()))


def _params(*sem):
    return pltpu.CompilerParams(dimension_semantics=sem, vmem_limit_bytes=VMEM_LIMIT)


def _block(n, target):
    if n <= target:
        return n
    b = target
    while n % b:
        b //= 2
    return b


def _rms_kernel(x_ref, g_ref, o_ref):
    x = x_ref[...]
    ms = jnp.mean(x * x, axis=-1, keepdims=True)
    o_ref[...] = (x * lax.rsqrt(ms + EPS) * g_ref[...]).astype(o_ref.dtype)


def rms_norm_bf16(x, g):
    m, d = x.shape
    bm = _block(m, 512)
    return pl.pallas_call(
        _rms_kernel,
        grid=(m // bm,),
        in_specs=[pl.BlockSpec((bm, d), lambda i: (i, 0)),
                  pl.BlockSpec((1, d), lambda i: (0, 0))],
        out_specs=pl.BlockSpec((bm, d), lambda i: (i, 0)),
        out_shape=jax.ShapeDtypeStruct((m, d), BF16),
        name="rms_norm",
        compiler_params=_params("parallel"),
    )(x, g.reshape(1, d))


def _mm_kernel(*refs, has_extra):
    refs = list(refs)
    a_ref = refs.pop(0)
    a2_ref = refs.pop(0) if has_extra else None
    w_ref = refs.pop(0)
    o_ref = refs.pop(0)
    o2_ref = refs.pop(0) if has_extra else None
    wb_ref, = refs

    def product(x_ref, y_ref):
        rows = _block(x_ref.shape[0], 1024)
        for r0 in range(0, x_ref.shape[0], rows):
            acc = jnp.dot(x_ref[r0:r0 + rows, :], wb_ref[...], preferred_element_type=F32)
            y_ref[r0:r0 + rows, :] = acc.astype(y_ref.dtype)

    @pl.when(pl.program_id(1) == 0)
    def _():
        wb_ref[...] = w_ref[...].astype(BF16)
        if has_extra:
            product(a2_ref, o2_ref)

    product(a_ref, o_ref)


def matmul_cols(a, w, col0, n, out_dtype, extra=None):
    m, k = a.shape
    bm = _block(m, MM_ROWS * 2 // jnp.dtype(out_dtype).itemsize)
    bn = _block(n, 1024)
    assert col0 % bn == 0
    off = col0 // bn
    in_specs = [pl.BlockSpec((bm, k), lambda j, i: (i, 0))]
    args = [a]
    out_specs = [pl.BlockSpec((bm, bn), lambda j, i: (i, j))]
    out_shape = [jax.ShapeDtypeStruct((m, n), out_dtype)]
    if extra is not None:
        m2 = extra.shape[0]
        in_specs.append(pl.BlockSpec((m2, k), lambda j, i: (0, 0)))
        args.append(extra)
        out_specs.append(pl.BlockSpec((m2, bn), lambda j, i: (0, j)))
        out_shape.append(jax.ShapeDtypeStruct((m2, n), out_dtype))
    in_specs.append(pl.BlockSpec((k, bn), lambda j, i: (0, j + off)))
    args.append(w)
    out = pl.pallas_call(
        functools.partial(_mm_kernel, has_extra=extra is not None),
        grid=(n // bn, m // bm),
        in_specs=in_specs,
        out_specs=out_specs,
        out_shape=out_shape,
        scratch_shapes=[pltpu.VMEM((k, bn), BF16)],
        name="matmul_cols",
        compiler_params=_params("arbitrary", "arbitrary"),
    )(*args)
    return out if extra is not None else out[0]


def _diff_lambda(lq1_ref, lk1_ref, lq2_ref, lk2_ref, lam_init):
    return (jnp.exp(jnp.sum(lq1_ref[...] * lk1_ref[...], keepdims=True))
            - jnp.exp(jnp.sum(lq2_ref[...] * lk2_ref[...], keepdims=True)) + lam_init)


def _head_norm(o, sub_ref, lam_init):
    ms = jnp.mean(o * o, axis=-1, keepdims=True)
    return o * lax.rsqrt(ms + EPS) * sub_ref[...] * (1.0 - lam_init)


def _own_alibi(n, slope):
    r = lax.broadcasted_iota(jnp.int32, (n, n), 0)
    c = lax.broadcasted_iota(jnp.int32, (n, n), 1)
    bias = slope * (r - jnp.abs(r - c)).astype(F32)
    return jnp.where(c // CHUNK <= r // CHUNK, bias, NEG_INF)


def _max_sq_norms(x, lane_masks):
    sq = jnp.square(x.astype(F32))
    return [jnp.max(jnp.sum(jnp.where(mask, sq, 0.0), axis=-1, keepdims=True)) for mask in lane_masks]


def _attn_a_kernel(q_ref, k_ref, v_ref, slope_ref, kf_ref, qf_ref, lq1_ref, lk1_ref, lq2_ref, lk2_ref,
                   sub_ref, o_ref, ke_scr, ve_scr, *, tq, lam_init):
    args = (q_ref, k_ref, v_ref, slope_ref, lq1_ref, lk1_ref, lq2_ref, lk2_ref, sub_ref, o_ref)
    _attn_a_bounded(*args, kf_ref, qf_ref, ke_scr, ve_scr, tq=tq, lam_init=lam_init)
    first_half = lax.broadcasted_iota(jnp.int32, (1, DK_A), 1) < DH_A
    q_sq = _max_sq_norms(q_ref[0] * (DH_A ** -0.5), (first_half, ~first_half))
    k_sq = _max_sq_norms(k_ref[0].astype(BF16), (first_half, ~first_half))
    bound_sq = jnp.maximum(q_sq[0] * k_sq[0], q_sq[1] * k_sq[1])

    @pl.when(bound_sq > SAFE_LOGIT ** 2 / 1.01)
    def _():
        _attn_a_general(*args, tq=tq, lam_init=lam_init)


def _position_features(t):
    pos = jnp.arange(t, dtype=jnp.int32)[:, None]
    lo = pos & 7
    hi = (pos - lo).astype(F32)
    lo = lo.astype(F32)
    one = jnp.ones((t, 1), F32)
    pad = jnp.zeros((t, LANES - 4), F32)
    k_side = jnp.concatenate([hi, lo, one, one, pad], axis=1).astype(BF16)
    q_side = jnp.concatenate([one, one, -hi, -lo, pad], axis=1)
    return k_side, q_side


def _attn_a_bounded(q_ref, k_ref, v_ref, slope_ref, lq1_ref, lk1_ref, lq2_ref, lk2_ref, sub_ref, o_ref,
                    kf_ref, qf_ref, ke_scr, ve_scr, *, tq, lam_init):
    t = q_ref.shape[1]
    slope = slope_ref[0][:, :1]
    lam = _diff_lambda(lq1_ref, lk1_ref, lq2_ref, lk2_ref, lam_init)
    first_half = lax.broadcasted_iota(jnp.int32, (1, DK_A), 1) < DH_A
    ke_scr[:, :DK_A] = k_ref[0].astype(BF16)
    ke_scr[:, DK_A:] = kf_ref[...]
    ve_scr[:, :DV_A] = v_ref[0].astype(BF16)
    ve_scr[:, DV_A:] = jnp.ones((t, DV_A), BF16)

    r = lax.broadcasted_iota(jnp.int32, (tq, tq), 0)
    c = lax.broadcasted_iota(jnp.int32, (tq, tq), 1)
    own_fix = jnp.where(c // CHUNK <= r // CHUNK, -2.0 * slope * jnp.maximum(c - r, 0).astype(F32), NEG_INF)
    own_fix = jnp.concatenate([own_fix, own_fix], axis=0)

    for i in range(t // tq):
        nb = i * tq
        q = q_ref[0, nb:nb + tq, :] * (DH_A ** -0.5)
        q_feat = (qf_ref[nb:nb + tq, :] * slope).astype(BF16)
        zero = jnp.zeros_like(q)
        qe = jnp.concatenate([jnp.concatenate([jnp.where(first_half, q, zero), q_feat], axis=1),
                              jnp.concatenate([jnp.where(first_half, zero, q), q_feat], axis=1)], axis=0)
        s_own = lax.dot_general(qe, ke_scr[nb:nb + tq, :], CONTRACT_LAST, preferred_element_type=F32)
        acc = jnp.dot(jnp.exp(s_own + own_fix).astype(BF16), ve_scr[nb:nb + tq, :],
                      preferred_element_type=F32)
        if nb:
            s_bef = lax.dot_general(qe, ke_scr[:nb, :], CONTRACT_LAST, preferred_element_type=F32)
            acc = acc + jnp.dot(jnp.exp(s_bef).astype(BF16), ve_scr[:nb, :], preferred_element_type=F32)
        o = acc[:, :DV_A] / acc[:, DV_A:]
        o = o[:tq] - lam * o[tq:]
        o_ref[0, nb:nb + tq, :] = _head_norm(o, sub_ref, lam_init).astype(o_ref.dtype)


def _attn_a_general(q_ref, k_ref, v_ref, slope_ref, lq1_ref, lk1_ref, lq2_ref, lk2_ref, sub_ref, o_ref,
                    *, tq, lam_init):
    t = q_ref.shape[1]
    n_blocks = t // tq
    slope = slope_ref[0][:, :1]
    lam = _diff_lambda(lq1_ref, lk1_ref, lq2_ref, lk2_ref, lam_init)
    k = k_ref[0].astype(BF16)
    v = v_ref[0].astype(BF16)
    first_half = lax.broadcasted_iota(jnp.int32, (1, DK_A), 1) < DH_A
    own_bias = _own_alibi(tq, slope)
    n_before_max = (n_blocks - 1) * tq
    if n_before_max:
        j = lax.broadcasted_iota(jnp.int32, (1, n_before_max), 1)
        before_bias = slope * (j - n_before_max).astype(F32)

    for i in range(n_blocks):
        nb = i * tq
        q = q_ref[0, nb:nb + tq, :] * (DH_A ** -0.5)
        k_own, v_own = k[nb:nb + tq], v[nb:nb + tq]

        def softmax_pv(qh):
            s_own = lax.dot_general(qh, k_own, CONTRACT_LAST, preferred_element_type=F32) + own_bias
            m = jnp.max(s_own, axis=-1, keepdims=True)
            if nb:
                s_bef = lax.dot_general(qh, k[:nb], CONTRACT_LAST, preferred_element_type=F32)
                s_bef = s_bef + before_bias[:, n_before_max - nb:]
                m = jnp.maximum(m, jnp.max(s_bef, axis=-1, keepdims=True))
            p_own = jnp.exp(s_own - m)
            l = jnp.sum(p_own, axis=-1, keepdims=True)
            pv = jnp.dot(p_own.astype(BF16), v_own, preferred_element_type=F32)
            if nb:
                p_bef = jnp.exp(s_bef - m)
                l = l + jnp.sum(p_bef, axis=-1, keepdims=True)
                pv = pv + jnp.dot(p_bef.astype(BF16), v[:nb], preferred_element_type=F32)
            return pv / l

        o = (softmax_pv(jnp.where(first_half, q, jnp.zeros_like(q)))
             - lam * softmax_pv(jnp.where(first_half, jnp.zeros_like(q), q)))
        o_ref[0, nb:nb + tq, :] = _head_norm(o, sub_ref, lam_init).astype(o_ref.dtype)


def _row(a):
    return a.reshape(1, -1).astype(F32)


def _const_spec(shape):
    return pl.BlockSpec(shape, lambda *_: (0,) * len(shape))


def attention_a(q, k, v, lq1, lk1, lq2, lk2, subln, lam_init):
    b, t, _ = q.shape
    tq = _block(t, A_BLOCK)
    assert tq % CHUNK == 0
    slopes = jnp.asarray([[[2.0 ** (-8.0 * (hh + 1) / H_A)] * LANES] for hh in range(H_A)], dtype=F32)
    head = lambda width: pl.BlockSpec((1, t, width), lambda bi, hi: (bi, 0, hi))
    k_feat, q_feat = _position_features(t)
    return pl.pallas_call(
        functools.partial(_attn_a_kernel, tq=tq, lam_init=lam_init),
        grid=(b, H_A),
        in_specs=[head(DK_A), head(DK_A), head(DV_A),
                  pl.BlockSpec((1, 1, LANES), lambda bi, hi: (hi, 0, 0)),
                  _const_spec((t, LANES)), _const_spec((t, LANES)),
                  _const_spec((1, DH_A)), _const_spec((1, DH_A)), _const_spec((1, DH_A)),
                  _const_spec((1, DH_A)), _const_spec((1, DV_A))],
        out_specs=head(DV_A),
        out_shape=jax.ShapeDtypeStruct((b, t, H_A * DV_A), BF16),
        scratch_shapes=[pltpu.VMEM((t, DK_A + LANES), BF16), pltpu.VMEM((t, 2 * DV_A), BF16)],
        name="attention_a",
        compiler_params=_params("parallel", "arbitrary"),
    )(q, k, v, slopes, k_feat, q_feat, _row(lq1), _row(lk1), _row(lq2), _row(lk2), _row(subln))


def _attn_a_decode_kernel(q_ref, kn_ref, vn_ref, ck_ref, cv_ref, lq1_ref, lk1_ref, lq2_ref, lk2_ref,
                          sub_ref, o_ref, *, lam_init):
    t = q_ref.shape[1]
    p_len = ck_ref.shape[1] // H_A
    lam = _diff_lambda(lq1_ref, lk1_ref, lq2_ref, lk2_ref, lam_init)
    first_half = lax.broadcasted_iota(jnp.int32, (1, DK_A), 1) < DH_A
    j = lax.broadcasted_iota(jnp.int32, (1, p_len), 1)
    before_dist = (j - p_len).astype(F32)
    for h in range(H_A):
        slope = 2.0 ** (-8.0 * (h + 1) / H_A)
        cols = slice(h * DK_A, (h + 1) * DK_A)
        q = q_ref[0, :, cols] * (DH_A ** -0.5)
        kc = ck_ref[0, pl.ds(h, p_len, stride=H_A), :].astype(BF16)
        vc = cv_ref[0, pl.ds(h, p_len, stride=H_A), :].astype(BF16)
        kn = kn_ref[0, :, cols].astype(BF16)
        vn = vn_ref[0, :, cols].astype(BF16)
        own_bias = _own_alibi(t, slope)
        before_bias = slope * before_dist

        def softmax_pv(qh):
            s_new = lax.dot_general(qh, kn, CONTRACT_LAST, preferred_element_type=F32) + own_bias
            s_old = lax.dot_general(qh, kc, CONTRACT_LAST, preferred_element_type=F32) + before_bias
            m = jnp.maximum(jnp.max(s_new, axis=-1, keepdims=True), jnp.max(s_old, axis=-1, keepdims=True))
            p_new = jnp.exp(s_new - m)
            p_old = jnp.exp(s_old - m)
            l = jnp.sum(p_new, axis=-1, keepdims=True) + jnp.sum(p_old, axis=-1, keepdims=True)
            pv = (jnp.dot(p_new.astype(BF16), vn, preferred_element_type=F32)
                  + jnp.dot(p_old.astype(BF16), vc, preferred_element_type=F32))
            return pv / l

        o = (softmax_pv(jnp.where(first_half, q, jnp.zeros_like(q)))
             - lam * softmax_pv(jnp.where(first_half, jnp.zeros_like(q), q)))
        o_ref[0, :, cols] = _head_norm(o, sub_ref, lam_init).astype(o_ref.dtype)


def attention_a_decode(q, k_new, v_new, cache_k, cache_v, lq1, lk1, lq2, lk2, subln, lam_init):
    b, t, w = q.shape
    p_len = cache_k.shape[1]
    assert p_len % CHUNK == 0 and t <= CHUNK
    rows = lambda a: a.reshape(b, p_len * H_A, a.shape[-1])
    new = pl.BlockSpec((1, t, w), lambda bi: (bi, 0, 0))
    old = pl.BlockSpec((1, p_len * H_A, DK_A), lambda bi: (bi, 0, 0))
    return pl.pallas_call(
        functools.partial(_attn_a_decode_kernel, lam_init=lam_init),
        grid=(b,),
        in_specs=[new, new, new, old, old,
                  _const_spec((1, DH_A)), _const_spec((1, DH_A)), _const_spec((1, DH_A)),
                  _const_spec((1, DH_A)), _const_spec((1, DV_A))],
        out_specs=new,
        out_shape=jax.ShapeDtypeStruct((b, t, w), BF16),
        name="attention_a_decode",
        compiler_params=_params("parallel"),
    )(q, k_new, v_new, rows(cache_k), rows(cache_v), _row(lq1), _row(lk1), _row(lq2), _row(lk2), _row(subln))


def _band_tile(tq):
    pad = -(-tq // LANES) * LANES
    return BAND_PAST + pad, pad


def _band_bias_mask(g_row, tq):
    tile_w, pad = _band_tile(tq)
    width = g_row.shape[1]
    assert width == tile_w + pad
    rolled = pltpu.roll(jnp.broadcast_to(g_row, (tq, width)), width - pad, 1, stride=1, stride_axis=0)
    r = lax.broadcasted_iota(jnp.int32, (tq, tile_w), 0)
    c = lax.broadcasted_iota(jnp.int32, (tq, tile_w), 1)
    qch = r // CHUNK
    kch = c // CHUNK - N_PREV_CHUNKS
    valid = (kch <= qch) & (kch >= qch - N_PREV_CHUNKS)
    return jnp.where(valid, rolled[:, :tile_w], NEG_INF)


def _toeplitz_rows(rel_bias, tq):
    tile_w, pad = _band_tile(tq)
    width = tile_w + pad
    n_lo = pad + BAND_PAST - MAX_REL
    n_hi = max(width - n_lo - (2 * MAX_REL + 1), 0)
    lo = jnp.broadcast_to(rel_bias[:, :1], (H_B, n_lo))
    hi = jnp.broadcast_to(rel_bias[:, -1:], (H_B, n_hi))
    g = jnp.concatenate([lo, rel_bias, hi], axis=1)[:, :width]
    return g.reshape(H_B, 1, width).astype(F32)


def _attn_b_kernel(q_ref, k_ref, v_ref, g_ref, o_ref, ve_scr, *, tq):
    t = q_ref.shape[1]
    tile_w, _ = _band_tile(tq)
    scale = DH_B ** -0.5
    bias_mask = _band_bias_mask(g_ref[0], tq)

    def groups():
        for gi in range(t // tq):
            qa = gi * tq
            lo = max(qa - BAND_PAST, 0)
            hi = qa + tq
            off = lo - qa + BAND_PAST
            assert off % LANES == 0 and off + hi - lo == tile_w
            yield slice(qa, hi), slice(lo, hi), off

    log2e = math.log2(math.e)
    k = k_ref[0].astype(BF16)
    ve_scr[:, :DH_B] = v_ref[0].astype(BF16)
    ve_scr[:, DH_B:] = jnp.ones((t, DH_B), BF16)
    bias2 = bias_mask * log2e
    for rows, keys, off in groups():
        s = lax.dot_general(q_ref[0, rows, :], k[keys], CONTRACT_LAST, preferred_element_type=F32)
        p = jnp.exp2(s * (scale * log2e) + bias2[:, off:])
        acc = jnp.dot(p.astype(BF16), ve_scr[keys, :], preferred_element_type=F32)
        o_ref[0, rows, :] = (acc[:, :DH_B] / acc[:, DH_B:]).astype(o_ref.dtype)

    (q_sq,), (k_sq,) = _max_sq_norms(q_ref[0], (True,)), _max_sq_norms(k, (True,))
    room = SAFE_LOGIT - jnp.max(jnp.abs(g_ref[0]))
    safe = jnp.logical_and(room > 0.0, q_sq * k_sq * (scale * scale * 1.01) <= room * room)

    @pl.when(jnp.logical_not(safe))
    def _():
        k = k_ref[0].astype(BF16)
        v = v_ref[0].astype(BF16)
        for rows, keys, off in groups():
            s = lax.dot_general(q_ref[0, rows, :], k[keys], CONTRACT_LAST, preferred_element_type=F32) * scale
            s = s + bias_mask[:, off:]
            m = jnp.max(s, axis=-1, keepdims=True)
            p = jnp.exp(s - m)
            l = jnp.sum(p, axis=-1, keepdims=True)
            o = jnp.dot(p.astype(BF16), v[keys], preferred_element_type=F32) / l
            o_ref[0, rows, :] = o.astype(o_ref.dtype)


def attention_b(q, k, v, rel_bias):
    b, t, _ = q.shape
    tq = _block(t, BAND_GROUP)
    assert tq % LANES == 0
    g = _toeplitz_rows(rel_bias, tq)
    head = pl.BlockSpec((1, t, DH_B), lambda bi, hi: (bi, 0, hi))
    return pl.pallas_call(
        functools.partial(_attn_b_kernel, tq=tq),
        grid=(b, H_B),
        in_specs=[head, head, head, pl.BlockSpec((1, 1, g.shape[2]), lambda bi, hi: (hi, 0, 0))],
        out_specs=head,
        out_shape=jax.ShapeDtypeStruct((b, t, H_B * DH_B), BF16),
        scratch_shapes=[pltpu.VMEM((t, 2 * DH_B), BF16)],
        name="attention_b",
        compiler_params=_params("parallel", "arbitrary"),
    )(q, k, v, g)


def _attn_b_decode_kernel(q_ref, kn_ref, vn_ref, ck_ref, cv_ref, g_ref, o_ref):
    t = q_ref.shape[1]
    scale = DH_B ** -0.5
    for h in range(H_B):
        cols = slice(h * DH_B, (h + 1) * DH_B)
        q = q_ref[0, :, cols]
        kc = ck_ref[0, pl.ds(h, BAND_PAST, stride=H_B), :].astype(BF16)
        vc = cv_ref[0, pl.ds(h, BAND_PAST, stride=H_B), :].astype(BF16)
        kn = kn_ref[0, :, cols].astype(BF16)
        vn = vn_ref[0, :, cols].astype(BF16)
        bias_mask = _band_bias_mask(g_ref[h], t)
        s_old = lax.dot_general(q, kc, CONTRACT_LAST, preferred_element_type=F32) * scale
        s_old = s_old + bias_mask[:, :BAND_PAST]
        s_new = lax.dot_general(q, kn, CONTRACT_LAST, preferred_element_type=F32) * scale
        s_new = s_new + bias_mask[:, BAND_PAST:BAND_PAST + t]
        m = jnp.maximum(jnp.max(s_new, axis=-1, keepdims=True), jnp.max(s_old, axis=-1, keepdims=True))
        p_new = jnp.exp(s_new - m)
        p_old = jnp.exp(s_old - m)
        l = jnp.sum(p_new, axis=-1, keepdims=True) + jnp.sum(p_old, axis=-1, keepdims=True)
        pv = (jnp.dot(p_new.astype(BF16), vn, preferred_element_type=F32)
              + jnp.dot(p_old.astype(BF16), vc, preferred_element_type=F32))
        o_ref[0, :, cols] = (pv / l).astype(o_ref.dtype)


def attention_b_decode(q, k_new, v_new, cache_k, cache_v, rel_bias, p_len):
    b, t, w = q.shape
    assert cache_k.shape[1] == BAND_PAST and p_len % CHUNK == 0 and p_len >= BAND_PAST and t <= CHUNK
    g = _toeplitz_rows(rel_bias, t)
    rows = lambda a: a.reshape(b, BAND_PAST * H_B, a.shape[-1])
    new = pl.BlockSpec((1, t, w), lambda bi: (bi, 0, 0))
    old = pl.BlockSpec((1, BAND_PAST * H_B, DH_B), lambda bi: (bi, 0, 0))
    return pl.pallas_call(
        _attn_b_decode_kernel,
        grid=(b,),
        in_specs=[new, new, new, old, old, _const_spec(g.shape)],
        out_specs=new,
        out_shape=jax.ShapeDtypeStruct((b, t, w), BF16),
        name="attention_b_decode",
        compiler_params=_params("parallel"),
    )(q, k_new, v_new, rows(cache_k), rows(cache_v), g)


def _attn_c_kernel(q_ref, k_ref, v_ref, o_ref, *, tq):
    t = q_ref.shape[1]
    scale = DH_C ** -0.5
    k = k_ref[0].astype(BF16)
    v = v_ref[0].astype(BF16)
    for i in range(t // tq):
        q = q_ref[0, i * tq:(i + 1) * tq, :]
        s = lax.dot_general(q, k, CONTRACT_LAST, preferred_element_type=F32) * scale
        m = jnp.max(s, axis=-1, keepdims=True)
        p = jnp.exp(s - m)
        l = jnp.sum(p, axis=-1, keepdims=True)
        o = jnp.dot(p.astype(BF16), v, preferred_element_type=F32) / l
        o_ref[0, i * tq:(i + 1) * tq, :] = o.astype(o_ref.dtype)


def attention_c(q, mk, mv):
    b, t, _ = q.shape
    n_mem = mk.shape[1]
    tq = _block(t, 512)
    return pl.pallas_call(
        functools.partial(_attn_c_kernel, tq=tq),
        grid=(b, H_C),
        in_specs=[pl.BlockSpec((1, t, DH_C), lambda bi, hi: (bi, 0, hi)),
                  pl.BlockSpec((1, n_mem, DH_C), lambda bi, hi: (bi, 0, hi)),
                  pl.BlockSpec((1, n_mem, DH_C), lambda bi, hi: (bi, 0, hi))],
        out_specs=pl.BlockSpec((1, t, DH_C), lambda bi, hi: (bi, 0, hi)),
        out_shape=jax.ShapeDtypeStruct((b, t, H_C * DH_C), BF16),
        name="attention_c",
        compiler_params=_params("parallel", "arbitrary"),
    )(q, mk, mv)


def _merge_kernel(oa_ref, ob_ref, oc_ref, ga_ref, gb_ref, gc_ref, ba_ref, bb_ref, bc_ref,
                  wa_ref, wb_ref, wc_ref, o_ref):
    def branch(o_r, g_r, b_r, w_r):
        gate = jax.nn.sigmoid(g_r[...].astype(F32) + b_r[...])
        return gate * jnp.dot(o_r[...], w_r[...], preferred_element_type=F32)

    merged = (branch(oa_ref, ga_ref, ba_ref, wa_ref) + branch(ob_ref, gb_ref, bb_ref, wb_ref)
              + branch(oc_ref, gc_ref, bc_ref, wc_ref))
    o_ref[...] = merged.astype(o_ref.dtype)


def merge_branches(oa, ob, oc, gate_logits, b_gate, wa, wb, wc):
    m, w_in = oa.shape
    d = wa.shape[1]
    bm = _block(m, MERGE_ROWS)
    row = lambda width: pl.BlockSpec((bm, width), lambda i: (i, 0))
    gate = lambda j: pl.BlockSpec((bm, d), lambda i: (i, j))
    bias = lambda j: pl.BlockSpec((1, d), lambda i: (0, j))
    weight = pl.BlockSpec((w_in, d), lambda i: (0, 0), pipeline_mode=pl.Buffered(1))
    bg = b_gate.reshape(1, N_BRANCH * d).astype(F32)
    return pl.pallas_call(
        _merge_kernel,
        grid=(m // bm,),
        in_specs=[row(w_in), row(w_in), row(w_in), gate(0), gate(1), gate(2),
                  bias(0), bias(1), bias(2), weight, weight, weight],
        out_specs=pl.BlockSpec((bm, d), lambda i: (i, 0)),
        out_shape=jax.ShapeDtypeStruct((m, d), BF16),
        name="merge_branches",
        compiler_params=_params("parallel"),
    )(oa, ob, oc, gate_logits, gate_logits, gate_logits, bg, bg, bg, wa, wb, wc)


def _proj_norm_res_kernel(*refs, next_norm, rows):
    if next_norm:
        a_ref, w_ref, x_ref, g_ref, g2_ref, o_ref, h_ref = refs
    else:
        a_ref, w_ref, x_ref, g_ref, o_ref = refs
    bm = a_ref.shape[0]
    for r0 in range(0, bm, rows):
        sl = slice(r0, r0 + rows)
        y = jnp.dot(a_ref[sl, :], w_ref[...], preferred_element_type=F32)
        ms = jnp.mean(y * y, axis=-1, keepdims=True)
        o = x_ref[sl, :] + y * lax.rsqrt(ms + EPS) * g_ref[...]
        o_ref[sl, :] = o
        if next_norm:
            ms2 = jnp.mean(o * o, axis=-1, keepdims=True)
            h_ref[sl, :] = (o * lax.rsqrt(ms2 + EPS) * g2_ref[...]).astype(h_ref.dtype)


def proj_norm_residual(a, w, x, g, bm, rows, next_g=None):
    m, k = a.shape
    d = w.shape[1]
    bm = _block(m, bm)
    rows = _block(bm, rows)
    vec = pl.BlockSpec((1, d), lambda i: (0, 0))
    row = pl.BlockSpec((bm, d), lambda i: (i, 0))
    in_specs = [pl.BlockSpec((bm, k), lambda i: (i, 0)),
                pl.BlockSpec((k, d), lambda i: (0, 0), pipeline_mode=pl.Buffered(1)), row, vec]
    args = [a, w, x, g.reshape(1, d).astype(F32)]
    out_specs, out_shape = row, jax.ShapeDtypeStruct((m, d), F32)
    if next_g is not None:
        in_specs.append(vec)
        args.append(next_g.reshape(1, d).astype(F32))
        out_specs = (row, row)
        out_shape = (out_shape, jax.ShapeDtypeStruct((m, d), BF16))
    return pl.pallas_call(
        functools.partial(_proj_norm_res_kernel, next_norm=next_g is not None, rows=rows),
        grid=(m // bm,),
        in_specs=in_specs,
        out_specs=out_specs,
        out_shape=out_shape,
        name="proj_norm_residual",
        compiler_params=_params("parallel"),
    )(*args)


def _ffn_in_kernel(h_ref, h2_ref, wa_ref, wb_ref, o_ref, o2_ref, wa_bf, wb_bf):
    def swiglu(h):
        a = jnp.dot(h, wa_bf[...], preferred_element_type=F32)
        b = jnp.dot(h, wb_bf[...], preferred_element_type=F32)
        return (jax.nn.silu(a) * b).astype(o_ref.dtype)

    @pl.when(pl.program_id(1) == 0)
    def _():
        wa_bf[...] = wa_ref[...].astype(BF16)
        wb_bf[...] = wb_ref[...].astype(BF16)
        o2_ref[...] = swiglu(h2_ref[...])

    o_ref[...] = swiglu(h_ref[...])


def ffn_in(h, extra, w):
    m, k = h.shape
    m2 = extra.shape[0]
    f = w.shape[1] // 2
    bm = _block(m, 1024)
    bf = 512
    assert f % bf == 0
    nf = f // bf
    return pl.pallas_call(
        _ffn_in_kernel,
        grid=(nf, m // bm),
        in_specs=[pl.BlockSpec((bm, k), lambda j, i: (i, 0)),
                  pl.BlockSpec((m2, k), lambda j, i: (0, 0)),
                  pl.BlockSpec((k, bf), lambda j, i: (0, j)),
                  pl.BlockSpec((k, bf), lambda j, i: (0, j + nf))],
        out_specs=[pl.BlockSpec((bm, bf), lambda j, i: (i, j)),
                   pl.BlockSpec((m2, bf), lambda j, i: (0, j))],
        out_shape=[jax.ShapeDtypeStruct((m, f), BF16), jax.ShapeDtypeStruct((m2, f), BF16)],
        scratch_shapes=[pltpu.VMEM((k, bf), BF16), pltpu.VMEM((k, bf), BF16)],
        name="ffn_in",
        compiler_params=_params("arbitrary", "arbitrary"),
    )(h, extra, w, w)


def _layer(xp, xd, mk_p, mv_p, mk_d, mv_d, caches, lam_init, norm_mix_pre, norm_mix_post, w_in, b_gate,
           lq1, lk1, lq2, lk2, subln_a, rel_bias, w_br_a, w_br_b, w_br_c, w_out,
           norm_ffn_pre, norm_ffn_post, w_ffn_in, w_ffn_out):
    d = xp.shape[-1]
    w_head = H_A * DK_A
    shapes = [xp.shape[:2], xd.shape[:2]]
    xs = [xp.reshape(-1, d), xd.reshape(-1, d)]
    hs = [rms_norm_bf16(x, norm_mix_pre) for x in xs]

    def proj(idx, dt):
        outs = matmul_cols(hs[0], w_in, idx * w_head, w_head, dt, extra=hs[1])
        return [o.reshape(*shp, w_head) for o, shp in zip(outs, shapes)]

    q_a, k_a, v_a = proj(0, BF16), proj(1, F32), proj(2, F32)
    q_b, k_b, v_b = proj(3, BF16), proj(4, F32), proj(5, F32)
    q_c = proj(6, BF16)
    gates = matmul_cols(hs[0], w_in, 7 * w_head, N_BRANCH * d, BF16, extra=hs[1])

    ca_k, ca_v, cb_k, cb_v = caches
    o_a = [attention_a(q_a[0], k_a[0], v_a[0], lq1, lk1, lq2, lk2, subln_a, lam_init),
           attention_a_decode(q_a[1], k_a[1], v_a[1], ca_k, ca_v, lq1, lk1, lq2, lk2, subln_a, lam_init)]
    o_b = [attention_b(q_b[0], k_b[0], v_b[0], rel_bias),
           attention_b_decode(q_b[1], k_b[1], v_b[1], cb_k, cb_v, rel_bias, ca_k.shape[1])]
    o_c = [attention_c(q_c[0], mk_p, mv_p), attention_c(q_c[1], mk_d, mv_d)]

    x1, h2 = [], []
    for i in range(2):
        m = xs[i].shape[0]
        merged = merge_branches(o_a[i].reshape(m, -1), o_b[i].reshape(m, -1), o_c[i].reshape(m, -1),
                                gates[i], b_gate, w_br_a, w_br_b, w_br_c)
        a, b = proj_norm_residual(merged, w_out, xs[i], norm_mix_post, bm=512, rows=PROJ_ROWS,
                                  next_g=norm_ffn_pre)
        x1.append(a)
        h2.append(b)
    acts = ffn_in(h2[0], h2[1], w_ffn_in)
    ys = [proj_norm_residual(acts[i], w_ffn_out, x1[i], norm_ffn_post, bm=256, rows=256).reshape(*shapes[i], d)
          for i in range(2)]
    return [(ys[i], k_a[i], v_a[i], k_b[i], v_b[i]) for i in range(2)]


def kernel(x_prompt, x_sample, cache_a_k, cache_a_v, cache_b_k, cache_b_v, cache_mem_k, cache_mem_v, mem_prompt, norm_mix_pre, norm_mix_post, norm_mem, w_in, b_gate, lambda_q1, lambda_k1, lambda_q2, lambda_k2, subln_a, rel_bias_b, w_mem_kv, w_br_a, w_br_b, w_br_c, w_out, norm_ffn_pre, norm_ffn_post, w_ffn_in, w_ffn_out):
    depth = w_in.shape[0]
    bsz, s, d = x_prompt.shape
    n_mem = mem_prompt.shape[1]
    lb_prompt = min(BAND_PAST, s)
    yp, ys = x_prompt, x_sample
    outs = [[] for _ in range(10)]
    heads = lambda a, nh: a.reshape(a.shape[0], a.shape[1], nh, a.shape[2] // nh)
    flat = lambda a: a.reshape(a.shape[0], a.shape[1], -1)
    for l in range(depth):
        lam_init = 0.8 - 0.6 * math.exp(-0.3 * l)
        bf = lambda w: w[l].astype(BF16)
        shared = (lam_init, norm_mix_pre[l], norm_mix_post[l], w_in[l], b_gate[l],
                  lambda_q1[l], lambda_k1[l], lambda_q2[l], lambda_k2[l], subln_a[l], rel_bias_b[l],
                  bf(w_br_a), bf(w_br_b), bf(w_br_c), bf(w_out),
                  norm_ffn_pre[l], norm_ffn_post[l], w_ffn_in[l], bf(w_ffn_out))

        mem_n = rms_norm_bf16(mem_prompt.reshape(bsz * n_mem, d), norm_mem[l])
        w_c = H_C * DH_C
        mk_p = matmul_cols(mem_n, w_mem_kv[l], 0, w_c, F32).reshape(bsz, n_mem, w_c)
        mv_p = matmul_cols(mem_n, w_mem_kv[l], w_c, w_c, F32).reshape(bsz, n_mem, w_c)
        caches = (cache_a_k[l], cache_a_v[l], cache_b_k[l], cache_b_v[l])
        (yp, ka, va, kb, vb), (ys, ka_d, va_d, kb_d, vb_d) = _layer(
            yp, ys, mk_p, mv_p, flat(cache_mem_k[l]), flat(cache_mem_v[l]), caches, *shared)
        new = [heads(ka, H_A), heads(va, H_A), heads(kb[:, s - lb_prompt:], H_B),
               heads(vb[:, s - lb_prompt:], H_B), heads(mk_p, H_C), heads(mv_p, H_C),
               heads(ka_d, H_A), heads(va_d, H_A), heads(kb_d, H_B), heads(vb_d, H_B)]
        for o, a in zip(outs, new):
            o.append(a)
    return (yp, ys) + tuple(jnp.stack(o) for o in outs)
```

```python
import functools
import math

import jax
import jax.numpy as jnp
from jax import lax
from jax.experimental import pallas as pl
from jax.experimental.pallas import tpu as pltpu

F32 = jnp.float32
BF16 = jnp.bfloat16

CHUNK = 64
H_A = 8
DH_A = 64
DK_A = 2 * DH_A
DV_A = 128
H_B = 8
DH_B = 128
N_PREV_CHUNKS = 8
BAND_PAST = N_PREV_CHUNKS * CHUNK
MAX_REL = 128
H_C = 4
DH_C = 256
N_BRANCH = 3
EPS = 1e-6
NEG_INF = -1e30

LANES = 128
VMEM_LIMIT = 56 * 1024 * 1024
BAND_GROUP = 256
A_BLOCK = 256
MM_ROWS = 2048
MERGE_ROWS = 256
PROJ_ROWS = 128
SAFE_LOGIT = 40.0

CONTRACT_LAST = (((1,), (1,)), ((), ()))


def _params(*sem):
    return pltpu.CompilerParams(dimension_semantics=sem, vmem_limit_bytes=VMEM_LIMIT)


def _block(n, target):
    if n <= target:
        return n
    b = target
    while n % b:
        b //= 2
    return b


def _rms_kernel(x_ref, g_ref, o_ref):
    x = x_ref[...]
    ms = jnp.mean(x * x, axis=-1, keepdims=True)
    o_ref[...] = (x * lax.rsqrt(ms + EPS) * g_ref[...]).astype(o_ref.dtype)


def rms_norm_bf16(x, g):
    m, d = x.shape
    bm = _block(m, 512)
    return pl.pallas_call(
        _rms_kernel,
        grid=(m // bm,),
        in_specs=[pl.BlockSpec((bm, d), lambda i: (i, 0)),
                  pl.BlockSpec((1, d), lambda i: (0, 0))],
        out_specs=pl.BlockSpec((bm, d), lambda i: (i, 0)),
        out_shape=jax.ShapeDtypeStruct((m, d), BF16),
        name="rms_norm",
        compiler_params=_params("parallel"),
    )(x, g.reshape(1, d))


def _mm_kernel(*refs, has_extra):
    refs = list(refs)
    a_ref = refs.pop(0)
    a2_ref = refs.pop(0) if has_extra else None
    w_ref = refs.pop(0)
    o_ref = refs.pop(0)
    o2_ref = refs.pop(0) if has_extra else None
    wb_ref, = refs

    def product(x_ref, y_ref):
        rows = _block(x_ref.shape[0], 1024)
        for r0 in range(0, x_ref.shape[0], rows):
            acc = jnp.dot(x_ref[r0:r0 + rows, :], wb_ref[...], preferred_element_type=F32)
            y_ref[r0:r0 + rows, :] = acc.astype(y_ref.dtype)

    @pl.when(pl.program_id(1) == 0)
    def _():
        wb_ref[...] = w_ref[...].astype(BF16)
        if has_extra:
            product(a2_ref, o2_ref)

    product(a_ref, o_ref)


def matmul_cols(a, w, col0, n, out_dtype, extra=None):
    m, k = a.shape
    bm = _block(m, MM_ROWS * 2 // jnp.dtype(out_dtype).itemsize)
    bn = _block(n, 1024)
    assert col0 % bn == 0
    off = col0 // bn
    in_specs = [pl.BlockSpec((bm, k), lambda j, i: (i, 0))]
    args = [a]
    out_specs = [pl.BlockSpec((bm, bn), lambda j, i: (i, j))]
    out_shape = [jax.ShapeDtypeStruct((m, n), out_dtype)]
    if extra is not None:
        m2 = extra.shape[0]
        in_specs.append(pl.BlockSpec((m2, k), lambda j, i: (0, 0)))
        args.append(extra)
        out_specs.append(pl.BlockSpec((m2, bn), lambda j, i: (0, j)))
        out_shape.append(jax.ShapeDtypeStruct((m2, n), out_dtype))
    in_specs.append(pl.BlockSpec((k, bn), lambda j, i: (0, j + off)))
    args.append(w)
    out = pl.pallas_call(
        functools.partial(_mm_kernel, has_extra=extra is not None),
        grid=(n // bn, m // bm),
        in_specs=in_specs,
        out_specs=out_specs,
        out_shape=out_shape,
        scratch_shapes=[pltpu.VMEM((k, bn), BF16)],
        name="matmul_cols",
        compiler_params=_params("arbitrary", "arbitrary"),
    )(*args)
    return out if extra is not None else out[0]


def _norm_mm_kernel(x_ref, g_ref, a2_ref, w_ref, h_ref, o_ref, o2_ref, wb_ref, *, rows):
    @pl.when(pl.program_id(0) == 0)
    def _():
        wb_ref[...] = w_ref[...].astype(BF16)
        o2_ref[...] = jnp.dot(a2_ref[...], wb_ref[...], preferred_element_type=F32).astype(o2_ref.dtype)

    for r0 in range(0, x_ref.shape[0], rows):
        x = x_ref[r0:r0 + rows, :]
        ms = jnp.mean(x * x, axis=-1, keepdims=True)
        h = (x * lax.rsqrt(ms + EPS) * g_ref[...]).astype(BF16)
        h_ref[r0:r0 + rows, :] = h
        o_ref[r0:r0 + rows, :] = jnp.dot(h, wb_ref[...], preferred_element_type=F32).astype(o_ref.dtype)


def norm_matmul_cols(x, g, w, col0, n, out_dtype, extra):
    m, k = x.shape
    m2 = extra.shape[0]
    bm = _block(m, 512)
    assert col0 % n == 0
    off = col0 // n
    return pl.pallas_call(
        functools.partial(_norm_mm_kernel, rows=_block(bm, 128)),
        grid=(m // bm,),
        in_specs=[pl.BlockSpec((bm, k), lambda i: (i, 0)),
                  pl.BlockSpec((1, k), lambda i: (0, 0)),
                  pl.BlockSpec((m2, k), lambda i: (0, 0)),
                  pl.BlockSpec((k, n), lambda i: (0, off))],
        out_specs=[pl.BlockSpec((bm, k), lambda i: (i, 0)),
                   pl.BlockSpec((bm, n), lambda i: (i, 0)),
                   pl.BlockSpec((m2, n), lambda i: (0, 0))],
        out_shape=[jax.ShapeDtypeStruct((m, k), BF16), jax.ShapeDtypeStruct((m, n), out_dtype),
                   jax.ShapeDtypeStruct((m2, n), out_dtype)],
        scratch_shapes=[pltpu.VMEM((k, n), BF16)],
        name="norm_matmul_cols",
        compiler_params=_params("arbitrary"),
    )(x, g.reshape(1, k).astype(F32), extra, w)


def _diff_lambda(lq1_ref, lk1_ref, lq2_ref, lk2_ref, lam_init):
    return (jnp.exp(jnp.sum(lq1_ref[...] * lk1_ref[...], keepdims=True))
            - jnp.exp(jnp.sum(lq2_ref[...] * lk2_ref[...], keepdims=True)) + lam_init)


def _head_norm(o, sub_ref, lam_init):
    ms = jnp.mean(o * o, axis=-1, keepdims=True)
    return o * lax.rsqrt(ms + EPS) * sub_ref[...] * (1.0 - lam_init)


def _own_alibi(n, slope):
    r = lax.broadcasted_iota(jnp.int32, (n, n), 0)
    c = lax.broadcasted_iota(jnp.int32, (n, n), 1)
    bias = slope * (r - jnp.abs(r - c)).astype(F32)
    return jnp.where(c // CHUNK <= r // CHUNK, bias, NEG_INF)


def _max_sq_norms(x, lane_masks):
    sq = jnp.square(x.astype(F32))
    return [jnp.max(jnp.sum(jnp.where(mask, sq, 0.0), axis=-1, keepdims=True)) for mask in lane_masks]


def _attn_a_kernel(q_ref, k_ref, v_ref, slope_ref, kf_ref, qf_ref, lq1_ref, lk1_ref, lq2_ref, lk2_ref,
                   sub_ref, o_ref, ke_scr, ve_scr, *, tq, lam_init):
    args = (q_ref, k_ref, v_ref, slope_ref, lq1_ref, lk1_ref, lq2_ref, lk2_ref, sub_ref, o_ref)
    _attn_a_bounded(*args, kf_ref, qf_ref, ke_scr, ve_scr, tq=tq, lam_init=lam_init)
    first_half = lax.broadcasted_iota(jnp.int32, (1, DK_A), 1) < DH_A
    q_sq = _max_sq_norms(q_ref[0] * (DH_A ** -0.5), (first_half, ~first_half))
    k_sq = _max_sq_norms(k_ref[0].astype(BF16), (first_half, ~first_half))
    bound_sq = jnp.maximum(q_sq[0] * k_sq[0], q_sq[1] * k_sq[1])

    @pl.when(bound_sq > SAFE_LOGIT ** 2 / 1.01)
    def _():
        _attn_a_general(*args, tq=tq, lam_init=lam_init)


def _position_features(t):
    pos = jnp.arange(t, dtype=jnp.int32)[:, None]
    lo = pos & 7
    hi = (pos - lo).astype(F32)
    lo = lo.astype(F32)
    one = jnp.ones((t, 1), F32)
    pad = jnp.zeros((t, LANES - 4), F32)
    k_side = jnp.concatenate([hi, lo, one, one, pad], axis=1).astype(BF16)
    q_side = jnp.concatenate([one, one, -hi, -lo, pad], axis=1)
    return k_side, q_side


def _attn_a_bounded(q_ref, k_ref, v_ref, slope_ref, lq1_ref, lk1_ref, lq2_ref, lk2_ref, sub_ref, o_ref,
                    kf_ref, qf_ref, ke_scr, ve_scr, *, tq, lam_init):
    t = q_ref.shape[1]
    slope = slope_ref[0][:, :1]
    lam = _diff_lambda(lq1_ref, lk1_ref, lq2_ref, lk2_ref, lam_init)
    first_half = lax.broadcasted_iota(jnp.int32, (1, DK_A), 1) < DH_A
    ke_scr[:, :DK_A] = k_ref[0].astype(BF16)
    ke_scr[:, DK_A:] = kf_ref[...]
    ve_scr[:, :DV_A] = v_ref[0].astype(BF16)
    ve_scr[:, DV_A:] = jnp.ones((t, DV_A), BF16)

    r = lax.broadcasted_iota(jnp.int32, (tq, tq), 0)
    c = lax.broadcasted_iota(jnp.int32, (tq, tq), 1)
    own_fix = jnp.where(c // CHUNK <= r // CHUNK, -2.0 * slope * jnp.maximum(c - r, 0).astype(F32), NEG_INF)
    own_fix = jnp.concatenate([own_fix, own_fix], axis=0)

    for i in range(t // tq):
        nb = i * tq
        q = q_ref[0, nb:nb + tq, :] * (DH_A ** -0.5)
        q_feat = (qf_ref[nb:nb + tq, :] * slope).astype(BF16)
        zero = jnp.zeros_like(q)
        qe = jnp.concatenate([jnp.concatenate([jnp.where(first_half, q, zero), q_feat], axis=1),
                              jnp.concatenate([jnp.where(first_half, zero, q), q_feat], axis=1)], axis=0)
        s_own = lax.dot_general(qe, ke_scr[nb:nb + tq, :], CONTRACT_LAST, preferred_element_type=F32)
        acc = jnp.dot(jnp.exp(s_own + own_fix).astype(BF16), ve_scr[nb:nb + tq, :],
                      preferred_element_type=F32)
        if nb:
            s_bef = lax.dot_general(qe, ke_scr[:nb, :], CONTRACT_LAST, preferred_element_type=F32)
            acc = acc + jnp.dot(jnp.exp(s_bef).astype(BF16), ve_scr[:nb, :], preferred_element_type=F32)
        o = acc[:, :DV_A] / acc[:, DV_A:]
        o = o[:tq] - lam * o[tq:]
        o_ref[0, nb:nb + tq, :] = _head_norm(o, sub_ref, lam_init).astype(o_ref.dtype)


def _attn_a_general(q_ref, k_ref, v_ref, slope_ref, lq1_ref, lk1_ref, lq2_ref, lk2_ref, sub_ref, o_ref,
                    *, tq, lam_init):
    t = q_ref.shape[1]
    n_blocks = t // tq
    slope = slope_ref[0][:, :1]
    lam = _diff_lambda(lq1_ref, lk1_ref, lq2_ref, lk2_ref, lam_init)
    k = k_ref[0].astype(BF16)
    v = v_ref[0].astype(BF16)
    first_half = lax.broadcasted_iota(jnp.int32, (1, DK_A), 1) < DH_A
    own_bias = _own_alibi(tq, slope)
    n_before_max = (n_blocks - 1) * tq
    if n_before_max:
        j = lax.broadcasted_iota(jnp.int32, (1, n_before_max), 1)
        before_bias = slope * (j - n_before_max).astype(F32)

    for i in range(n_blocks):
        nb = i * tq
        q = q_ref[0, nb:nb + tq, :] * (DH_A ** -0.5)
        k_own, v_own = k[nb:nb + tq], v[nb:nb + tq]

        def softmax_pv(qh):
            s_own = lax.dot_general(qh, k_own, CONTRACT_LAST, preferred_element_type=F32) + own_bias
            m = jnp.max(s_own, axis=-1, keepdims=True)
            if nb:
                s_bef = lax.dot_general(qh, k[:nb], CONTRACT_LAST, preferred_element_type=F32)
                s_bef = s_bef + before_bias[:, n_before_max - nb:]
                m = jnp.maximum(m, jnp.max(s_bef, axis=-1, keepdims=True))
            p_own = jnp.exp(s_own - m)
            l = jnp.sum(p_own, axis=-1, keepdims=True)
            pv = jnp.dot(p_own.astype(BF16), v_own, preferred_element_type=F32)
            if nb:
                p_bef = jnp.exp(s_bef - m)
                l = l + jnp.sum(p_bef, axis=-1, keepdims=True)
                pv = pv + jnp.dot(p_bef.astype(BF16), v[:nb], preferred_element_type=F32)
            return pv / l

        o = (softmax_pv(jnp.where(first_half, q, jnp.zeros_like(q)))
             - lam * softmax_pv(jnp.where(first_half, jnp.zeros_like(q), q)))
        o_ref[0, nb:nb + tq, :] = _head_norm(o, sub_ref, lam_init).astype(o_ref.dtype)


def _row(a):
    return a.reshape(1, -1).astype(F32)


def _const_spec(shape):
    return pl.BlockSpec(shape, lambda *_: (0,) * len(shape))


def attention_a(q, k, v, lq1, lk1, lq2, lk2, subln, lam_init):
    b, t, _ = q.shape
    tq = _block(t, A_BLOCK)
    assert tq % CHUNK == 0
    slopes = jnp.asarray([[[2.0 ** (-8.0 * (hh + 1) / H_A)] * LANES] for hh in range(H_A)], dtype=F32)
    head = lambda width: pl.BlockSpec((1, t, width), lambda bi, hi: (bi, 0, hi))
    k_feat, q_feat = _position_features(t)
    return pl.pallas_call(
        functools.partial(_attn_a_kernel, tq=tq, lam_init=lam_init),
        grid=(b, H_A),
        in_specs=[head(DK_A), head(DK_A), head(DV_A),
                  pl.BlockSpec((1, 1, LANES), lambda bi, hi: (hi, 0, 0)),
                  _const_spec((t, LANES)), _const_spec((t, LANES)),
                  _const_spec((1, DH_A)), _const_spec((1, DH_A)), _const_spec((1, DH_A)),
                  _const_spec((1, DH_A)), _const_spec((1, DV_A))],
        out_specs=head(DV_A),
        out_shape=jax.ShapeDtypeStruct((b, t, H_A * DV_A), BF16),
        scratch_shapes=[pltpu.VMEM((t, DK_A + LANES), BF16), pltpu.VMEM((t, 2 * DV_A), BF16)],
        name="attention_a",
        compiler_params=_params("parallel", "arbitrary"),
    )(q, k, v, slopes, k_feat, q_feat, _row(lq1), _row(lk1), _row(lq2), _row(lk2), _row(subln))


def _attn_a_decode_kernel(q_ref, kn_ref, vn_ref, ck_ref, cv_ref, lq1_ref, lk1_ref, lq2_ref, lk2_ref,
                          sub_ref, o_ref, *, lam_init):
    t = q_ref.shape[1]
    p_len = ck_ref.shape[1] // H_A
    lam = _diff_lambda(lq1_ref, lk1_ref, lq2_ref, lk2_ref, lam_init)
    first_half = lax.broadcasted_iota(jnp.int32, (1, DK_A), 1) < DH_A
    j = lax.broadcasted_iota(jnp.int32, (1, p_len), 1)
    before_dist = (j - p_len).astype(F32)
    for h in range(H_A):
        slope = 2.0 ** (-8.0 * (h + 1) / H_A)
        cols = slice(h * DK_A, (h + 1) * DK_A)
        q = q_ref[0, :, cols] * (DH_A ** -0.5)
        kc = ck_ref[0, pl.ds(h, p_len, stride=H_A), :].astype(BF16)
        vc = cv_ref[0, pl.ds(h, p_len, stride=H_A), :].astype(BF16)
        kn = kn_ref[0, :, cols].astype(BF16)
        vn = vn_ref[0, :, cols].astype(BF16)
        own_bias = _own_alibi(t, slope)
        before_bias = slope * before_dist

        def softmax_pv(qh):
            s_new = lax.dot_general(qh, kn, CONTRACT_LAST, preferred_element_type=F32) + own_bias
            s_old = lax.dot_general(qh, kc, CONTRACT_LAST, preferred_element_type=F32) + before_bias
            m = jnp.maximum(jnp.max(s_new, axis=-1, keepdims=True), jnp.max(s_old, axis=-1, keepdims=True))
            p_new = jnp.exp(s_new - m)
            p_old = jnp.exp(s_old - m)
            l = jnp.sum(p_new, axis=-1, keepdims=True) + jnp.sum(p_old, axis=-1, keepdims=True)
            pv = (jnp.dot(p_new.astype(BF16), vn, preferred_element_type=F32)
                  + jnp.dot(p_old.astype(BF16), vc, preferred_element_type=F32))
            return pv / l

        o = (softmax_pv(jnp.where(first_half, q, jnp.zeros_like(q)))
             - lam * softmax_pv(jnp.where(first_half, jnp.zeros_like(q), q)))
        o_ref[0, :, cols] = _head_norm(o, sub_ref, lam_init).astype(o_ref.dtype)


def attention_a_decode(q, k_new, v_new, cache_k, cache_v, lq1, lk1, lq2, lk2, subln, lam_init):
    b, t, w = q.shape
    p_len = cache_k.shape[1]
    assert p_len % CHUNK == 0 and t <= CHUNK
    rows = lambda a: a.reshape(b, p_len * H_A, a.shape[-1])
    new = pl.BlockSpec((1, t, w), lambda bi: (bi, 0, 0))
    old = pl.BlockSpec((1, p_len * H_A, DK_A), lambda bi: (bi, 0, 0))
    return pl.pallas_call(
        functools.partial(_attn_a_decode_kernel, lam_init=lam_init),
        grid=(b,),
        in_specs=[new, new, new, old, old,
                  _const_spec((1, DH_A)), _const_spec((1, DH_A)), _const_spec((1, DH_A)),
                  _const_spec((1, DH_A)), _const_spec((1, DV_A))],
        out_specs=new,
        out_shape=jax.ShapeDtypeStruct((b, t, w), BF16),
        name="attention_a_decode",
        compiler_params=_params("parallel"),
    )(q, k_new, v_new, rows(cache_k), rows(cache_v), _row(lq1), _row(lk1), _row(lq2), _row(lk2), _row(subln))


def _band_tile(tq):
    pad = -(-tq // LANES) * LANES
    return BAND_PAST + pad, pad


def _band_bias_mask(g_row, tq):
    tile_w, pad = _band_tile(tq)
    width = g_row.shape[1]
    assert width == tile_w + pad
    rolled = pltpu.roll(jnp.broadcast_to(g_row, (tq, width)), width - pad, 1, stride=1, stride_axis=0)
    r = lax.broadcasted_iota(jnp.int32, (tq, tile_w), 0)
    c = lax.broadcasted_iota(jnp.int32, (tq, tile_w), 1)
    qch = r // CHUNK
    kch = c // CHUNK - N_PREV_CHUNKS
    valid = (kch <= qch) & (kch >= qch - N_PREV_CHUNKS)
    return jnp.where(valid, rolled[:, :tile_w], NEG_INF)


def _toeplitz_rows(rel_bias, tq):
    tile_w, pad = _band_tile(tq)
    width = tile_w + pad
    n_lo = pad + BAND_PAST - MAX_REL
    n_hi = max(width - n_lo - (2 * MAX_REL + 1), 0)
    lo = jnp.broadcast_to(rel_bias[:, :1], (H_B, n_lo))
    hi = jnp.broadcast_to(rel_bias[:, -1:], (H_B, n_hi))
    g = jnp.concatenate([lo, rel_bias, hi], axis=1)[:, :width]
    return g.reshape(H_B, 1, width).astype(F32)


def _attn_b_kernel(q_ref, k_ref, v_ref, g_ref, o_ref, ve_scr, *, tq):
    t = q_ref.shape[1]
    tile_w, _ = _band_tile(tq)
    scale = DH_B ** -0.5
    bias_mask = _band_bias_mask(g_ref[0], tq)

    def groups():
        for gi in range(t // tq):
            qa = gi * tq
            lo = max(qa - BAND_PAST, 0)
            hi = qa + tq
            off = lo - qa + BAND_PAST
            assert off % LANES == 0 and off + hi - lo == tile_w
            yield slice(qa, hi), slice(lo, hi), off

    log2e = math.log2(math.e)
    k = k_ref[0].astype(BF16)
    ve_scr[:, :DH_B] = v_ref[0].astype(BF16)
    ve_scr[:, DH_B:] = jnp.ones((t, DH_B), BF16)
    bias2 = bias_mask * log2e
    for rows, keys, off in groups():
        s = lax.dot_general(q_ref[0, rows, :], k[keys], CONTRACT_LAST, preferred_element_type=F32)
        p = jnp.exp2(s * (scale * log2e) + bias2[:, off:])
        acc = jnp.dot(p.astype(BF16), ve_scr[keys, :], preferred_element_type=F32)
        o_ref[0, rows, :] = (acc[:, :DH_B] / acc[:, DH_B:]).astype(o_ref.dtype)

    (q_sq,), (k_sq,) = _max_sq_norms(q_ref[0], (True,)), _max_sq_norms(k, (True,))
    room = SAFE_LOGIT - jnp.max(jnp.abs(g_ref[0]))
    safe = jnp.logical_and(room > 0.0, q_sq * k_sq * (scale * scale * 1.01) <= room * room)

    @pl.when(jnp.logical_not(safe))
    def _():
        k = k_ref[0].astype(BF16)
        v = v_ref[0].astype(BF16)
        for rows, keys, off in groups():
            s = lax.dot_general(q_ref[0, rows, :], k[keys], CONTRACT_LAST, preferred_element_type=F32) * scale
            s = s + bias_mask[:, off:]
            m = jnp.max(s, axis=-1, keepdims=True)
            p = jnp.exp(s - m)
            l = jnp.sum(p, axis=-1, keepdims=True)
            o = jnp.dot(p.astype(BF16), v[keys], preferred_element_type=F32) / l
            o_ref[0, rows, :] = o.astype(o_ref.dtype)


def attention_b(q, k, v, rel_bias):
    b, t, _ = q.shape
    tq = _block(t, BAND_GROUP)
    assert tq % LANES == 0
    g = _toeplitz_rows(rel_bias, tq)
    head = pl.BlockSpec((1, t, DH_B), lambda bi, hi: (bi, 0, hi))
    return pl.pallas_call(
        functools.partial(_attn_b_kernel, tq=tq),
        grid=(b, H_B),
        in_specs=[head, head, head, pl.BlockSpec((1, 1, g.shape[2]), lambda bi, hi: (hi, 0, 0))],
        out_specs=head,
        out_shape=jax.ShapeDtypeStruct((b, t, H_B * DH_B), BF16),
        scratch_shapes=[pltpu.VMEM((t, 2 * DH_B), BF16)],
        name="attention_b",
        compiler_params=_params("parallel", "arbitrary"),
    )(q, k, v, g)


def _attn_b_decode_kernel(q_ref, kn_ref, vn_ref, ck_ref, cv_ref, g_ref, o_ref):
    t = q_ref.shape[1]
    scale = DH_B ** -0.5
    for h in range(H_B):
        cols = slice(h * DH_B, (h + 1) * DH_B)
        q = q_ref[0, :, cols]
        kc = ck_ref[0, pl.ds(h, BAND_PAST, stride=H_B), :].astype(BF16)
        vc = cv_ref[0, pl.ds(h, BAND_PAST, stride=H_B), :].astype(BF16)
        kn = kn_ref[0, :, cols].astype(BF16)
        vn = vn_ref[0, :, cols].astype(BF16)
        bias_mask = _band_bias_mask(g_ref[h], t)
        s_old = lax.dot_general(q, kc, CONTRACT_LAST, preferred_element_type=F32) * scale
        s_old = s_old + bias_mask[:, :BAND_PAST]
        s_new = lax.dot_general(q, kn, CONTRACT_LAST, preferred_element_type=F32) * scale
        s_new = s_new + bias_mask[:, BAND_PAST:BAND_PAST + t]
        m = jnp.maximum(jnp.max(s_new, axis=-1, keepdims=True), jnp.max(s_old, axis=-1, keepdims=True))
        p_new = jnp.exp(s_new - m)
        p_old = jnp.exp(s_old - m)
        l = jnp.sum(p_new, axis=-1, keepdims=True) + jnp.sum(p_old, axis=-1, keepdims=True)
        pv = (jnp.dot(p_new.astype(BF16), vn, preferred_element_type=F32)
              + jnp.dot(p_old.astype(BF16), vc, preferred_element_type=F32))
        o_ref[0, :, cols] = (pv / l).astype(o_ref.dtype)


def attention_b_decode(q, k_new, v_new, cache_k, cache_v, rel_bias, p_len):
    b, t, w = q.shape
    assert cache_k.shape[1] == BAND_PAST and p_len % CHUNK == 0 and p_len >= BAND_PAST and t <= CHUNK
    g = _toeplitz_rows(rel_bias, t)
    rows = lambda a: a.reshape(b, BAND_PAST * H_B, a.shape[-1])
    new = pl.BlockSpec((1, t, w), lambda bi: (bi, 0, 0))
    old = pl.BlockSpec((1, BAND_PAST * H_B, DH_B), lambda bi: (bi, 0, 0))
    return pl.pallas_call(
        _attn_b_decode_kernel,
        grid=(b,),
        in_specs=[new, new, new, old, old, _const_spec(g.shape)],
        out_specs=new,
        out_shape=jax.ShapeDtypeStruct((b, t, w), BF16),
        name="attention_b_decode",
        compiler_params=_params("parallel"),
    )(q, k_new, v_new, rows(cache_k), rows(cache_v), g)


def _attn_c_kernel(q_ref, k_ref, v_ref, o_ref, *, tq):
    t = q_ref.shape[1]
    scale = DH_C ** -0.5
    k = k_ref[0].astype(BF16)
    v = v_ref[0].astype(BF16)
    for i in range(t // tq):
        q = q_ref[0, i * tq:(i + 1) * tq, :]
        s = lax.dot_general(q, k, CONTRACT_LAST, preferred_element_type=F32) * scale
        m = jnp.max(s, axis=-1, keepdims=True)
        p = jnp.exp(s - m)
        l = jnp.sum(p, axis=-1, keepdims=True)
        o = jnp.dot(p.astype(BF16), v, preferred_element_type=F32) / l
        o_ref[0, i * tq:(i + 1) * tq, :] = o.astype(o_ref.dtype)


def attention_c(q, mk, mv):
    b, t, _ = q.shape
    n_mem = mk.shape[1]
    tq = _block(t, 512)
    return pl.pallas_call(
        functools.partial(_attn_c_kernel, tq=tq),
        grid=(b, H_C),
        in_specs=[pl.BlockSpec((1, t, DH_C), lambda bi, hi: (bi, 0, hi)),
                  pl.BlockSpec((1, n_mem, DH_C), lambda bi, hi: (bi, 0, hi)),
                  pl.BlockSpec((1, n_mem, DH_C), lambda bi, hi: (bi, 0, hi))],
        out_specs=pl.BlockSpec((1, t, DH_C), lambda bi, hi: (bi, 0, hi)),
        out_shape=jax.ShapeDtypeStruct((b, t, H_C * DH_C), BF16),
        name="attention_c",
        compiler_params=_params("parallel", "arbitrary"),
    )(q, mk, mv)


def _merge_kernel(oa_ref, ob_ref, oc_ref, ga_ref, gb_ref, gc_ref, ba_ref, bb_ref, bc_ref,
                  wa_ref, wb_ref, wc_ref, o_ref):
    def branch(o_r, g_r, b_r, w_r):
        gate = jax.nn.sigmoid(g_r[...].astype(F32) + b_r[...])
        return gate * jnp.dot(o_r[...], w_r[...], preferred_element_type=F32)

    merged = (branch(oa_ref, ga_ref, ba_ref, wa_ref) + branch(ob_ref, gb_ref, bb_ref, wb_ref)
              + branch(oc_ref, gc_ref, bc_ref, wc_ref))
    o_ref[...] = merged.astype(o_ref.dtype)


def merge_branches(oa, ob, oc, gate_logits, b_gate, wa, wb, wc):
    m, w_in = oa.shape
    d = wa.shape[1]
    bm = _block(m, MERGE_ROWS)
    row = lambda width: pl.BlockSpec((bm, width), lambda i: (i, 0))
    gate = lambda j: pl.BlockSpec((bm, d), lambda i: (i, j))
    bias = lambda j: pl.BlockSpec((1, d), lambda i: (0, j))
    weight = pl.BlockSpec((w_in, d), lambda i: (0, 0), pipeline_mode=pl.Buffered(1))
    bg = b_gate.reshape(1, N_BRANCH * d).astype(F32)
    return pl.pallas_call(
        _merge_kernel,
        grid=(m // bm,),
        in_specs=[row(w_in), row(w_in), row(w_in), gate(0), gate(1), gate(2),
                  bias(0), bias(1), bias(2), weight, weight, weight],
        out_specs=pl.BlockSpec((bm, d), lambda i: (i, 0)),
        out_shape=jax.ShapeDtypeStruct((m, d), BF16),
        name="merge_branches",
        compiler_params=_params("parallel"),
    )(oa, ob, oc, gate_logits, gate_logits, gate_logits, bg, bg, bg, wa, wb, wc)


def _proj_norm_res_kernel(*refs, next_norm, rows):
    if next_norm:
        a_ref, w_ref, x_ref, g_ref, g2_ref, o_ref, h_ref = refs
    else:
        a_ref, w_ref, x_ref, g_ref, o_ref = refs
    bm = a_ref.shape[0]
    for r0 in range(0, bm, rows):
        sl = slice(r0, r0 + rows)
        y = jnp.dot(a_ref[sl, :], w_ref[...], preferred_element_type=F32)
        ms = jnp.mean(y * y, axis=-1, keepdims=True)
        o = x_ref[sl, :] + y * lax.rsqrt(ms + EPS) * g_ref[...]
        o_ref[sl, :] = o
        if next_norm:
            ms2 = jnp.mean(o * o, axis=-1, keepdims=True)
            h_ref[sl, :] = (o * lax.rsqrt(ms2 + EPS) * g2_ref[...]).astype(h_ref.dtype)


def proj_norm_residual(a, w, x, g, bm, rows, next_g=None):
    m, k = a.shape
    d = w.shape[1]
    bm = _block(m, bm)
    rows = _block(bm, rows)
    vec = pl.BlockSpec((1, d), lambda i: (0, 0))
    row = pl.BlockSpec((bm, d), lambda i: (i, 0))
    in_specs = [pl.BlockSpec((bm, k), lambda i: (i, 0)),
                pl.BlockSpec((k, d), lambda i: (0, 0), pipeline_mode=pl.Buffered(1)), row, vec]
    args = [a, w, x, g.reshape(1, d).astype(F32)]
    out_specs, out_shape = row, jax.ShapeDtypeStruct((m, d), F32)
    if next_g is not None:
        in_specs.append(vec)
        args.append(next_g.reshape(1, d).astype(F32))
        out_specs = (row, row)
        out_shape = (out_shape, jax.ShapeDtypeStruct((m, d), BF16))
    return pl.pallas_call(
        functools.partial(_proj_norm_res_kernel, next_norm=next_g is not None, rows=rows),
        grid=(m // bm,),
        in_specs=in_specs,
        out_specs=out_specs,
        out_shape=out_shape,
        name="proj_norm_residual",
        compiler_params=_params("parallel"),
    )(*args)


def _ffn_in_kernel(h_ref, h2_ref, wa_ref, wb_ref, o_ref, o2_ref, wa_bf, wb_bf):
    def swiglu(h):
        a = jnp.dot(h, wa_bf[...], preferred_element_type=F32)
        b = jnp.dot(h, wb_bf[...], preferred_element_type=F32)
        return (jax.nn.silu(a) * b).astype(o_ref.dtype)

    @pl.when(pl.program_id(1) == 0)
    def _():
        wa_bf[...] = wa_ref[...].astype(BF16)
        wb_bf[...] = wb_ref[...].astype(BF16)
        o2_ref[...] = swiglu(h2_ref[...])

    o_ref[...] = swiglu(h_ref[...])


def ffn_in(h, extra, w):
    m, k = h.shape
    m2 = extra.shape[0]
    f = w.shape[1] // 2
    bm = _block(m, 1024)
    bf = 512
    assert f % bf == 0
    nf = f // bf
    return pl.pallas_call(
        _ffn_in_kernel,
        grid=(nf, m // bm),
        in_specs=[pl.BlockSpec((bm, k), lambda j, i: (i, 0)),
                  pl.BlockSpec((m2, k), lambda j, i: (0, 0)),
                  pl.BlockSpec((k, bf), lambda j, i: (0, j)),
                  pl.BlockSpec((k, bf), lambda j, i: (0, j + nf))],
        out_specs=[pl.BlockSpec((bm, bf), lambda j, i: (i, j)),
                   pl.BlockSpec((m2, bf), lambda j, i: (0, j))],
        out_shape=[jax.ShapeDtypeStruct((m, f), BF16), jax.ShapeDtypeStruct((m2, f), BF16)],
        scratch_shapes=[pltpu.VMEM((k, bf), BF16), pltpu.VMEM((k, bf), BF16)],
        name="ffn_in",
        compiler_params=_params("arbitrary", "arbitrary"),
    )(h, extra, w, w)


def _layer(xp, xd, mk_p, mv_p, mk_d, mv_d, caches, lam_init, norm_mix_pre, norm_mix_post, w_in, b_gate,
           lq1, lk1, lq2, lk2, subln_a, rel_bias, w_br_a, w_br_b, w_br_c, w_out,
           norm_ffn_pre, norm_ffn_post, w_ffn_in, w_ffn_out):
    d = xp.shape[-1]
    w_head = H_A * DK_A
    shapes = [xp.shape[:2], xd.shape[:2]]
    xs = [xp.reshape(-1, d), xd.reshape(-1, d)]
    h_d = rms_norm_bf16(xs[1], norm_mix_pre)
    h_p, qa_p, qa_d = norm_matmul_cols(xs[0], norm_mix_pre, w_in, 0, w_head, BF16, h_d)
    hs = [h_p, h_d]
    split = lambda outs: [o.reshape(*shp, w_head) for o, shp in zip(outs, shapes)]
    proj = lambda idx, dt: split(matmul_cols(hs[0], w_in, idx * w_head, w_head, dt, extra=hs[1]))
    q_a, k_a, v_a = split([qa_p, qa_d]), proj(1, F32), proj(2, F32)
    q_b, k_b, v_b = proj(3, BF16), proj(4, F32), proj(5, F32)
    q_c = proj(6, BF16)
    gates = matmul_cols(hs[0], w_in, 7 * w_head, N_BRANCH * d, BF16, extra=hs[1])

    ca_k, ca_v, cb_k, cb_v = caches
    o_a = [attention_a(q_a[0], k_a[0], v_a[0], lq1, lk1, lq2, lk2, subln_a, lam_init),
           attention_a_decode(q_a[1], k_a[1], v_a[1], ca_k, ca_v, lq1, lk1, lq2, lk2, subln_a, lam_init)]
    o_b = [attention_b(q_b[0], k_b[0], v_b[0], rel_bias),
           attention_b_decode(q_b[1], k_b[1], v_b[1], cb_k, cb_v, rel_bias, ca_k.shape[1])]
    o_c = [attention_c(q_c[0], mk_p, mv_p), attention_c(q_c[1], mk_d, mv_d)]

    x1, h2 = [], []
    for i in range(2):
        m = xs[i].shape[0]
        merged = merge_branches(o_a[i].reshape(m, -1), o_b[i].reshape(m, -1), o_c[i].reshape(m, -1),
                                gates[i], b_gate, w_br_a, w_br_b, w_br_c)
        a, b = proj_norm_residual(merged, w_out, xs[i], norm_mix_post, bm=512, rows=PROJ_ROWS,
                                  next_g=norm_ffn_pre)
        x1.append(a)
        h2.append(b)
    acts = ffn_in(h2[0], h2[1], w_ffn_in)
    ys = [proj_norm_residual(acts[i], w_ffn_out, x1[i], norm_ffn_post, bm=256, rows=256).reshape(*shapes[i], d)
          for i in range(2)]
    return [(ys[i], k_a[i], v_a[i], k_b[i], v_b[i]) for i in range(2)]


def kernel(x_prompt, x_sample, cache_a_k, cache_a_v, cache_b_k, cache_b_v, cache_mem_k, cache_mem_v, mem_prompt, norm_mix_pre, norm_mix_post, norm_mem, w_in, b_gate, lambda_q1, lambda_k1, lambda_q2, lambda_k2, subln_a, rel_bias_b, w_mem_kv, w_br_a, w_br_b, w_br_c, w_out, norm_ffn_pre, norm_ffn_post, w_ffn_in, w_ffn_out):
    depth = w_in.shape[0]
    bsz, s, d = x_prompt.shape
    n_mem = mem_prompt.shape[1]
    lb_prompt = min(BAND_PAST, s)
    yp, ys = x_prompt, x_sample
    outs = [[] for _ in range(10)]
    heads = lambda a, nh: a.reshape(a.shape[0], a.shape[1], nh, a.shape[2] // nh)
    flat = lambda a: a.reshape(a.shape[0], a.shape[1], -1)
    for l in range(depth):
        lam_init = 0.8 - 0.6 * math.exp(-0.3 * l)
        bf = lambda w: w[l].astype(BF16)
        shared = (lam_init, norm_mix_pre[l], norm_mix_post[l], w_in[l], b_gate[l],
                  lambda_q1[l], lambda_k1[l], lambda_q2[l], lambda_k2[l], subln_a[l], rel_bias_b[l],
                  bf(w_br_a), bf(w_br_b), bf(w_br_c), bf(w_out),
                  norm_ffn_pre[l], norm_ffn_post[l], w_ffn_in[l], bf(w_ffn_out))

        mem_n = rms_norm_bf16(mem_prompt.reshape(bsz * n_mem, d), norm_mem[l])
        w_c = H_C * DH_C
        mk_p = matmul_cols(mem_n, w_mem_kv[l], 0, w_c, F32).reshape(bsz, n_mem, w_c)
        mv_p = matmul_cols(mem_n, w_mem_kv[l], w_c, w_c, F32).reshape(bsz, n_mem, w_c)
        caches = (cache_a_k[l], cache_a_v[l], cache_b_k[l], cache_b_v[l])
        (yp, ka, va, kb, vb), (ys, ka_d, va_d, kb_d, vb_d) = _layer(
            yp, ys, mk_p, mv_p, flat(cache_mem_k[l]), flat(cache_mem_v[l]), caches, *shared)
        new = [heads(ka, H_A), heads(va, H_A), heads(kb[:, s - lb_prompt:], H_B),
               heads(vb[:, s - lb_prompt:], H_B), heads(mk_p, H_C), heads(mv_p, H_C),
               heads(ka_d, H_A), heads(va_d, H_A), heads(kb_d, H_B), heads(vb_d, H_B)]
        for o, a in zip(outs, new):
            o.append(a)
    return (yp, ys) + tuple(jnp.stack(o) for o in outs)
```

```python
import functools
import math

import jax
import jax.numpy as jnp
from jax import lax
from jax.experimental import pallas as pl
from jax.experimental.pallas import tpu as pltpu

F32 = jnp.float32
BF16 = jnp.bfloat16

CHUNK = 64
H_A = 8
DH_A = 64
DK_A = 2 * DH_A
DV_A = 128
H_B = 8
DH_B = 128
N_PREV_CHUNKS = 8
BAND_PAST = N_PREV_CHUNKS * CHUNK
MAX_REL = 128
H_C = 4
DH_C = 256
N_BRANCH = 3
EPS = 1e-6
NEG_INF = -1e30

LANES = 128
VMEM_LIMIT = 56 * 1024 * 1024
BAND_GROUP = 256
A_BLOCK = 256
MM_ROWS = 2048
MERGE_ROWS = 256
MERGE_CHUNK = 256
PROJ_ROWS = 128
SAFE_LOGIT = 40.0

CONTRACT_LAST = (((1,), (1,)), ((), ()))


def _params(*sem):
    return pltpu.CompilerParams(dimension_semantics=sem, vmem_limit_bytes=VMEM_LIMIT)


def _block(n, target):
    if n <= target:
        return n
    b = target
    while n % b:
        b //= 2
    return b


def _rms_kernel(x_ref, g_ref, o_ref):
    x = x_ref[...]
    ms = jnp.mean(x * x, axis=-1, keepdims=True)
    o_ref[...] = (x * lax.rsqrt(ms + EPS) * g_ref[...]).astype(o_ref.dtype)


def rms_norm_bf16(x, g):
    m, d = x.shape
    bm = _block(m, 512)
    return pl.pallas_call(
        _rms_kernel,
        grid=(m // bm,),
        in_specs=[pl.BlockSpec((bm, d), lambda i: (i, 0)),
                  pl.BlockSpec((1, d), lambda i: (0, 0))],
        out_specs=pl.BlockSpec((bm, d), lambda i: (i, 0)),
        out_shape=jax.ShapeDtypeStruct((m, d), BF16),
        name="rms_norm",
        compiler_params=_params("parallel"),
    )(x, g.reshape(1, d))


def _mm_kernel(*refs, has_extra, tail):
    refs = list(refs)
    a_ref = refs.pop(0)
    a2_ref = refs.pop(0) if has_extra else None
    w_ref = refs.pop(0)
    o_ref = refs.pop(0)
    o2_ref = refs.pop(0) if has_extra else None
    t_ref = refs.pop(0) if tail else None
    wb_ref, = refs

    def product(x_ref, y_ref):
        rows = _block(x_ref.shape[0], 1024)
        for r0 in range(0, x_ref.shape[0], rows):
            acc = jnp.dot(x_ref[r0:r0 + rows, :], wb_ref[...], preferred_element_type=F32)
            y_ref[r0:r0 + rows, :] = acc.astype(y_ref.dtype)

    @pl.when(pl.program_id(1) == 0)
    def _():
        wb_ref[...] = w_ref[...].astype(BF16)
        if has_extra:
            product(a2_ref, o2_ref)

    product(a_ref, o_ref)
    if tail:
        blocks, rows = tail

        @pl.when(pl.program_id(1) % blocks == blocks - 1)
        def _():
            t_ref[...] = o_ref[o_ref.shape[0] - rows:, :]


def matmul_cols(a, w, col0, n, out_dtype, extra=None, tail=None):
    m, k = a.shape
    bm = _block(m, MM_ROWS * 2 // jnp.dtype(out_dtype).itemsize)
    bn = _block(n, 1024)
    assert col0 % bn == 0
    off = col0 // bn
    in_specs = [pl.BlockSpec((bm, k), lambda j, i: (i, 0))]
    args = [a]
    out_specs = [pl.BlockSpec((bm, bn), lambda j, i: (i, j))]
    out_shape = [jax.ShapeDtypeStruct((m, n), out_dtype)]
    if extra is not None:
        m2 = extra.shape[0]
        in_specs.append(pl.BlockSpec((m2, k), lambda j, i: (0, 0)))
        args.append(extra)
        out_specs.append(pl.BlockSpec((m2, bn), lambda j, i: (0, j)))
        out_shape.append(jax.ShapeDtypeStruct((m2, n), out_dtype))
    if tail is not None:
        seq, keep = tail
        assert seq % bm == 0 and keep <= bm and m % seq == 0
        blocks = seq // bm
        out_specs.append(pl.BlockSpec((keep, bn), lambda j, i: (i // blocks, j)))
        out_shape.append(jax.ShapeDtypeStruct((m // seq * keep, n), out_dtype))
        tail = (blocks, keep)
    in_specs.append(pl.BlockSpec((k, bn), lambda j, i: (0, j + off)))
    args.append(w)
    return pl.pallas_call(
        functools.partial(_mm_kernel, has_extra=extra is not None, tail=tail),
        grid=(n // bn, m // bm),
        in_specs=in_specs,
        out_specs=out_specs,
        out_shape=out_shape,
        scratch_shapes=[pltpu.VMEM((k, bn), BF16)],
        name="matmul_cols",
        compiler_params=_params("arbitrary", "arbitrary"),
    )(*args)


def _norm_mm_kernel(x_ref, g_ref, a2_ref, w_ref, h_ref, o_ref, o2_ref, wb_ref, *, rows):
    @pl.when(pl.program_id(0) == 0)
    def _():
        wb_ref[...] = w_ref[...].astype(BF16)
        o2_ref[...] = jnp.dot(a2_ref[...], wb_ref[...], preferred_element_type=F32).astype(o2_ref.dtype)

    for r0 in range(0, x_ref.shape[0], rows):
        x = x_ref[r0:r0 + rows, :]
        ms = jnp.mean(x * x, axis=-1, keepdims=True)
        h = (x * lax.rsqrt(ms + EPS) * g_ref[...]).astype(BF16)
        h_ref[r0:r0 + rows, :] = h
        o_ref[r0:r0 + rows, :] = jnp.dot(h, wb_ref[...], preferred_element_type=F32).astype(o_ref.dtype)


def norm_matmul_cols(x, g, w, col0, n, out_dtype, extra):
    m, k = x.shape
    m2 = extra.shape[0]
    bm = _block(m, 512)
    assert col0 % n == 0
    off = col0 // n
    return pl.pallas_call(
        functools.partial(_norm_mm_kernel, rows=_block(bm, 128)),
        grid=(m // bm,),
        in_specs=[pl.BlockSpec((bm, k), lambda i: (i, 0)),
                  pl.BlockSpec((1, k), lambda i: (0, 0)),
                  pl.BlockSpec((m2, k), lambda i: (0, 0)),
                  pl.BlockSpec((k, n), lambda i: (0, off))],
        out_specs=[pl.BlockSpec((bm, k), lambda i: (i, 0)),
                   pl.BlockSpec((bm, n), lambda i: (i, 0)),
                   pl.BlockSpec((m2, n), lambda i: (0, 0))],
        out_shape=[jax.ShapeDtypeStruct((m, k), BF16), jax.ShapeDtypeStruct((m, n), out_dtype),
                   jax.ShapeDtypeStruct((m2, n), out_dtype)],
        scratch_shapes=[pltpu.VMEM((k, n), BF16)],
        name="norm_matmul_cols",
        compiler_params=_params("arbitrary"),
    )(x, g.reshape(1, k).astype(F32), extra, w)


def _diff_lambda(lq1_ref, lk1_ref, lq2_ref, lk2_ref, lam_init):
    return (jnp.exp(jnp.sum(lq1_ref[...] * lk1_ref[...], keepdims=True))
            - jnp.exp(jnp.sum(lq2_ref[...] * lk2_ref[...], keepdims=True)) + lam_init)


def _head_norm(o, sub_ref, lam_init):
    ms = jnp.mean(o * o, axis=-1, keepdims=True)
    return o * lax.rsqrt(ms + EPS) * sub_ref[...] * (1.0 - lam_init)


def _own_alibi(n, slope):
    r = lax.broadcasted_iota(jnp.int32, (n, n), 0)
    c = lax.broadcasted_iota(jnp.int32, (n, n), 1)
    bias = slope * (r - jnp.abs(r - c)).astype(F32)
    return jnp.where(c // CHUNK <= r // CHUNK, bias, NEG_INF)


def _max_sq_norms(x, lane_masks):
    sq = jnp.square(x.astype(F32))
    return [jnp.max(jnp.sum(jnp.where(mask, sq, 0.0), axis=-1, keepdims=True)) for mask in lane_masks]


def _attn_a_kernel(q_ref, k_ref, v_ref, slope_ref, kf_ref, qf_ref, lq1_ref, lk1_ref, lq2_ref, lk2_ref,
                   sub_ref, o_ref, ke_scr, ve_scr, *, tq, lam_init):
    args = (q_ref, k_ref, v_ref, slope_ref, lq1_ref, lk1_ref, lq2_ref, lk2_ref, sub_ref, o_ref)
    _attn_a_bounded(*args, kf_ref, qf_ref, ke_scr, ve_scr, tq=tq, lam_init=lam_init)
    first_half = lax.broadcasted_iota(jnp.int32, (1, DK_A), 1) < DH_A
    q_sq = _max_sq_norms(q_ref[0] * (DH_A ** -0.5), (first_half, ~first_half))
    k_sq = _max_sq_norms(k_ref[0].astype(BF16), (first_half, ~first_half))
    bound_sq = jnp.maximum(q_sq[0] * k_sq[0], q_sq[1] * k_sq[1])

    @pl.when(bound_sq > SAFE_LOGIT ** 2 / 1.01)
    def _():
        _attn_a_general(*args, tq=tq, lam_init=lam_init)


def _position_features(t):
    pos = jnp.arange(t, dtype=jnp.int32)[:, None]
    lo = pos & 7
    hi = (pos - lo).astype(F32)
    lo = lo.astype(F32)
    one = jnp.ones((t, 1), F32)
    pad = jnp.zeros((t, LANES - 4), F32)
    k_side = jnp.concatenate([hi, lo, one, one, pad], axis=1).astype(BF16)
    q_side = jnp.concatenate([one, one, -hi, -lo, pad], axis=1)
    return k_side, q_side


def _attn_a_bounded(q_ref, k_ref, v_ref, slope_ref, lq1_ref, lk1_ref, lq2_ref, lk2_ref, sub_ref, o_ref,
                    kf_ref, qf_ref, ke_scr, ve_scr, *, tq, lam_init):
    t = q_ref.shape[1]
    slope = slope_ref[0][:, :1]
    lam = _diff_lambda(lq1_ref, lk1_ref, lq2_ref, lk2_ref, lam_init)
    first_half = lax.broadcasted_iota(jnp.int32, (1, DK_A), 1) < DH_A
    ke_scr[:, :DK_A] = k_ref[0].astype(BF16)
    ke_scr[:, DK_A:] = kf_ref[...]
    ve_scr[:, :DV_A] = v_ref[0].astype(BF16)
    ve_scr[:, DV_A:] = jnp.ones((t, DV_A), BF16)

    r = lax.broadcasted_iota(jnp.int32, (tq, tq), 0)
    c = lax.broadcasted_iota(jnp.int32, (tq, tq), 1)
    own_fix = jnp.where(c // CHUNK <= r // CHUNK, -2.0 * slope * jnp.maximum(c - r, 0).astype(F32), NEG_INF)
    own_fix = jnp.concatenate([own_fix, own_fix], axis=0)

    for i in reversed(range(t // tq)):
        nb = i * tq
        q = q_ref[0, nb:nb + tq, :] * (DH_A ** -0.5)
        q_feat = (qf_ref[nb:nb + tq, :] * slope).astype(BF16)
        zero = jnp.zeros_like(q)
        qe = jnp.concatenate([jnp.concatenate([jnp.where(first_half, q, zero), q_feat], axis=1),
                              jnp.concatenate([jnp.where(first_half, zero, q), q_feat], axis=1)], axis=0)
        s_own = lax.dot_general(qe, ke_scr[nb:nb + tq, :], CONTRACT_LAST, preferred_element_type=F32)
        acc = jnp.dot(jnp.exp(s_own + own_fix).astype(BF16), ve_scr[nb:nb + tq, :],
                      preferred_element_type=F32)
        if nb:
            s_bef = lax.dot_general(qe, ke_scr[:nb, :], CONTRACT_LAST, preferred_element_type=F32)
            acc = acc + jnp.dot(jnp.exp(s_bef).astype(BF16), ve_scr[:nb, :], preferred_element_type=F32)
        o = acc[:, :DV_A] / acc[:, DV_A:]
        o = o[:tq] - lam * o[tq:]
        o_ref[0, nb:nb + tq, :] = _head_norm(o, sub_ref, lam_init).astype(o_ref.dtype)


def _attn_a_general(q_ref, k_ref, v_ref, slope_ref, lq1_ref, lk1_ref, lq2_ref, lk2_ref, sub_ref, o_ref,
                    *, tq, lam_init):
    t = q_ref.shape[1]
    n_blocks = t // tq
    slope = slope_ref[0][:, :1]
    lam = _diff_lambda(lq1_ref, lk1_ref, lq2_ref, lk2_ref, lam_init)
    k = k_ref[0].astype(BF16)
    v = v_ref[0].astype(BF16)
    first_half = lax.broadcasted_iota(jnp.int32, (1, DK_A), 1) < DH_A
    own_bias = _own_alibi(tq, slope)
    n_before_max = (n_blocks - 1) * tq
    if n_before_max:
        j = lax.broadcasted_iota(jnp.int32, (1, n_before_max), 1)
        before_bias = slope * (j - n_before_max).astype(F32)

    for i in range(n_blocks):
        nb = i * tq
        q = q_ref[0, nb:nb + tq, :] * (DH_A ** -0.5)
        k_own, v_own = k[nb:nb + tq], v[nb:nb + tq]

        def softmax_pv(qh):
            s_own = lax.dot_general(qh, k_own, CONTRACT_LAST, preferred_element_type=F32) + own_bias
            m = jnp.max(s_own, axis=-1, keepdims=True)
            if nb:
                s_bef = lax.dot_general(qh, k[:nb], CONTRACT_LAST, preferred_element_type=F32)
                s_bef = s_bef + before_bias[:, n_before_max - nb:]
                m = jnp.maximum(m, jnp.max(s_bef, axis=-1, keepdims=True))
            p_own = jnp.exp(s_own - m)
            l = jnp.sum(p_own, axis=-1, keepdims=True)
            pv = jnp.dot(p_own.astype(BF16), v_own, preferred_element_type=F32)
            if nb:
                p_bef = jnp.exp(s_bef - m)
                l = l + jnp.sum(p_bef, axis=-1, keepdims=True)
                pv = pv + jnp.dot(p_bef.astype(BF16), v[:nb], preferred_element_type=F32)
            return pv / l

        o = (softmax_pv(jnp.where(first_half, q, jnp.zeros_like(q)))
             - lam * softmax_pv(jnp.where(first_half, jnp.zeros_like(q), q)))
        o_ref[0, nb:nb + tq, :] = _head_norm(o, sub_ref, lam_init).astype(o_ref.dtype)


def _row(a):
    return a.reshape(1, -1).astype(F32)


def _const_spec(shape):
    return pl.BlockSpec(shape, lambda *_: (0,) * len(shape))


def attention_a(q, k, v, lq1, lk1, lq2, lk2, subln, lam_init):
    b, t, _ = q.shape
    tq = _block(t, A_BLOCK)
    assert tq % CHUNK == 0
    slopes = jnp.asarray([[[2.0 ** (-8.0 * (hh + 1) / H_A)] * LANES] for hh in range(H_A)], dtype=F32)
    head = lambda width: pl.BlockSpec((1, t, width), lambda bi, hi: (bi, 0, hi))
    k_feat, q_feat = _position_features(t)
    return pl.pallas_call(
        functools.partial(_attn_a_kernel, tq=tq, lam_init=lam_init),
        grid=(b, H_A),
        in_specs=[head(DK_A), head(DK_A), head(DV_A),
                  pl.BlockSpec((1, 1, LANES), lambda bi, hi: (hi, 0, 0)),
                  _const_spec((t, LANES)), _const_spec((t, LANES)),
                  _const_spec((1, DH_A)), _const_spec((1, DH_A)), _const_spec((1, DH_A)),
                  _const_spec((1, DH_A)), _const_spec((1, DV_A))],
        out_specs=head(DV_A),
        out_shape=jax.ShapeDtypeStruct((b, t, H_A * DV_A), BF16),
        scratch_shapes=[pltpu.VMEM((t, DK_A + LANES), BF16), pltpu.VMEM((t, 2 * DV_A), BF16)],
        name="attention_a",
        compiler_params=_params("parallel", "arbitrary"),
    )(q, k, v, slopes, k_feat, q_feat, _row(lq1), _row(lk1), _row(lq2), _row(lk2), _row(subln))


def _attn_a_decode_kernel(q_ref, kn_ref, vn_ref, ck_ref, cv_ref, lq1_ref, lk1_ref, lq2_ref, lk2_ref,
                          sub_ref, o_ref, *, lam_init):
    t = q_ref.shape[1]
    p_len = ck_ref.shape[1] // H_A
    lam = _diff_lambda(lq1_ref, lk1_ref, lq2_ref, lk2_ref, lam_init)
    first_half = lax.broadcasted_iota(jnp.int32, (1, DK_A), 1) < DH_A
    j = lax.broadcasted_iota(jnp.int32, (1, p_len), 1)
    before_dist = (j - p_len).astype(F32)
    for h in range(H_A):
        slope = 2.0 ** (-8.0 * (h + 1) / H_A)
        cols = slice(h * DK_A, (h + 1) * DK_A)
        q = q_ref[0, :, cols] * (DH_A ** -0.5)
        kc = ck_ref[0, pl.ds(h, p_len, stride=H_A), :].astype(BF16)
        vc = cv_ref[0, pl.ds(h, p_len, stride=H_A), :].astype(BF16)
        kn = kn_ref[0, :, cols].astype(BF16)
        vn = vn_ref[0, :, cols].astype(BF16)
        own_bias = _own_alibi(t, slope)
        before_bias = slope * before_dist

        def softmax_pv(qh):
            s_new = lax.dot_general(qh, kn, CONTRACT_LAST, preferred_element_type=F32) + own_bias
            s_old = lax.dot_general(qh, kc, CONTRACT_LAST, preferred_element_type=F32) + before_bias
            m = jnp.maximum(jnp.max(s_new, axis=-1, keepdims=True), jnp.max(s_old, axis=-1, keepdims=True))
            p_new = jnp.exp(s_new - m)
            p_old = jnp.exp(s_old - m)
            l = jnp.sum(p_new, axis=-1, keepdims=True) + jnp.sum(p_old, axis=-1, keepdims=True)
            pv = (jnp.dot(p_new.astype(BF16), vn, preferred_element_type=F32)
                  + jnp.dot(p_old.astype(BF16), vc, preferred_element_type=F32))
            return pv / l

        o = (softmax_pv(jnp.where(first_half, q, jnp.zeros_like(q)))
             - lam * softmax_pv(jnp.where(first_half, jnp.zeros_like(q), q)))
        o_ref[0, :, cols] = _head_norm(o, sub_ref, lam_init).astype(o_ref.dtype)


def attention_a_decode(q, k_new, v_new, cache_k, cache_v, lq1, lk1, lq2, lk2, subln, lam_init):
    b, t, w = q.shape
    p_len = cache_k.shape[1]
    assert p_len % CHUNK == 0 and t <= CHUNK
    rows = lambda a: a.reshape(b, p_len * H_A, a.shape[-1])
    new = pl.BlockSpec((1, t, w), lambda bi: (bi, 0, 0))
    old = pl.BlockSpec((1, p_len * H_A, DK_A), lambda bi: (bi, 0, 0))
    return pl.pallas_call(
        functools.partial(_attn_a_decode_kernel, lam_init=lam_init),
        grid=(b,),
        in_specs=[new, new, new, old, old,
                  _const_spec((1, DH_A)), _const_spec((1, DH_A)), _const_spec((1, DH_A)),
                  _const_spec((1, DH_A)), _const_spec((1, DV_A))],
        out_specs=new,
        out_shape=jax.ShapeDtypeStruct((b, t, w), BF16),
        name="attention_a_decode",
        compiler_params=_params("parallel"),
    )(q, k_new, v_new, rows(cache_k), rows(cache_v), _row(lq1), _row(lk1), _row(lq2), _row(lk2), _row(subln))


def _band_tile(tq):
    pad = -(-tq // LANES) * LANES
    return BAND_PAST + pad, pad


def _band_bias_mask(g_row, tq):
    tile_w, pad = _band_tile(tq)
    width = g_row.shape[1]
    assert width == tile_w + pad
    rolled = pltpu.roll(jnp.broadcast_to(g_row, (tq, width)), width - pad, 1, stride=1, stride_axis=0)
    r = lax.broadcasted_iota(jnp.int32, (tq, tile_w), 0)
    c = lax.broadcasted_iota(jnp.int32, (tq, tile_w), 1)
    qch = r // CHUNK
    kch = c // CHUNK - N_PREV_CHUNKS
    valid = (kch <= qch) & (kch >= qch - N_PREV_CHUNKS)
    return jnp.where(valid, rolled[:, :tile_w], NEG_INF)


def _toeplitz_rows(rel_bias, tq):
    tile_w, pad = _band_tile(tq)
    width = tile_w + pad
    n_lo = pad + BAND_PAST - MAX_REL
    n_hi = max(width - n_lo - (2 * MAX_REL + 1), 0)
    lo = jnp.broadcast_to(rel_bias[:, :1], (H_B, n_lo))
    hi = jnp.broadcast_to(rel_bias[:, -1:], (H_B, n_hi))
    g = jnp.concatenate([lo, rel_bias, hi], axis=1)[:, :width]
    return g.reshape(H_B, 1, width).astype(F32)


def _attn_b_kernel(q_ref, k_ref, v_ref, g_ref, o_ref, ve_scr, *, tq):
    t = q_ref.shape[1]
    tile_w, _ = _band_tile(tq)
    scale = DH_B ** -0.5
    bias_mask = _band_bias_mask(g_ref[0], tq)

    def groups():
        for gi in range(t // tq):
            qa = gi * tq
            lo = max(qa - BAND_PAST, 0)
            hi = qa + tq
            off = lo - qa + BAND_PAST
            assert off % LANES == 0 and off + hi - lo == tile_w
            yield slice(qa, hi), slice(lo, hi), off

    log2e = math.log2(math.e)
    k = k_ref[0].astype(BF16)
    ve_scr[:, :DH_B] = v_ref[0].astype(BF16)
    ve_scr[:, DH_B:] = jnp.ones((t, DH_B), BF16)
    bias2 = bias_mask * log2e
    for rows, keys, off in groups():
        s = lax.dot_general(q_ref[0, rows, :], k[keys], CONTRACT_LAST, preferred_element_type=F32)
        p = jnp.exp2(s * (scale * log2e) + bias2[:, off:])
        acc = jnp.dot(p.astype(BF16), ve_scr[keys, :], preferred_element_type=F32)
        o_ref[0, rows, :] = (acc[:, :DH_B] / acc[:, DH_B:]).astype(o_ref.dtype)

    (q_sq,), (k_sq,) = _max_sq_norms(q_ref[0], (True,)), _max_sq_norms(k, (True,))
    room = SAFE_LOGIT - jnp.max(jnp.abs(g_ref[0]))
    safe = jnp.logical_and(room > 0.0, q_sq * k_sq * (scale * scale * 1.01) <= room * room)

    @pl.when(jnp.logical_not(safe))
    def _():
        k = k_ref[0].astype(BF16)
        v = v_ref[0].astype(BF16)
        for rows, keys, off in groups():
            s = lax.dot_general(q_ref[0, rows, :], k[keys], CONTRACT_LAST, preferred_element_type=F32) * scale
            s = s + bias_mask[:, off:]
            m = jnp.max(s, axis=-1, keepdims=True)
            p = jnp.exp(s - m)
            l = jnp.sum(p, axis=-1, keepdims=True)
            o = jnp.dot(p.astype(BF16), v[keys], preferred_element_type=F32) / l
            o_ref[0, rows, :] = o.astype(o_ref.dtype)


def attention_b(q, k, v, rel_bias):
    b, t, _ = q.shape
    tq = _block(t, BAND_GROUP)
    assert tq % LANES == 0
    g = _toeplitz_rows(rel_bias, tq)
    head = pl.BlockSpec((1, t, DH_B), lambda bi, hi: (bi, 0, hi))
    return pl.pallas_call(
        functools.partial(_attn_b_kernel, tq=tq),
        grid=(b, H_B),
        in_specs=[head, head, head, pl.BlockSpec((1, 1, g.shape[2]), lambda bi, hi: (hi, 0, 0))],
        out_specs=head,
        out_shape=jax.ShapeDtypeStruct((b, t, H_B * DH_B), BF16),
        scratch_shapes=[pltpu.VMEM((t, 2 * DH_B), BF16)],
        name="attention_b",
        compiler_params=_params("parallel", "arbitrary"),
    )(q, k, v, g)


def _attn_b_decode_kernel(q_ref, kn_ref, vn_ref, ck_ref, cv_ref, g_ref, o_ref):
    t = q_ref.shape[1]
    scale = DH_B ** -0.5
    for h in range(H_B):
        cols = slice(h * DH_B, (h + 1) * DH_B)
        q = q_ref[0, :, cols]
        kc = ck_ref[0, pl.ds(h, BAND_PAST, stride=H_B), :].astype(BF16)
        vc = cv_ref[0, pl.ds(h, BAND_PAST, stride=H_B), :].astype(BF16)
        kn = kn_ref[0, :, cols].astype(BF16)
        vn = vn_ref[0, :, cols].astype(BF16)
        bias_mask = _band_bias_mask(g_ref[h], t)
        s_old = lax.dot_general(q, kc, CONTRACT_LAST, preferred_element_type=F32) * scale
        s_old = s_old + bias_mask[:, :BAND_PAST]
        s_new = lax.dot_general(q, kn, CONTRACT_LAST, preferred_element_type=F32) * scale
        s_new = s_new + bias_mask[:, BAND_PAST:BAND_PAST + t]
        m = jnp.maximum(jnp.max(s_new, axis=-1, keepdims=True), jnp.max(s_old, axis=-1, keepdims=True))
        p_new = jnp.exp(s_new - m)
        p_old = jnp.exp(s_old - m)
        l = jnp.sum(p_new, axis=-1, keepdims=True) + jnp.sum(p_old, axis=-1, keepdims=True)
        pv = (jnp.dot(p_new.astype(BF16), vn, preferred_element_type=F32)
              + jnp.dot(p_old.astype(BF16), vc, preferred_element_type=F32))
        o_ref[0, :, cols] = (pv / l).astype(o_ref.dtype)


def attention_b_decode(q, k_new, v_new, cache_k, cache_v, rel_bias, p_len):
    b, t, w = q.shape
    assert cache_k.shape[1] == BAND_PAST and p_len % CHUNK == 0 and p_len >= BAND_PAST and t <= CHUNK
    g = _toeplitz_rows(rel_bias, t)
    rows = lambda a: a.reshape(b, BAND_PAST * H_B, a.shape[-1])
    new = pl.BlockSpec((1, t, w), lambda bi: (bi, 0, 0))
    old = pl.BlockSpec((1, BAND_PAST * H_B, DH_B), lambda bi: (bi, 0, 0))
    return pl.pallas_call(
        _attn_b_decode_kernel,
        grid=(b,),
        in_specs=[new, new, new, old, old, _const_spec(g.shape)],
        out_specs=new,
        out_shape=jax.ShapeDtypeStruct((b, t, w), BF16),
        name="attention_b_decode",
        compiler_params=_params("parallel"),
    )(q, k_new, v_new, rows(cache_k), rows(cache_v), g)


def _attn_c_kernel(q_ref, k_ref, v_ref, o_ref, *, tq):
    t = q_ref.shape[1]
    scale = DH_C ** -0.5
    k = k_ref[0].astype(BF16)
    v = v_ref[0].astype(BF16)

    def attend(probabilities):
        for i in range(t // tq):
            rows = slice(i * tq, (i + 1) * tq)
            s = lax.dot_general(q_ref[0, rows, :], k, CONTRACT_LAST, preferred_element_type=F32)
            p = probabilities(s)
            l = jnp.sum(p, axis=-1, keepdims=True)
            o = jnp.dot(p.astype(BF16), v, preferred_element_type=F32) / l
            o_ref[0, rows, :] = o.astype(o_ref.dtype)

    attend(lambda s: jnp.exp2(s * (scale * math.log2(math.e))))
    (q_sq,), (k_sq,) = _max_sq_norms(q_ref[0], (True,)), _max_sq_norms(k, (True,))

    @pl.when(q_sq * k_sq * (scale * scale * 1.01) > SAFE_LOGIT ** 2)
    def _():
        def shifted(s):
            s = s * scale
            return jnp.exp(s - jnp.max(s, axis=-1, keepdims=True))
        attend(shifted)


def attention_c(q, mk, mv):
    b, t, _ = q.shape
    n_mem = mk.shape[1]
    tq = _block(t, 512)
    return pl.pallas_call(
        functools.partial(_attn_c_kernel, tq=tq),
        grid=(b, H_C),
        in_specs=[pl.BlockSpec((1, t, DH_C), lambda bi, hi: (bi, 0, hi)),
                  pl.BlockSpec((1, n_mem, DH_C), lambda bi, hi: (bi, 0, hi)),
                  pl.BlockSpec((1, n_mem, DH_C), lambda bi, hi: (bi, 0, hi))],
        out_specs=pl.BlockSpec((1, t, DH_C), lambda bi, hi: (bi, 0, hi)),
        out_shape=jax.ShapeDtypeStruct((b, t, H_C * DH_C), BF16),
        name="attention_c",
        compiler_params=_params("parallel", "arbitrary"),
    )(q, mk, mv)


def _merge_kernel(oa_ref, ob_ref, oc_ref, ga_ref, gb_ref, gc_ref, ba_ref, bb_ref, bc_ref,
                  wa_ref, wb_ref, wc_ref, o_ref):
    rows = _block(o_ref.shape[0], MERGE_CHUNK)
    for r0 in range(0, o_ref.shape[0], rows):
        sl = slice(r0, r0 + rows)

        def branch(o_r, g_r, b_r, w_r):
            gate = jax.nn.sigmoid(g_r[sl, :].astype(F32) + b_r[...])
            return gate * jnp.dot(o_r[sl, :], w_r[...], preferred_element_type=F32)

        merged = (branch(oa_ref, ga_ref, ba_ref, wa_ref) + branch(ob_ref, gb_ref, bb_ref, wb_ref)
                  + branch(oc_ref, gc_ref, bc_ref, wc_ref))
        o_ref[sl, :] = merged.astype(o_ref.dtype)


def merge_branches(oa, ob, oc, gate_logits, b_gate, wa, wb, wc):
    m, w_in = oa.shape
    d = wa.shape[1]
    bm = _block(m, MERGE_ROWS)
    row = lambda width: pl.BlockSpec((bm, width), lambda i: (i, 0))
    gate = lambda j: pl.BlockSpec((bm, d), lambda i: (i, j))
    bias = lambda j: pl.BlockSpec((1, d), lambda i: (0, j))
    weight = pl.BlockSpec((w_in, d), lambda i: (0, 0), pipeline_mode=pl.Buffered(1))
    bg = b_gate.reshape(1, N_BRANCH * d).astype(F32)
    return pl.pallas_call(
        _merge_kernel,
        grid=(m // bm,),
        in_specs=[row(w_in), row(w_in), row(w_in), gate(0), gate(1), gate(2),
                  bias(0), bias(1), bias(2), weight, weight, weight],
        out_specs=pl.BlockSpec((bm, d), lambda i: (i, 0)),
        out_shape=jax.ShapeDtypeStruct((m, d), BF16),
        name="merge_branches",
        compiler_params=_params("parallel"),
    )(oa, ob, oc, gate_logits, gate_logits, gate_logits, bg, bg, bg, wa, wb, wc)


def _proj_norm_res_kernel(*refs, next_norm, rows):
    if next_norm:
        a_ref, w_ref, x_ref, g_ref, g2_ref, o_ref, h_ref = refs
    else:
        a_ref, w_ref, x_ref, g_ref, o_ref = refs
    bm = a_ref.shape[0]
    for r0 in range(0, bm, rows):
        sl = slice(r0, r0 + rows)
        y = jnp.dot(a_ref[sl, :], w_ref[...], preferred_element_type=F32)
        ms = jnp.mean(y * y, axis=-1, keepdims=True)
        o = x_ref[sl, :] + y * lax.rsqrt(ms + EPS) * g_ref[...]
        o_ref[sl, :] = o
        if next_norm:
            ms2 = jnp.mean(o * o, axis=-1, keepdims=True)
            h_ref[sl, :] = (o * lax.rsqrt(ms2 + EPS) * g2_ref[...]).astype(h_ref.dtype)


def proj_norm_residual(a, w, x, g, bm, rows, next_g=None):
    m, k = a.shape
    d = w.shape[1]
    bm = _block(m, bm)
    rows = _block(bm, rows)
    vec = pl.BlockSpec((1, d), lambda i: (0, 0))
    row = pl.BlockSpec((bm, d), lambda i: (i, 0))
    in_specs = [pl.BlockSpec((bm, k), lambda i: (i, 0)),
                pl.BlockSpec((k, d), lambda i: (0, 0), pipeline_mode=pl.Buffered(1)), row, vec]
    args = [a, w, x, g.reshape(1, d).astype(F32)]
    out_specs, out_shape = row, jax.ShapeDtypeStruct((m, d), F32)
    if next_g is not None:
        in_specs.append(vec)
        args.append(next_g.reshape(1, d).astype(F32))
        out_specs = (row, row)
        out_shape = (out_shape, jax.ShapeDtypeStruct((m, d), BF16))
    return pl.pallas_call(
        functools.partial(_proj_norm_res_kernel, next_norm=next_g is not None, rows=rows),
        grid=(m // bm,),
        in_specs=in_specs,
        out_specs=out_specs,
        out_shape=out_shape,
        name="proj_norm_residual",
        compiler_params=_params("parallel"),
    )(*args)


def _ffn_in_kernel(h_ref, h2_ref, wa_ref, wb_ref, o_ref, o2_ref, wa_bf, wb_bf):
    def swiglu(h):
        a = jnp.dot(h, wa_bf[...], preferred_element_type=F32)
        b = jnp.dot(h, wb_bf[...], preferred_element_type=F32)
        return (jax.nn.silu(a) * b).astype(o_ref.dtype)

    @pl.when(pl.program_id(1) == 0)
    def _():
        wa_bf[...] = wa_ref[...].astype(BF16)
        wb_bf[...] = wb_ref[...].astype(BF16)
        o2_ref[...] = swiglu(h2_ref[...])

    o_ref[...] = swiglu(h_ref[...])


def ffn_in(h, extra, w):
    m, k = h.shape
    m2 = extra.shape[0]
    f = w.shape[1] // 2
    bm = _block(m, 1024)
    bf = 512
    assert f % bf == 0
    nf = f // bf
    return pl.pallas_call(
        _ffn_in_kernel,
        grid=(nf, m // bm),
        in_specs=[pl.BlockSpec((bm, k), lambda j, i: (i, 0)),
                  pl.BlockSpec((m2, k), lambda j, i: (0, 0)),
                  pl.BlockSpec((k, bf), lambda j, i: (0, j)),
                  pl.BlockSpec((k, bf), lambda j, i: (0, j + nf))],
        out_specs=[pl.BlockSpec((bm, bf), lambda j, i: (i, j)),
                   pl.BlockSpec((m2, bf), lambda j, i: (0, j))],
        out_shape=[jax.ShapeDtypeStruct((m, f), BF16), jax.ShapeDtypeStruct((m2, f), BF16)],
        scratch_shapes=[pltpu.VMEM((k, bf), BF16), pltpu.VMEM((k, bf), BF16)],
        name="ffn_in",
        compiler_params=_params("arbitrary", "arbitrary"),
    )(h, extra, w, w)


def _layer(xp, xd, mk_p, mv_p, mk_d, mv_d, caches, lam_init, norm_mix_pre, norm_mix_post, w_in, b_gate,
           lq1, lk1, lq2, lk2, subln_a, rel_bias, w_br_a, w_br_b, w_br_c, w_out,
           norm_ffn_pre, norm_ffn_post, w_ffn_in, w_ffn_out):
    d = xp.shape[-1]
    w_head = H_A * DK_A
    shapes = [xp.shape[:2], xd.shape[:2]]
    xs = [xp.reshape(-1, d), xd.reshape(-1, d)]
    h_d = rms_norm_bf16(xs[1], norm_mix_pre)
    h_p, qa_p, qa_d = norm_matmul_cols(xs[0], norm_mix_pre, w_in, 0, w_head, BF16, h_d)
    hs = [h_p, h_d]
    split = lambda outs: [o.reshape(*shp, w_head) for o, shp in zip(outs, shapes)]
    proj = lambda idx, dt: split(matmul_cols(hs[0], w_in, idx * w_head, w_head, dt, extra=hs[1]))
    q_a, k_a, v_a = split([qa_p, qa_d]), proj(1, F32), proj(2, F32)
    q_b, q_c = proj(3, BF16), proj(6, BF16)
    seq = shapes[0][1]
    keep = min(BAND_PAST, seq)
    *k_b, kb_tail = matmul_cols(hs[0], w_in, 4 * w_head, w_head, F32, extra=hs[1], tail=(seq, keep))
    *v_b, vb_tail = matmul_cols(hs[0], w_in, 5 * w_head, w_head, F32, extra=hs[1], tail=(seq, keep))
    k_b, v_b = split(k_b), split(v_b)
    tails = [t.reshape(shapes[0][0], keep, w_head) for t in (kb_tail, vb_tail)]
    gates = matmul_cols(hs[0], w_in, 7 * w_head, N_BRANCH * d, BF16, extra=hs[1])

    ca_k, ca_v, cb_k, cb_v = caches
    o_a = [attention_a(q_a[0], k_a[0], v_a[0], lq1, lk1, lq2, lk2, subln_a, lam_init),
           attention_a_decode(q_a[1], k_a[1], v_a[1], ca_k, ca_v, lq1, lk1, lq2, lk2, subln_a, lam_init)]
    o_b = [attention_b(q_b[0], k_b[0], v_b[0], rel_bias),
           attention_b_decode(q_b[1], k_b[1], v_b[1], cb_k, cb_v, rel_bias, ca_k.shape[1])]
    o_c = [attention_c(q_c[0], mk_p, mv_p), attention_c(q_c[1], mk_d, mv_d)]

    x1, h2 = [], []
    for i in range(2):
        m = xs[i].shape[0]
        merged = merge_branches(o_a[i].reshape(m, -1), o_b[i].reshape(m, -1), o_c[i].reshape(m, -1),
                                gates[i], b_gate, w_br_a, w_br_b, w_br_c)
        a, b = proj_norm_residual(merged, w_out, xs[i], norm_mix_post, bm=512, rows=PROJ_ROWS,
                                  next_g=norm_ffn_pre)
        x1.append(a)
        h2.append(b)
    acts = ffn_in(h2[0], h2[1], w_ffn_in)
    ys = [proj_norm_residual(acts[i], w_ffn_out, x1[i], norm_ffn_post, bm=256, rows=256).reshape(*shapes[i], d)
          for i in range(2)]
    return (ys[0], k_a[0], v_a[0], *tails), (ys[1], k_a[1], v_a[1], k_b[1], v_b[1])


def kernel(x_prompt, x_sample, cache_a_k, cache_a_v, cache_b_k, cache_b_v, cache_mem_k, cache_mem_v, mem_prompt, norm_mix_pre, norm_mix_post, norm_mem, w_in, b_gate, lambda_q1, lambda_k1, lambda_q2, lambda_k2, subln_a, rel_bias_b, w_mem_kv, w_br_a, w_br_b, w_br_c, w_out, norm_ffn_pre, norm_ffn_post, w_ffn_in, w_ffn_out):
    depth = w_in.shape[0]
    bsz, s, d = x_prompt.shape
    n_mem = mem_prompt.shape[1]
    lb_prompt = min(BAND_PAST, s)
    yp, ys = x_prompt, x_sample
    outs = [[] for _ in range(10)]
    heads = lambda a, nh: a.reshape(a.shape[0], a.shape[1], nh, a.shape[2] // nh)
    flat = lambda a: a.reshape(a.shape[0], a.shape[1], -1)
    for l in range(depth):
        lam_init = 0.8 - 0.6 * math.exp(-0.3 * l)
        bf = lambda w: w[l].astype(BF16)
        shared = (lam_init, norm_mix_pre[l], norm_mix_post[l], w_in[l], b_gate[l],
                  lambda_q1[l], lambda_k1[l], lambda_q2[l], lambda_k2[l], subln_a[l], rel_bias_b[l],
                  bf(w_br_a), bf(w_br_b), bf(w_br_c), bf(w_out),
                  norm_ffn_pre[l], norm_ffn_post[l], w_ffn_in[l], bf(w_ffn_out))

        mem_n = rms_norm_bf16(mem_prompt.reshape(bsz * n_mem, d), norm_mem[l])
        w_c = H_C * DH_C
        mk_p = matmul_cols(mem_n, w_mem_kv[l], 0, w_c, F32)[0].reshape(bsz, n_mem, w_c)
        mv_p = matmul_cols(mem_n, w_mem_kv[l], w_c, w_c, F32)[0].reshape(bsz, n_mem, w_c)
        caches = (cache_a_k[l], cache_a_v[l], cache_b_k[l], cache_b_v[l])
        (yp, ka, va, kb, vb), (ys, ka_d, va_d, kb_d, vb_d) = _layer(
            yp, ys, mk_p, mv_p, flat(cache_mem_k[l]), flat(cache_mem_v[l]), caches, *shared)
        new = [heads(ka, H_A), heads(va, H_A), heads(kb, H_B), heads(vb, H_B), heads(mk_p, H_C), heads(mv_p, H_C),
               heads(ka_d, H_A), heads(va_d, H_A), heads(kb_d, H_B), heads(vb_d, H_B)]
        for o, a in zip(outs, new):
            o.append(a)
    return (yp, ys) + tuple(jnp.stack(o) for o in outs)
```

```python
import functools
import math

import jax
import jax.numpy as jnp
from jax import lax
from jax.experimental import pallas as pl
from jax.experimental.pallas import tpu as pltpu

F32 = jnp.float32
BF16 = jnp.bfloat16

CHUNK = 64
H_A = 8
DH_A = 64
DK_A = 2 * DH_A
DV_A = 128
H_B = 8
DH_B = 128
N_PREV_CHUNKS = 8
BAND_PAST = N_PREV_CHUNKS * CHUNK
MAX_REL = 128
H_C = 4
DH_C = 256
N_BRANCH = 3
EPS = 1e-6
NEG_INF = -1e30

LANES = 128
VMEM_LIMIT = 60 * 1024 * 1024
BAND_GROUP = 256
A_BLOCK = 256
MM_ROWS = 2048
MERGE_ROWS = 256
MERGE_CHUNK = 256
PROJ_ROWS = 128
SAFE_LOGIT = 40.0

CONTRACT_LAST = (((1,), (1,)), ((), ()))


def _params(*sem):
    return pltpu.CompilerParams(dimension_semantics=sem, vmem_limit_bytes=VMEM_LIMIT)


def _block(n, target):
    if n <= target:
        return n
    b = target
    while n % b:
        b //= 2
    return b


def _rms_kernel(x_ref, g_ref, o_ref):
    x = x_ref[...]
    ms = jnp.mean(x * x, axis=-1, keepdims=True)
    o_ref[...] = (x * lax.rsqrt(ms + EPS) * g_ref[...]).astype(o_ref.dtype)


def rms_norm_bf16(x, g):
    m, d = x.shape
    bm = _block(m, 512)
    return pl.pallas_call(
        _rms_kernel,
        grid=(m // bm,),
        in_specs=[pl.BlockSpec((bm, d), lambda i: (i, 0)),
                  pl.BlockSpec((1, d), lambda i: (0, 0))],
        out_specs=pl.BlockSpec((bm, d), lambda i: (i, 0)),
        out_shape=jax.ShapeDtypeStruct((m, d), BF16),
        name="rms_norm",
        compiler_params=_params("parallel"),
    )(x, g.reshape(1, d))


def _mm_kernel(*refs, has_extra, tail, head_major):
    refs = list(refs)
    a_ref = refs.pop(0)
    a2_ref = refs.pop(0) if has_extra else None
    w_ref = refs.pop(0)
    o_ref = refs.pop(0)
    o2_ref = refs.pop(0) if has_extra else None
    t_ref = refs.pop(0) if tail else None
    wb_ref, = refs

    def product(x_ref, y_ref):
        rows = _block(x_ref.shape[0], 1024)
        for r0 in range(0, x_ref.shape[0], rows):
            acc = jnp.dot(x_ref[r0:r0 + rows, :], wb_ref[...], preferred_element_type=F32)
            if head_major:
                heads = acc.shape[1] // LANES
                for h in range(heads):
                    y_ref[pl.ds(r0 * heads + h, rows, stride=heads), :] = (
                        acc[:, h * LANES:(h + 1) * LANES].astype(y_ref.dtype))
            else:
                y_ref[r0:r0 + rows, :] = acc.astype(y_ref.dtype)

    @pl.when(pl.program_id(1) == 0)
    def _():
        wb_ref[...] = w_ref[...].astype(BF16)
        if has_extra:
            product(a2_ref, o2_ref)

    product(a_ref, o_ref)
    if tail:
        blocks, rows = tail

        @pl.when(pl.program_id(1) % blocks == blocks - 1)
        def _():
            t_ref[...] = o_ref[o_ref.shape[0] - rows:, :]


def matmul_cols(a, w, col0, n, out_dtype, extra=None, tail=None, head_major=False):
    m, k = a.shape
    bm = _block(m, MM_ROWS * 2 // jnp.dtype(out_dtype).itemsize)
    bn = _block(n, 1024)
    assert col0 % bn == 0
    off = col0 // bn
    in_specs = [pl.BlockSpec((bm, k), lambda j, i: (i, 0))]
    args = [a]
    if head_major:
        assert bn == n and tail is None
        heads = n // LANES
        out_block = lambda rows: pl.BlockSpec((rows * heads, LANES), lambda j, i: (i if rows == bm else 0, 0))
        out_array = lambda rows: jax.ShapeDtypeStruct((rows * heads, LANES), out_dtype)
    else:
        out_block = lambda rows: pl.BlockSpec((rows, bn), lambda j, i: (i if rows == bm else 0, j))
        out_array = lambda rows: jax.ShapeDtypeStruct((rows, n), out_dtype)
    out_specs = [out_block(bm)]
    out_shape = [out_array(m)]
    if extra is not None:
        m2 = extra.shape[0]
        assert m2 != bm
        in_specs.append(pl.BlockSpec((m2, k), lambda j, i: (0, 0)))
        args.append(extra)
        out_specs.append(out_block(m2))
        out_shape.append(out_array(m2))
    if tail is not None:
        seq, keep = tail
        assert seq % bm == 0 and keep <= bm and m % seq == 0
        blocks = seq // bm
        out_specs.append(pl.BlockSpec((keep, bn), lambda j, i: (i // blocks, j)))
        out_shape.append(jax.ShapeDtypeStruct((m // seq * keep, n), out_dtype))
        tail = (blocks, keep)
    in_specs.append(pl.BlockSpec((k, bn), lambda j, i: (0, j + off)))
    args.append(w)
    return pl.pallas_call(
        functools.partial(_mm_kernel, has_extra=extra is not None, tail=tail, head_major=head_major),
        grid=(n // bn, m // bm),
        in_specs=in_specs,
        out_specs=out_specs,
        out_shape=out_shape,
        scratch_shapes=[pltpu.VMEM((k, bn), BF16)],
        name="matmul_cols",
        compiler_params=_params("arbitrary", "arbitrary"),
    )(*args)


def _norm_mm_kernel(x_ref, g_ref, a2_ref, w_ref, h_ref, o_ref, o2_ref, wb_ref, *, rows):
    @pl.when(pl.program_id(0) == 0)
    def _():
        wb_ref[...] = w_ref[...].astype(BF16)
        o2_ref[...] = jnp.dot(a2_ref[...], wb_ref[...], preferred_element_type=F32).astype(o2_ref.dtype)

    for r0 in range(0, x_ref.shape[0], rows):
        x = x_ref[r0:r0 + rows, :]
        ms = jnp.mean(x * x, axis=-1, keepdims=True)
        h = (x * lax.rsqrt(ms + EPS) * g_ref[...]).astype(BF16)
        h_ref[r0:r0 + rows, :] = h
        o_ref[r0:r0 + rows, :] = jnp.dot(h, wb_ref[...], preferred_element_type=F32).astype(o_ref.dtype)


def norm_matmul_cols(x, g, w, col0, n, out_dtype, extra):
    m, k = x.shape
    m2 = extra.shape[0]
    bm = _block(m, 512)
    assert col0 % n == 0
    off = col0 // n
    return pl.pallas_call(
        functools.partial(_norm_mm_kernel, rows=_block(bm, 128)),
        grid=(m // bm,),
        in_specs=[pl.BlockSpec((bm, k), lambda i: (i, 0)),
                  pl.BlockSpec((1, k), lambda i: (0, 0)),
                  pl.BlockSpec((m2, k), lambda i: (0, 0)),
                  pl.BlockSpec((k, n), lambda i: (0, off))],
        out_specs=[pl.BlockSpec((bm, k), lambda i: (i, 0)),
                   pl.BlockSpec((bm, n), lambda i: (i, 0)),
                   pl.BlockSpec((m2, n), lambda i: (0, 0))],
        out_shape=[jax.ShapeDtypeStruct((m, k), BF16), jax.ShapeDtypeStruct((m, n), out_dtype),
                   jax.ShapeDtypeStruct((m2, n), out_dtype)],
        scratch_shapes=[pltpu.VMEM((k, n), BF16)],
        name="norm_matmul_cols",
        compiler_params=_params("arbitrary"),
    )(x, g.reshape(1, k).astype(F32), extra, w)


def _diff_lambda(lq1_ref, lk1_ref, lq2_ref, lk2_ref, lam_init):
    return (jnp.exp(jnp.sum(lq1_ref[...] * lk1_ref[...], keepdims=True))
            - jnp.exp(jnp.sum(lq2_ref[...] * lk2_ref[...], keepdims=True)) + lam_init)


def _head_norm(o, sub_ref, lam_init):
    ms = jnp.mean(o * o, axis=-1, keepdims=True)
    return o * lax.rsqrt(ms + EPS) * sub_ref[...] * (1.0 - lam_init)


def _own_alibi(n, slope):
    r = lax.broadcasted_iota(jnp.int32, (n, n), 0)
    c = lax.broadcasted_iota(jnp.int32, (n, n), 1)
    bias = slope * (r - jnp.abs(r - c)).astype(F32)
    return jnp.where(c // CHUNK <= r // CHUNK, bias, NEG_INF)


def _head_rows(ref):
    return ref[0, pl.ds(pl.program_id(1), ref.shape[1] // H_A, stride=H_A), :]


def _max_sq_norms(x, lane_masks):
    sq = jnp.square(x.astype(F32))
    return [jnp.max(jnp.sum(jnp.where(mask, sq, 0.0), axis=-1, keepdims=True)) for mask in lane_masks]


def _attn_a_kernel(q_ref, k_ref, v_ref, slope_ref, kf_ref, qf_ref, lq1_ref, lk1_ref, lq2_ref, lk2_ref,
                   sub_ref, o_ref, ke_scr, ve_scr, *, tq, lam_init):
    args = (q_ref, k_ref, v_ref, slope_ref, lq1_ref, lk1_ref, lq2_ref, lk2_ref, sub_ref, o_ref)
    _attn_a_bounded(*args, kf_ref, qf_ref, ke_scr, ve_scr, tq=tq, lam_init=lam_init)
    first_half = lax.broadcasted_iota(jnp.int32, (1, DK_A), 1) < DH_A
    q_sq = _max_sq_norms(q_ref[0] * (DH_A ** -0.5), (first_half, ~first_half))
    k_sq = _max_sq_norms(_head_rows(k_ref).astype(BF16), (first_half, ~first_half))
    bound_sq = jnp.maximum(q_sq[0] * k_sq[0], q_sq[1] * k_sq[1])

    @pl.when(bound_sq > SAFE_LOGIT ** 2 / 1.01)
    def _():
        _attn_a_general(*args, tq=tq, lam_init=lam_init)


def _position_features(t):
    pos = jnp.arange(t, dtype=jnp.int32)[:, None]
    lo = pos & 7
    hi = (pos - lo).astype(F32)
    lo = lo.astype(F32)
    one = jnp.ones((t, 1), F32)
    pad = jnp.zeros((t, LANES - 4), F32)
    k_side = jnp.concatenate([hi, lo, one, one, pad], axis=1).astype(BF16)
    q_side = jnp.concatenate([one, one, -hi, -lo, pad], axis=1)
    return k_side, q_side


def _attn_a_bounded(q_ref, k_ref, v_ref, slope_ref, lq1_ref, lk1_ref, lq2_ref, lk2_ref, sub_ref, o_ref,
                    kf_ref, qf_ref, ke_scr, ve_scr, *, tq, lam_init):
    t = q_ref.shape[1]
    slope = slope_ref[0][:, :1]
    lam = _diff_lambda(lq1_ref, lk1_ref, lq2_ref, lk2_ref, lam_init)
    first_half = lax.broadcasted_iota(jnp.int32, (1, DK_A), 1) < DH_A
    ke_scr[:, :DK_A] = _head_rows(k_ref).astype(BF16)
    ke_scr[:, DK_A:] = kf_ref[...]
    ve_scr[:, :DV_A] = _head_rows(v_ref).astype(BF16)
    ve_scr[:, DV_A:] = jnp.ones((t, DV_A), BF16)

    r = lax.broadcasted_iota(jnp.int32, (tq, tq), 0)
    c = lax.broadcasted_iota(jnp.int32, (tq, tq), 1)
    own_fix = jnp.where(c // CHUNK <= r // CHUNK, -2.0 * slope * jnp.maximum(c - r, 0).astype(F32), NEG_INF)
    own_fix = jnp.concatenate([own_fix, own_fix], axis=0)

    for i in reversed(range(t // tq)):
        nb = i * tq
        q = q_ref[0, nb:nb + tq, :] * (DH_A ** -0.5)
        q_feat = (qf_ref[nb:nb + tq, :] * slope).astype(BF16)
        zero = jnp.zeros_like(q)
        qe = jnp.concatenate([jnp.concatenate([jnp.where(first_half, q, zero), q_feat], axis=1),
                              jnp.concatenate([jnp.where(first_half, zero, q), q_feat], axis=1)], axis=0)
        s_own = lax.dot_general(qe, ke_scr[nb:nb + tq, :], CONTRACT_LAST, preferred_element_type=F32)
        acc = jnp.dot(jnp.exp(s_own + own_fix).astype(BF16), ve_scr[nb:nb + tq, :],
                      preferred_element_type=F32)
        if nb:
            s_bef = lax.dot_general(qe, ke_scr[:nb, :], CONTRACT_LAST, preferred_element_type=F32)
            acc = acc + jnp.dot(jnp.exp(s_bef).astype(BF16), ve_scr[:nb, :], preferred_element_type=F32)
        o = acc[:, :DV_A] / acc[:, DV_A:]
        o = o[:tq] - lam * o[tq:]
        o_ref[0, nb:nb + tq, :] = _head_norm(o, sub_ref, lam_init).astype(o_ref.dtype)


def _attn_a_general(q_ref, k_ref, v_ref, slope_ref, lq1_ref, lk1_ref, lq2_ref, lk2_ref, sub_ref, o_ref,
                    *, tq, lam_init):
    t = q_ref.shape[1]
    n_blocks = t // tq
    slope = slope_ref[0][:, :1]
    lam = _diff_lambda(lq1_ref, lk1_ref, lq2_ref, lk2_ref, lam_init)
    k = _head_rows(k_ref).astype(BF16)
    v = _head_rows(v_ref).astype(BF16)
    first_half = lax.broadcasted_iota(jnp.int32, (1, DK_A), 1) < DH_A
    own_bias = _own_alibi(tq, slope)
    n_before_max = (n_blocks - 1) * tq
    if n_before_max:
        j = lax.broadcasted_iota(jnp.int32, (1, n_before_max), 1)
        before_bias = slope * (j - n_before_max).astype(F32)

    for i in range(n_blocks):
        nb = i * tq
        q = q_ref[0, nb:nb + tq, :] * (DH_A ** -0.5)
        k_own, v_own = k[nb:nb + tq], v[nb:nb + tq]

        def softmax_pv(qh):
            s_own = lax.dot_general(qh, k_own, CONTRACT_LAST, preferred_element_type=F32) + own_bias
            m = jnp.max(s_own, axis=-1, keepdims=True)
            if nb:
                s_bef = lax.dot_general(qh, k[:nb], CONTRACT_LAST, preferred_element_type=F32)
                s_bef = s_bef + before_bias[:, n_before_max - nb:]
                m = jnp.maximum(m, jnp.max(s_bef, axis=-1, keepdims=True))
            p_own = jnp.exp(s_own - m)
            l = jnp.sum(p_own, axis=-1, keepdims=True)
            pv = jnp.dot(p_own.astype(BF16), v_own, preferred_element_type=F32)
            if nb:
                p_bef = jnp.exp(s_bef - m)
                l = l + jnp.sum(p_bef, axis=-1, keepdims=True)
                pv = pv + jnp.dot(p_bef.astype(BF16), v[:nb], preferred_element_type=F32)
            return pv / l

        o = (softmax_pv(jnp.where(first_half, q, jnp.zeros_like(q)))
             - lam * softmax_pv(jnp.where(first_half, jnp.zeros_like(q), q)))
        o_ref[0, nb:nb + tq, :] = _head_norm(o, sub_ref, lam_init).astype(o_ref.dtype)


def _row(a):
    return a.reshape(1, -1).astype(F32)


def _const_spec(shape):
    return pl.BlockSpec(shape, lambda *_: (0,) * len(shape))


def attention_a(q, k, v, lq1, lk1, lq2, lk2, subln, lam_init):
    b, t, _ = q.shape
    tq = _block(t, A_BLOCK)
    assert tq % CHUNK == 0
    slopes = jnp.asarray([[[2.0 ** (-8.0 * (hh + 1) / H_A)] * LANES] for hh in range(H_A)], dtype=F32)
    head = lambda width: pl.BlockSpec((1, t, width), lambda bi, hi: (bi, 0, hi))
    all_heads = pl.BlockSpec((1, t * H_A, DK_A), lambda bi, hi: (bi, 0, 0))
    k_feat, q_feat = _position_features(t)
    return pl.pallas_call(
        functools.partial(_attn_a_kernel, tq=tq, lam_init=lam_init),
        grid=(b, H_A),
        in_specs=[head(DK_A), all_heads, all_heads,
                  pl.BlockSpec((1, 1, LANES), lambda bi, hi: (hi, 0, 0)),
                  _const_spec((t, LANES)), _const_spec((t, LANES)),
                  _const_spec((1, DH_A)), _const_spec((1, DH_A)), _const_spec((1, DH_A)),
                  _const_spec((1, DH_A)), _const_spec((1, DV_A))],
        out_specs=head(DV_A),
        out_shape=jax.ShapeDtypeStruct((b, t, H_A * DV_A), BF16),
        scratch_shapes=[pltpu.VMEM((t, DK_A + LANES), BF16), pltpu.VMEM((t, 2 * DV_A), BF16)],
        name="attention_a",
        compiler_params=_params("parallel", "arbitrary"),
    )(q, k, v, slopes, k_feat, q_feat, _row(lq1), _row(lk1), _row(lq2), _row(lk2), _row(subln))


def _attn_a_decode_kernel(q_ref, kn_ref, vn_ref, ck_ref, cv_ref, lq1_ref, lk1_ref, lq2_ref, lk2_ref,
                          sub_ref, o_ref, *, lam_init):
    t = q_ref.shape[1]
    p_len = ck_ref.shape[1] // H_A
    lam = _diff_lambda(lq1_ref, lk1_ref, lq2_ref, lk2_ref, lam_init)
    first_half = lax.broadcasted_iota(jnp.int32, (1, DK_A), 1) < DH_A
    j = lax.broadcasted_iota(jnp.int32, (1, p_len), 1)
    before_dist = (j - p_len).astype(F32)
    for h in range(H_A):
        slope = 2.0 ** (-8.0 * (h + 1) / H_A)
        cols = slice(h * DK_A, (h + 1) * DK_A)
        q = q_ref[0, :, cols] * (DH_A ** -0.5)
        kc = ck_ref[0, pl.ds(h, p_len, stride=H_A), :].astype(BF16)
        vc = cv_ref[0, pl.ds(h, p_len, stride=H_A), :].astype(BF16)
        kn = kn_ref[0, pl.ds(h, t, stride=H_A), :].astype(BF16)
        vn = vn_ref[0, pl.ds(h, t, stride=H_A), :].astype(BF16)
        own_bias = _own_alibi(t, slope)
        before_bias = slope * before_dist

        def softmax_pv(qh):
            s_new = lax.dot_general(qh, kn, CONTRACT_LAST, preferred_element_type=F32) + own_bias
            s_old = lax.dot_general(qh, kc, CONTRACT_LAST, preferred_element_type=F32) + before_bias
            m = jnp.maximum(jnp.max(s_new, axis=-1, keepdims=True), jnp.max(s_old, axis=-1, keepdims=True))
            p_new = jnp.exp(s_new - m)
            p_old = jnp.exp(s_old - m)
            l = jnp.sum(p_new, axis=-1, keepdims=True) + jnp.sum(p_old, axis=-1, keepdims=True)
            pv = (jnp.dot(p_new.astype(BF16), vn, preferred_element_type=F32)
                  + jnp.dot(p_old.astype(BF16), vc, preferred_element_type=F32))
            return pv / l

        o = (softmax_pv(jnp.where(first_half, q, jnp.zeros_like(q)))
             - lam * softmax_pv(jnp.where(first_half, jnp.zeros_like(q), q)))
        o_ref[0, :, cols] = _head_norm(o, sub_ref, lam_init).astype(o_ref.dtype)


def attention_a_decode(q, k_new, v_new, cache_k, cache_v, lq1, lk1, lq2, lk2, subln, lam_init):
    b, t, w = q.shape
    p_len = cache_k.shape[1]
    assert p_len % CHUNK == 0 and t <= CHUNK
    rows = lambda a: a.reshape(b, p_len * H_A, a.shape[-1])
    new = pl.BlockSpec((1, t, w), lambda bi: (bi, 0, 0))
    new_kv = pl.BlockSpec((1, t * H_A, DK_A), lambda bi: (bi, 0, 0))
    old = pl.BlockSpec((1, p_len * H_A, DK_A), lambda bi: (bi, 0, 0))
    return pl.pallas_call(
        functools.partial(_attn_a_decode_kernel, lam_init=lam_init),
        grid=(b,),
        in_specs=[new, new_kv, new_kv, old, old,
                  _const_spec((1, DH_A)), _const_spec((1, DH_A)), _const_spec((1, DH_A)),
                  _const_spec((1, DH_A)), _const_spec((1, DV_A))],
        out_specs=new,
        out_shape=jax.ShapeDtypeStruct((b, t, w), BF16),
        name="attention_a_decode",
        compiler_params=_params("parallel"),
    )(q, k_new, v_new, rows(cache_k), rows(cache_v), _row(lq1), _row(lk1), _row(lq2), _row(lk2), _row(subln))


def _band_tile(tq):
    pad = -(-tq // LANES) * LANES
    return BAND_PAST + pad, pad


def _band_bias_mask(g_row, tq):
    tile_w, pad = _band_tile(tq)
    width = g_row.shape[1]
    assert width == tile_w + pad
    rolled = pltpu.roll(jnp.broadcast_to(g_row, (tq, width)), width - pad, 1, stride=1, stride_axis=0)
    r = lax.broadcasted_iota(jnp.int32, (tq, tile_w), 0)
    c = lax.broadcasted_iota(jnp.int32, (tq, tile_w), 1)
    qch = r // CHUNK
    kch = c // CHUNK - N_PREV_CHUNKS
    valid = (kch <= qch) & (kch >= qch - N_PREV_CHUNKS)
    return jnp.where(valid, rolled[:, :tile_w], NEG_INF)


def _toeplitz_rows(rel_bias, tq):
    tile_w, pad = _band_tile(tq)
    width = tile_w + pad
    n_lo = pad + BAND_PAST - MAX_REL
    n_hi = max(width - n_lo - (2 * MAX_REL + 1), 0)
    lo = jnp.broadcast_to(rel_bias[:, :1], (H_B, n_lo))
    hi = jnp.broadcast_to(rel_bias[:, -1:], (H_B, n_hi))
    g = jnp.concatenate([lo, rel_bias, hi], axis=1)[:, :width]
    return g.reshape(H_B, 1, width).astype(F32)


def _attn_b_kernel(q_ref, k_ref, v_ref, g_ref, o_ref, ve_scr, *, tq):
    t = q_ref.shape[1]
    tile_w, _ = _band_tile(tq)
    scale = DH_B ** -0.5
    bias_mask = _band_bias_mask(g_ref[0], tq)

    def groups():
        for gi in range(t // tq):
            qa = gi * tq
            lo = max(qa - BAND_PAST, 0)
            hi = qa + tq
            off = lo - qa + BAND_PAST
            assert off % LANES == 0 and off + hi - lo == tile_w
            yield slice(qa, hi), slice(lo, hi), off

    log2e = math.log2(math.e)
    k = k_ref[0].astype(BF16)
    ve_scr[:, :DH_B] = v_ref[0].astype(BF16)
    ve_scr[:, DH_B:] = jnp.ones((t, DH_B), BF16)
    bias2 = bias_mask * log2e
    for rows, keys, off in groups():
        s = lax.dot_general(q_ref[0, rows, :], k[keys], CONTRACT_LAST, preferred_element_type=F32)
        p = jnp.exp2(s * (scale * log2e) + bias2[:, off:])
        acc = jnp.dot(p.astype(BF16), ve_scr[keys, :], preferred_element_type=F32)
        o_ref[0, rows, :] = (acc[:, :DH_B] / acc[:, DH_B:]).astype(o_ref.dtype)

    (q_sq,), (k_sq,) = _max_sq_norms(q_ref[0], (True,)), _max_sq_norms(k, (True,))
    room = SAFE_LOGIT - jnp.max(jnp.abs(g_ref[0]))
    safe = jnp.logical_and(room > 0.0, q_sq * k_sq * (scale * scale * 1.01) <= room * room)

    @pl.when(jnp.logical_not(safe))
    def _():
        k = k_ref[0].astype(BF16)
        v = v_ref[0].astype(BF16)
        for rows, keys, off in groups():
            s = lax.dot_general(q_ref[0, rows, :], k[keys], CONTRACT_LAST, preferred_element_type=F32) * scale
            s = s + bias_mask[:, off:]
            m = jnp.max(s, axis=-1, keepdims=True)
            p = jnp.exp(s - m)
            l = jnp.sum(p, axis=-1, keepdims=True)
            o = jnp.dot(p.astype(BF16), v[keys], preferred_element_type=F32) / l
            o_ref[0, rows, :] = o.astype(o_ref.dtype)


def attention_b(q, k, v, rel_bias):
    b, t, _ = q.shape
    tq = _block(t, BAND_GROUP)
    assert tq % LANES == 0
    g = _toeplitz_rows(rel_bias, tq)
    head = pl.BlockSpec((1, t, DH_B), lambda bi, hi: (bi, 0, hi))
    return pl.pallas_call(
        functools.partial(_attn_b_kernel, tq=tq),
        grid=(b, H_B),
        in_specs=[head, head, head, pl.BlockSpec((1, 1, g.shape[2]), lambda bi, hi: (hi, 0, 0))],
        out_specs=head,
        out_shape=jax.ShapeDtypeStruct((b, t, H_B * DH_B), BF16),
        scratch_shapes=[pltpu.VMEM((t, 2 * DH_B), BF16)],
        name="attention_b",
        compiler_params=_params("parallel", "arbitrary"),
    )(q, k, v, g)


def _attn_b_decode_kernel(q_ref, kn_ref, vn_ref, ck_ref, cv_ref, g_ref, o_ref):
    t = q_ref.shape[1]
    scale = DH_B ** -0.5
    for h in range(H_B):
        cols = slice(h * DH_B, (h + 1) * DH_B)
        q = q_ref[0, :, cols]
        kc = ck_ref[0, pl.ds(h, BAND_PAST, stride=H_B), :].astype(BF16)
        vc = cv_ref[0, pl.ds(h, BAND_PAST, stride=H_B), :].astype(BF16)
        kn = kn_ref[0, :, cols].astype(BF16)
        vn = vn_ref[0, :, cols].astype(BF16)
        bias_mask = _band_bias_mask(g_ref[h], t)
        s_old = lax.dot_general(q, kc, CONTRACT_LAST, preferred_element_type=F32) * scale
        s_old = s_old + bias_mask[:, :BAND_PAST]
        s_new = lax.dot_general(q, kn, CONTRACT_LAST, preferred_element_type=F32) * scale
        s_new = s_new + bias_mask[:, BAND_PAST:BAND_PAST + t]
        m = jnp.maximum(jnp.max(s_new, axis=-1, keepdims=True), jnp.max(s_old, axis=-1, keepdims=True))
        p_new = jnp.exp(s_new - m)
        p_old = jnp.exp(s_old - m)
        l = jnp.sum(p_new, axis=-1, keepdims=True) + jnp.sum(p_old, axis=-1, keepdims=True)
        pv = (jnp.dot(p_new.astype(BF16), vn, preferred_element_type=F32)
              + jnp.dot(p_old.astype(BF16), vc, preferred_element_type=F32))
        o_ref[0, :, cols] = (pv / l).astype(o_ref.dtype)


def attention_b_decode(q, k_new, v_new, cache_k, cache_v, rel_bias, p_len):
    b, t, w = q.shape
    assert cache_k.shape[1] == BAND_PAST and p_len % CHUNK == 0 and p_len >= BAND_PAST and t <= CHUNK
    g = _toeplitz_rows(rel_bias, t)
    rows = lambda a: a.reshape(b, BAND_PAST * H_B, a.shape[-1])
    new = pl.BlockSpec((1, t, w), lambda bi: (bi, 0, 0))
    old = pl.BlockSpec((1, BAND_PAST * H_B, DH_B), lambda bi: (bi, 0, 0))
    return pl.pallas_call(
        _attn_b_decode_kernel,
        grid=(b,),
        in_specs=[new, new, new, old, old, _const_spec(g.shape)],
        out_specs=new,
        out_shape=jax.ShapeDtypeStruct((b, t, w), BF16),
        name="attention_b_decode",
        compiler_params=_params("parallel"),
    )(q, k_new, v_new, rows(cache_k), rows(cache_v), g)


def _attn_c_kernel(q_ref, k_ref, v_ref, o_ref, *, tq):
    t = q_ref.shape[1]
    scale = DH_C ** -0.5
    k = k_ref[0].astype(BF16)
    v = v_ref[0].astype(BF16)

    def attend(probabilities):
        for i in range(t // tq):
            rows = slice(i * tq, (i + 1) * tq)
            s = lax.dot_general(q_ref[0, rows, :], k, CONTRACT_LAST, preferred_element_type=F32)
            p = probabilities(s)
            l = jnp.sum(p, axis=-1, keepdims=True)
            o = jnp.dot(p.astype(BF16), v, preferred_element_type=F32) / l
            o_ref[0, rows, :] = o.astype(o_ref.dtype)

    attend(lambda s: jnp.exp2(s * (scale * math.log2(math.e))))
    (q_sq,), (k_sq,) = _max_sq_norms(q_ref[0], (True,)), _max_sq_norms(k, (True,))

    @pl.when(q_sq * k_sq * (scale * scale * 1.01) > SAFE_LOGIT ** 2)
    def _():
        def shifted(s):
            s = s * scale
            return jnp.exp(s - jnp.max(s, axis=-1, keepdims=True))
        attend(shifted)


def attention_c(q, mk, mv):
    b, t, _ = q.shape
    n_mem = mk.shape[1]
    tq = _block(t, 512)
    return pl.pallas_call(
        functools.partial(_attn_c_kernel, tq=tq),
        grid=(b, H_C),
        in_specs=[pl.BlockSpec((1, t, DH_C), lambda bi, hi: (bi, 0, hi)),
                  pl.BlockSpec((1, n_mem, DH_C), lambda bi, hi: (bi, 0, hi)),
                  pl.BlockSpec((1, n_mem, DH_C), lambda bi, hi: (bi, 0, hi))],
        out_specs=pl.BlockSpec((1, t, DH_C), lambda bi, hi: (bi, 0, hi)),
        out_shape=jax.ShapeDtypeStruct((b, t, H_C * DH_C), BF16),
        name="attention_c",
        compiler_params=_params("parallel", "arbitrary"),
    )(q, mk, mv)


def _merge_kernel(oa_ref, ob_ref, oc_ref, ga_ref, gb_ref, gc_ref, ba_ref, bb_ref, bc_ref,
                  wa_ref, wb_ref, wc_ref, o_ref):
    rows = _block(o_ref.shape[0], MERGE_CHUNK)
    for r0 in range(0, o_ref.shape[0], rows):
        sl = slice(r0, r0 + rows)

        def branch(o_r, g_r, b_r, w_r):
            gate = jax.nn.sigmoid(g_r[sl, :].astype(F32) + b_r[...])
            return gate * jnp.dot(o_r[sl, :], w_r[...], preferred_element_type=F32)

        merged = (branch(oa_ref, ga_ref, ba_ref, wa_ref) + branch(ob_ref, gb_ref, bb_ref, wb_ref)
                  + branch(oc_ref, gc_ref, bc_ref, wc_ref))
        o_ref[sl, :] = merged.astype(o_ref.dtype)


def merge_branches(oa, ob, oc, gate_logits, b_gate, wa, wb, wc):
    m, w_in = oa.shape
    d = wa.shape[1]
    bm = _block(m, MERGE_ROWS)
    row = lambda width: pl.BlockSpec((bm, width), lambda i: (i, 0))
    gate = lambda j: pl.BlockSpec((bm, d), lambda i: (i, j))
    bias = lambda j: pl.BlockSpec((1, d), lambda i: (0, j))
    weight = pl.BlockSpec((w_in, d), lambda i: (0, 0), pipeline_mode=pl.Buffered(1))
    bg = b_gate.reshape(1, N_BRANCH * d).astype(F32)
    return pl.pallas_call(
        _merge_kernel,
        grid=(m // bm,),
        in_specs=[row(w_in), row(w_in), row(w_in), gate(0), gate(1), gate(2),
                  bias(0), bias(1), bias(2), weight, weight, weight],
        out_specs=pl.BlockSpec((bm, d), lambda i: (i, 0)),
        out_shape=jax.ShapeDtypeStruct((m, d), BF16),
        name="merge_branches",
        compiler_params=_params("parallel"),
    )(oa, ob, oc, gate_logits, gate_logits, gate_logits, bg, bg, bg, wa, wb, wc)


def _proj_norm_res_kernel(*refs, next_norm, rows):
    if next_norm:
        a_ref, w_ref, x_ref, g_ref, g2_ref, o_ref, h_ref = refs
    else:
        a_ref, w_ref, x_ref, g_ref, o_ref = refs
    bm = a_ref.shape[0]
    for r0 in range(0, bm, rows):
        sl = slice(r0, r0 + rows)
        y = jnp.dot(a_ref[sl, :], w_ref[...], preferred_element_type=F32)
        ms = jnp.mean(y * y, axis=-1, keepdims=True)
        o = x_ref[sl, :] + y * lax.rsqrt(ms + EPS) * g_ref[...]
        o_ref[sl, :] = o
        if next_norm:
            ms2 = jnp.mean(o * o, axis=-1, keepdims=True)
            h_ref[sl, :] = (o * lax.rsqrt(ms2 + EPS) * g2_ref[...]).astype(h_ref.dtype)


def proj_norm_residual(a, w, x, g, bm, rows, next_g=None):
    m, k = a.shape
    d = w.shape[1]
    bm = _block(m, bm)
    rows = _block(bm, rows)
    vec = pl.BlockSpec((1, d), lambda i: (0, 0))
    row = pl.BlockSpec((bm, d), lambda i: (i, 0))
    in_specs = [pl.BlockSpec((bm, k), lambda i: (i, 0)),
                pl.BlockSpec((k, d), lambda i: (0, 0), pipeline_mode=pl.Buffered(1)), row, vec]
    args = [a, w, x, g.reshape(1, d).astype(F32)]
    out_specs, out_shape = row, jax.ShapeDtypeStruct((m, d), F32)
    if next_g is not None:
        in_specs.append(vec)
        args.append(next_g.reshape(1, d).astype(F32))
        out_specs = (row, row)
        out_shape = (out_shape, jax.ShapeDtypeStruct((m, d), BF16))
    return pl.pallas_call(
        functools.partial(_proj_norm_res_kernel, next_norm=next_g is not None, rows=rows),
        grid=(m // bm,),
        in_specs=in_specs,
        out_specs=out_specs,
        out_shape=out_shape,
        name="proj_norm_residual",
        compiler_params=_params("parallel"),
    )(*args)


def _ffn_in_kernel(h_ref, h2_ref, wa_ref, wb_ref, o_ref, o2_ref, wa_bf, wb_bf):
    def swiglu(h):
        a = jnp.dot(h, wa_bf[...], preferred_element_type=F32)
        b = jnp.dot(h, wb_bf[...], preferred_element_type=F32)
        return (jax.nn.silu(a) * b).astype(o_ref.dtype)

    @pl.when(pl.program_id(1) == 0)
    def _():
        wa_bf[...] = wa_ref[...].astype(BF16)
        wb_bf[...] = wb_ref[...].astype(BF16)
        o2_ref[...] = swiglu(h2_ref[...])

    o_ref[...] = swiglu(h_ref[...])


def ffn_in(h, extra, w):
    m, k = h.shape
    m2 = extra.shape[0]
    f = w.shape[1] // 2
    bm = _block(m, 1024)
    bf = 512
    assert f % bf == 0
    nf = f // bf
    return pl.pallas_call(
        _ffn_in_kernel,
        grid=(nf, m // bm),
        in_specs=[pl.BlockSpec((bm, k), lambda j, i: (i, 0)),
                  pl.BlockSpec((m2, k), lambda j, i: (0, 0)),
                  pl.BlockSpec((k, bf), lambda j, i: (0, j)),
                  pl.BlockSpec((k, bf), lambda j, i: (0, j + nf))],
        out_specs=[pl.BlockSpec((bm, bf), lambda j, i: (i, j)),
                   pl.BlockSpec((m2, bf), lambda j, i: (0, j))],
        out_shape=[jax.ShapeDtypeStruct((m, f), BF16), jax.ShapeDtypeStruct((m2, f), BF16)],
        scratch_shapes=[pltpu.VMEM((k, bf), BF16), pltpu.VMEM((k, bf), BF16)],
        name="ffn_in",
        compiler_params=_params("arbitrary", "arbitrary"),
    )(h, extra, w, w)


def _layer(xp, xd, mk_p, mv_p, mk_d, mv_d, caches, lam_init, norm_mix_pre, norm_mix_post, w_in, b_gate,
           lq1, lk1, lq2, lk2, subln_a, rel_bias, w_br_a, w_br_b, w_br_c, w_out,
           norm_ffn_pre, norm_ffn_post, w_ffn_in, w_ffn_out):
    d = xp.shape[-1]
    w_head = H_A * DK_A
    shapes = [xp.shape[:2], xd.shape[:2]]
    xs = [xp.reshape(-1, d), xd.reshape(-1, d)]
    h_d = rms_norm_bf16(xs[1], norm_mix_pre)
    h_p, qa_p, qa_d = norm_matmul_cols(xs[0], norm_mix_pre, w_in, 0, w_head, BF16, h_d)
    hs = [h_p, h_d]
    split = lambda outs: [o.reshape(*shp, w_head) for o, shp in zip(outs, shapes)]
    proj = lambda idx, dt: split(matmul_cols(hs[0], w_in, idx * w_head, w_head, dt, extra=hs[1]))
    q_a, q_b, q_c = split([qa_p, qa_d]), proj(3, BF16), proj(6, BF16)
    head_major = lambda idx: [o.reshape(shp[0], shp[1] * H_A, DK_A) for o, shp in zip(
        matmul_cols(hs[0], w_in, idx * w_head, w_head, F32, extra=hs[1], head_major=True), shapes)]
    k_a, v_a = head_major(1), head_major(2)
    seq = shapes[0][1]
    keep = min(BAND_PAST, seq)
    *k_b, kb_tail = matmul_cols(hs[0], w_in, 4 * w_head, w_head, F32, extra=hs[1], tail=(seq, keep))
    *v_b, vb_tail = matmul_cols(hs[0], w_in, 5 * w_head, w_head, F32, extra=hs[1], tail=(seq, keep))
    k_b, v_b = split(k_b), split(v_b)
    tails = [t.reshape(shapes[0][0], keep, w_head) for t in (kb_tail, vb_tail)]
    gates = matmul_cols(hs[0], w_in, 7 * w_head, N_BRANCH * d, BF16, extra=hs[1])

    ca_k, ca_v, cb_k, cb_v = caches
    o_a = [attention_a(q_a[0], k_a[0], v_a[0], lq1, lk1, lq2, lk2, subln_a, lam_init),
           attention_a_decode(q_a[1], k_a[1], v_a[1], ca_k, ca_v, lq1, lk1, lq2, lk2, subln_a, lam_init)]
    o_b = [attention_b(q_b[0], k_b[0], v_b[0], rel_bias),
           attention_b_decode(q_b[1], k_b[1], v_b[1], cb_k, cb_v, rel_bias, ca_k.shape[1])]
    o_c = [attention_c(q_c[0], mk_p, mv_p), attention_c(q_c[1], mk_d, mv_d)]

    x1, h2 = [], []
    for i in range(2):
        m = xs[i].shape[0]
        merged = merge_branches(o_a[i].reshape(m, -1), o_b[i].reshape(m, -1), o_c[i].reshape(m, -1),
                                gates[i], b_gate, w_br_a, w_br_b, w_br_c)
        a, b = proj_norm_residual(merged, w_out, xs[i], norm_mix_post, bm=512, rows=PROJ_ROWS,
                                  next_g=norm_ffn_pre)
        x1.append(a)
        h2.append(b)
    acts = ffn_in(h2[0], h2[1], w_ffn_in)
    ys = [proj_norm_residual(acts[i], w_ffn_out, x1[i], norm_ffn_post, bm=256, rows=256).reshape(*shapes[i], d)
          for i in range(2)]
    return (ys[0], k_a[0], v_a[0], *tails), (ys[1], k_a[1], v_a[1], k_b[1], v_b[1])


def kernel(x_prompt, x_sample, cache_a_k, cache_a_v, cache_b_k, cache_b_v, cache_mem_k, cache_mem_v, mem_prompt, norm_mix_pre, norm_mix_post, norm_mem, w_in, b_gate, lambda_q1, lambda_k1, lambda_q2, lambda_k2, subln_a, rel_bias_b, w_mem_kv, w_br_a, w_br_b, w_br_c, w_out, norm_ffn_pre, norm_ffn_post, w_ffn_in, w_ffn_out):
    depth = w_in.shape[0]
    bsz, s, d = x_prompt.shape
    n_mem = mem_prompt.shape[1]
    lb_prompt = min(BAND_PAST, s)
    yp, ys = x_prompt, x_sample
    outs = [[] for _ in range(10)]
    heads = lambda a, nh: a.reshape(a.shape[0], a.shape[1], nh, a.shape[2] // nh)
    flat = lambda a: a.reshape(a.shape[0], a.shape[1], -1)
    for l in range(depth):
        lam_init = 0.8 - 0.6 * math.exp(-0.3 * l)
        bf = lambda w: w[l].astype(BF16)
        shared = (lam_init, norm_mix_pre[l], norm_mix_post[l], w_in[l], b_gate[l],
                  lambda_q1[l], lambda_k1[l], lambda_q2[l], lambda_k2[l], subln_a[l], rel_bias_b[l],
                  bf(w_br_a), bf(w_br_b), bf(w_br_c), bf(w_out),
                  norm_ffn_pre[l], norm_ffn_post[l], w_ffn_in[l], bf(w_ffn_out))

        mem_n = rms_norm_bf16(mem_prompt.reshape(bsz * n_mem, d), norm_mem[l])
        w_c = H_C * DH_C
        mk_p = matmul_cols(mem_n, w_mem_kv[l], 0, w_c, F32)[0].reshape(bsz, n_mem, w_c)
        mv_p = matmul_cols(mem_n, w_mem_kv[l], w_c, w_c, F32)[0].reshape(bsz, n_mem, w_c)
        caches = (cache_a_k[l], cache_a_v[l], cache_b_k[l], cache_b_v[l])
        (yp, ka, va, kb, vb), (ys, ka_d, va_d, kb_d, vb_d) = _layer(
            yp, ys, mk_p, mv_p, flat(cache_mem_k[l]), flat(cache_mem_v[l]), caches, *shared)
        per_head = lambda a: a.reshape(a.shape[0], a.shape[1] // H_A, H_A, a.shape[2])
        new = [per_head(ka), per_head(va), heads(kb, H_B), heads(vb, H_B), heads(mk_p, H_C), heads(mv_p, H_C),
               per_head(ka_d), per_head(va_d), heads(kb_d, H_B), heads(vb_d, H_B)]
        for o, a in zip(outs, new):
            o.append(a)
    return (yp, ys) + tuple(jnp.stack(o) for o in outs)
```

```python
import functools
import math

import jax
import jax.numpy as jnp
from jax import lax
from jax.experimental import pallas as pl
from jax.experimental.pallas import tpu as pltpu

F32 = jnp.float32
BF16 = jnp.bfloat16

CHUNK = 64
H_A = 8
DH_A = 64
DK_A = 2 * DH_A
DV_A = 128
H_B = 8
DH_B = 128
N_PREV_CHUNKS = 8
BAND_PAST = N_PREV_CHUNKS * CHUNK
MAX_REL = 128
H_C = 4
DH_C = 256
N_BRANCH = 3
EPS = 1e-6
NEG_INF = -1e30

LANES = 128
VMEM_LIMIT = 56 * 1024 * 1024
BAND_GROUP = 256
A_BLOCK = 256
MM_ROWS = 2048
MERGE_ROWS = 256
MERGE_CHUNK = 256
PROJ_ROWS = 128
SAFE_LOGIT = 40.0

CONTRACT_LAST = (((1,), (1,)), ((), ()))


def _params(*sem):
    return pltpu.CompilerParams(dimension_semantics=sem, vmem_limit_bytes=VMEM_LIMIT)


def _block(n, target):
    if n <= target:
        return n
    b = target
    while n % b:
        b //= 2
    return b


def _rms_kernel(x_ref, g_ref, o_ref):
    x = x_ref[...]
    ms = jnp.mean(x * x, axis=-1, keepdims=True)
    o_ref[...] = (x * lax.rsqrt(ms + EPS) * g_ref[...]).astype(o_ref.dtype)


def rms_norm_bf16(x, g):
    m, d = x.shape
    bm = _block(m, 512)
    return pl.pallas_call(
        _rms_kernel,
        grid=(m // bm,),
        in_specs=[pl.BlockSpec((bm, d), lambda i: (i, 0)),
                  pl.BlockSpec((1, d), lambda i: (0, 0))],
        out_specs=pl.BlockSpec((bm, d), lambda i: (i, 0)),
        out_shape=jax.ShapeDtypeStruct((m, d), BF16),
        name="rms_norm",
        compiler_params=_params("parallel"),
    )(x, g.reshape(1, d))


def _mm_kernel(*refs, has_extra, tail, head_major):
    refs = list(refs)
    a_ref = refs.pop(0)
    a2_ref = refs.pop(0) if has_extra else None
    w_ref = refs.pop(0)
    o_ref = refs.pop(0)
    o2_ref = refs.pop(0) if has_extra else None
    t_ref = refs.pop(0) if tail else None
    hm_ref = refs.pop(0) if head_major else None
    wb_ref, = refs

    def product(x_ref, y_ref, z_ref=None):
        rows = _block(x_ref.shape[0], 1024)
        for r0 in range(0, x_ref.shape[0], rows):
            acc = jnp.dot(x_ref[r0:r0 + rows, :], wb_ref[...], preferred_element_type=F32)
            y_ref[r0:r0 + rows, :] = acc.astype(y_ref.dtype)
            if z_ref is not None:
                heads = acc.shape[1] // LANES
                for h in range(heads):
                    z_ref[pl.ds(r0 * heads + h, rows, stride=heads), :] = (
                        acc[:, h * LANES:(h + 1) * LANES].astype(z_ref.dtype))

    @pl.when(pl.program_id(1) == 0)
    def _():
        wb_ref[...] = w_ref[...].astype(BF16)
        if has_extra:
            product(a2_ref, o2_ref)

    product(a_ref, o_ref, hm_ref)
    if tail:
        blocks, rows = tail

        @pl.when(pl.program_id(1) % blocks == blocks - 1)
        def _():
            t_ref[...] = o_ref[o_ref.shape[0] - rows:, :]


def matmul_cols(a, w, col0, n, out_dtype, extra=None, tail=None, head_major=False):
    m, k = a.shape
    bm = _block(m, MM_ROWS * 2 // jnp.dtype(out_dtype).itemsize)
    bn = _block(n, 1024)
    assert col0 % bn == 0
    off = col0 // bn
    in_specs = [pl.BlockSpec((bm, k), lambda j, i: (i, 0))]
    args = [a]
    out_specs = [pl.BlockSpec((bm, bn), lambda j, i: (i, j))]
    out_shape = [jax.ShapeDtypeStruct((m, n), out_dtype)]
    if extra is not None:
        m2 = extra.shape[0]
        in_specs.append(pl.BlockSpec((m2, k), lambda j, i: (0, 0)))
        args.append(extra)
        out_specs.append(pl.BlockSpec((m2, bn), lambda j, i: (0, j)))
        out_shape.append(jax.ShapeDtypeStruct((m2, n), out_dtype))
    if tail is not None:
        seq, keep = tail
        assert seq % bm == 0 and keep <= bm and m % seq == 0
        blocks = seq // bm
        out_specs.append(pl.BlockSpec((keep, bn), lambda j, i: (i // blocks, j)))
        out_shape.append(jax.ShapeDtypeStruct((m // seq * keep, n), out_dtype))
        tail = (blocks, keep)
    if head_major:
        assert bn == n
        heads = n // LANES
        out_specs.append(pl.BlockSpec((bm * heads, LANES), lambda j, i: (i, 0)))
        out_shape.append(jax.ShapeDtypeStruct((m * heads, LANES), out_dtype))
    in_specs.append(pl.BlockSpec((k, bn), lambda j, i: (0, j + off)))
    args.append(w)
    return pl.pallas_call(
        functools.partial(_mm_kernel, has_extra=extra is not None, tail=tail, head_major=head_major),
        grid=(n // bn, m // bm),
        in_specs=in_specs,
        out_specs=out_specs,
        out_shape=out_shape,
        scratch_shapes=[pltpu.VMEM((k, bn), BF16)],
        name="matmul_cols",
        compiler_params=_params("arbitrary", "arbitrary"),
    )(*args)


def _norm_mm_kernel(x_ref, g_ref, a2_ref, w_ref, h_ref, o_ref, o2_ref, wb_ref, *, rows):
    @pl.when(pl.program_id(0) == 0)
    def _():
        wb_ref[...] = w_ref[...].astype(BF16)
        o2_ref[...] = jnp.dot(a2_ref[...], wb_ref[...], preferred_element_type=F32).astype(o2_ref.dtype)

    for r0 in range(0, x_ref.shape[0], rows):
        x = x_ref[r0:r0 + rows, :]
        ms = jnp.mean(x * x, axis=-1, keepdims=True)
        h = (x * lax.rsqrt(ms + EPS) * g_ref[...]).astype(BF16)
        h_ref[r0:r0 + rows, :] = h
        o_ref[r0:r0 + rows, :] = jnp.dot(h, wb_ref[...], preferred_element_type=F32).astype(o_ref.dtype)


def norm_matmul_cols(x, g, w, col0, n, out_dtype, extra):
    m, k = x.shape
    m2 = extra.shape[0]
    bm = _block(m, 512)
    assert col0 % n == 0
    off = col0 // n
    return pl.pallas_call(
        functools.partial(_norm_mm_kernel, rows=_block(bm, 128)),
        grid=(m // bm,),
        in_specs=[pl.BlockSpec((bm, k), lambda i: (i, 0)),
                  pl.BlockSpec((1, k), lambda i: (0, 0)),
                  pl.BlockSpec((m2, k), lambda i: (0, 0)),
                  pl.BlockSpec((k, n), lambda i: (0, off))],
        out_specs=[pl.BlockSpec((bm, k), lambda i: (i, 0)),
                   pl.BlockSpec((bm, n), lambda i: (i, 0)),
                   pl.BlockSpec((m2, n), lambda i: (0, 0))],
        out_shape=[jax.ShapeDtypeStruct((m, k), BF16), jax.ShapeDtypeStruct((m, n), out_dtype),
                   jax.ShapeDtypeStruct((m2, n), out_dtype)],
        scratch_shapes=[pltpu.VMEM((k, n), BF16)],
        name="norm_matmul_cols",
        compiler_params=_params("arbitrary"),
    )(x, g.reshape(1, k).astype(F32), extra, w)


def _diff_lambda(lq1_ref, lk1_ref, lq2_ref, lk2_ref, lam_init):
    return (jnp.exp(jnp.sum(lq1_ref[...] * lk1_ref[...], keepdims=True))
            - jnp.exp(jnp.sum(lq2_ref[...] * lk2_ref[...], keepdims=True)) + lam_init)


def _head_norm(o, sub_ref, lam_init):
    ms = jnp.mean(o * o, axis=-1, keepdims=True)
    return o * lax.rsqrt(ms + EPS) * sub_ref[...] * (1.0 - lam_init)


def _own_alibi(n, slope):
    r = lax.broadcasted_iota(jnp.int32, (n, n), 0)
    c = lax.broadcasted_iota(jnp.int32, (n, n), 1)
    bias = slope * (r - jnp.abs(r - c)).astype(F32)
    return jnp.where(c // CHUNK <= r // CHUNK, bias, NEG_INF)


def _max_sq_norms(x, lane_masks):
    sq = jnp.square(x.astype(F32))
    return [jnp.max(jnp.sum(jnp.where(mask, sq, 0.0), axis=-1, keepdims=True)) for mask in lane_masks]


def _attn_a_kernel(q_ref, k_ref, v_ref, slope_ref, kf_ref, qf_ref, lq1_ref, lk1_ref, lq2_ref, lk2_ref,
                   sub_ref, o_ref, ke_scr, ve_scr, *, tq, lam_init):
    args = (q_ref, k_ref, v_ref, slope_ref, lq1_ref, lk1_ref, lq2_ref, lk2_ref, sub_ref, o_ref)
    _attn_a_bounded(*args, kf_ref, qf_ref, ke_scr, ve_scr, tq=tq, lam_init=lam_init)
    first_half = lax.broadcasted_iota(jnp.int32, (1, DK_A), 1) < DH_A
    q_sq = _max_sq_norms(q_ref[0] * (DH_A ** -0.5), (first_half, ~first_half))
    k_sq = _max_sq_norms(ke_scr[:, :DK_A], (first_half, ~first_half))
    bound_sq = jnp.maximum(q_sq[0] * k_sq[0], q_sq[1] * k_sq[1])

    @pl.when(bound_sq > SAFE_LOGIT ** 2 / 1.01)
    def _():
        _attn_a_general(*args, tq=tq, lam_init=lam_init)


def _position_features(t):
    pos = jnp.arange(t, dtype=jnp.int32)[:, None]
    lo = pos & 7
    hi = (pos - lo).astype(F32)
    lo = lo.astype(F32)
    one = jnp.ones((t, 1), F32)
    pad = jnp.zeros((t, LANES - 4), F32)
    k_side = jnp.concatenate([hi, lo, one, one, pad], axis=1).astype(BF16)
    q_side = jnp.concatenate([one, one, -hi, -lo, pad], axis=1)
    return k_side, q_side


def _attn_a_bounded(q_ref, k_ref, v_ref, slope_ref, lq1_ref, lk1_ref, lq2_ref, lk2_ref, sub_ref, o_ref,
                    kf_ref, qf_ref, ke_scr, ve_scr, *, tq, lam_init):
    t = q_ref.shape[1]
    slope = slope_ref[0][:, :1]
    lam = _diff_lambda(lq1_ref, lk1_ref, lq2_ref, lk2_ref, lam_init)
    first_half = lax.broadcasted_iota(jnp.int32, (1, DK_A), 1) < DH_A
    ke_scr[:, :DK_A] = k_ref[0].astype(BF16)
    ke_scr[:, DK_A:] = kf_ref[...]
    ve_scr[:, :DV_A] = v_ref[0].astype(BF16)
    ve_scr[:, DV_A:] = jnp.ones((t, DV_A), BF16)

    r = lax.broadcasted_iota(jnp.int32, (tq, tq), 0)
    c = lax.broadcasted_iota(jnp.int32, (tq, tq), 1)
    own_fix = jnp.where(c // CHUNK <= r // CHUNK, -2.0 * slope * jnp.maximum(c - r, 0).astype(F32), NEG_INF)
    own_fix = jnp.concatenate([own_fix, own_fix], axis=0)

    for i in reversed(range(t // tq)):
        nb = i * tq
        q = q_ref[0, nb:nb + tq, :] * (DH_A ** -0.5)
        q_feat = (qf_ref[nb:nb + tq, :] * slope).astype(BF16)
        zero = jnp.zeros_like(q)
        qe = jnp.concatenate([jnp.concatenate([jnp.where(first_half, q, zero), q_feat], axis=1),
                              jnp.concatenate([jnp.where(first_half, zero, q), q_feat], axis=1)], axis=0)
        s_own = lax.dot_general(qe, ke_scr[nb:nb + tq, :], CONTRACT_LAST, preferred_element_type=F32)
        acc = jnp.dot(jnp.exp(s_own + own_fix).astype(BF16), ve_scr[nb:nb + tq, :],
                      preferred_element_type=F32)
        if nb:
            s_bef = lax.dot_general(qe, ke_scr[:nb, :], CONTRACT_LAST, preferred_element_type=F32)
            acc = acc + jnp.dot(jnp.exp(s_bef).astype(BF16), ve_scr[:nb, :], preferred_element_type=F32)
        o = acc[:, :DV_A] / acc[:, DV_A:]
        o = o[:tq] - lam * o[tq:]
        o_ref[0, nb:nb + tq, :] = _head_norm(o, sub_ref, lam_init).astype(o_ref.dtype)


def _attn_a_general(q_ref, k_ref, v_ref, slope_ref, lq1_ref, lk1_ref, lq2_ref, lk2_ref, sub_ref, o_ref,
                    *, tq, lam_init):
    t = q_ref.shape[1]
    n_blocks = t // tq
    slope = slope_ref[0][:, :1]
    lam = _diff_lambda(lq1_ref, lk1_ref, lq2_ref, lk2_ref, lam_init)
    k = k_ref[0].astype(BF16)
    v = v_ref[0].astype(BF16)
    first_half = lax.broadcasted_iota(jnp.int32, (1, DK_A), 1) < DH_A
    own_bias = _own_alibi(tq, slope)
    n_before_max = (n_blocks - 1) * tq
    if n_before_max:
        j = lax.broadcasted_iota(jnp.int32, (1, n_before_max), 1)
        before_bias = slope * (j - n_before_max).astype(F32)

    for i in range(n_blocks):
        nb = i * tq
        q = q_ref[0, nb:nb + tq, :] * (DH_A ** -0.5)
        k_own, v_own = k[nb:nb + tq], v[nb:nb + tq]

        def softmax_pv(qh):
            s_own = lax.dot_general(qh, k_own, CONTRACT_LAST, preferred_element_type=F32) + own_bias
            m = jnp.max(s_own, axis=-1, keepdims=True)
            if nb:
                s_bef = lax.dot_general(qh, k[:nb], CONTRACT_LAST, preferred_element_type=F32)
                s_bef = s_bef + before_bias[:, n_before_max - nb:]
                m = jnp.maximum(m, jnp.max(s_bef, axis=-1, keepdims=True))
            p_own = jnp.exp(s_own - m)
            l = jnp.sum(p_own, axis=-1, keepdims=True)
            pv = jnp.dot(p_own.astype(BF16), v_own, preferred_element_type=F32)
            if nb:
                p_bef = jnp.exp(s_bef - m)
                l = l + jnp.sum(p_bef, axis=-1, keepdims=True)
                pv = pv + jnp.dot(p_bef.astype(BF16), v[:nb], preferred_element_type=F32)
            return pv / l

        o = (softmax_pv(jnp.where(first_half, q, jnp.zeros_like(q)))
             - lam * softmax_pv(jnp.where(first_half, jnp.zeros_like(q), q)))
        o_ref[0, nb:nb + tq, :] = _head_norm(o, sub_ref, lam_init).astype(o_ref.dtype)


def _row(a):
    return a.reshape(1, -1).astype(F32)


def _const_spec(shape):
    return pl.BlockSpec(shape, lambda *_: (0,) * len(shape))


def attention_a(q, k, v, lq1, lk1, lq2, lk2, subln, lam_init):
    b, t, _ = q.shape
    tq = _block(t, A_BLOCK)
    assert tq % CHUNK == 0
    slopes = jnp.asarray([[[2.0 ** (-8.0 * (hh + 1) / H_A)] * LANES] for hh in range(H_A)], dtype=F32)
    head = lambda width: pl.BlockSpec((1, t, width), lambda bi, hi: (bi, 0, hi))
    k_feat, q_feat = _position_features(t)
    return pl.pallas_call(
        functools.partial(_attn_a_kernel, tq=tq, lam_init=lam_init),
        grid=(b, H_A),
        in_specs=[head(DK_A), head(DK_A), head(DV_A),
                  pl.BlockSpec((1, 1, LANES), lambda bi, hi: (hi, 0, 0)),
                  _const_spec((t, LANES)), _const_spec((t, LANES)),
                  _const_spec((1, DH_A)), _const_spec((1, DH_A)), _const_spec((1, DH_A)),
                  _const_spec((1, DH_A)), _const_spec((1, DV_A))],
        out_specs=head(DV_A),
        out_shape=jax.ShapeDtypeStruct((b, t, H_A * DV_A), BF16),
        scratch_shapes=[pltpu.VMEM((t, DK_A + LANES), BF16), pltpu.VMEM((t, 2 * DV_A), BF16)],
        name="attention_a",
        compiler_params=_params("parallel", "arbitrary"),
    )(q, k, v, slopes, k_feat, q_feat, _row(lq1), _row(lk1), _row(lq2), _row(lk2), _row(subln))


def _attn_a_decode_kernel(q_ref, kn_ref, vn_ref, ck_ref, cv_ref, lq1_ref, lk1_ref, lq2_ref, lk2_ref,
                          sub_ref, o_ref, *, lam_init):
    t = q_ref.shape[1]
    p_len = ck_ref.shape[1] // H_A
    lam = _diff_lambda(lq1_ref, lk1_ref, lq2_ref, lk2_ref, lam_init)
    first_half = lax.broadcasted_iota(jnp.int32, (1, DK_A), 1) < DH_A
    j = lax.broadcasted_iota(jnp.int32, (1, p_len), 1)
    before_dist = (j - p_len).astype(F32)
    for h in range(H_A):
        slope = 2.0 ** (-8.0 * (h + 1) / H_A)
        cols = slice(h * DK_A, (h + 1) * DK_A)
        q = q_ref[0, :, cols] * (DH_A ** -0.5)
        kc = ck_ref[0, pl.ds(h, p_len, stride=H_A), :].astype(BF16)
        vc = cv_ref[0, pl.ds(h, p_len, stride=H_A), :].astype(BF16)
        kn = kn_ref[0, :, cols].astype(BF16)
        vn = vn_ref[0, :, cols].astype(BF16)
        own_bias = _own_alibi(t, slope)
        before_bias = slope * before_dist

        def softmax_pv(qh):
            s_new = lax.dot_general(qh, kn, CONTRACT_LAST, preferred_element_type=F32) + own_bias
            s_old = lax.dot_general(qh, kc, CONTRACT_LAST, preferred_element_type=F32) + before_bias
            m = jnp.maximum(jnp.max(s_new, axis=-1, keepdims=True), jnp.max(s_old, axis=-1, keepdims=True))
            p_new = jnp.exp(s_new - m)
            p_old = jnp.exp(s_old - m)
            l = jnp.sum(p_new, axis=-1, keepdims=True) + jnp.sum(p_old, axis=-1, keepdims=True)
            pv = (jnp.dot(p_new.astype(BF16), vn, preferred_element_type=F32)
                  + jnp.dot(p_old.astype(BF16), vc, preferred_element_type=F32))
            return pv / l

        o = (softmax_pv(jnp.where(first_half, q, jnp.zeros_like(q)))
             - lam * softmax_pv(jnp.where(first_half, jnp.zeros_like(q), q)))
        o_ref[0, :, cols] = _head_norm(o, sub_ref, lam_init).astype(o_ref.dtype)


def attention_a_decode(q, k_new, v_new, cache_k, cache_v, lq1, lk1, lq2, lk2, subln, lam_init):
    b, t, w = q.shape
    p_len = cache_k.shape[1]
    assert p_len % CHUNK == 0 and t <= CHUNK
    rows = lambda a: a.reshape(b, p_len * H_A, a.shape[-1])
    new = pl.BlockSpec((1, t, w), lambda bi: (bi, 0, 0))
    old = pl.BlockSpec((1, p_len * H_A, DK_A), lambda bi: (bi, 0, 0))
    return pl.pallas_call(
        functools.partial(_attn_a_decode_kernel, lam_init=lam_init),
        grid=(b,),
        in_specs=[new, new, new, old, old,
                  _const_spec((1, DH_A)), _const_spec((1, DH_A)), _const_spec((1, DH_A)),
                  _const_spec((1, DH_A)), _const_spec((1, DV_A))],
        out_specs=new,
        out_shape=jax.ShapeDtypeStruct((b, t, w), BF16),
        name="attention_a_decode",
        compiler_params=_params("parallel"),
    )(q, k_new, v_new, rows(cache_k), rows(cache_v), _row(lq1), _row(lk1), _row(lq2), _row(lk2), _row(subln))


def _band_tile(tq):
    pad = -(-tq // LANES) * LANES
    return BAND_PAST + pad, pad


def _band_bias_mask(g_row, tq):
    tile_w, pad = _band_tile(tq)
    width = g_row.shape[1]
    assert width == tile_w + pad
    rolled = pltpu.roll(jnp.broadcast_to(g_row, (tq, width)), width - pad, 1, stride=1, stride_axis=0)
    r = lax.broadcasted_iota(jnp.int32, (tq, tile_w), 0)
    c = lax.broadcasted_iota(jnp.int32, (tq, tile_w), 1)
    qch = r // CHUNK
    kch = c // CHUNK - N_PREV_CHUNKS
    valid = (kch <= qch) & (kch >= qch - N_PREV_CHUNKS)
    return jnp.where(valid, rolled[:, :tile_w], NEG_INF)


def _toeplitz_rows(rel_bias, tq):
    tile_w, pad = _band_tile(tq)
    width = tile_w + pad
    n_lo = pad + BAND_PAST - MAX_REL
    n_hi = max(width - n_lo - (2 * MAX_REL + 1), 0)
    lo = jnp.broadcast_to(rel_bias[:, :1], (H_B, n_lo))
    hi = jnp.broadcast_to(rel_bias[:, -1:], (H_B, n_hi))
    g = jnp.concatenate([lo, rel_bias, hi], axis=1)[:, :width]
    return g.reshape(H_B, 1, width).astype(F32)


def _attn_b_kernel(q_ref, k_ref, v_ref, g_ref, o_ref, ve_scr, *, tq):
    t = q_ref.shape[1]
    tile_w, _ = _band_tile(tq)
    scale = DH_B ** -0.5
    bias_mask = _band_bias_mask(g_ref[0], tq)

    def groups():
        for gi in range(t // tq):
            qa = gi * tq
            lo = max(qa - BAND_PAST, 0)
            hi = qa + tq
            off = lo - qa + BAND_PAST
            assert off % LANES == 0 and off + hi - lo == tile_w
            yield slice(qa, hi), slice(lo, hi), off

    log2e = math.log2(math.e)
    k = k_ref[0].astype(BF16)
    ve_scr[:, :DH_B] = v_ref[0].astype(BF16)
    ve_scr[:, DH_B:] = jnp.ones((t, DH_B), BF16)
    bias2 = bias_mask * log2e
    for rows, keys, off in groups():
        s = lax.dot_general(q_ref[0, rows, :], k[keys], CONTRACT_LAST, preferred_element_type=F32)
        p = jnp.exp2(s * (scale * log2e) + bias2[:, off:])
        acc = jnp.dot(p.astype(BF16), ve_scr[keys, :], preferred_element_type=F32)
        o_ref[0, rows, :] = (acc[:, :DH_B] / acc[:, DH_B:]).astype(o_ref.dtype)

    (q_sq,), (k_sq,) = _max_sq_norms(q_ref[0], (True,)), _max_sq_norms(k, (True,))
    room = SAFE_LOGIT - jnp.max(jnp.abs(g_ref[0]))
    safe = jnp.logical_and(room > 0.0, q_sq * k_sq * (scale * scale * 1.01) <= room * room)

    @pl.when(jnp.logical_not(safe))
    def _():
        k = k_ref[0].astype(BF16)
        v = v_ref[0].astype(BF16)
        for rows, keys, off in groups():
            s = lax.dot_general(q_ref[0, rows, :], k[keys], CONTRACT_LAST, preferred_element_type=F32) * scale
            s = s + bias_mask[:, off:]
            m = jnp.max(s, axis=-1, keepdims=True)
            p = jnp.exp(s - m)
            l = jnp.sum(p, axis=-1, keepdims=True)
            o = jnp.dot(p.astype(BF16), v[keys], preferred_element_type=F32) / l
            o_ref[0, rows, :] = o.astype(o_ref.dtype)


def attention_b(q, k, v, rel_bias):
    b, t, _ = q.shape
    tq = _block(t, BAND_GROUP)
    assert tq % LANES == 0
    g = _toeplitz_rows(rel_bias, tq)
    head = pl.BlockSpec((1, t, DH_B), lambda bi, hi: (bi, 0, hi))
    return pl.pallas_call(
        functools.partial(_attn_b_kernel, tq=tq),
        grid=(b, H_B),
        in_specs=[head, head, head, pl.BlockSpec((1, 1, g.shape[2]), lambda bi, hi: (hi, 0, 0))],
        out_specs=head,
        out_shape=jax.ShapeDtypeStruct((b, t, H_B * DH_B), BF16),
        scratch_shapes=[pltpu.VMEM((t, 2 * DH_B), BF16)],
        name="attention_b",
        compiler_params=_params("parallel", "arbitrary"),
    )(q, k, v, g)


def _attn_b_decode_kernel(q_ref, kn_ref, vn_ref, ck_ref, cv_ref, g_ref, o_ref):
    t = q_ref.shape[1]
    scale = DH_B ** -0.5
    for h in range(H_B):
        cols = slice(h * DH_B, (h + 1) * DH_B)
        q = q_ref[0, :, cols]
        kc = ck_ref[0, pl.ds(h, BAND_PAST, stride=H_B), :].astype(BF16)
        vc = cv_ref[0, pl.ds(h, BAND_PAST, stride=H_B), :].astype(BF16)
        kn = kn_ref[0, :, cols].astype(BF16)
        vn = vn_ref[0, :, cols].astype(BF16)
        bias_mask = _band_bias_mask(g_ref[h], t)
        s_old = lax.dot_general(q, kc, CONTRACT_LAST, preferred_element_type=F32) * scale
        s_old = s_old + bias_mask[:, :BAND_PAST]
        s_new = lax.dot_general(q, kn, CONTRACT_LAST, preferred_element_type=F32) * scale
        s_new = s_new + bias_mask[:, BAND_PAST:BAND_PAST + t]
        m = jnp.maximum(jnp.max(s_new, axis=-1, keepdims=True), jnp.max(s_old, axis=-1, keepdims=True))
        p_new = jnp.exp(s_new - m)
        p_old = jnp.exp(s_old - m)
        l = jnp.sum(p_new, axis=-1, keepdims=True) + jnp.sum(p_old, axis=-1, keepdims=True)
        pv = (jnp.dot(p_new.astype(BF16), vn, preferred_element_type=F32)
              + jnp.dot(p_old.astype(BF16), vc, preferred_element_type=F32))
        o_ref[0, :, cols] = (pv / l).astype(o_ref.dtype)


def attention_b_decode(q, k_new, v_new, cache_k, cache_v, rel_bias, p_len):
    b, t, w = q.shape
    assert cache_k.shape[1] == BAND_PAST and p_len % CHUNK == 0 and p_len >= BAND_PAST and t <= CHUNK
    g = _toeplitz_rows(rel_bias, t)
    rows = lambda a: a.reshape(b, BAND_PAST * H_B, a.shape[-1])
    new = pl.BlockSpec((1, t, w), lambda bi: (bi, 0, 0))
    old = pl.BlockSpec((1, BAND_PAST * H_B, DH_B), lambda bi: (bi, 0, 0))
    return pl.pallas_call(
        _attn_b_decode_kernel,
        grid=(b,),
        in_specs=[new, new, new, old, old, _const_spec(g.shape)],
        out_specs=new,
        out_shape=jax.ShapeDtypeStruct((b, t, w), BF16),
        name="attention_b_decode",
        compiler_params=_params("parallel"),
    )(q, k_new, v_new, rows(cache_k), rows(cache_v), g)


def _attn_c_kernel(q_ref, k_ref, v_ref, o_ref, *, tq):
    t = q_ref.shape[1]
    scale = DH_C ** -0.5
    k = k_ref[0].astype(BF16)
    v = v_ref[0].astype(BF16)

    def attend(probabilities):
        for i in range(t // tq):
            rows = slice(i * tq, (i + 1) * tq)
            s = lax.dot_general(q_ref[0, rows, :], k, CONTRACT_LAST, preferred_element_type=F32)
            p = probabilities(s)
            l = jnp.sum(p, axis=-1, keepdims=True)
            o = jnp.dot(p.astype(BF16), v, preferred_element_type=F32) / l
            o_ref[0, rows, :] = o.astype(o_ref.dtype)

    attend(lambda s: jnp.exp2(s * (scale * math.log2(math.e))))
    (q_sq,), (k_sq,) = _max_sq_norms(q_ref[0], (True,)), _max_sq_norms(k, (True,))

    @pl.when(q_sq * k_sq * (scale * scale * 1.01) > SAFE_LOGIT ** 2)
    def _():
        def shifted(s):
            s = s * scale
            return jnp.exp(s - jnp.max(s, axis=-1, keepdims=True))
        attend(shifted)


def attention_c(q, mk, mv):
    b, t, _ = q.shape
    n_mem = mk.shape[1]
    tq = _block(t, 512)
    return pl.pallas_call(
        functools.partial(_attn_c_kernel, tq=tq),
        grid=(b, H_C),
        in_specs=[pl.BlockSpec((1, t, DH_C), lambda bi, hi: (bi, 0, hi)),
                  pl.BlockSpec((1, n_mem, DH_C), lambda bi, hi: (bi, 0, hi)),
                  pl.BlockSpec((1, n_mem, DH_C), lambda bi, hi: (bi, 0, hi))],
        out_specs=pl.BlockSpec((1, t, DH_C), lambda bi, hi: (bi, 0, hi)),
        out_shape=jax.ShapeDtypeStruct((b, t, H_C * DH_C), BF16),
        name="attention_c",
        compiler_params=_params("parallel", "arbitrary"),
    )(q, mk, mv)


def _merge_kernel(oa_ref, ob_ref, oc_ref, ga_ref, gb_ref, gc_ref, ba_ref, bb_ref, bc_ref,
                  wa_ref, wb_ref, wc_ref, o_ref):
    rows = _block(o_ref.shape[0], MERGE_CHUNK)
    for r0 in range(0, o_ref.shape[0], rows):
        sl = slice(r0, r0 + rows)

        def branch(o_r, g_r, b_r, w_r):
            gate = jax.nn.sigmoid(g_r[sl, :].astype(F32) + b_r[...])
            return gate * jnp.dot(o_r[sl, :], w_r[...], preferred_element_type=F32)

        merged = (branch(oa_ref, ga_ref, ba_ref, wa_ref) + branch(ob_ref, gb_ref, bb_ref, wb_ref)
                  + branch(oc_ref, gc_ref, bc_ref, wc_ref))
        o_ref[sl, :] = merged.astype(o_ref.dtype)


def merge_branches(oa, ob, oc, gate_logits, b_gate, wa, wb, wc):
    m, w_in = oa.shape
    d = wa.shape[1]
    bm = _block(m, MERGE_ROWS)
    row = lambda width: pl.BlockSpec((bm, width), lambda i: (i, 0))
    gate = lambda j: pl.BlockSpec((bm, d), lambda i: (i, j))
    bias = lambda j: pl.BlockSpec((1, d), lambda i: (0, j))
    weight = pl.BlockSpec((w_in, d), lambda i: (0, 0), pipeline_mode=pl.Buffered(1))
    bg = b_gate.reshape(1, N_BRANCH * d).astype(F32)
    return pl.pallas_call(
        _merge_kernel,
        grid=(m // bm,),
        in_specs=[row(w_in), row(w_in), row(w_in), gate(0), gate(1), gate(2),
                  bias(0), bias(1), bias(2), weight, weight, weight],
        out_specs=pl.BlockSpec((bm, d), lambda i: (i, 0)),
        out_shape=jax.ShapeDtypeStruct((m, d), BF16),
        name="merge_branches",
        compiler_params=_params("parallel"),
    )(oa, ob, oc, gate_logits, gate_logits, gate_logits, bg, bg, bg, wa, wb, wc)


def _proj_norm_res_kernel(*refs, next_norm, rows):
    if next_norm:
        a_ref, w_ref, x_ref, g_ref, g2_ref, o_ref, h_ref = refs
    else:
        a_ref, w_ref, x_ref, g_ref, o_ref = refs
    bm = a_ref.shape[0]
    for r0 in range(0, bm, rows):
        sl = slice(r0, r0 + rows)
        y = jnp.dot(a_ref[sl, :], w_ref[...], preferred_element_type=F32)
        ms = jnp.mean(y * y, axis=-1, keepdims=True)
        o = x_ref[sl, :] + y * lax.rsqrt(ms + EPS) * g_ref[...]
        o_ref[sl, :] = o
        if next_norm:
            ms2 = jnp.mean(o * o, axis=-1, keepdims=True)
            h_ref[sl, :] = (o * lax.rsqrt(ms2 + EPS) * g2_ref[...]).astype(h_ref.dtype)


def proj_norm_residual(a, w, x, g, bm, rows, next_g=None):
    m, k = a.shape
    d = w.shape[1]
    bm = _block(m, bm)
    rows = _block(bm, rows)
    vec = pl.BlockSpec((1, d), lambda i: (0, 0))
    row = pl.BlockSpec((bm, d), lambda i: (i, 0))
    in_specs = [pl.BlockSpec((bm, k), lambda i: (i, 0)),
                pl.BlockSpec((k, d), lambda i: (0, 0), pipeline_mode=pl.Buffered(1)), row, vec]
    args = [a, w, x, g.reshape(1, d).astype(F32)]
    out_specs, out_shape = row, jax.ShapeDtypeStruct((m, d), F32)
    if next_g is not None:
        in_specs.append(vec)
        args.append(next_g.reshape(1, d).astype(F32))
        out_specs = (row, row)
        out_shape = (out_shape, jax.ShapeDtypeStruct((m, d), BF16))
    return pl.pallas_call(
        functools.partial(_proj_norm_res_kernel, next_norm=next_g is not None, rows=rows),
        grid=(m // bm,),
        in_specs=in_specs,
        out_specs=out_specs,
        out_shape=out_shape,
        name="proj_norm_residual",
        compiler_params=_params("parallel"),
    )(*args)


def _ffn_in_kernel(h_ref, h2_ref, wa_ref, wb_ref, o_ref, o2_ref, wa_bf, wb_bf):
    def swiglu(h):
        a = jnp.dot(h, wa_bf[...], preferred_element_type=F32)
        b = jnp.dot(h, wb_bf[...], preferred_element_type=F32)
        return (jax.nn.silu(a) * b).astype(o_ref.dtype)

    @pl.when(pl.program_id(1) == 0)
    def _():
        wa_bf[...] = wa_ref[...].astype(BF16)
        wb_bf[...] = wb_ref[...].astype(BF16)
        o2_ref[...] = swiglu(h2_ref[...])

    o_ref[...] = swiglu(h_ref[...])


def ffn_in(h, extra, w):
    m, k = h.shape
    m2 = extra.shape[0]
    f = w.shape[1] // 2
    bm = _block(m, 1024)
    bf = 512
    assert f % bf == 0
    nf = f // bf
    return pl.pallas_call(
        _ffn_in_kernel,
        grid=(nf, m // bm),
        in_specs=[pl.BlockSpec((bm, k), lambda j, i: (i, 0)),
                  pl.BlockSpec((m2, k), lambda j, i: (0, 0)),
                  pl.BlockSpec((k, bf), lambda j, i: (0, j)),
                  pl.BlockSpec((k, bf), lambda j, i: (0, j + nf))],
        out_specs=[pl.BlockSpec((bm, bf), lambda j, i: (i, j)),
                   pl.BlockSpec((m2, bf), lambda j, i: (0, j))],
        out_shape=[jax.ShapeDtypeStruct((m, f), BF16), jax.ShapeDtypeStruct((m2, f), BF16)],
        scratch_shapes=[pltpu.VMEM((k, bf), BF16), pltpu.VMEM((k, bf), BF16)],
        name="ffn_in",
        compiler_params=_params("arbitrary", "arbitrary"),
    )(h, extra, w, w)


def _layer(xp, xd, mk_p, mv_p, mk_d, mv_d, caches, lam_init, norm_mix_pre, norm_mix_post, w_in, b_gate,
           lq1, lk1, lq2, lk2, subln_a, rel_bias, w_br_a, w_br_b, w_br_c, w_out,
           norm_ffn_pre, norm_ffn_post, w_ffn_in, w_ffn_out):
    d = xp.shape[-1]
    w_head = H_A * DK_A
    shapes = [xp.shape[:2], xd.shape[:2]]
    xs = [xp.reshape(-1, d), xd.reshape(-1, d)]
    h_d = rms_norm_bf16(xs[1], norm_mix_pre)
    h_p, qa_p, qa_d = norm_matmul_cols(xs[0], norm_mix_pre, w_in, 0, w_head, BF16, h_d)
    hs = [h_p, h_d]
    split = lambda outs: [o.reshape(*shp, w_head) for o, shp in zip(outs, shapes)]
    proj = lambda idx, dt: split(matmul_cols(hs[0], w_in, idx * w_head, w_head, dt, extra=hs[1]))
    q_a, q_b, q_c = split([qa_p, qa_d]), proj(3, BF16), proj(6, BF16)
    *k_a, ka_cache = matmul_cols(hs[0], w_in, 1 * w_head, w_head, F32, extra=hs[1], head_major=True)
    *v_a, va_cache = matmul_cols(hs[0], w_in, 2 * w_head, w_head, F32, extra=hs[1], head_major=True)
    k_a, v_a = split(k_a), split(v_a)
    a_caches = [c.reshape(*shapes[0], H_A, DK_A) for c in (ka_cache, va_cache)]
    seq = shapes[0][1]
    keep = min(BAND_PAST, seq)
    *k_b, kb_tail = matmul_cols(hs[0], w_in, 4 * w_head, w_head, F32, extra=hs[1], tail=(seq, keep))
    *v_b, vb_tail = matmul_cols(hs[0], w_in, 5 * w_head, w_head, F32, extra=hs[1], tail=(seq, keep))
    k_b, v_b = split(k_b), split(v_b)
    tails = [t.reshape(shapes[0][0], keep, w_head) for t in (kb_tail, vb_tail)]
    gates = matmul_cols(hs[0], w_in, 7 * w_head, N_BRANCH * d, BF16, extra=hs[1])

    ca_k, ca_v, cb_k, cb_v = caches
    o_a = [attention_a(q_a[0], k_a[0], v_a[0], lq1, lk1, lq2, lk2, subln_a, lam_init),
           attention_a_decode(q_a[1], k_a[1], v_a[1], ca_k, ca_v, lq1, lk1, lq2, lk2, subln_a, lam_init)]
    o_b = [attention_b(q_b[0], k_b[0], v_b[0], rel_bias),
           attention_b_decode(q_b[1], k_b[1], v_b[1], cb_k, cb_v, rel_bias, ca_k.shape[1])]
    o_c = [attention_c(q_c[0], mk_p, mv_p), attention_c(q_c[1], mk_d, mv_d)]

    x1, h2 = [], []
    for i in range(2):
        m = xs[i].shape[0]
        merged = merge_branches(o_a[i].reshape(m, -1), o_b[i].reshape(m, -1), o_c[i].reshape(m, -1),
                                gates[i], b_gate, w_br_a, w_br_b, w_br_c)
        a, b = proj_norm_residual(merged, w_out, xs[i], norm_mix_post, bm=512, rows=PROJ_ROWS,
                                  next_g=norm_ffn_pre)
        x1.append(a)
        h2.append(b)
    acts = ffn_in(h2[0], h2[1], w_ffn_in)
    ys = [proj_norm_residual(acts[i], w_ffn_out, x1[i], norm_ffn_post, bm=256, rows=256).reshape(*shapes[i], d)
          for i in range(2)]
    return (ys[0], *a_caches, *tails), (ys[1], k_a[1], v_a[1], k_b[1], v_b[1])


def kernel(x_prompt, x_sample, cache_a_k, cache_a_v, cache_b_k, cache_b_v, cache_mem_k, cache_mem_v, mem_prompt, norm_mix_pre, norm_mix_post, norm_mem, w_in, b_gate, lambda_q1, lambda_k1, lambda_q2, lambda_k2, subln_a, rel_bias_b, w_mem_kv, w_br_a, w_br_b, w_br_c, w_out, norm_ffn_pre, norm_ffn_post, w_ffn_in, w_ffn_out):
    depth = w_in.shape[0]
    bsz, s, d = x_prompt.shape
    n_mem = mem_prompt.shape[1]
    lb_prompt = min(BAND_PAST, s)
    yp, ys = x_prompt, x_sample
    outs = [[] for _ in range(10)]
    heads = lambda a, nh: a.reshape(a.shape[0], a.shape[1], nh, a.shape[2] // nh)
    flat = lambda a: a.reshape(a.shape[0], a.shape[1], -1)
    for l in range(depth):
        lam_init = 0.8 - 0.6 * math.exp(-0.3 * l)
        bf = lambda w: w[l].astype(BF16)
        shared = (lam_init, norm_mix_pre[l], norm_mix_post[l], w_in[l], b_gate[l],
                  lambda_q1[l], lambda_k1[l], lambda_q2[l], lambda_k2[l], subln_a[l], rel_bias_b[l],
                  bf(w_br_a), bf(w_br_b), bf(w_br_c), bf(w_out),
                  norm_ffn_pre[l], norm_ffn_post[l], w_ffn_in[l], bf(w_ffn_out))

        mem_n = rms_norm_bf16(mem_prompt.reshape(bsz * n_mem, d), norm_mem[l])
        w_c = H_C * DH_C
        mk_p = matmul_cols(mem_n, w_mem_kv[l], 0, w_c, F32)[0].reshape(bsz, n_mem, w_c)
        mv_p = matmul_cols(mem_n, w_mem_kv[l], w_c, w_c, F32)[0].reshape(bsz, n_mem, w_c)
        caches = (cache_a_k[l], cache_a_v[l], cache_b_k[l], cache_b_v[l])
        (yp, ka, va, kb, vb), (ys, ka_d, va_d, kb_d, vb_d) = _layer(
            yp, ys, mk_p, mv_p, flat(cache_mem_k[l]), flat(cache_mem_v[l]), caches, *shared)
        new = [ka, va, heads(kb, H_B), heads(vb, H_B), heads(mk_p, H_C), heads(mv_p, H_C),
               heads(ka_d, H_A), heads(va_d, H_A), heads(kb_d, H_B), heads(vb_d, H_B)]
        for o, a in zip(outs, new):
            o.append(a)
    return (yp, ys) + tuple(jnp.stack(o) for o in outs)
```

```python
import functools
import math

import jax
import jax.numpy as jnp
from jax import lax
from jax.experimental import pallas as pl
from jax.experimental.pallas import tpu as pltpu

F32 = jnp.float32
BF16 = jnp.bfloat16

CHUNK = 64
H_A = 8
DH_A = 64
DK_A = 2 * DH_A
DV_A = 128
H_B = 8
DH_B = 128
N_PREV_CHUNKS = 8
BAND_PAST = N_PREV_CHUNKS * CHUNK
MAX_REL = 128
H_C = 4
DH_C = 256
N_BRANCH = 3
EPS = 1e-6
NEG_INF = -1e30

LANES = 128
VMEM_LIMIT = 56 * 1024 * 1024
BAND_GROUP = 256
A_BLOCK = 256
MM_ROWS = 2048
MERGE_ROWS = 256
MERGE_CHUNK = 256
PROJ_ROWS = 128
SAFE_LOGIT = 40.0

CONTRACT_LAST = (((1,), (1,)), ((), ()))


def _params(*sem):
    return pltpu.CompilerParams(dimension_semantics=sem, vmem_limit_bytes=VMEM_LIMIT)


def _block(n, target):
    if n <= target:
        return n
    b = target
    while n % b:
        b //= 2
    return b


def _rms_kernel(x_ref, g_ref, o_ref):
    x = x_ref[...]
    ms = jnp.mean(x * x, axis=-1, keepdims=True)
    o_ref[...] = (x * lax.rsqrt(ms + EPS) * g_ref[...]).astype(o_ref.dtype)


def rms_norm_bf16(x, g):
    m, d = x.shape
    bm = _block(m, 512)
    return pl.pallas_call(
        _rms_kernel,
        grid=(m // bm,),
        in_specs=[pl.BlockSpec((bm, d), lambda i: (i, 0)),
                  pl.BlockSpec((1, d), lambda i: (0, 0))],
        out_specs=pl.BlockSpec((bm, d), lambda i: (i, 0)),
        out_shape=jax.ShapeDtypeStruct((m, d), BF16),
        name="rms_norm",
        compiler_params=_params("parallel"),
    )(x, g.reshape(1, d))


def _mm_kernel(*refs, has_extra, tail, head_major):
    refs = list(refs)
    a_ref = refs.pop(0)
    a2_ref = refs.pop(0) if has_extra else None
    w_ref = refs.pop(0)
    o_ref = refs.pop(0)
    o2_ref = refs.pop(0) if has_extra else None
    t_ref = refs.pop(0) if tail else None
    hm_ref = refs.pop(0) if head_major else None
    wb_ref, = refs

    def product(x_ref, y_ref, z_ref=None):
        rows = _block(x_ref.shape[0], 1024)
        for r0 in range(0, x_ref.shape[0], rows):
            acc = jnp.dot(x_ref[r0:r0 + rows, :], wb_ref[...], preferred_element_type=F32)
            y_ref[r0:r0 + rows, :] = acc.astype(y_ref.dtype)
            if z_ref is not None:
                heads = acc.shape[1] // LANES
                for h in range(heads):
                    slot = (h % head_major) * (heads // head_major) + h // head_major
                    z_ref[pl.ds(r0 * heads + slot, rows, stride=heads), :] = (
                        acc[:, h * LANES:(h + 1) * LANES].astype(z_ref.dtype))

    @pl.when(pl.program_id(1) == 0)
    def _():
        wb_ref[...] = w_ref[...].astype(BF16)
        if has_extra:
            product(a2_ref, o2_ref)

    product(a_ref, o_ref, hm_ref)
    if tail:
        blocks, rows = tail

        @pl.when(pl.program_id(1) % blocks == blocks - 1)
        def _():
            heads = o_ref.shape[1] // LANES
            for h in range(heads):
                t_ref[pl.ds(h, rows, stride=heads), :] = o_ref[o_ref.shape[0] - rows:, h * LANES:(h + 1) * LANES]


def matmul_cols(a, w, col0, n, out_dtype, extra=None, tail=None, head_major=0):
    m, k = a.shape
    bm = _block(m, MM_ROWS * 2 // jnp.dtype(out_dtype).itemsize)
    bn = _block(n, 1024)
    assert col0 % bn == 0
    off = col0 // bn
    in_specs = [pl.BlockSpec((bm, k), lambda j, i: (i, 0))]
    args = [a]
    out_specs = [pl.BlockSpec((bm, bn), lambda j, i: (i, j))]
    out_shape = [jax.ShapeDtypeStruct((m, n), out_dtype)]
    if extra is not None:
        m2 = extra.shape[0]
        in_specs.append(pl.BlockSpec((m2, k), lambda j, i: (0, 0)))
        args.append(extra)
        out_specs.append(pl.BlockSpec((m2, bn), lambda j, i: (0, j)))
        out_shape.append(jax.ShapeDtypeStruct((m2, n), out_dtype))
    if tail is not None:
        seq, keep = tail
        assert seq % bm == 0 and keep <= bm and m % seq == 0
        blocks = seq // bm
        assert bn == n
        out_specs.append(pl.BlockSpec((keep * (n // LANES), LANES), lambda j, i: (i // blocks, 0)))
        out_shape.append(jax.ShapeDtypeStruct((m // seq * keep * (n // LANES), LANES), out_dtype))
        tail = (blocks, keep)
    if head_major:
        assert bn == n
        heads = n // LANES
        out_specs.append(pl.BlockSpec((bm * heads, LANES), lambda j, i: (i, 0)))
        out_shape.append(jax.ShapeDtypeStruct((m * heads, LANES), out_dtype))
    in_specs.append(pl.BlockSpec((k, bn), lambda j, i: (0, j + off)))
    args.append(w)
    return pl.pallas_call(
        functools.partial(_mm_kernel, has_extra=extra is not None, tail=tail, head_major=head_major),
        grid=(n // bn, m // bm),
        in_specs=in_specs,
        out_specs=out_specs,
        out_shape=out_shape,
        scratch_shapes=[pltpu.VMEM((k, bn), BF16)],
        name="matmul_cols",
        compiler_params=_params("arbitrary", "arbitrary"),
    )(*args)


def _norm_mm_kernel(x_ref, g_ref, a2_ref, w_ref, h_ref, o_ref, o2_ref, wb_ref, *, rows):
    @pl.when(pl.program_id(0) == 0)
    def _():
        wb_ref[...] = w_ref[...].astype(BF16)
        o2_ref[...] = jnp.dot(a2_ref[...], wb_ref[...], preferred_element_type=F32).astype(o2_ref.dtype)

    for r0 in range(0, x_ref.shape[0], rows):
        x = x_ref[r0:r0 + rows, :]
        ms = jnp.mean(x * x, axis=-1, keepdims=True)
        h = (x * lax.rsqrt(ms + EPS) * g_ref[...]).astype(BF16)
        h_ref[r0:r0 + rows, :] = h
        o_ref[r0:r0 + rows, :] = jnp.dot(h, wb_ref[...], preferred_element_type=F32).astype(o_ref.dtype)


def norm_matmul_cols(x, g, w, col0, n, out_dtype, extra):
    m, k = x.shape
    m2 = extra.shape[0]
    bm = _block(m, 512)
    assert col0 % n == 0
    off = col0 // n
    return pl.pallas_call(
        functools.partial(_norm_mm_kernel, rows=_block(bm, 128)),
        grid=(m // bm,),
        in_specs=[pl.BlockSpec((bm, k), lambda i: (i, 0)),
                  pl.BlockSpec((1, k), lambda i: (0, 0)),
                  pl.BlockSpec((m2, k), lambda i: (0, 0)),
                  pl.BlockSpec((k, n), lambda i: (0, off))],
        out_specs=[pl.BlockSpec((bm, k), lambda i: (i, 0)),
                   pl.BlockSpec((bm, n), lambda i: (i, 0)),
                   pl.BlockSpec((m2, n), lambda i: (0, 0))],
        out_shape=[jax.ShapeDtypeStruct((m, k), BF16), jax.ShapeDtypeStruct((m, n), out_dtype),
                   jax.ShapeDtypeStruct((m2, n), out_dtype)],
        scratch_shapes=[pltpu.VMEM((k, n), BF16)],
        name="norm_matmul_cols",
        compiler_params=_params("arbitrary"),
    )(x, g.reshape(1, k).astype(F32), extra, w)


def _diff_lambda(lq1_ref, lk1_ref, lq2_ref, lk2_ref, lam_init):
    return (jnp.exp(jnp.sum(lq1_ref[...] * lk1_ref[...], keepdims=True))
            - jnp.exp(jnp.sum(lq2_ref[...] * lk2_ref[...], keepdims=True)) + lam_init)


def _head_norm(o, sub_ref, lam_init):
    ms = jnp.mean(o * o, axis=-1, keepdims=True)
    return o * lax.rsqrt(ms + EPS) * sub_ref[...] * (1.0 - lam_init)


def _own_alibi(n, slope):
    r = lax.broadcasted_iota(jnp.int32, (n, n), 0)
    c = lax.broadcasted_iota(jnp.int32, (n, n), 1)
    bias = slope * (r - jnp.abs(r - c)).astype(F32)
    return jnp.where(c // CHUNK <= r // CHUNK, bias, NEG_INF)


def _max_sq_norms(x, lane_masks):
    sq = jnp.square(x.astype(F32))
    return [jnp.max(jnp.sum(jnp.where(mask, sq, 0.0), axis=-1, keepdims=True)) for mask in lane_masks]


def _attn_a_kernel(q_ref, k_ref, v_ref, slope_ref, kf_ref, qf_ref, lq1_ref, lk1_ref, lq2_ref, lk2_ref,
                   sub_ref, o_ref, ke_scr, ve_scr, *, tq, lam_init):
    args = (q_ref, k_ref, v_ref, slope_ref, lq1_ref, lk1_ref, lq2_ref, lk2_ref, sub_ref, o_ref)
    _attn_a_bounded(*args, kf_ref, qf_ref, ke_scr, ve_scr, tq=tq, lam_init=lam_init)
    first_half = lax.broadcasted_iota(jnp.int32, (1, DK_A), 1) < DH_A
    q_sq = _max_sq_norms(q_ref[0] * (DH_A ** -0.5), (first_half, ~first_half))
    k_sq = _max_sq_norms(ke_scr[:, :DK_A], (first_half, ~first_half))
    bound_sq = jnp.maximum(q_sq[0] * k_sq[0], q_sq[1] * k_sq[1])

    @pl.when(bound_sq > SAFE_LOGIT ** 2 / 1.01)
    def _():
        _attn_a_general(*args, tq=tq, lam_init=lam_init)


def _position_features(t):
    pos = jnp.arange(t, dtype=jnp.int32)[:, None]
    lo = pos & 7
    hi = (pos - lo).astype(F32)
    lo = lo.astype(F32)
    one = jnp.ones((t, 1), F32)
    pad = jnp.zeros((t, LANES - 4), F32)
    k_side = jnp.concatenate([hi, lo, one, one, pad], axis=1).astype(BF16)
    q_side = jnp.concatenate([one, one, -hi, -lo, pad], axis=1)
    return k_side, q_side


def _attn_a_bounded(q_ref, k_ref, v_ref, slope_ref, lq1_ref, lk1_ref, lq2_ref, lk2_ref, sub_ref, o_ref,
                    kf_ref, qf_ref, ke_scr, ve_scr, *, tq, lam_init):
    t = q_ref.shape[1]
    slope = slope_ref[0][:, :1]
    lam = _diff_lambda(lq1_ref, lk1_ref, lq2_ref, lk2_ref, lam_init)
    first_half = lax.broadcasted_iota(jnp.int32, (1, DK_A), 1) < DH_A
    ke_scr[:, :DK_A] = k_ref[0].astype(BF16)
    ke_scr[:, DK_A:] = kf_ref[...]
    ve_scr[:, :DV_A] = v_ref[0].astype(BF16)
    ve_scr[:, DV_A:] = jnp.ones((t, DV_A), BF16)

    r = lax.broadcasted_iota(jnp.int32, (tq, tq), 0)
    c = lax.broadcasted_iota(jnp.int32, (tq, tq), 1)
    own_fix = jnp.where(c // CHUNK <= r // CHUNK, -2.0 * slope * jnp.maximum(c - r, 0).astype(F32), NEG_INF)
    own_fix = jnp.concatenate([own_fix, own_fix], axis=0)

    for i in reversed(range(t // tq)):
        nb = i * tq
        q = q_ref[0, nb:nb + tq, :] * (DH_A ** -0.5)
        q_feat = (qf_ref[nb:nb + tq, :] * slope).astype(BF16)
        zero = jnp.zeros_like(q)
        qe = jnp.concatenate([jnp.concatenate([jnp.where(first_half, q, zero), q_feat], axis=1),
                              jnp.concatenate([jnp.where(first_half, zero, q), q_feat], axis=1)], axis=0)
        s_own = lax.dot_general(qe, ke_scr[nb:nb + tq, :], CONTRACT_LAST, preferred_element_type=F32)
        acc = jnp.dot(jnp.exp(s_own + own_fix).astype(BF16), ve_scr[nb:nb + tq, :],
                      preferred_element_type=F32)
        if nb:
            s_bef = lax.dot_general(qe, ke_scr[:nb, :], CONTRACT_LAST, preferred_element_type=F32)
            acc = acc + jnp.dot(jnp.exp(s_bef).astype(BF16), ve_scr[:nb, :], preferred_element_type=F32)
        o = acc[:, :DV_A] / acc[:, DV_A:]
        o = o[:tq] - lam * o[tq:]
        o_ref[0, nb:nb + tq, :] = _head_norm(o, sub_ref, lam_init).astype(o_ref.dtype)


def _attn_a_general(q_ref, k_ref, v_ref, slope_ref, lq1_ref, lk1_ref, lq2_ref, lk2_ref, sub_ref, o_ref,
                    *, tq, lam_init):
    t = q_ref.shape[1]
    n_blocks = t // tq
    slope = slope_ref[0][:, :1]
    lam = _diff_lambda(lq1_ref, lk1_ref, lq2_ref, lk2_ref, lam_init)
    k = k_ref[0].astype(BF16)
    v = v_ref[0].astype(BF16)
    first_half = lax.broadcasted_iota(jnp.int32, (1, DK_A), 1) < DH_A
    own_bias = _own_alibi(tq, slope)
    n_before_max = (n_blocks - 1) * tq
    if n_before_max:
        j = lax.broadcasted_iota(jnp.int32, (1, n_before_max), 1)
        before_bias = slope * (j - n_before_max).astype(F32)

    for i in range(n_blocks):
        nb = i * tq
        q = q_ref[0, nb:nb + tq, :] * (DH_A ** -0.5)
        k_own, v_own = k[nb:nb + tq], v[nb:nb + tq]

        def softmax_pv(qh):
            s_own = lax.dot_general(qh, k_own, CONTRACT_LAST, preferred_element_type=F32) + own_bias
            m = jnp.max(s_own, axis=-1, keepdims=True)
            if nb:
                s_bef = lax.dot_general(qh, k[:nb], CONTRACT_LAST, preferred_element_type=F32)
                s_bef = s_bef + before_bias[:, n_before_max - nb:]
                m = jnp.maximum(m, jnp.max(s_bef, axis=-1, keepdims=True))
            p_own = jnp.exp(s_own - m)
            l = jnp.sum(p_own, axis=-1, keepdims=True)
            pv = jnp.dot(p_own.astype(BF16), v_own, preferred_element_type=F32)
            if nb:
                p_bef = jnp.exp(s_bef - m)
                l = l + jnp.sum(p_bef, axis=-1, keepdims=True)
                pv = pv + jnp.dot(p_bef.astype(BF16), v[:nb], preferred_element_type=F32)
            return pv / l

        o = (softmax_pv(jnp.where(first_half, q, jnp.zeros_like(q)))
             - lam * softmax_pv(jnp.where(first_half, jnp.zeros_like(q), q)))
        o_ref[0, nb:nb + tq, :] = _head_norm(o, sub_ref, lam_init).astype(o_ref.dtype)


def _row(a):
    return a.reshape(1, -1).astype(F32)


def _const_spec(shape):
    return pl.BlockSpec(shape, lambda *_: (0,) * len(shape))


def attention_a(q, k, v, lq1, lk1, lq2, lk2, subln, lam_init):
    b, t, _ = q.shape
    tq = _block(t, A_BLOCK)
    assert tq % CHUNK == 0
    slopes = jnp.asarray([[[2.0 ** (-8.0 * (hh + 1) / H_A)] * LANES] for hh in range(H_A)], dtype=F32)
    head = lambda width: pl.BlockSpec((1, t, width), lambda bi, hi: (bi, 0, hi))
    k_feat, q_feat = _position_features(t)
    return pl.pallas_call(
        functools.partial(_attn_a_kernel, tq=tq, lam_init=lam_init),
        grid=(b, H_A),
        in_specs=[head(DK_A), head(DK_A), head(DV_A),
                  pl.BlockSpec((1, 1, LANES), lambda bi, hi: (hi, 0, 0)),
                  _const_spec((t, LANES)), _const_spec((t, LANES)),
                  _const_spec((1, DH_A)), _const_spec((1, DH_A)), _const_spec((1, DH_A)),
                  _const_spec((1, DH_A)), _const_spec((1, DV_A))],
        out_specs=head(DV_A),
        out_shape=jax.ShapeDtypeStruct((b, t, H_A * DV_A), BF16),
        scratch_shapes=[pltpu.VMEM((t, DK_A + LANES), BF16), pltpu.VMEM((t, 2 * DV_A), BF16)],
        name="attention_a",
        compiler_params=_params("parallel", "arbitrary"),
    )(q, k, v, slopes, k_feat, q_feat, _row(lq1), _row(lk1), _row(lq2), _row(lk2), _row(subln))


def _attn_a_decode_kernel(q_ref, kn_ref, vn_ref, ck_ref, cv_ref, lq1_ref, lk1_ref, lq2_ref, lk2_ref,
                          sub_ref, o_ref, *, lam_init):
    t = q_ref.shape[1]
    p_len = ck_ref.shape[1] // H_A
    lam = _diff_lambda(lq1_ref, lk1_ref, lq2_ref, lk2_ref, lam_init)
    first_half = lax.broadcasted_iota(jnp.int32, (1, DK_A), 1) < DH_A
    j = lax.broadcasted_iota(jnp.int32, (1, p_len), 1)
    before_dist = (j - p_len).astype(F32)
    for h in range(H_A):
        slope = 2.0 ** (-8.0 * (h + 1) / H_A)
        cols = slice(h * DK_A, (h + 1) * DK_A)
        q = q_ref[0, :, cols] * (DH_A ** -0.5)
        kc = ck_ref[0, pl.ds(h, p_len, stride=H_A), :].astype(BF16)
        vc = cv_ref[0, pl.ds(h, p_len, stride=H_A), :].astype(BF16)
        kn = kn_ref[0, :, cols].astype(BF16)
        vn = vn_ref[0, :, cols].astype(BF16)
        own_bias = _own_alibi(t, slope)
        before_bias = slope * before_dist

        def softmax_pv(qh):
            s_new = lax.dot_general(qh, kn, CONTRACT_LAST, preferred_element_type=F32) + own_bias
            s_old = lax.dot_general(qh, kc, CONTRACT_LAST, preferred_element_type=F32) + before_bias
            m = jnp.maximum(jnp.max(s_new, axis=-1, keepdims=True), jnp.max(s_old, axis=-1, keepdims=True))
            p_new = jnp.exp(s_new - m)
            p_old = jnp.exp(s_old - m)
            l = jnp.sum(p_new, axis=-1, keepdims=True) + jnp.sum(p_old, axis=-1, keepdims=True)
            pv = (jnp.dot(p_new.astype(BF16), vn, preferred_element_type=F32)
                  + jnp.dot(p_old.astype(BF16), vc, preferred_element_type=F32))
            return pv / l

        o = (softmax_pv(jnp.where(first_half, q, jnp.zeros_like(q)))
             - lam * softmax_pv(jnp.where(first_half, jnp.zeros_like(q), q)))
        o_ref[0, :, cols] = _head_norm(o, sub_ref, lam_init).astype(o_ref.dtype)


def attention_a_decode(q, k_new, v_new, cache_k, cache_v, lq1, lk1, lq2, lk2, subln, lam_init):
    b, t, w = q.shape
    p_len = cache_k.shape[1]
    assert p_len % CHUNK == 0 and t <= CHUNK
    rows = lambda a: a.reshape(b, p_len * H_A, a.shape[-1])
    new = pl.BlockSpec((1, t, w), lambda bi: (bi, 0, 0))
    old = pl.BlockSpec((1, p_len * H_A, DK_A), lambda bi: (bi, 0, 0))
    return pl.pallas_call(
        functools.partial(_attn_a_decode_kernel, lam_init=lam_init),
        grid=(b,),
        in_specs=[new, new, new, old, old,
                  _const_spec((1, DH_A)), _const_spec((1, DH_A)), _const_spec((1, DH_A)),
                  _const_spec((1, DH_A)), _const_spec((1, DV_A))],
        out_specs=new,
        out_shape=jax.ShapeDtypeStruct((b, t, w), BF16),
        name="attention_a_decode",
        compiler_params=_params("parallel"),
    )(q, k_new, v_new, rows(cache_k), rows(cache_v), _row(lq1), _row(lk1), _row(lq2), _row(lk2), _row(subln))


def _band_tile(tq):
    pad = -(-tq // LANES) * LANES
    return BAND_PAST + pad, pad


def _band_bias_mask(g_row, tq):
    tile_w, pad = _band_tile(tq)
    width = g_row.shape[1]
    assert width == tile_w + pad
    rolled = pltpu.roll(jnp.broadcast_to(g_row, (tq, width)), width - pad, 1, stride=1, stride_axis=0)
    r = lax.broadcasted_iota(jnp.int32, (tq, tile_w), 0)
    c = lax.broadcasted_iota(jnp.int32, (tq, tile_w), 1)
    qch = r // CHUNK
    kch = c // CHUNK - N_PREV_CHUNKS
    valid = (kch <= qch) & (kch >= qch - N_PREV_CHUNKS)
    return jnp.where(valid, rolled[:, :tile_w], NEG_INF)


def _toeplitz_rows(rel_bias, tq):
    tile_w, pad = _band_tile(tq)
    width = tile_w + pad
    n_lo = pad + BAND_PAST - MAX_REL
    n_hi = max(width - n_lo - (2 * MAX_REL + 1), 0)
    lo = jnp.broadcast_to(rel_bias[:, :1], (H_B, n_lo))
    hi = jnp.broadcast_to(rel_bias[:, -1:], (H_B, n_hi))
    g = jnp.concatenate([lo, rel_bias, hi], axis=1)[:, :width]
    return g.reshape(H_B, 1, width).astype(F32)


def _attn_b_kernel(q_ref, k_ref, v_ref, g_ref, o_ref, ve_scr, *, tq):
    t = q_ref.shape[1]
    tile_w, _ = _band_tile(tq)
    scale = DH_B ** -0.5
    bias_mask = _band_bias_mask(g_ref[0], tq)

    def groups():
        for gi in range(t // tq):
            qa = gi * tq
            lo = max(qa - BAND_PAST, 0)
            hi = qa + tq
            off = lo - qa + BAND_PAST
            assert off % LANES == 0 and off + hi - lo == tile_w
            yield slice(qa, hi), slice(lo, hi), off

    log2e = math.log2(math.e)
    k = k_ref[0].astype(BF16)
    ve_scr[:, :DH_B] = v_ref[0].astype(BF16)
    ve_scr[:, DH_B:] = jnp.ones((t, DH_B), BF16)
    bias2 = bias_mask * log2e
    for rows, keys, off in groups():
        s = lax.dot_general(q_ref[0, rows, :], k[keys], CONTRACT_LAST, preferred_element_type=F32)
        p = jnp.exp2(s * (scale * log2e) + bias2[:, off:])
        acc = jnp.dot(p.astype(BF16), ve_scr[keys, :], preferred_element_type=F32)
        o_ref[0, rows, :] = (acc[:, :DH_B] / acc[:, DH_B:]).astype(o_ref.dtype)

    (q_sq,), (k_sq,) = _max_sq_norms(q_ref[0], (True,)), _max_sq_norms(k, (True,))
    room = SAFE_LOGIT - jnp.max(jnp.abs(g_ref[0]))
    safe = jnp.logical_and(room > 0.0, q_sq * k_sq * (scale * scale * 1.01) <= room * room)

    @pl.when(jnp.logical_not(safe))
    def _():
        k = k_ref[0].astype(BF16)
        v = v_ref[0].astype(BF16)
        for rows, keys, off in groups():
            s = lax.dot_general(q_ref[0, rows, :], k[keys], CONTRACT_LAST, preferred_element_type=F32) * scale
            s = s + bias_mask[:, off:]
            m = jnp.max(s, axis=-1, keepdims=True)
            p = jnp.exp(s - m)
            l = jnp.sum(p, axis=-1, keepdims=True)
            o = jnp.dot(p.astype(BF16), v[keys], preferred_element_type=F32) / l
            o_ref[0, rows, :] = o.astype(o_ref.dtype)


def attention_b(q, k, v, rel_bias):
    b, t, _ = q.shape
    tq = _block(t, BAND_GROUP)
    assert tq % LANES == 0
    g = _toeplitz_rows(rel_bias, tq)
    head = pl.BlockSpec((1, t, DH_B), lambda bi, hi: (bi, 0, hi))
    return pl.pallas_call(
        functools.partial(_attn_b_kernel, tq=tq),
        grid=(b, H_B),
        in_specs=[head, head, head, pl.BlockSpec((1, 1, g.shape[2]), lambda bi, hi: (hi, 0, 0))],
        out_specs=head,
        out_shape=jax.ShapeDtypeStruct((b, t, H_B * DH_B), BF16),
        scratch_shapes=[pltpu.VMEM((t, 2 * DH_B), BF16)],
        name="attention_b",
        compiler_params=_params("parallel", "arbitrary"),
    )(q, k, v, g)


def _attn_b_decode_kernel(q_ref, kn_ref, vn_ref, ck_ref, cv_ref, g_ref, o_ref):
    t = q_ref.shape[1]
    scale = DH_B ** -0.5
    for h in range(H_B):
        cols = slice(h * DH_B, (h + 1) * DH_B)
        q = q_ref[0, :, cols]
        kc = ck_ref[0, pl.ds(h, BAND_PAST, stride=H_B), :].astype(BF16)
        vc = cv_ref[0, pl.ds(h, BAND_PAST, stride=H_B), :].astype(BF16)
        kn = kn_ref[0, :, cols].astype(BF16)
        vn = vn_ref[0, :, cols].astype(BF16)
        bias_mask = _band_bias_mask(g_ref[h], t)
        s_old = lax.dot_general(q, kc, CONTRACT_LAST, preferred_element_type=F32) * scale
        s_old = s_old + bias_mask[:, :BAND_PAST]
        s_new = lax.dot_general(q, kn, CONTRACT_LAST, preferred_element_type=F32) * scale
        s_new = s_new + bias_mask[:, BAND_PAST:BAND_PAST + t]
        m = jnp.maximum(jnp.max(s_new, axis=-1, keepdims=True), jnp.max(s_old, axis=-1, keepdims=True))
        p_new = jnp.exp(s_new - m)
        p_old = jnp.exp(s_old - m)
        l = jnp.sum(p_new, axis=-1, keepdims=True) + jnp.sum(p_old, axis=-1, keepdims=True)
        pv = (jnp.dot(p_new.astype(BF16), vn, preferred_element_type=F32)
              + jnp.dot(p_old.astype(BF16), vc, preferred_element_type=F32))
        o_ref[0, :, cols] = (pv / l).astype(o_ref.dtype)


def attention_b_decode(q, k_new, v_new, cache_k, cache_v, rel_bias, p_len):
    b, t, w = q.shape
    assert cache_k.shape[1] == BAND_PAST and p_len % CHUNK == 0 and p_len >= BAND_PAST and t <= CHUNK
    g = _toeplitz_rows(rel_bias, t)
    rows = lambda a: a.reshape(b, BAND_PAST * H_B, a.shape[-1])
    new = pl.BlockSpec((1, t, w), lambda bi: (bi, 0, 0))
    old = pl.BlockSpec((1, BAND_PAST * H_B, DH_B), lambda bi: (bi, 0, 0))
    return pl.pallas_call(
        _attn_b_decode_kernel,
        grid=(b,),
        in_specs=[new, new, new, old, old, _const_spec(g.shape)],
        out_specs=new,
        out_shape=jax.ShapeDtypeStruct((b, t, w), BF16),
        name="attention_b_decode",
        compiler_params=_params("parallel"),
    )(q, k_new, v_new, rows(cache_k), rows(cache_v), g)


def _attn_c_kernel(q_ref, k_ref, v_ref, o_ref, *, tq):
    t = q_ref.shape[1]
    scale = DH_C ** -0.5
    for h in range(H_C):
        cols = slice(h * DH_C, (h + 1) * DH_C)
        k = k_ref[0, :, cols].astype(BF16)
        v = v_ref[0, :, cols].astype(BF16)

        def attend(probabilities):
            for i in range(t // tq):
                rows = slice(i * tq, (i + 1) * tq)
                s = lax.dot_general(q_ref[0, rows, cols], k, CONTRACT_LAST, preferred_element_type=F32)
                p = probabilities(s)
                l = jnp.sum(p, axis=-1, keepdims=True)
                o = jnp.dot(p.astype(BF16), v, preferred_element_type=F32) / l
                o_ref[0, rows, cols] = o.astype(o_ref.dtype)

        attend(lambda s: jnp.exp2(s * (scale * math.log2(math.e))))
        (q_sq,), (k_sq,) = _max_sq_norms(q_ref[0, :, cols], (True,)), _max_sq_norms(k, (True,))

        @pl.when(q_sq * k_sq * (scale * scale * 1.01) > SAFE_LOGIT ** 2)
        def _():
            def shifted(s):
                s = s * scale
                return jnp.exp(s - jnp.max(s, axis=-1, keepdims=True))
            attend(shifted)


def attention_c(q, mk, mv):
    b, t, w = q.shape
    n_mem = mk.shape[1]
    tq = _block(t, 512)
    return pl.pallas_call(
        functools.partial(_attn_c_kernel, tq=tq),
        grid=(b,),
        in_specs=[pl.BlockSpec((1, t, w), lambda bi: (bi, 0, 0)),
                  pl.BlockSpec((1, n_mem, w), lambda bi: (bi, 0, 0)),
                  pl.BlockSpec((1, n_mem, w), lambda bi: (bi, 0, 0))],
        out_specs=pl.BlockSpec((1, t, w), lambda bi: (bi, 0, 0)),
        out_shape=jax.ShapeDtypeStruct((b, t, w), BF16),
        name="attention_c",
        compiler_params=_params("parallel"),
    )(q, mk, mv)


def _merge_kernel(oa_ref, ob_ref, oc_ref, ga_ref, gb_ref, gc_ref, ba_ref, bb_ref, bc_ref,
                  wa_ref, wb_ref, wc_ref, o_ref):
    rows = _block(o_ref.shape[0], MERGE_CHUNK)
    for r0 in range(0, o_ref.shape[0], rows):
        sl = slice(r0, r0 + rows)

        def branch(o_r, g_r, b_r, w_r):
            gate = jax.nn.sigmoid(g_r[sl, :].astype(F32) + b_r[...])
            return gate * jnp.dot(o_r[sl, :], w_r[...], preferred_element_type=F32)

        merged = (branch(oa_ref, ga_ref, ba_ref, wa_ref) + branch(ob_ref, gb_ref, bb_ref, wb_ref)
                  + branch(oc_ref, gc_ref, bc_ref, wc_ref))
        o_ref[sl, :] = merged.astype(o_ref.dtype)


def merge_branches(oa, ob, oc, gate_logits, b_gate, wa, wb, wc):
    m, w_in = oa.shape
    d = wa.shape[1]
    bm = _block(m, MERGE_ROWS)
    row = lambda width: pl.BlockSpec((bm, width), lambda i: (i, 0))
    gate = lambda j: pl.BlockSpec((bm, d), lambda i: (i, j))
    bias = lambda j: pl.BlockSpec((1, d), lambda i: (0, j))
    weight = pl.BlockSpec((w_in, d), lambda i: (0, 0), pipeline_mode=pl.Buffered(1))
    bg = b_gate.reshape(1, N_BRANCH * d).astype(F32)
    return pl.pallas_call(
        _merge_kernel,
        grid=(m // bm,),
        in_specs=[row(w_in), row(w_in), row(w_in), gate(0), gate(1), gate(2),
                  bias(0), bias(1), bias(2), weight, weight, weight],
        out_specs=pl.BlockSpec((bm, d), lambda i: (i, 0)),
        out_shape=jax.ShapeDtypeStruct((m, d), BF16),
        name="merge_branches",
        compiler_params=_params("parallel"),
    )(oa, ob, oc, gate_logits, gate_logits, gate_logits, bg, bg, bg, wa, wb, wc)


def _proj_norm_res_kernel(*refs, next_norm, rows):
    if next_norm:
        a_ref, w_ref, x_ref, g_ref, g2_ref, o_ref, h_ref = refs
    else:
        a_ref, w_ref, x_ref, g_ref, o_ref = refs
    bm = a_ref.shape[0]
    for r0 in range(0, bm, rows):
        sl = slice(r0, r0 + rows)
        y = jnp.dot(a_ref[sl, :], w_ref[...], preferred_element_type=F32)
        ms = jnp.mean(y * y, axis=-1, keepdims=True)
        o = x_ref[sl, :] + y * lax.rsqrt(ms + EPS) * g_ref[...]
        o_ref[sl, :] = o
        if next_norm:
            ms2 = jnp.mean(o * o, axis=-1, keepdims=True)
            h_ref[sl, :] = (o * lax.rsqrt(ms2 + EPS) * g2_ref[...]).astype(h_ref.dtype)


def proj_norm_residual(a, w, x, g, bm, rows, next_g=None):
    m, k = a.shape
    d = w.shape[1]
    bm = _block(m, bm)
    rows = _block(bm, rows)
    vec = pl.BlockSpec((1, d), lambda i: (0, 0))
    row = pl.BlockSpec((bm, d), lambda i: (i, 0))
    in_specs = [pl.BlockSpec((bm, k), lambda i: (i, 0)),
                pl.BlockSpec((k, d), lambda i: (0, 0), pipeline_mode=pl.Buffered(1)), row, vec]
    args = [a, w, x, g.reshape(1, d).astype(F32)]
    out_specs, out_shape = row, jax.ShapeDtypeStruct((m, d), F32)
    if next_g is not None:
        in_specs.append(vec)
        args.append(next_g.reshape(1, d).astype(F32))
        out_specs = (row, row)
        out_shape = (out_shape, jax.ShapeDtypeStruct((m, d), BF16))
    return pl.pallas_call(
        functools.partial(_proj_norm_res_kernel, next_norm=next_g is not None, rows=rows),
        grid=(m // bm,),
        in_specs=in_specs,
        out_specs=out_specs,
        out_shape=out_shape,
        name="proj_norm_residual",
        compiler_params=_params("parallel"),
    )(*args)


def _ffn_in_kernel(h_ref, h2_ref, wa_ref, wb_ref, o_ref, o2_ref, wa_bf, wb_bf):
    def swiglu(h):
        a = jnp.dot(h, wa_bf[...], preferred_element_type=F32)
        b = jnp.dot(h, wb_bf[...], preferred_element_type=F32)
        return (jax.nn.silu(a) * b).astype(o_ref.dtype)

    @pl.when(pl.program_id(1) == 0)
    def _():
        wa_bf[...] = wa_ref[...].astype(BF16)
        wb_bf[...] = wb_ref[...].astype(BF16)
        o2_ref[...] = swiglu(h2_ref[...])

    o_ref[...] = swiglu(h_ref[...])


def ffn_in(h, extra, w):
    m, k = h.shape
    m2 = extra.shape[0]
    f = w.shape[1] // 2
    bm = _block(m, 1024)
    bf = 512
    assert f % bf == 0
    nf = f // bf
    return pl.pallas_call(
        _ffn_in_kernel,
        grid=(nf, m // bm),
        in_specs=[pl.BlockSpec((bm, k), lambda j, i: (i, 0)),
                  pl.BlockSpec((m2, k), lambda j, i: (0, 0)),
                  pl.BlockSpec((k, bf), lambda j, i: (0, j)),
                  pl.BlockSpec((k, bf), lambda j, i: (0, j + nf))],
        out_specs=[pl.BlockSpec((bm, bf), lambda j, i: (i, j)),
                   pl.BlockSpec((m2, bf), lambda j, i: (0, j))],
        out_shape=[jax.ShapeDtypeStruct((m, f), BF16), jax.ShapeDtypeStruct((m2, f), BF16)],
        scratch_shapes=[pltpu.VMEM((k, bf), BF16), pltpu.VMEM((k, bf), BF16)],
        name="ffn_in",
        compiler_params=_params("arbitrary", "arbitrary"),
    )(h, extra, w, w)


def _layer(xp, xd, mk_p, mv_p, mk_d, mv_d, caches, lam_init, norm_mix_pre, norm_mix_post, w_in, b_gate,
           lq1, lk1, lq2, lk2, subln_a, rel_bias, w_br_a, w_br_b, w_br_c, w_out,
           norm_ffn_pre, norm_ffn_post, w_ffn_in, w_ffn_out):
    d = xp.shape[-1]
    w_head = H_A * DK_A
    shapes = [xp.shape[:2], xd.shape[:2]]
    xs = [xp.reshape(-1, d), xd.reshape(-1, d)]
    h_d = rms_norm_bf16(xs[1], norm_mix_pre)
    h_p, qa_p, qa_d = norm_matmul_cols(xs[0], norm_mix_pre, w_in, 0, w_head, BF16, h_d)
    hs = [h_p, h_d]
    split = lambda outs: [o.reshape(*shp, w_head) for o, shp in zip(outs, shapes)]
    proj = lambda idx, dt: split(matmul_cols(hs[0], w_in, idx * w_head, w_head, dt, extra=hs[1]))
    q_a, q_b, q_c = split([qa_p, qa_d]), proj(3, BF16), proj(6, BF16)
    *k_a, ka_cache = matmul_cols(hs[0], w_in, 1 * w_head, w_head, F32, extra=hs[1], head_major=DK_A // LANES)
    *v_a, va_cache = matmul_cols(hs[0], w_in, 2 * w_head, w_head, F32, extra=hs[1], head_major=DV_A // LANES)
    k_a, v_a = split(k_a), split(v_a)
    a_caches = [c.reshape(*shapes[0], H_A, DK_A) for c in (ka_cache, va_cache)]
    seq = shapes[0][1]
    keep = min(BAND_PAST, seq)
    *k_b, kb_tail = matmul_cols(hs[0], w_in, 4 * w_head, w_head, F32, extra=hs[1], tail=(seq, keep))
    *v_b, vb_tail = matmul_cols(hs[0], w_in, 5 * w_head, w_head, F32, extra=hs[1], tail=(seq, keep))
    k_b, v_b = split(k_b), split(v_b)
    tails = [t.reshape(shapes[0][0], keep, H_B, DH_B) for t in (kb_tail, vb_tail)]
    gates = matmul_cols(hs[0], w_in, 7 * w_head, N_BRANCH * d, BF16, extra=hs[1])

    ca_k, ca_v, cb_k, cb_v = caches
    o_a = [attention_a(q_a[0], k_a[0], v_a[0], lq1, lk1, lq2, lk2, subln_a, lam_init),
           attention_a_decode(q_a[1], k_a[1], v_a[1], ca_k, ca_v, lq1, lk1, lq2, lk2, subln_a, lam_init)]
    o_b = [attention_b(q_b[0], k_b[0], v_b[0], rel_bias),
           attention_b_decode(q_b[1], k_b[1], v_b[1], cb_k, cb_v, rel_bias, ca_k.shape[1])]
    o_c = [attention_c(q_c[0], mk_p, mv_p), attention_c(q_c[1], mk_d, mv_d)]

    x1, h2 = [], []
    for i in range(2):
        m = xs[i].shape[0]
        merged = merge_branches(o_a[i].reshape(m, -1), o_b[i].reshape(m, -1), o_c[i].reshape(m, -1),
                                gates[i], b_gate, w_br_a, w_br_b, w_br_c)
        a, b = proj_norm_residual(merged, w_out, xs[i], norm_mix_post, bm=512, rows=PROJ_ROWS,
                                  next_g=norm_ffn_pre)
        x1.append(a)
        h2.append(b)
    acts = ffn_in(h2[0], h2[1], w_ffn_in)
    ys = [proj_norm_residual(acts[i], w_ffn_out, x1[i], norm_ffn_post, bm=256, rows=256).reshape(*shapes[i], d)
          for i in range(2)]
    return (ys[0], *a_caches, *tails), (ys[1], k_a[1], v_a[1], k_b[1], v_b[1])


def kernel(x_prompt, x_sample, cache_a_k, cache_a_v, cache_b_k, cache_b_v, cache_mem_k, cache_mem_v, mem_prompt, norm_mix_pre, norm_mix_post, norm_mem, w_in, b_gate, lambda_q1, lambda_k1, lambda_q2, lambda_k2, subln_a, rel_bias_b, w_mem_kv, w_br_a, w_br_b, w_br_c, w_out, norm_ffn_pre, norm_ffn_post, w_ffn_in, w_ffn_out):
    depth = w_in.shape[0]
    bsz, s, d = x_prompt.shape
    n_mem = mem_prompt.shape[1]
    lb_prompt = min(BAND_PAST, s)
    yp, ys = x_prompt, x_sample
    outs = [[] for _ in range(10)]
    heads = lambda a, nh: a.reshape(a.shape[0], a.shape[1], nh, a.shape[2] // nh)
    flat = lambda a: a.reshape(a.shape[0], a.shape[1], -1)
    for l in range(depth):
        lam_init = 0.8 - 0.6 * math.exp(-0.3 * l)
        bf = lambda w: w[l].astype(BF16)
        shared = (lam_init, norm_mix_pre[l], norm_mix_post[l], w_in[l], b_gate[l],
                  lambda_q1[l], lambda_k1[l], lambda_q2[l], lambda_k2[l], subln_a[l], rel_bias_b[l],
                  bf(w_br_a), bf(w_br_b), bf(w_br_c), bf(w_out),
                  norm_ffn_pre[l], norm_ffn_post[l], w_ffn_in[l], bf(w_ffn_out))

        mem_n = rms_norm_bf16(mem_prompt.reshape(bsz * n_mem, d), norm_mem[l])
        w_c = H_C * DH_C
        mk_p, mk_cache = matmul_cols(mem_n, w_mem_kv[l], 0, w_c, F32, head_major=DH_C // LANES)
        mv_p, mv_cache = matmul_cols(mem_n, w_mem_kv[l], w_c, w_c, F32, head_major=DH_C // LANES)
        mk_p, mv_p = mk_p.reshape(bsz, n_mem, w_c), mv_p.reshape(bsz, n_mem, w_c)
        caches = (cache_a_k[l], cache_a_v[l], cache_b_k[l], cache_b_v[l])
        (yp, ka, va, kb, vb), (ys, ka_d, va_d, kb_d, vb_d) = _layer(
            yp, ys, mk_p, mv_p, flat(cache_mem_k[l]), flat(cache_mem_v[l]), caches, *shared)
        mem_cache = lambda c: c.reshape(bsz, n_mem, DH_C // LANES, H_C, LANES).transpose(0, 1, 3, 2, 4).reshape(
            bsz, n_mem, H_C, DH_C)
        new = [ka, va, kb, vb, mem_cache(mk_cache), mem_cache(mv_cache),
               heads(ka_d, H_A), heads(va_d, H_A), heads(kb_d, H_B), heads(vb_d, H_B)]
        for o, a in zip(outs, new):
            o.append(a)
    return (yp, ys) + tuple(jnp.stack(o) for o in outs)
```

```python
import functools
import math

import jax
import jax.numpy as jnp
from jax import lax
from jax.experimental import pallas as pl
from jax.experimental.pallas import tpu as pltpu

F32 = jnp.float32
BF16 = jnp.bfloat16

CHUNK = 64
H_A = 8
DH_A = 64
DK_A = 2 * DH_A
DV_A = 128
H_B = 8
DH_B = 128
N_PREV_CHUNKS = 8
BAND_PAST = N_PREV_CHUNKS * CHUNK
MAX_REL = 128
H_C = 4
DH_C = 256
N_BRANCH = 3
EPS = 1e-6
NEG_INF = -1e30

LANES = 128
VMEM_LIMIT = 56 * 1024 * 1024
BAND_GROUP = 256
A_BLOCK = 256
MM_ROWS = 2048
MERGE_ROWS = 256
MERGE_CHUNK = 256
PROJ_ROWS = 128
SAFE_LOGIT = 40.0

CONTRACT_LAST = (((1,), (1,)), ((), ()))


def _params(*sem):
    return pltpu.CompilerParams(dimension_semantics=sem, vmem_limit_bytes=VMEM_LIMIT)


def _block(n, target):
    if n <= target:
        return n
    b = target
    while n % b:
        b //= 2
    return b


def _rms_kernel(x_ref, g_ref, o_ref):
    x = x_ref[...]
    ms = jnp.mean(x * x, axis=-1, keepdims=True)
    o_ref[...] = (x * lax.rsqrt(ms + EPS) * g_ref[...]).astype(o_ref.dtype)


def rms_norm_bf16(x, g):
    m, d = x.shape
    bm = _block(m, 512)
    return pl.pallas_call(
        _rms_kernel,
        grid=(m // bm,),
        in_specs=[pl.BlockSpec((bm, d), lambda i: (i, 0)),
                  pl.BlockSpec((1, d), lambda i: (0, 0))],
        out_specs=pl.BlockSpec((bm, d), lambda i: (i, 0)),
        out_shape=jax.ShapeDtypeStruct((m, d), BF16),
        name="rms_norm",
        compiler_params=_params("parallel"),
    )(x, g.reshape(1, d))


def _mm_kernel(*refs, has_extra, tail, head_major):
    refs = list(refs)
    a_ref = refs.pop(0)
    a2_ref = refs.pop(0) if has_extra else None
    w_ref = refs.pop(0)
    o_ref = refs.pop(0)
    o2_ref = refs.pop(0) if has_extra else None
    t_ref = refs.pop(0) if tail else None
    hm_ref = refs.pop(0) if head_major else None
    wb_ref, = refs

    def product(x_ref, y_ref, z_ref=None):
        rows = _block(x_ref.shape[0], 1024)
        for r0 in range(0, x_ref.shape[0], rows):
            acc = jnp.dot(x_ref[r0:r0 + rows, :], wb_ref[...], preferred_element_type=F32)
            y_ref[r0:r0 + rows, :] = acc.astype(y_ref.dtype)
            if z_ref is not None:
                heads = acc.shape[1] // LANES
                for h in range(heads):
                    slot = (h % head_major) * (heads // head_major) + h // head_major
                    z_ref[pl.ds(r0 * heads + slot, rows, stride=heads), :] = (
                        acc[:, h * LANES:(h + 1) * LANES].astype(z_ref.dtype))

    @pl.when(pl.program_id(1) == 0)
    def _():
        wb_ref[...] = w_ref[...].astype(BF16)
        if has_extra:
            product(a2_ref, o2_ref)

    product(a_ref, o_ref, hm_ref)
    if tail:
        blocks, rows = tail

        @pl.when(pl.program_id(1) % blocks == blocks - 1)
        def _():
            heads = o_ref.shape[1] // LANES
            for h in range(heads):
                t_ref[pl.ds(h, rows, stride=heads), :] = o_ref[o_ref.shape[0] - rows:, h * LANES:(h + 1) * LANES]


def matmul_cols(a, w, col0, n, out_dtype, extra=None, tail=None, head_major=0):
    m, k = a.shape
    bm = _block(m, MM_ROWS * 2 // jnp.dtype(out_dtype).itemsize)
    bn = _block(n, 1024)
    assert col0 % bn == 0
    off = col0 // bn
    in_specs = [pl.BlockSpec((bm, k), lambda j, i: (i, 0))]
    args = [a]
    out_specs = [pl.BlockSpec((bm, bn), lambda j, i: (i, j))]
    out_shape = [jax.ShapeDtypeStruct((m, n), out_dtype)]
    if extra is not None:
        m2 = extra.shape[0]
        in_specs.append(pl.BlockSpec((m2, k), lambda j, i: (0, 0)))
        args.append(extra)
        out_specs.append(pl.BlockSpec((m2, bn), lambda j, i: (0, j)))
        out_shape.append(jax.ShapeDtypeStruct((m2, n), out_dtype))
    if tail is not None:
        seq, keep = tail
        assert seq % bm == 0 and keep <= bm and m % seq == 0
        blocks = seq // bm
        assert bn == n
        out_specs.append(pl.BlockSpec((keep * (n // LANES), LANES), lambda j, i: (i // blocks, 0)))
        out_shape.append(jax.ShapeDtypeStruct((m // seq * keep * (n // LANES), LANES), out_dtype))
        tail = (blocks, keep)
    if head_major:
        assert bn == n
        heads = n // LANES
        out_specs.append(pl.BlockSpec((bm * heads, LANES), lambda j, i: (i, 0)))
        out_shape.append(jax.ShapeDtypeStruct((m * heads, LANES), out_dtype))
    in_specs.append(pl.BlockSpec((k, bn), lambda j, i: (0, j + off)))
    args.append(w)
    return pl.pallas_call(
        functools.partial(_mm_kernel, has_extra=extra is not None, tail=tail, head_major=head_major),
        grid=(n // bn, m // bm),
        in_specs=in_specs,
        out_specs=out_specs,
        out_shape=out_shape,
        scratch_shapes=[pltpu.VMEM((k, bn), BF16)],
        name="matmul_cols",
        compiler_params=_params("arbitrary", "arbitrary"),
    )(*args)


def _norm_mm_kernel(x_ref, g_ref, a2_ref, w_ref, h_ref, o_ref, o2_ref, wb_ref, *, rows):
    @pl.when(pl.program_id(0) == 0)
    def _():
        wb_ref[...] = w_ref[...].astype(BF16)
        o2_ref[...] = jnp.dot(a2_ref[...], wb_ref[...], preferred_element_type=F32).astype(o2_ref.dtype)

    for r0 in range(0, x_ref.shape[0], rows):
        x = x_ref[r0:r0 + rows, :]
        ms = jnp.mean(x * x, axis=-1, keepdims=True)
        h = (x * lax.rsqrt(ms + EPS) * g_ref[...]).astype(BF16)
        h_ref[r0:r0 + rows, :] = h
        o_ref[r0:r0 + rows, :] = jnp.dot(h, wb_ref[...], preferred_element_type=F32).astype(o_ref.dtype)


def norm_matmul_cols(x, g, w, col0, n, out_dtype, extra):
    m, k = x.shape
    m2 = extra.shape[0]
    bm = _block(m, 512)
    assert col0 % n == 0
    off = col0 // n
    return pl.pallas_call(
        functools.partial(_norm_mm_kernel, rows=_block(bm, 128)),
        grid=(m // bm,),
        in_specs=[pl.BlockSpec((bm, k), lambda i: (i, 0)),
                  pl.BlockSpec((1, k), lambda i: (0, 0)),
                  pl.BlockSpec((m2, k), lambda i: (0, 0)),
                  pl.BlockSpec((k, n), lambda i: (0, off))],
        out_specs=[pl.BlockSpec((bm, k), lambda i: (i, 0)),
                   pl.BlockSpec((bm, n), lambda i: (i, 0)),
                   pl.BlockSpec((m2, n), lambda i: (0, 0))],
        out_shape=[jax.ShapeDtypeStruct((m, k), BF16), jax.ShapeDtypeStruct((m, n), out_dtype),
                   jax.ShapeDtypeStruct((m2, n), out_dtype)],
        scratch_shapes=[pltpu.VMEM((k, n), BF16)],
        name="norm_matmul_cols",
        compiler_params=_params("arbitrary"),
    )(x, g.reshape(1, k).astype(F32), extra, w)


def _diff_lambda(lq1_ref, lk1_ref, lq2_ref, lk2_ref, lam_init):
    return (jnp.exp(jnp.sum(lq1_ref[...] * lk1_ref[...], keepdims=True))
            - jnp.exp(jnp.sum(lq2_ref[...] * lk2_ref[...], keepdims=True)) + lam_init)


def _head_norm(o, sub_ref, lam_init):
    ms = jnp.mean(o * o, axis=-1, keepdims=True)
    return o * lax.rsqrt(ms + EPS) * sub_ref[...] * (1.0 - lam_init)


def _own_alibi(n, slope):
    r = lax.broadcasted_iota(jnp.int32, (n, n), 0)
    c = lax.broadcasted_iota(jnp.int32, (n, n), 1)
    bias = slope * (r - jnp.abs(r - c)).astype(F32)
    return jnp.where(c // CHUNK <= r // CHUNK, bias, NEG_INF)


def _max_sq_norms(x, lane_masks):
    sq = jnp.square(x.astype(F32))
    return [jnp.max(jnp.sum(jnp.where(mask, sq, 0.0), axis=-1, keepdims=True)) for mask in lane_masks]


def _attn_a_kernel(q_ref, k_ref, v_ref, slope_ref, kf_ref, qf_ref, lq1_ref, lk1_ref, lq2_ref, lk2_ref,
                   sub_ref, o_ref, ke_scr, ve_scr, *, tq, lam_init):
    args = (q_ref, k_ref, v_ref, slope_ref, lq1_ref, lk1_ref, lq2_ref, lk2_ref, sub_ref, o_ref)
    _attn_a_bounded(*args, kf_ref, qf_ref, ke_scr, ve_scr, tq=tq, lam_init=lam_init)
    first_half = lax.broadcasted_iota(jnp.int32, (1, DK_A), 1) < DH_A
    q_sq = _max_sq_norms(q_ref[0] * (DH_A ** -0.5), (first_half, ~first_half))
    k_sq = _max_sq_norms(ke_scr[:, :DK_A], (first_half, ~first_half))
    bound_sq = jnp.maximum(q_sq[0] * k_sq[0], q_sq[1] * k_sq[1])

    @pl.when(bound_sq > SAFE_LOGIT ** 2 / 1.01)
    def _():
        _attn_a_general(*args, tq=tq, lam_init=lam_init)


def _position_features(t):
    pos = jnp.arange(t, dtype=jnp.int32)[:, None]
    lo = pos & 7
    hi = (pos - lo).astype(F32)
    lo = lo.astype(F32)
    one = jnp.ones((t, 1), F32)
    pad = jnp.zeros((t, LANES - 4), F32)
    k_side = jnp.concatenate([hi, lo, one, one, pad], axis=1).astype(BF16)
    q_side = jnp.concatenate([one, one, -hi, -lo, pad], axis=1)
    return k_side, q_side


def _attn_a_bounded(q_ref, k_ref, v_ref, slope_ref, lq1_ref, lk1_ref, lq2_ref, lk2_ref, sub_ref, o_ref,
                    kf_ref, qf_ref, ke_scr, ve_scr, *, tq, lam_init):
    t = q_ref.shape[1]
    slope = slope_ref[0][:, :1]
    lam = _diff_lambda(lq1_ref, lk1_ref, lq2_ref, lk2_ref, lam_init)
    first_half = lax.broadcasted_iota(jnp.int32, (1, DK_A), 1) < DH_A
    ke_scr[:, :DK_A] = k_ref[0].astype(BF16)
    ke_scr[:, DK_A:] = kf_ref[...]
    ve_scr[:, :DV_A] = v_ref[0].astype(BF16)
    ve_scr[:, DV_A:] = jnp.ones((t, DV_A), BF16)

    r = lax.broadcasted_iota(jnp.int32, (tq, tq), 0)
    c = lax.broadcasted_iota(jnp.int32, (tq, tq), 1)
    own_fix = jnp.where(c // CHUNK <= r // CHUNK, -2.0 * slope * jnp.maximum(c - r, 0).astype(F32), NEG_INF)
    own_fix = jnp.concatenate([own_fix, own_fix], axis=0)

    for i in reversed(range(t // tq)):
        nb = i * tq
        q = q_ref[0, nb:nb + tq, :] * (DH_A ** -0.5)
        q_feat = (qf_ref[nb:nb + tq, :] * slope).astype(BF16)
        zero = jnp.zeros_like(q)
        qe = jnp.concatenate([jnp.concatenate([jnp.where(first_half, q, zero), q_feat], axis=1),
                              jnp.concatenate([jnp.where(first_half, zero, q), q_feat], axis=1)], axis=0)
        s_own = lax.dot_general(qe, ke_scr[nb:nb + tq, :], CONTRACT_LAST, preferred_element_type=F32)
        acc = jnp.dot(jnp.exp(s_own + own_fix).astype(BF16), ve_scr[nb:nb + tq, :],
                      preferred_element_type=F32)
        if nb:
            s_bef = lax.dot_general(qe, ke_scr[:nb, :], CONTRACT_LAST, preferred_element_type=F32)
            acc = acc + jnp.dot(jnp.exp(s_bef).astype(BF16), ve_scr[:nb, :], preferred_element_type=F32)
        o = acc[:, :DV_A] / acc[:, DV_A:]
        o = o[:tq] - lam * o[tq:]
        o_ref[0, nb:nb + tq, :] = _head_norm(o, sub_ref, lam_init).astype(o_ref.dtype)


def _attn_a_general(q_ref, k_ref, v_ref, slope_ref, lq1_ref, lk1_ref, lq2_ref, lk2_ref, sub_ref, o_ref,
                    *, tq, lam_init):
    t = q_ref.shape[1]
    n_blocks = t // tq
    slope = slope_ref[0][:, :1]
    lam = _diff_lambda(lq1_ref, lk1_ref, lq2_ref, lk2_ref, lam_init)
    k = k_ref[0].astype(BF16)
    v = v_ref[0].astype(BF16)
    first_half = lax.broadcasted_iota(jnp.int32, (1, DK_A), 1) < DH_A
    own_bias = _own_alibi(tq, slope)
    n_before_max = (n_blocks - 1) * tq
    if n_before_max:
        j = lax.broadcasted_iota(jnp.int32, (1, n_before_max), 1)
        before_bias = slope * (j - n_before_max).astype(F32)

    for i in range(n_blocks):
        nb = i * tq
        q = q_ref[0, nb:nb + tq, :] * (DH_A ** -0.5)
        k_own, v_own = k[nb:nb + tq], v[nb:nb + tq]

        def softmax_pv(qh):
            s_own = lax.dot_general(qh, k_own, CONTRACT_LAST, preferred_element_type=F32) + own_bias
            m = jnp.max(s_own, axis=-1, keepdims=True)
            if nb:
                s_bef = lax.dot_general(qh, k[:nb], CONTRACT_LAST, preferred_element_type=F32)
                s_bef = s_bef + before_bias[:, n_before_max - nb:]
                m = jnp.maximum(m, jnp.max(s_bef, axis=-1, keepdims=True))
            p_own = jnp.exp(s_own - m)
            l = jnp.sum(p_own, axis=-1, keepdims=True)
            pv = jnp.dot(p_own.astype(BF16), v_own, preferred_element_type=F32)
            if nb:
                p_bef = jnp.exp(s_bef - m)
                l = l + jnp.sum(p_bef, axis=-1, keepdims=True)
                pv = pv + jnp.dot(p_bef.astype(BF16), v[:nb], preferred_element_type=F32)
            return pv / l

        o = (softmax_pv(jnp.where(first_half, q, jnp.zeros_like(q)))
             - lam * softmax_pv(jnp.where(first_half, jnp.zeros_like(q), q)))
        o_ref[0, nb:nb + tq, :] = _head_norm(o, sub_ref, lam_init).astype(o_ref.dtype)


def _row(a):
    return a.reshape(1, -1).astype(F32)


def _const_spec(shape):
    return pl.BlockSpec(shape, lambda *_: (0,) * len(shape))


def attention_a(q, k, v, lq1, lk1, lq2, lk2, subln, lam_init):
    b, t, _ = q.shape
    tq = _block(t, A_BLOCK)
    assert tq % CHUNK == 0
    slopes = jnp.asarray([[[2.0 ** (-8.0 * (hh + 1) / H_A)] * LANES] for hh in range(H_A)], dtype=F32)
    head = lambda width: pl.BlockSpec((1, t, width), lambda bi, hi: (bi, 0, hi))
    k_feat, q_feat = _position_features(t)
    return pl.pallas_call(
        functools.partial(_attn_a_kernel, tq=tq, lam_init=lam_init),
        grid=(b, H_A),
        in_specs=[head(DK_A), head(DK_A), head(DV_A),
                  pl.BlockSpec((1, 1, LANES), lambda bi, hi: (hi, 0, 0)),
                  _const_spec((t, LANES)), _const_spec((t, LANES)),
                  _const_spec((1, DH_A)), _const_spec((1, DH_A)), _const_spec((1, DH_A)),
                  _const_spec((1, DH_A)), _const_spec((1, DV_A))],
        out_specs=head(DV_A),
        out_shape=jax.ShapeDtypeStruct((b, t, H_A * DV_A), BF16),
        scratch_shapes=[pltpu.VMEM((t, DK_A + LANES), BF16), pltpu.VMEM((t, 2 * DV_A), BF16)],
        name="attention_a",
        compiler_params=_params("parallel", "arbitrary"),
    )(q, k, v, slopes, k_feat, q_feat, _row(lq1), _row(lk1), _row(lq2), _row(lk2), _row(subln))


def _attn_a_decode_kernel(q_ref, kn_ref, vn_ref, ck_ref, cv_ref, lq1_ref, lk1_ref, lq2_ref, lk2_ref,
                          sub_ref, o_ref, *, lam_init):
    t = q_ref.shape[1]
    p_len = ck_ref.shape[1] // H_A
    lam = _diff_lambda(lq1_ref, lk1_ref, lq2_ref, lk2_ref, lam_init)
    first_half = lax.broadcasted_iota(jnp.int32, (1, DK_A), 1) < DH_A
    j = lax.broadcasted_iota(jnp.int32, (1, p_len), 1)
    before_dist = (j - p_len).astype(F32)
    for h in range(H_A):
        slope = 2.0 ** (-8.0 * (h + 1) / H_A)
        cols = slice(h * DK_A, (h + 1) * DK_A)
        q = q_ref[0, :, cols] * (DH_A ** -0.5)
        kc = ck_ref[0, pl.ds(h, p_len, stride=H_A), :].astype(BF16)
        vc = cv_ref[0, pl.ds(h, p_len, stride=H_A), :].astype(BF16)
        kn = kn_ref[0, :, cols].astype(BF16)
        vn = vn_ref[0, :, cols].astype(BF16)
        own_bias = _own_alibi(t, slope)
        before_bias = slope * before_dist

        def softmax_pv(qh):
            s_new = lax.dot_general(qh, kn, CONTRACT_LAST, preferred_element_type=F32) + own_bias
            s_old = lax.dot_general(qh, kc, CONTRACT_LAST, preferred_element_type=F32) + before_bias
            m = jnp.maximum(jnp.max(s_new, axis=-1, keepdims=True), jnp.max(s_old, axis=-1, keepdims=True))
            p_new = jnp.exp(s_new - m)
            p_old = jnp.exp(s_old - m)
            l = jnp.sum(p_new, axis=-1, keepdims=True) + jnp.sum(p_old, axis=-1, keepdims=True)
            pv = (jnp.dot(p_new.astype(BF16), vn, preferred_element_type=F32)
                  + jnp.dot(p_old.astype(BF16), vc, preferred_element_type=F32))
            return pv / l

        o = (softmax_pv(jnp.where(first_half, q, jnp.zeros_like(q)))
             - lam * softmax_pv(jnp.where(first_half, jnp.zeros_like(q), q)))
        o_ref[0, :, cols] = _head_norm(o, sub_ref, lam_init).astype(o_ref.dtype)


def attention_a_decode(q, k_new, v_new, cache_k, cache_v, lq1, lk1, lq2, lk2, subln, lam_init):
    b, t, w = q.shape
    p_len = cache_k.shape[1]
    assert p_len % CHUNK == 0 and t <= CHUNK
    rows = lambda a: a.reshape(b, p_len * H_A, a.shape[-1])
    new = pl.BlockSpec((1, t, w), lambda bi: (bi, 0, 0))
    old = pl.BlockSpec((1, p_len * H_A, DK_A), lambda bi: (bi, 0, 0))
    return pl.pallas_call(
        functools.partial(_attn_a_decode_kernel, lam_init=lam_init),
        grid=(b,),
        in_specs=[new, new, new, old, old,
                  _const_spec((1, DH_A)), _const_spec((1, DH_A)), _const_spec((1, DH_A)),
                  _const_spec((1, DH_A)), _const_spec((1, DV_A))],
        out_specs=new,
        out_shape=jax.ShapeDtypeStruct((b, t, w), BF16),
        name="attention_a_decode",
        compiler_params=_params("parallel"),
    )(q, k_new, v_new, rows(cache_k), rows(cache_v), _row(lq1), _row(lk1), _row(lq2), _row(lk2), _row(subln))


def _band_tile(tq):
    pad = -(-tq // LANES) * LANES
    return BAND_PAST + pad, pad


def _band_bias_mask(g_row, tq):
    tile_w, pad = _band_tile(tq)
    width = g_row.shape[1]
    assert width == tile_w + pad
    rolled = pltpu.roll(jnp.broadcast_to(g_row, (tq, width)), width - pad, 1, stride=1, stride_axis=0)
    r = lax.broadcasted_iota(jnp.int32, (tq, tile_w), 0)
    c = lax.broadcasted_iota(jnp.int32, (tq, tile_w), 1)
    qch = r // CHUNK
    kch = c // CHUNK - N_PREV_CHUNKS
    valid = (kch <= qch) & (kch >= qch - N_PREV_CHUNKS)
    return jnp.where(valid, rolled[:, :tile_w], NEG_INF)


def _toeplitz_rows(rel_bias, tq):
    tile_w, pad = _band_tile(tq)
    width = tile_w + pad
    n_lo = pad + BAND_PAST - MAX_REL
    n_hi = max(width - n_lo - (2 * MAX_REL + 1), 0)
    lo = jnp.broadcast_to(rel_bias[:, :1], (H_B, n_lo))
    hi = jnp.broadcast_to(rel_bias[:, -1:], (H_B, n_hi))
    g = jnp.concatenate([lo, rel_bias, hi], axis=1)[:, :width]
    return g.reshape(H_B, 1, width).astype(F32)


def _attn_b_kernel(q_ref, k_ref, v_ref, g_ref, o_ref, ve_scr, *, tq):
    t = q_ref.shape[1]
    tile_w, _ = _band_tile(tq)
    scale = DH_B ** -0.5
    bias_mask = _band_bias_mask(g_ref[0], tq)

    def groups():
        for gi in range(t // tq):
            qa = gi * tq
            lo = max(qa - BAND_PAST, 0)
            hi = qa + tq
            off = lo - qa + BAND_PAST
            assert off % LANES == 0 and off + hi - lo == tile_w
            yield slice(qa, hi), slice(lo, hi), off

    log2e = math.log2(math.e)
    k = k_ref[0].astype(BF16)
    ve_scr[:, :DH_B] = v_ref[0].astype(BF16)
    ve_scr[:, DH_B:] = jnp.ones((t, DH_B), BF16)
    bias2 = bias_mask * log2e
    for rows, keys, off in groups():
        s = lax.dot_general(q_ref[0, rows, :], k[keys], CONTRACT_LAST, preferred_element_type=F32)
        p = jnp.exp2(s * (scale * log2e) + bias2[:, off:])
        acc = jnp.dot(p.astype(BF16), ve_scr[keys, :], preferred_element_type=F32)
        o_ref[0, rows, :] = (acc[:, :DH_B] / acc[:, DH_B:]).astype(o_ref.dtype)

    (q_sq,), (k_sq,) = _max_sq_norms(q_ref[0], (True,)), _max_sq_norms(k, (True,))
    room = SAFE_LOGIT - jnp.max(jnp.abs(g_ref[0]))
    safe = jnp.logical_and(room > 0.0, q_sq * k_sq * (scale * scale * 1.01) <= room * room)

    @pl.when(jnp.logical_not(safe))
    def _():
        k = k_ref[0].astype(BF16)
        v = v_ref[0].astype(BF16)
        for rows, keys, off in groups():
            s = lax.dot_general(q_ref[0, rows, :], k[keys], CONTRACT_LAST, preferred_element_type=F32) * scale
            s = s + bias_mask[:, off:]
            m = jnp.max(s, axis=-1, keepdims=True)
            p = jnp.exp(s - m)
            l = jnp.sum(p, axis=-1, keepdims=True)
            o = jnp.dot(p.astype(BF16), v[keys], preferred_element_type=F32) / l
            o_ref[0, rows, :] = o.astype(o_ref.dtype)


def attention_b(q, k, v, rel_bias):
    b, t, _ = q.shape
    tq = _block(t, BAND_GROUP)
    assert tq % LANES == 0
    g = _toeplitz_rows(rel_bias, tq)
    head = pl.BlockSpec((1, t, DH_B), lambda bi, hi: (bi, 0, hi))
    return pl.pallas_call(
        functools.partial(_attn_b_kernel, tq=tq),
        grid=(b, H_B),
        in_specs=[head, head, head, pl.BlockSpec((1, 1, g.shape[2]), lambda bi, hi: (hi, 0, 0))],
        out_specs=head,
        out_shape=jax.ShapeDtypeStruct((b, t, H_B * DH_B), BF16),
        scratch_shapes=[pltpu.VMEM((t, 2 * DH_B), BF16)],
        name="attention_b",
        compiler_params=_params("parallel", "arbitrary"),
    )(q, k, v, g)


def _attn_b_decode_kernel(q_ref, kn_ref, vn_ref, ck_ref, cv_ref, g_ref, o_ref):
    t = q_ref.shape[1]
    scale = DH_B ** -0.5
    for h in range(H_B):
        cols = slice(h * DH_B, (h + 1) * DH_B)
        q = q_ref[0, :, cols]
        kc = ck_ref[0, pl.ds(h, BAND_PAST, stride=H_B), :].astype(BF16)
        vc = cv_ref[0, pl.ds(h, BAND_PAST, stride=H_B), :].astype(BF16)
        kn = kn_ref[0, :, cols].astype(BF16)
        vn = vn_ref[0, :, cols].astype(BF16)
        bias_mask = _band_bias_mask(g_ref[h], t)
        s_old = lax.dot_general(q, kc, CONTRACT_LAST, preferred_element_type=F32) * scale
        s_old = s_old + bias_mask[:, :BAND_PAST]
        s_new = lax.dot_general(q, kn, CONTRACT_LAST, preferred_element_type=F32) * scale
        s_new = s_new + bias_mask[:, BAND_PAST:BAND_PAST + t]
        m = jnp.maximum(jnp.max(s_new, axis=-1, keepdims=True), jnp.max(s_old, axis=-1, keepdims=True))
        p_new = jnp.exp(s_new - m)
        p_old = jnp.exp(s_old - m)
        l = jnp.sum(p_new, axis=-1, keepdims=True) + jnp.sum(p_old, axis=-1, keepdims=True)
        pv = (jnp.dot(p_new.astype(BF16), vn, preferred_element_type=F32)
              + jnp.dot(p_old.astype(BF16), vc, preferred_element_type=F32))
        o_ref[0, :, cols] = (pv / l).astype(o_ref.dtype)


def attention_b_decode(q, k_new, v_new, cache_k, cache_v, rel_bias, p_len):
    b, t, w = q.shape
    assert cache_k.shape[1] == BAND_PAST and p_len % CHUNK == 0 and p_len >= BAND_PAST and t <= CHUNK
    g = _toeplitz_rows(rel_bias, t)
    rows = lambda a: a.reshape(b, BAND_PAST * H_B, a.shape[-1])
    new = pl.BlockSpec((1, t, w), lambda bi: (bi, 0, 0))
    old = pl.BlockSpec((1, BAND_PAST * H_B, DH_B), lambda bi: (bi, 0, 0))
    return pl.pallas_call(
        _attn_b_decode_kernel,
        grid=(b,),
        in_specs=[new, new, new, old, old, _const_spec(g.shape)],
        out_specs=new,
        out_shape=jax.ShapeDtypeStruct((b, t, w), BF16),
        name="attention_b_decode",
        compiler_params=_params("parallel"),
    )(q, k_new, v_new, rows(cache_k), rows(cache_v), g)


def _attn_c_kernel(q_ref, k_ref, v_ref, o_ref, *, tq, cache_order):
    t = q_ref.shape[1]
    scale = DH_C ** -0.5
    blocks = DH_C // LANES

    def head_rows(ref, h):
        if not cache_order:
            return ref[0, :, h * DH_C:(h + 1) * DH_C]
        n_mem = ref.shape[1] // (blocks * H_C)
        return jnp.concatenate([ref[0, pl.ds(j * H_C + h, n_mem, stride=blocks * H_C), :]
                                for j in range(blocks)], axis=1)

    for h in range(H_C):
        cols = slice(h * DH_C, (h + 1) * DH_C)
        k = head_rows(k_ref, h).astype(BF16)
        v = head_rows(v_ref, h).astype(BF16)

        def attend(probabilities):
            for i in range(t // tq):
                rows = slice(i * tq, (i + 1) * tq)
                s = lax.dot_general(q_ref[0, rows, cols], k, CONTRACT_LAST, preferred_element_type=F32)
                p = probabilities(s)
                l = jnp.sum(p, axis=-1, keepdims=True)
                o = jnp.dot(p.astype(BF16), v, preferred_element_type=F32) / l
                o_ref[0, rows, cols] = o.astype(o_ref.dtype)

        attend(lambda s: jnp.exp2(s * (scale * math.log2(math.e))))
        (q_sq,), (k_sq,) = _max_sq_norms(q_ref[0, :, cols], (True,)), _max_sq_norms(k, (True,))

        @pl.when(q_sq * k_sq * (scale * scale * 1.01) > SAFE_LOGIT ** 2)
        def _():
            def shifted(s):
                s = s * scale
                return jnp.exp(s - jnp.max(s, axis=-1, keepdims=True))
            attend(shifted)


def attention_c(q, mk, mv):
    b, t, w = q.shape
    cache_order = mk.ndim == 4
    if cache_order:
        n_mem, blocks = mk.shape[1], DH_C // LANES
        tiles = lambda c: c.reshape(b, n_mem, H_C, blocks, LANES).transpose(0, 1, 3, 2, 4).reshape(
            b, n_mem * blocks * H_C, LANES)
        mk, mv = tiles(mk), tiles(mv)
    kv_spec = pl.BlockSpec((1,) + mk.shape[1:], lambda bi: (bi, 0, 0))
    tq = _block(t, 512)
    return pl.pallas_call(
        functools.partial(_attn_c_kernel, tq=tq, cache_order=cache_order),
        grid=(b,),
        in_specs=[pl.BlockSpec((1, t, w), lambda bi: (bi, 0, 0)), kv_spec, kv_spec],
        out_specs=pl.BlockSpec((1, t, w), lambda bi: (bi, 0, 0)),
        out_shape=jax.ShapeDtypeStruct((b, t, w), BF16),
        name="attention_c",
        compiler_params=_params("parallel"),
    )(q, mk, mv)


def _merge_kernel(oa_ref, ob_ref, oc_ref, ga_ref, gb_ref, gc_ref, ba_ref, bb_ref, bc_ref,
                  wa_f32, wb_f32, wc_f32, o_ref, wa_ref, wb_ref, wc_ref):
    @pl.when(pl.program_id(0) == 0)
    def _():
        wa_ref[...] = wa_f32[...].astype(BF16)
        wb_ref[...] = wb_f32[...].astype(BF16)
        wc_ref[...] = wc_f32[...].astype(BF16)

    rows = _block(o_ref.shape[0], MERGE_CHUNK)
    for r0 in range(0, o_ref.shape[0], rows):
        sl = slice(r0, r0 + rows)

        def branch(o_r, g_r, b_r, w_r):
            gate = jax.nn.sigmoid(g_r[sl, :].astype(F32) + b_r[...])
            return gate * jnp.dot(o_r[sl, :], w_r[...], preferred_element_type=F32)

        merged = (branch(oa_ref, ga_ref, ba_ref, wa_ref) + branch(ob_ref, gb_ref, bb_ref, wb_ref)
                  + branch(oc_ref, gc_ref, bc_ref, wc_ref))
        o_ref[sl, :] = merged.astype(o_ref.dtype)


def merge_branches(oa, ob, oc, gate_logits, b_gate, wa, wb, wc):
    m, w_in = oa.shape
    d = wa.shape[1]
    bm = _block(m, MERGE_ROWS)
    row = lambda width: pl.BlockSpec((bm, width), lambda i: (i, 0))
    gate = lambda j: pl.BlockSpec((bm, d), lambda i: (i, j))
    bias = lambda j: pl.BlockSpec((1, d), lambda i: (0, j))
    weight = pl.BlockSpec((w_in, d), lambda i: (0, 0), pipeline_mode=pl.Buffered(1))
    bg = b_gate.reshape(1, N_BRANCH * d).astype(F32)
    return pl.pallas_call(
        _merge_kernel,
        grid=(m // bm,),
        in_specs=[row(w_in), row(w_in), row(w_in), gate(0), gate(1), gate(2),
                  bias(0), bias(1), bias(2), weight, weight, weight],
        out_specs=pl.BlockSpec((bm, d), lambda i: (i, 0)),
        out_shape=jax.ShapeDtypeStruct((m, d), BF16),
        scratch_shapes=[pltpu.VMEM((w_in, d), BF16)] * N_BRANCH,
        name="merge_branches",
        compiler_params=_params("arbitrary"),
    )(oa, ob, oc, gate_logits, gate_logits, gate_logits, bg, bg, bg, wa, wb, wc)


def _proj_norm_res_kernel(*refs, next_norm, rows, cast_w):
    refs = list(refs)
    if cast_w:
        w_bf = refs.pop()
    if next_norm:
        a_ref, w_ref, x_ref, g_ref, g2_ref, o_ref, h_ref = refs
    else:
        a_ref, w_ref, x_ref, g_ref, o_ref = refs
    if cast_w:
        @pl.when(pl.program_id(0) == 0)
        def _():
            w_bf[...] = w_ref[...].astype(BF16)
        w_ref = w_bf
    bm = a_ref.shape[0]
    for r0 in range(0, bm, rows):
        sl = slice(r0, r0 + rows)
        y = jnp.dot(a_ref[sl, :], w_ref[...], preferred_element_type=F32)
        ms = jnp.mean(y * y, axis=-1, keepdims=True)
        o = x_ref[sl, :] + y * lax.rsqrt(ms + EPS) * g_ref[...]
        o_ref[sl, :] = o
        if next_norm:
            ms2 = jnp.mean(o * o, axis=-1, keepdims=True)
            h_ref[sl, :] = (o * lax.rsqrt(ms2 + EPS) * g2_ref[...]).astype(h_ref.dtype)


def proj_norm_residual(a, w, x, g, bm, rows, next_g=None):
    m, k = a.shape
    d = w.shape[1]
    bm = _block(m, bm)
    rows = _block(bm, rows)
    vec = pl.BlockSpec((1, d), lambda i: (0, 0))
    row = pl.BlockSpec((bm, d), lambda i: (i, 0))
    in_specs = [pl.BlockSpec((bm, k), lambda i: (i, 0)),
                pl.BlockSpec((k, d), lambda i: (0, 0), pipeline_mode=pl.Buffered(1)), row, vec]
    args = [a, w, x, g.reshape(1, d).astype(F32)]
    out_specs, out_shape = row, jax.ShapeDtypeStruct((m, d), F32)
    if next_g is not None:
        in_specs.append(vec)
        args.append(next_g.reshape(1, d).astype(F32))
        out_specs = (row, row)
        out_shape = (out_shape, jax.ShapeDtypeStruct((m, d), BF16))
    cast_w = w.dtype != BF16
    return pl.pallas_call(
        functools.partial(_proj_norm_res_kernel, next_norm=next_g is not None, rows=rows, cast_w=cast_w),
        grid=(m // bm,),
        in_specs=in_specs,
        out_specs=out_specs,
        out_shape=out_shape,
        scratch_shapes=[pltpu.VMEM((k, d), BF16)] if cast_w else [],
        name="proj_norm_residual",
        compiler_params=_params("arbitrary"),
    )(*args)


def _ffn_in_kernel(h_ref, h2_ref, wa_ref, wb_ref, o_ref, o2_ref, wa_bf, wb_bf):
    def swiglu(h):
        a = jnp.dot(h, wa_bf[...], preferred_element_type=F32)
        b = jnp.dot(h, wb_bf[...], preferred_element_type=F32)
        return (jax.nn.silu(a) * b).astype(o_ref.dtype)

    @pl.when(pl.program_id(1) == 0)
    def _():
        wa_bf[...] = wa_ref[...].astype(BF16)
        wb_bf[...] = wb_ref[...].astype(BF16)
        o2_ref[...] = swiglu(h2_ref[...])

    o_ref[...] = swiglu(h_ref[...])


def ffn_in(h, extra, w):
    m, k = h.shape
    m2 = extra.shape[0]
    f = w.shape[1] // 2
    bm = _block(m, 1024)
    bf = 512
    assert f % bf == 0
    nf = f // bf
    return pl.pallas_call(
        _ffn_in_kernel,
        grid=(nf, m // bm),
        in_specs=[pl.BlockSpec((bm, k), lambda j, i: (i, 0)),
                  pl.BlockSpec((m2, k), lambda j, i: (0, 0)),
                  pl.BlockSpec((k, bf), lambda j, i: (0, j)),
                  pl.BlockSpec((k, bf), lambda j, i: (0, j + nf))],
        out_specs=[pl.BlockSpec((bm, bf), lambda j, i: (i, j)),
                   pl.BlockSpec((m2, bf), lambda j, i: (0, j))],
        out_shape=[jax.ShapeDtypeStruct((m, f), BF16), jax.ShapeDtypeStruct((m2, f), BF16)],
        scratch_shapes=[pltpu.VMEM((k, bf), BF16), pltpu.VMEM((k, bf), BF16)],
        name="ffn_in",
        compiler_params=_params("arbitrary", "arbitrary"),
    )(h, extra, w, w)


def _layer(xp, xd, mk_p, mv_p, mk_d, mv_d, caches, lam_init, norm_mix_pre, norm_mix_post, w_in, b_gate,
           lq1, lk1, lq2, lk2, subln_a, rel_bias, w_br_a, w_br_b, w_br_c, w_out,
           norm_ffn_pre, norm_ffn_post, w_ffn_in, w_ffn_out):
    d = xp.shape[-1]
    w_head = H_A * DK_A
    shapes = [xp.shape[:2], xd.shape[:2]]
    xs = [xp.reshape(-1, d), xd.reshape(-1, d)]
    h_d = rms_norm_bf16(xs[1], norm_mix_pre)
    h_p, qa_p, qa_d = norm_matmul_cols(xs[0], norm_mix_pre, w_in, 0, w_head, BF16, h_d)
    hs = [h_p, h_d]
    split = lambda outs: [o.reshape(*shp, w_head) for o, shp in zip(outs, shapes)]
    proj = lambda idx, dt: split(matmul_cols(hs[0], w_in, idx * w_head, w_head, dt, extra=hs[1]))
    q_a, q_b, q_c = split([qa_p, qa_d]), proj(3, BF16), proj(6, BF16)
    *k_a, ka_cache = matmul_cols(hs[0], w_in, 1 * w_head, w_head, F32, extra=hs[1], head_major=DK_A // LANES)
    *v_a, va_cache = matmul_cols(hs[0], w_in, 2 * w_head, w_head, F32, extra=hs[1], head_major=DV_A // LANES)
    k_a, v_a = split(k_a), split(v_a)
    a_caches = [c.reshape(*shapes[0], H_A, DK_A) for c in (ka_cache, va_cache)]
    seq = shapes[0][1]
    keep = min(BAND_PAST, seq)
    *k_b, kb_tail = matmul_cols(hs[0], w_in, 4 * w_head, w_head, F32, extra=hs[1], tail=(seq, keep))
    *v_b, vb_tail = matmul_cols(hs[0], w_in, 5 * w_head, w_head, F32, extra=hs[1], tail=(seq, keep))
    k_b, v_b = split(k_b), split(v_b)
    tails = [t.reshape(shapes[0][0], keep, H_B, DH_B) for t in (kb_tail, vb_tail)]
    gates = matmul_cols(hs[0], w_in, 7 * w_head, N_BRANCH * d, BF16, extra=hs[1])

    ca_k, ca_v, cb_k, cb_v = caches
    o_a = [attention_a(q_a[0], k_a[0], v_a[0], lq1, lk1, lq2, lk2, subln_a, lam_init),
           attention_a_decode(q_a[1], k_a[1], v_a[1], ca_k, ca_v, lq1, lk1, lq2, lk2, subln_a, lam_init)]
    o_b = [attention_b(q_b[0], k_b[0], v_b[0], rel_bias),
           attention_b_decode(q_b[1], k_b[1], v_b[1], cb_k, cb_v, rel_bias, ca_k.shape[1])]
    o_c = [attention_c(q_c[0], mk_p, mv_p), attention_c(q_c[1], mk_d, mv_d)]

    x1, h2 = [], []
    for i in range(2):
        m = xs[i].shape[0]
        merged = merge_branches(o_a[i].reshape(m, -1), o_b[i].reshape(m, -1), o_c[i].reshape(m, -1),
                                gates[i], b_gate, w_br_a, w_br_b, w_br_c)
        a, b = proj_norm_residual(merged, w_out, xs[i], norm_mix_post, bm=512, rows=PROJ_ROWS,
                                  next_g=norm_ffn_pre)
        x1.append(a)
        h2.append(b)
    acts = ffn_in(h2[0], h2[1], w_ffn_in)
    ys = [proj_norm_residual(acts[i], w_ffn_out, x1[i], norm_ffn_post, bm=256, rows=256).reshape(*shapes[i], d)
          for i in range(2)]
    return (ys[0], *a_caches, *tails), (ys[1], k_a[1], v_a[1], k_b[1], v_b[1])


def kernel(x_prompt, x_sample, cache_a_k, cache_a_v, cache_b_k, cache_b_v, cache_mem_k, cache_mem_v, mem_prompt, norm_mix_pre, norm_mix_post, norm_mem, w_in, b_gate, lambda_q1, lambda_k1, lambda_q2, lambda_k2, subln_a, rel_bias_b, w_mem_kv, w_br_a, w_br_b, w_br_c, w_out, norm_ffn_pre, norm_ffn_post, w_ffn_in, w_ffn_out):
    depth = w_in.shape[0]
    bsz, s, d = x_prompt.shape
    n_mem = mem_prompt.shape[1]
    lb_prompt = min(BAND_PAST, s)
    yp, ys = x_prompt, x_sample
    outs = [[] for _ in range(10)]
    heads = lambda a, nh: a.reshape(a.shape[0], a.shape[1], nh, a.shape[2] // nh)
    flat = lambda a: a.reshape(a.shape[0], a.shape[1], -1)
    for l in range(depth):
        lam_init = 0.8 - 0.6 * math.exp(-0.3 * l)
        shared = (lam_init, norm_mix_pre[l], norm_mix_post[l], w_in[l], b_gate[l],
                  lambda_q1[l], lambda_k1[l], lambda_q2[l], lambda_k2[l], subln_a[l], rel_bias_b[l],
                  w_br_a[l], w_br_b[l], w_br_c[l], w_out[l],
                  norm_ffn_pre[l], norm_ffn_post[l], w_ffn_in[l], w_ffn_out[l].astype(BF16))

        mem_n = rms_norm_bf16(mem_prompt.reshape(bsz * n_mem, d), norm_mem[l])
        w_c = H_C * DH_C
        mk_p, mk_cache = matmul_cols(mem_n, w_mem_kv[l], 0, w_c, F32, head_major=DH_C // LANES)
        mv_p, mv_cache = matmul_cols(mem_n, w_mem_kv[l], w_c, w_c, F32, head_major=DH_C // LANES)
        mk_p, mv_p = mk_p.reshape(bsz, n_mem, w_c), mv_p.reshape(bsz, n_mem, w_c)
        caches = (cache_a_k[l], cache_a_v[l], cache_b_k[l], cache_b_v[l])
        (yp, ka, va, kb, vb), (ys, ka_d, va_d, kb_d, vb_d) = _layer(
            yp, ys, mk_p, mv_p, cache_mem_k[l], cache_mem_v[l], caches, *shared)
        mem_cache = lambda c: c.reshape(bsz, n_mem, DH_C // LANES, H_C, LANES).transpose(0, 1, 3, 2, 4).reshape(
            bsz, n_mem, H_C, DH_C)
        new = [ka, va, kb, vb, mem_cache(mk_cache), mem_cache(mv_cache),
               heads(ka_d, H_A), heads(va_d, H_A), heads(kb_d, H_B), heads(vb_d, H_B)]
        for o, a in zip(outs, new):
            o.append(a)
    return (yp, ys) + tuple(jnp.stack(o) for o in outs)
```

```python
import functools
import math

import jax
import jax.numpy as jnp
from jax import lax
from jax.experimental import pallas as pl
from jax.experimental.pallas import tpu as pltpu

F32 = jnp.float32
BF16 = jnp.bfloat16

CHUNK = 64
H_A = 8
DH_A = 64
DK_A = 2 * DH_A
DV_A = 128
H_B = 8
DH_B = 128
N_PREV_CHUNKS = 8
BAND_PAST = N_PREV_CHUNKS * CHUNK
MAX_REL = 128
H_C = 4
DH_C = 256
N_BRANCH = 3
EPS = 1e-6
NEG_INF = -1e30

LANES = 128
VMEM_LIMIT = 56 * 1024 * 1024
BAND_GROUP = 256
A_BLOCK = 256
MM_ROWS = 2048
MERGE_ROWS = 256
MERGE_CHUNK = 256
PROJ_ROWS = 128
SAFE_LOGIT = 40.0

CONTRACT_LAST = (((1,), (1,)), ((), ()))


def _params(*sem):
    return pltpu.CompilerParams(dimension_semantics=sem, vmem_limit_bytes=VMEM_LIMIT)


def _block(n, target):
    if n <= target:
        return n
    b = target
    while n % b:
        b //= 2
    return b


def _rms_kernel(x_ref, g_ref, o_ref):
    x = x_ref[...]
    ms = jnp.mean(x * x, axis=-1, keepdims=True)
    o_ref[...] = (x * lax.rsqrt(ms + EPS) * g_ref[...]).astype(o_ref.dtype)


def rms_norm_bf16(x, g):
    m, d = x.shape
    bm = _block(m, 512)
    return pl.pallas_call(
        _rms_kernel,
        grid=(m // bm,),
        in_specs=[pl.BlockSpec((bm, d), lambda i: (i, 0)),
                  pl.BlockSpec((1, d), lambda i: (0, 0))],
        out_specs=pl.BlockSpec((bm, d), lambda i: (i, 0)),
        out_shape=jax.ShapeDtypeStruct((m, d), BF16),
        name="rms_norm",
        compiler_params=_params("parallel"),
    )(x, g.reshape(1, d))


def _mm_kernel(*refs, has_extra, tail, head_major):
    refs = list(refs)
    a_ref = refs.pop(0)
    a2_ref = refs.pop(0) if has_extra else None
    w_ref = refs.pop(0)
    o_ref = refs.pop(0)
    o2_ref = refs.pop(0) if has_extra else None
    t_ref = refs.pop(0) if tail else None
    hm_ref = refs.pop(0) if head_major else None
    wb_ref, = refs

    def product(x_ref, y_ref, z_ref=None, tail_ref=None):
        n_rows = x_ref.shape[0]
        rows = _block(n_rows, 1024)
        for r0 in range(0, n_rows, rows):
            acc = jnp.dot(x_ref[r0:r0 + rows, :], wb_ref[...], preferred_element_type=F32)
            y_ref[r0:r0 + rows, :] = acc.astype(y_ref.dtype)
            heads = acc.shape[1] // LANES
            if z_ref is not None:
                for h in range(heads):
                    slot = (h % head_major) * (heads // head_major) + h // head_major
                    z_ref[pl.ds(r0 * heads + slot, rows, stride=heads), :] = (
                        acc[:, h * LANES:(h + 1) * LANES].astype(z_ref.dtype))
            if tail_ref is not None:
                blocks, keep = tail
                first = max(r0, n_rows - keep)
                if first < r0 + rows:
                    @pl.when(pl.program_id(1) % blocks == blocks - 1)
                    def _():
                        for h in range(heads):
                            tail_ref[pl.ds((first - (n_rows - keep)) * heads + h, r0 + rows - first,
                                           stride=heads), :] = (
                                acc[first - r0:, h * LANES:(h + 1) * LANES].astype(tail_ref.dtype))

    @pl.when(pl.program_id(1) == 0)
    def _():
        wb_ref[...] = w_ref[...].astype(BF16)
        if has_extra:
            product(a2_ref, o2_ref)

    product(a_ref, o_ref, hm_ref, t_ref)


def matmul_cols(a, w, col0, n, out_dtype, extra=None, tail=None, head_major=0, cache_dtype=None):
    m, k = a.shape
    cache_dtype = cache_dtype or out_dtype
    side_bytes = jnp.dtype(cache_dtype).itemsize if (tail or head_major) else 0
    bm = _block(m, MM_ROWS * 2 // max(jnp.dtype(out_dtype).itemsize, side_bytes))
    bn = _block(n, 1024)
    assert col0 % bn == 0
    off = col0 // bn
    in_specs = [pl.BlockSpec((bm, k), lambda j, i: (i, 0))]
    args = [a]
    out_specs = [pl.BlockSpec((bm, bn), lambda j, i: (i, j))]
    out_shape = [jax.ShapeDtypeStruct((m, n), out_dtype)]
    if extra is not None:
        m2 = extra.shape[0]
        in_specs.append(pl.BlockSpec((m2, k), lambda j, i: (0, 0)))
        args.append(extra)
        out_specs.append(pl.BlockSpec((m2, bn), lambda j, i: (0, j)))
        out_shape.append(jax.ShapeDtypeStruct((m2, n), cache_dtype))
    if tail is not None:
        seq, keep = tail
        assert seq % bm == 0 and keep <= bm and m % seq == 0
        blocks = seq // bm
        assert bn == n
        out_specs.append(pl.BlockSpec((keep * (n // LANES), LANES), lambda j, i: (i // blocks, 0)))
        out_shape.append(jax.ShapeDtypeStruct((m // seq * keep * (n // LANES), LANES), cache_dtype))
        tail = (blocks, keep)
    if head_major:
        assert bn == n
        heads = n // LANES
        out_specs.append(pl.BlockSpec((bm * heads, LANES), lambda j, i: (i, 0)))
        out_shape.append(jax.ShapeDtypeStruct((m * heads, LANES), cache_dtype))
    in_specs.append(pl.BlockSpec((k, bn), lambda j, i: (0, j + off)))
    args.append(w)
    return pl.pallas_call(
        functools.partial(_mm_kernel, has_extra=extra is not None, tail=tail, head_major=head_major),
        grid=(n // bn, m // bm),
        in_specs=in_specs,
        out_specs=out_specs,
        out_shape=out_shape,
        scratch_shapes=[pltpu.VMEM((k, bn), BF16)],
        name="matmul_cols",
        compiler_params=_params("arbitrary", "arbitrary"),
    )(*args)


def _norm_mm_kernel(x_ref, g_ref, a2_ref, w_ref, h_ref, o_ref, o2_ref, wb_ref, *, rows):
    @pl.when(pl.program_id(0) == 0)
    def _():
        wb_ref[...] = w_ref[...].astype(BF16)
        o2_ref[...] = jnp.dot(a2_ref[...], wb_ref[...], preferred_element_type=F32).astype(o2_ref.dtype)

    for r0 in range(0, x_ref.shape[0], rows):
        x = x_ref[r0:r0 + rows, :]
        ms = jnp.mean(x * x, axis=-1, keepdims=True)
        h = (x * lax.rsqrt(ms + EPS) * g_ref[...]).astype(BF16)
        h_ref[r0:r0 + rows, :] = h
        o_ref[r0:r0 + rows, :] = jnp.dot(h, wb_ref[...], preferred_element_type=F32).astype(o_ref.dtype)


def norm_matmul_cols(x, g, w, col0, n, out_dtype, extra):
    m, k = x.shape
    m2 = extra.shape[0]
    bm = _block(m, 512)
    assert col0 % n == 0
    off = col0 // n
    return pl.pallas_call(
        functools.partial(_norm_mm_kernel, rows=_block(bm, 128)),
        grid=(m // bm,),
        in_specs=[pl.BlockSpec((bm, k), lambda i: (i, 0)),
                  pl.BlockSpec((1, k), lambda i: (0, 0)),
                  pl.BlockSpec((m2, k), lambda i: (0, 0)),
                  pl.BlockSpec((k, n), lambda i: (0, off))],
        out_specs=[pl.BlockSpec((bm, k), lambda i: (i, 0)),
                   pl.BlockSpec((bm, n), lambda i: (i, 0)),
                   pl.BlockSpec((m2, n), lambda i: (0, 0))],
        out_shape=[jax.ShapeDtypeStruct((m, k), BF16), jax.ShapeDtypeStruct((m, n), out_dtype),
                   jax.ShapeDtypeStruct((m2, n), out_dtype)],
        scratch_shapes=[pltpu.VMEM((k, n), BF16)],
        name="norm_matmul_cols",
        compiler_params=_params("arbitrary"),
    )(x, g.reshape(1, k).astype(F32), extra, w)


def _diff_lambda(lq1_ref, lk1_ref, lq2_ref, lk2_ref, lam_init):
    return (jnp.exp(jnp.sum(lq1_ref[...] * lk1_ref[...], keepdims=True))
            - jnp.exp(jnp.sum(lq2_ref[...] * lk2_ref[...], keepdims=True)) + lam_init)


def _head_norm(o, sub_ref, lam_init):
    ms = jnp.mean(o * o, axis=-1, keepdims=True)
    return o * lax.rsqrt(ms + EPS) * sub_ref[...] * (1.0 - lam_init)


def _own_alibi(n, slope):
    r = lax.broadcasted_iota(jnp.int32, (n, n), 0)
    c = lax.broadcasted_iota(jnp.int32, (n, n), 1)
    bias = slope * (r - jnp.abs(r - c)).astype(F32)
    return jnp.where(c // CHUNK <= r // CHUNK, bias, NEG_INF)


def _max_sq_norms(x, lane_masks):
    sq = jnp.square(x.astype(F32))
    return [jnp.max(jnp.sum(jnp.where(mask, sq, 0.0), axis=-1, keepdims=True)) for mask in lane_masks]


def _attn_a_kernel(q_ref, k_ref, v_ref, slope_ref, kf_ref, qf_ref, lq1_ref, lk1_ref, lq2_ref, lk2_ref,
                   sub_ref, o_ref, ke_scr, ve_scr, *, tq, lam_init):
    args = (q_ref, k_ref, v_ref, slope_ref, lq1_ref, lk1_ref, lq2_ref, lk2_ref, sub_ref, o_ref)
    _attn_a_bounded(*args, kf_ref, qf_ref, ke_scr, ve_scr, tq=tq, lam_init=lam_init)
    first_half = lax.broadcasted_iota(jnp.int32, (1, DK_A), 1) < DH_A
    q_sq = _max_sq_norms(q_ref[0] * (DH_A ** -0.5), (first_half, ~first_half))
    k_sq = _max_sq_norms(ke_scr[:, :DK_A], (first_half, ~first_half))
    bound_sq = jnp.maximum(q_sq[0] * k_sq[0], q_sq[1] * k_sq[1])

    @pl.when(bound_sq > SAFE_LOGIT ** 2 / 1.01)
    def _():
        _attn_a_general(*args, tq=tq, lam_init=lam_init)


def _position_features(t):
    pos = jnp.arange(t, dtype=jnp.int32)[:, None]
    lo = pos & 7
    hi = (pos - lo).astype(F32)
    lo = lo.astype(F32)
    one = jnp.ones((t, 1), F32)
    pad = jnp.zeros((t, LANES - 4), F32)
    k_side = jnp.concatenate([hi, lo, one, one, pad], axis=1).astype(BF16)
    q_side = jnp.concatenate([one, one, -hi, -lo, pad], axis=1)
    return k_side, q_side


def _attn_a_bounded(q_ref, k_ref, v_ref, slope_ref, lq1_ref, lk1_ref, lq2_ref, lk2_ref, sub_ref, o_ref,
                    kf_ref, qf_ref, ke_scr, ve_scr, *, tq, lam_init):
    t = q_ref.shape[1]
    slope = slope_ref[0][:, :1]
    lam = _diff_lambda(lq1_ref, lk1_ref, lq2_ref, lk2_ref, lam_init)
    first_half = lax.broadcasted_iota(jnp.int32, (1, DK_A), 1) < DH_A
    ke_scr[:, :DK_A] = k_ref[0].astype(BF16)
    ke_scr[:, DK_A:] = kf_ref[...]
    ve_scr[:, :DV_A] = v_ref[0].astype(BF16)
    ve_scr[:, DV_A:] = jnp.ones((t, DV_A), BF16)

    r = lax.broadcasted_iota(jnp.int32, (tq, tq), 0)
    c = lax.broadcasted_iota(jnp.int32, (tq, tq), 1)
    own_fix = jnp.where(c // CHUNK <= r // CHUNK, -2.0 * slope * jnp.maximum(c - r, 0).astype(F32), NEG_INF)
    own_fix = jnp.concatenate([own_fix, own_fix], axis=0)

    for i in reversed(range(t // tq)):
        nb = i * tq
        q = q_ref[0, nb:nb + tq, :] * (DH_A ** -0.5)
        q_feat = (qf_ref[nb:nb + tq, :] * slope).astype(BF16)
        zero = jnp.zeros_like(q)
        qe = jnp.concatenate([jnp.concatenate([jnp.where(first_half, q, zero), q_feat], axis=1),
                              jnp.concatenate([jnp.where(first_half, zero, q), q_feat], axis=1)], axis=0)
        s_own = lax.dot_general(qe, ke_scr[nb:nb + tq, :], CONTRACT_LAST, preferred_element_type=F32)
        acc = jnp.dot(jnp.exp(s_own + own_fix).astype(BF16), ve_scr[nb:nb + tq, :],
                      preferred_element_type=F32)
        if nb:
            s_bef = lax.dot_general(qe, ke_scr[:nb, :], CONTRACT_LAST, preferred_element_type=F32)
            acc = acc + jnp.dot(jnp.exp(s_bef).astype(BF16), ve_scr[:nb, :], preferred_element_type=F32)
        o = acc[:, :DV_A] / acc[:, DV_A:]
        o = o[:tq] - lam * o[tq:]
        o_ref[0, nb:nb + tq, :] = _head_norm(o, sub_ref, lam_init).astype(o_ref.dtype)


def _attn_a_general(q_ref, k_ref, v_ref, slope_ref, lq1_ref, lk1_ref, lq2_ref, lk2_ref, sub_ref, o_ref,
                    *, tq, lam_init):
    t = q_ref.shape[1]
    n_blocks = t // tq
    slope = slope_ref[0][:, :1]
    lam = _diff_lambda(lq1_ref, lk1_ref, lq2_ref, lk2_ref, lam_init)
    k = k_ref[0].astype(BF16)
    v = v_ref[0].astype(BF16)
    first_half = lax.broadcasted_iota(jnp.int32, (1, DK_A), 1) < DH_A
    own_bias = _own_alibi(tq, slope)
    n_before_max = (n_blocks - 1) * tq
    if n_before_max:
        j = lax.broadcasted_iota(jnp.int32, (1, n_before_max), 1)
        before_bias = slope * (j - n_before_max).astype(F32)

    for i in range(n_blocks):
        nb = i * tq
        q = q_ref[0, nb:nb + tq, :] * (DH_A ** -0.5)
        k_own, v_own = k[nb:nb + tq], v[nb:nb + tq]

        def softmax_pv(qh):
            s_own = lax.dot_general(qh, k_own, CONTRACT_LAST, preferred_element_type=F32) + own_bias
            m = jnp.max(s_own, axis=-1, keepdims=True)
            if nb:
                s_bef = lax.dot_general(qh, k[:nb], CONTRACT_LAST, preferred_element_type=F32)
                s_bef = s_bef + before_bias[:, n_before_max - nb:]
                m = jnp.maximum(m, jnp.max(s_bef, axis=-1, keepdims=True))
            p_own = jnp.exp(s_own - m)
            l = jnp.sum(p_own, axis=-1, keepdims=True)
            pv = jnp.dot(p_own.astype(BF16), v_own, preferred_element_type=F32)
            if nb:
                p_bef = jnp.exp(s_bef - m)
                l = l + jnp.sum(p_bef, axis=-1, keepdims=True)
                pv = pv + jnp.dot(p_bef.astype(BF16), v[:nb], preferred_element_type=F32)
            return pv / l

        o = (softmax_pv(jnp.where(first_half, q, jnp.zeros_like(q)))
             - lam * softmax_pv(jnp.where(first_half, jnp.zeros_like(q), q)))
        o_ref[0, nb:nb + tq, :] = _head_norm(o, sub_ref, lam_init).astype(o_ref.dtype)


def _row(a):
    return a.reshape(1, -1).astype(F32)


def _const_spec(shape):
    return pl.BlockSpec(shape, lambda *_: (0,) * len(shape))


def attention_a(q, k, v, lq1, lk1, lq2, lk2, subln, lam_init):
    b, t, _ = q.shape
    tq = _block(t, A_BLOCK)
    assert tq % CHUNK == 0
    slopes = jnp.asarray([[[2.0 ** (-8.0 * (hh + 1) / H_A)] * LANES] for hh in range(H_A)], dtype=F32)
    head = lambda width: pl.BlockSpec((1, t, width), lambda bi, hi: (bi, 0, hi))
    k_feat, q_feat = _position_features(t)
    return pl.pallas_call(
        functools.partial(_attn_a_kernel, tq=tq, lam_init=lam_init),
        grid=(b, H_A),
        in_specs=[head(DK_A), head(DK_A), head(DV_A),
                  pl.BlockSpec((1, 1, LANES), lambda bi, hi: (hi, 0, 0)),
                  _const_spec((t, LANES)), _const_spec((t, LANES)),
                  _const_spec((1, DH_A)), _const_spec((1, DH_A)), _const_spec((1, DH_A)),
                  _const_spec((1, DH_A)), _const_spec((1, DV_A))],
        out_specs=head(DV_A),
        out_shape=jax.ShapeDtypeStruct((b, t, H_A * DV_A), BF16),
        scratch_shapes=[pltpu.VMEM((t, DK_A + LANES), BF16), pltpu.VMEM((t, 2 * DV_A), BF16)],
        name="attention_a",
        compiler_params=_params("parallel", "arbitrary"),
    )(q, k, v, slopes, k_feat, q_feat, _row(lq1), _row(lk1), _row(lq2), _row(lk2), _row(subln))


def _attn_a_decode_kernel(q_ref, kn_ref, vn_ref, ck_ref, cv_ref, lq1_ref, lk1_ref, lq2_ref, lk2_ref,
                          sub_ref, o_ref, *, lam_init):
    t = q_ref.shape[1]
    p_len = ck_ref.shape[1] // H_A
    lam = _diff_lambda(lq1_ref, lk1_ref, lq2_ref, lk2_ref, lam_init)
    first_half = lax.broadcasted_iota(jnp.int32, (1, DK_A), 1) < DH_A
    j = lax.broadcasted_iota(jnp.int32, (1, p_len), 1)
    before_dist = (j - p_len).astype(F32)
    for h in range(H_A):
        slope = 2.0 ** (-8.0 * (h + 1) / H_A)
        cols = slice(h * DK_A, (h + 1) * DK_A)
        q = q_ref[0, :, cols] * (DH_A ** -0.5)
        kc = ck_ref[0, pl.ds(h, p_len, stride=H_A), :].astype(BF16)
        vc = cv_ref[0, pl.ds(h, p_len, stride=H_A), :].astype(BF16)
        kn = kn_ref[0, :, cols].astype(BF16)
        vn = vn_ref[0, :, cols].astype(BF16)
        own_bias = _own_alibi(t, slope)
        before_bias = slope * before_dist

        def softmax_pv(qh):
            s_new = lax.dot_general(qh, kn, CONTRACT_LAST, preferred_element_type=F32) + own_bias
            s_old = lax.dot_general(qh, kc, CONTRACT_LAST, preferred_element_type=F32) + before_bias
            m = jnp.maximum(jnp.max(s_new, axis=-1, keepdims=True), jnp.max(s_old, axis=-1, keepdims=True))
            p_new = jnp.exp(s_new - m)
            p_old = jnp.exp(s_old - m)
            l = jnp.sum(p_new, axis=-1, keepdims=True) + jnp.sum(p_old, axis=-1, keepdims=True)
            pv = (jnp.dot(p_new.astype(BF16), vn, preferred_element_type=F32)
                  + jnp.dot(p_old.astype(BF16), vc, preferred_element_type=F32))
            return pv / l

        o = (softmax_pv(jnp.where(first_half, q, jnp.zeros_like(q)))
             - lam * softmax_pv(jnp.where(first_half, jnp.zeros_like(q), q)))
        o_ref[0, :, cols] = _head_norm(o, sub_ref, lam_init).astype(o_ref.dtype)


def attention_a_decode(q, k_new, v_new, cache_k, cache_v, lq1, lk1, lq2, lk2, subln, lam_init):
    b, t, w = q.shape
    p_len = cache_k.shape[1]
    assert p_len % CHUNK == 0 and t <= CHUNK
    rows = lambda a: a.reshape(b, p_len * H_A, a.shape[-1])
    new = pl.BlockSpec((1, t, w), lambda bi: (bi, 0, 0))
    old = pl.BlockSpec((1, p_len * H_A, DK_A), lambda bi: (bi, 0, 0))
    return pl.pallas_call(
        functools.partial(_attn_a_decode_kernel, lam_init=lam_init),
        grid=(b,),
        in_specs=[new, new, new, old, old,
                  _const_spec((1, DH_A)), _const_spec((1, DH_A)), _const_spec((1, DH_A)),
                  _const_spec((1, DH_A)), _const_spec((1, DV_A))],
        out_specs=new,
        out_shape=jax.ShapeDtypeStruct((b, t, w), BF16),
        name="attention_a_decode",
        compiler_params=_params("parallel"),
    )(q, k_new, v_new, rows(cache_k), rows(cache_v), _row(lq1), _row(lk1), _row(lq2), _row(lk2), _row(subln))


def _band_tile(tq):
    pad = -(-tq // LANES) * LANES
    return BAND_PAST + pad, pad


def _band_bias_mask(g_row, tq):
    tile_w, pad = _band_tile(tq)
    width = g_row.shape[1]
    assert width == tile_w + pad
    rolled = pltpu.roll(jnp.broadcast_to(g_row, (tq, width)), width - pad, 1, stride=1, stride_axis=0)
    r = lax.broadcasted_iota(jnp.int32, (tq, tile_w), 0)
    c = lax.broadcasted_iota(jnp.int32, (tq, tile_w), 1)
    qch = r // CHUNK
    kch = c // CHUNK - N_PREV_CHUNKS
    valid = (kch <= qch) & (kch >= qch - N_PREV_CHUNKS)
    return jnp.where(valid, rolled[:, :tile_w], NEG_INF)


def _toeplitz_rows(rel_bias, tq):
    tile_w, pad = _band_tile(tq)
    width = tile_w + pad
    n_lo = pad + BAND_PAST - MAX_REL
    n_hi = max(width - n_lo - (2 * MAX_REL + 1), 0)
    lo = jnp.broadcast_to(rel_bias[:, :1], (H_B, n_lo))
    hi = jnp.broadcast_to(rel_bias[:, -1:], (H_B, n_hi))
    g = jnp.concatenate([lo, rel_bias, hi], axis=1)[:, :width]
    return g.reshape(H_B, 1, width).astype(F32)


def _attn_b_kernel(q_ref, k_ref, v_ref, g_ref, o_ref, ve_scr, *, tq):
    t = q_ref.shape[1]
    tile_w, _ = _band_tile(tq)
    scale = DH_B ** -0.5
    bias_mask = _band_bias_mask(g_ref[0], tq)

    def groups():
        for gi in range(t // tq):
            qa = gi * tq
            lo = max(qa - BAND_PAST, 0)
            hi = qa + tq
            off = lo - qa + BAND_PAST
            assert off % LANES == 0 and off + hi - lo == tile_w
            yield slice(qa, hi), slice(lo, hi), off

    log2e = math.log2(math.e)
    k = k_ref[0].astype(BF16)
    ve_scr[:, :DH_B] = v_ref[0].astype(BF16)
    ve_scr[:, DH_B:] = jnp.ones((t, DH_B), BF16)
    bias2 = bias_mask * log2e
    for rows, keys, off in groups():
        s = lax.dot_general(q_ref[0, rows, :], k[keys], CONTRACT_LAST, preferred_element_type=F32)
        p = jnp.exp2(s * (scale * log2e) + bias2[:, off:])
        acc = jnp.dot(p.astype(BF16), ve_scr[keys, :], preferred_element_type=F32)
        o_ref[0, rows, :] = (acc[:, :DH_B] / acc[:, DH_B:]).astype(o_ref.dtype)

    (q_sq,), (k_sq,) = _max_sq_norms(q_ref[0], (True,)), _max_sq_norms(k, (True,))
    room = SAFE_LOGIT - jnp.max(jnp.abs(g_ref[0]))
    safe = jnp.logical_and(room > 0.0, q_sq * k_sq * (scale * scale * 1.01) <= room * room)

    @pl.when(jnp.logical_not(safe))
    def _():
        k = k_ref[0].astype(BF16)
        v = v_ref[0].astype(BF16)
        for rows, keys, off in groups():
            s = lax.dot_general(q_ref[0, rows, :], k[keys], CONTRACT_LAST, preferred_element_type=F32) * scale
            s = s + bias_mask[:, off:]
            m = jnp.max(s, axis=-1, keepdims=True)
            p = jnp.exp(s - m)
            l = jnp.sum(p, axis=-1, keepdims=True)
            o = jnp.dot(p.astype(BF16), v[keys], preferred_element_type=F32) / l
            o_ref[0, rows, :] = o.astype(o_ref.dtype)


def attention_b(q, k, v, rel_bias):
    b, t, _ = q.shape
    tq = _block(t, BAND_GROUP)
    assert tq % LANES == 0
    g = _toeplitz_rows(rel_bias, tq)
    head = pl.BlockSpec((1, t, DH_B), lambda bi, hi: (bi, 0, hi))
    return pl.pallas_call(
        functools.partial(_attn_b_kernel, tq=tq),
        grid=(b, H_B),
        in_specs=[head, head, head, pl.BlockSpec((1, 1, g.shape[2]), lambda bi, hi: (hi, 0, 0))],
        out_specs=head,
        out_shape=jax.ShapeDtypeStruct((b, t, H_B * DH_B), BF16),
        scratch_shapes=[pltpu.VMEM((t, 2 * DH_B), BF16)],
        name="attention_b",
        compiler_params=_params("parallel", "arbitrary"),
    )(q, k, v, g)


def _attn_b_decode_kernel(q_ref, kn_ref, vn_ref, ck_ref, cv_ref, g_ref, o_ref):
    t = q_ref.shape[1]
    scale = DH_B ** -0.5
    for h in range(H_B):
        cols = slice(h * DH_B, (h + 1) * DH_B)
        q = q_ref[0, :, cols]
        kc = ck_ref[0, pl.ds(h, BAND_PAST, stride=H_B), :].astype(BF16)
        vc = cv_ref[0, pl.ds(h, BAND_PAST, stride=H_B), :].astype(BF16)
        kn = kn_ref[0, :, cols].astype(BF16)
        vn = vn_ref[0, :, cols].astype(BF16)
        bias_mask = _band_bias_mask(g_ref[h], t)
        s_old = lax.dot_general(q, kc, CONTRACT_LAST, preferred_element_type=F32) * scale
        s_old = s_old + bias_mask[:, :BAND_PAST]
        s_new = lax.dot_general(q, kn, CONTRACT_LAST, preferred_element_type=F32) * scale
        s_new = s_new + bias_mask[:, BAND_PAST:BAND_PAST + t]
        m = jnp.maximum(jnp.max(s_new, axis=-1, keepdims=True), jnp.max(s_old, axis=-1, keepdims=True))
        p_new = jnp.exp(s_new - m)
        p_old = jnp.exp(s_old - m)
        l = jnp.sum(p_new, axis=-1, keepdims=True) + jnp.sum(p_old, axis=-1, keepdims=True)
        pv = (jnp.dot(p_new.astype(BF16), vn, preferred_element_type=F32)
              + jnp.dot(p_old.astype(BF16), vc, preferred_element_type=F32))
        o_ref[0, :, cols] = (pv / l).astype(o_ref.dtype)


def attention_b_decode(q, k_new, v_new, cache_k, cache_v, rel_bias, p_len):
    b, t, w = q.shape
    assert cache_k.shape[1] == BAND_PAST and p_len % CHUNK == 0 and p_len >= BAND_PAST and t <= CHUNK
    g = _toeplitz_rows(rel_bias, t)
    rows = lambda a: a.reshape(b, BAND_PAST * H_B, a.shape[-1])
    new = pl.BlockSpec((1, t, w), lambda bi: (bi, 0, 0))
    old = pl.BlockSpec((1, BAND_PAST * H_B, DH_B), lambda bi: (bi, 0, 0))
    return pl.pallas_call(
        _attn_b_decode_kernel,
        grid=(b,),
        in_specs=[new, new, new, old, old, _const_spec(g.shape)],
        out_specs=new,
        out_shape=jax.ShapeDtypeStruct((b, t, w), BF16),
        name="attention_b_decode",
        compiler_params=_params("parallel"),
    )(q, k_new, v_new, rows(cache_k), rows(cache_v), g)


def _attn_c_kernel(q_ref, k_ref, v_ref, o_ref, *, tq, cache_order):
    t = q_ref.shape[1]
    scale = DH_C ** -0.5
    blocks = DH_C // LANES

    def head_rows(ref, h):
        if not cache_order:
            return ref[0, :, h * DH_C:(h + 1) * DH_C]
        n_mem = ref.shape[1] // (blocks * H_C)
        return jnp.concatenate([ref[0, pl.ds(j * H_C + h, n_mem, stride=blocks * H_C), :]
                                for j in range(blocks)], axis=1)

    for h in range(H_C):
        cols = slice(h * DH_C, (h + 1) * DH_C)
        k = head_rows(k_ref, h).astype(BF16)
        v = head_rows(v_ref, h).astype(BF16)

        def attend(probabilities):
            for i in range(t // tq):
                rows = slice(i * tq, (i + 1) * tq)
                s = lax.dot_general(q_ref[0, rows, cols], k, CONTRACT_LAST, preferred_element_type=F32)
                p = probabilities(s)
                l = jnp.sum(p, axis=-1, keepdims=True)
                o = jnp.dot(p.astype(BF16), v, preferred_element_type=F32) / l
                o_ref[0, rows, cols] = o.astype(o_ref.dtype)

        attend(lambda s: jnp.exp2(s * (scale * math.log2(math.e))))
        (q_sq,), (k_sq,) = _max_sq_norms(q_ref[0, :, cols], (True,)), _max_sq_norms(k, (True,))

        @pl.when(q_sq * k_sq * (scale * scale * 1.01) > SAFE_LOGIT ** 2)
        def _():
            def shifted(s):
                s = s * scale
                return jnp.exp(s - jnp.max(s, axis=-1, keepdims=True))
            attend(shifted)


def attention_c(q, mk, mv):
    b, t, w = q.shape
    cache_order = mk.ndim == 4
    if cache_order:
        n_mem, blocks = mk.shape[1], DH_C // LANES
        tiles = lambda c: c.reshape(b, n_mem, H_C, blocks, LANES).transpose(0, 1, 3, 2, 4).reshape(
            b, n_mem * blocks * H_C, LANES)
        mk, mv = tiles(mk), tiles(mv)
    kv_spec = pl.BlockSpec((1,) + mk.shape[1:], lambda bi: (bi, 0, 0))
    tq = _block(t, 512)
    return pl.pallas_call(
        functools.partial(_attn_c_kernel, tq=tq, cache_order=cache_order),
        grid=(b,),
        in_specs=[pl.BlockSpec((1, t, w), lambda bi: (bi, 0, 0)), kv_spec, kv_spec],
        out_specs=pl.BlockSpec((1, t, w), lambda bi: (bi, 0, 0)),
        out_shape=jax.ShapeDtypeStruct((b, t, w), BF16),
        name="attention_c",
        compiler_params=_params("parallel"),
    )(q, mk, mv)


def _merge_kernel(oa_ref, ob_ref, oc_ref, ga_ref, gb_ref, gc_ref, ba_ref, bb_ref, bc_ref,
                  wa_f32, wb_f32, wc_f32, o_ref, wa_ref, wb_ref, wc_ref):
    @pl.when(pl.program_id(0) == 0)
    def _():
        wa_ref[...] = wa_f32[...].astype(BF16)
        wb_ref[...] = wb_f32[...].astype(BF16)
        wc_ref[...] = wc_f32[...].astype(BF16)

    rows = _block(o_ref.shape[0], MERGE_CHUNK)
    for r0 in range(0, o_ref.shape[0], rows):
        sl = slice(r0, r0 + rows)

        def branch(o_r, g_r, b_r, w_r):
            gate = jax.nn.sigmoid(g_r[sl, :].astype(F32) + b_r[...])
            return gate * jnp.dot(o_r[sl, :], w_r[...], preferred_element_type=F32)

        merged = (branch(oa_ref, ga_ref, ba_ref, wa_ref) + branch(ob_ref, gb_ref, bb_ref, wb_ref)
                  + branch(oc_ref, gc_ref, bc_ref, wc_ref))
        o_ref[sl, :] = merged.astype(o_ref.dtype)


def merge_branches(oa, ob, oc, gate_logits, b_gate, wa, wb, wc):
    m, w_in = oa.shape
    d = wa.shape[1]
    bm = _block(m, MERGE_ROWS)
    row = lambda width: pl.BlockSpec((bm, width), lambda i: (i, 0))
    gate = lambda j: pl.BlockSpec((bm, d), lambda i: (i, j))
    bias = lambda j: pl.BlockSpec((1, d), lambda i: (0, j))
    weight = pl.BlockSpec((w_in, d), lambda i: (0, 0), pipeline_mode=pl.Buffered(1))
    bg = b_gate.reshape(1, N_BRANCH * d).astype(F32)
    return pl.pallas_call(
        _merge_kernel,
        grid=(m // bm,),
        in_specs=[row(w_in), row(w_in), row(w_in), gate(0), gate(1), gate(2),
                  bias(0), bias(1), bias(2), weight, weight, weight],
        out_specs=pl.BlockSpec((bm, d), lambda i: (i, 0)),
        out_shape=jax.ShapeDtypeStruct((m, d), BF16),
        scratch_shapes=[pltpu.VMEM((w_in, d), BF16)] * N_BRANCH,
        name="merge_branches",
        compiler_params=_params("arbitrary"),
    )(oa, ob, oc, gate_logits, gate_logits, gate_logits, bg, bg, bg, wa, wb, wc)


def _proj_norm_res_kernel(*refs, next_norm, rows, cast_w):
    refs = list(refs)
    if cast_w:
        w_bf = refs.pop()
    if next_norm:
        a_ref, w_ref, x_ref, g_ref, g2_ref, o_ref, h_ref = refs
    else:
        a_ref, w_ref, x_ref, g_ref, o_ref = refs
    if cast_w:
        @pl.when(pl.program_id(0) == 0)
        def _():
            w_bf[...] = w_ref[...].astype(BF16)
        w_ref = w_bf
    bm = a_ref.shape[0]
    for r0 in range(0, bm, rows):
        sl = slice(r0, r0 + rows)
        y = jnp.dot(a_ref[sl, :], w_ref[...], preferred_element_type=F32)
        ms = jnp.mean(y * y, axis=-1, keepdims=True)
        o = x_ref[sl, :] + y * lax.rsqrt(ms + EPS) * g_ref[...]
        o_ref[sl, :] = o
        if next_norm:
            ms2 = jnp.mean(o * o, axis=-1, keepdims=True)
            h_ref[sl, :] = (o * lax.rsqrt(ms2 + EPS) * g2_ref[...]).astype(h_ref.dtype)


def proj_norm_residual(a, w, x, g, bm, rows, next_g=None):
    m, k = a.shape
    d = w.shape[1]
    bm = _block(m, bm)
    rows = _block(bm, rows)
    vec = pl.BlockSpec((1, d), lambda i: (0, 0))
    row = pl.BlockSpec((bm, d), lambda i: (i, 0))
    in_specs = [pl.BlockSpec((bm, k), lambda i: (i, 0)),
                pl.BlockSpec((k, d), lambda i: (0, 0), pipeline_mode=pl.Buffered(1)), row, vec]
    args = [a, w, x, g.reshape(1, d).astype(F32)]
    out_specs, out_shape = row, jax.ShapeDtypeStruct((m, d), F32)
    if next_g is not None:
        in_specs.append(vec)
        args.append(next_g.reshape(1, d).astype(F32))
        out_specs = (row, row)
        out_shape = (out_shape, jax.ShapeDtypeStruct((m, d), BF16))
    cast_w = w.dtype != BF16
    return pl.pallas_call(
        functools.partial(_proj_norm_res_kernel, next_norm=next_g is not None, rows=rows, cast_w=cast_w),
        grid=(m // bm,),
        in_specs=in_specs,
        out_specs=out_specs,
        out_shape=out_shape,
        scratch_shapes=[pltpu.VMEM((k, d), BF16)] if cast_w else [],
        name="proj_norm_residual",
        compiler_params=_params("arbitrary"),
    )(*args)


def _ffn_in_kernel(h_ref, h2_ref, wa_ref, wb_ref, o_ref, o2_ref, wa_bf, wb_bf):
    def swiglu(h):
        a = jnp.dot(h, wa_bf[...], preferred_element_type=F32)
        b = jnp.dot(h, wb_bf[...], preferred_element_type=F32)
        return (jax.nn.silu(a) * b).astype(o_ref.dtype)

    @pl.when(pl.program_id(1) == 0)
    def _():
        wa_bf[...] = wa_ref[...].astype(BF16)
        wb_bf[...] = wb_ref[...].astype(BF16)
        o2_ref[...] = swiglu(h2_ref[...])

    o_ref[...] = swiglu(h_ref[...])


def ffn_in(h, extra, w):
    m, k = h.shape
    m2 = extra.shape[0]
    f = w.shape[1] // 2
    bm = _block(m, 1024)
    bf = 512
    assert f % bf == 0
    nf = f // bf
    return pl.pallas_call(
        _ffn_in_kernel,
        grid=(nf, m // bm),
        in_specs=[pl.BlockSpec((bm, k), lambda j, i: (i, 0)),
                  pl.BlockSpec((m2, k), lambda j, i: (0, 0)),
                  pl.BlockSpec((k, bf), lambda j, i: (0, j)),
                  pl.BlockSpec((k, bf), lambda j, i: (0, j + nf))],
        out_specs=[pl.BlockSpec((bm, bf), lambda j, i: (i, j)),
                   pl.BlockSpec((m2, bf), lambda j, i: (0, j))],
        out_shape=[jax.ShapeDtypeStruct((m, f), BF16), jax.ShapeDtypeStruct((m2, f), BF16)],
        scratch_shapes=[pltpu.VMEM((k, bf), BF16), pltpu.VMEM((k, bf), BF16)],
        name="ffn_in",
        compiler_params=_params("arbitrary", "arbitrary"),
    )(h, extra, w, w)


def _layer(xp, xd, mk_p, mv_p, mk_d, mv_d, caches, lam_init, norm_mix_pre, norm_mix_post, w_in, b_gate,
           lq1, lk1, lq2, lk2, subln_a, rel_bias, w_br_a, w_br_b, w_br_c, w_out,
           norm_ffn_pre, norm_ffn_post, w_ffn_in, w_ffn_out):
    d = xp.shape[-1]
    w_head = H_A * DK_A
    shapes = [xp.shape[:2], xd.shape[:2]]
    xs = [xp.reshape(-1, d), xd.reshape(-1, d)]
    h_d = rms_norm_bf16(xs[1], norm_mix_pre)
    h_p, qa_p, qa_d = norm_matmul_cols(xs[0], norm_mix_pre, w_in, 0, w_head, BF16, h_d)
    hs = [h_p, h_d]
    split = lambda outs: [o.reshape(*shp, w_head) for o, shp in zip(outs, shapes)]
    proj = lambda idx, dt: split(matmul_cols(hs[0], w_in, idx * w_head, w_head, dt, extra=hs[1]))
    q_a, q_b, q_c = split([qa_p, qa_d]), proj(3, BF16), proj(6, BF16)
    kv = dict(extra=hs[1], cache_dtype=F32)
    *k_a, ka_cache = matmul_cols(hs[0], w_in, 1 * w_head, w_head, BF16, head_major=DK_A // LANES, **kv)
    *v_a, va_cache = matmul_cols(hs[0], w_in, 2 * w_head, w_head, BF16, head_major=DV_A // LANES, **kv)
    k_a, v_a = split(k_a), split(v_a)
    a_caches = [c.reshape(*shapes[0], H_A, DK_A) for c in (ka_cache, va_cache)]
    seq = shapes[0][1]
    keep = min(BAND_PAST, seq)
    *k_b, kb_tail = matmul_cols(hs[0], w_in, 4 * w_head, w_head, BF16, tail=(seq, keep), **kv)
    *v_b, vb_tail = matmul_cols(hs[0], w_in, 5 * w_head, w_head, BF16, tail=(seq, keep), **kv)
    k_b, v_b = split(k_b), split(v_b)
    tails = [t.reshape(shapes[0][0], keep, H_B, DH_B) for t in (kb_tail, vb_tail)]
    gates = matmul_cols(hs[0], w_in, 7 * w_head, N_BRANCH * d, BF16, extra=hs[1])

    ca_k, ca_v, cb_k, cb_v = caches
    o_a = [attention_a(q_a[0], k_a[0], v_a[0], lq1, lk1, lq2, lk2, subln_a, lam_init),
           attention_a_decode(q_a[1], k_a[1], v_a[1], ca_k, ca_v, lq1, lk1, lq2, lk2, subln_a, lam_init)]
    o_b = [attention_b(q_b[0], k_b[0], v_b[0], rel_bias),
           attention_b_decode(q_b[1], k_b[1], v_b[1], cb_k, cb_v, rel_bias, ca_k.shape[1])]
    o_c = [attention_c(q_c[0], mk_p, mv_p), attention_c(q_c[1], mk_d, mv_d)]

    x1, h2 = [], []
    for i in range(2):
        m = xs[i].shape[0]
        merged = merge_branches(o_a[i].reshape(m, -1), o_b[i].reshape(m, -1), o_c[i].reshape(m, -1),
                                gates[i], b_gate, w_br_a, w_br_b, w_br_c)
        a, b = proj_norm_residual(merged, w_out, xs[i], norm_mix_post, bm=512, rows=PROJ_ROWS,
                                  next_g=norm_ffn_pre)
        x1.append(a)
        h2.append(b)
    acts = ffn_in(h2[0], h2[1], w_ffn_in)
    ys = [proj_norm_residual(acts[i], w_ffn_out, x1[i], norm_ffn_post, bm=256, rows=256).reshape(*shapes[i], d)
          for i in range(2)]
    return (ys[0], *a_caches, *tails), (ys[1], k_a[1], v_a[1], k_b[1], v_b[1])


def kernel(x_prompt, x_sample, cache_a_k, cache_a_v, cache_b_k, cache_b_v, cache_mem_k, cache_mem_v, mem_prompt, norm_mix_pre, norm_mix_post, norm_mem, w_in, b_gate, lambda_q1, lambda_k1, lambda_q2, lambda_k2, subln_a, rel_bias_b, w_mem_kv, w_br_a, w_br_b, w_br_c, w_out, norm_ffn_pre, norm_ffn_post, w_ffn_in, w_ffn_out):
    depth = w_in.shape[0]
    bsz, s, d = x_prompt.shape
    n_mem = mem_prompt.shape[1]
    lb_prompt = min(BAND_PAST, s)
    yp, ys = x_prompt, x_sample
    outs = [[] for _ in range(10)]
    heads = lambda a, nh: a.reshape(a.shape[0], a.shape[1], nh, a.shape[2] // nh)
    flat = lambda a: a.reshape(a.shape[0], a.shape[1], -1)
    for l in range(depth):
        lam_init = 0.8 - 0.6 * math.exp(-0.3 * l)
        shared = (lam_init, norm_mix_pre[l], norm_mix_post[l], w_in[l], b_gate[l],
                  lambda_q1[l], lambda_k1[l], lambda_q2[l], lambda_k2[l], subln_a[l], rel_bias_b[l],
                  w_br_a[l], w_br_b[l], w_br_c[l], w_out[l],
                  norm_ffn_pre[l], norm_ffn_post[l], w_ffn_in[l], w_ffn_out[l].astype(BF16))

        mem_n = rms_norm_bf16(mem_prompt.reshape(bsz * n_mem, d), norm_mem[l])
        w_c = H_C * DH_C
        mk_p, mk_cache = matmul_cols(mem_n, w_mem_kv[l], 0, w_c, BF16, head_major=DH_C // LANES, cache_dtype=F32)
        mv_p, mv_cache = matmul_cols(mem_n, w_mem_kv[l], w_c, w_c, BF16, head_major=DH_C // LANES, cache_dtype=F32)
        mk_p, mv_p = mk_p.reshape(bsz, n_mem, w_c), mv_p.reshape(bsz, n_mem, w_c)
        caches = (cache_a_k[l], cache_a_v[l], cache_b_k[l], cache_b_v[l])
        (yp, ka, va, kb, vb), (ys, ka_d, va_d, kb_d, vb_d) = _layer(
            yp, ys, mk_p, mv_p, cache_mem_k[l], cache_mem_v[l], caches, *shared)
        mem_cache = lambda c: c.reshape(bsz, n_mem, DH_C // LANES, H_C, LANES).transpose(0, 1, 3, 2, 4).reshape(
            bsz, n_mem, H_C, DH_C)
        new = [ka, va, kb, vb, mem_cache(mk_cache), mem_cache(mv_cache),
               heads(ka_d, H_A), heads(va_d, H_A), heads(kb_d, H_B), heads(vb_d, H_B)]
        for o, a in zip(outs, new):
            o.append(a)
    return (yp, ys) + tuple(jnp.stack(o) for o in outs)
```

```python
import functools
import math

import jax
import jax.numpy as jnp
from jax import lax
from jax.experimental import pallas as pl
from jax.experimental.pallas import tpu as pltpu

F32 = jnp.float32
BF16 = jnp.bfloat16

CHUNK = 64
H_A = 8
DH_A = 64
DK_A = 2 * DH_A
DV_A = 128
H_B = 8
DH_B = 128
N_PREV_CHUNKS = 8
BAND_PAST = N_PREV_CHUNKS * CHUNK
MAX_REL = 128
H_C = 4
DH_C = 256
N_BRANCH = 3
EPS = 1e-6
NEG_INF = -1e30

LANES = 128
VMEM_LIMIT = 56 * 1024 * 1024
BAND_GROUP = 256
A_BLOCK = 256
A_HEADS = 2
MM_ROWS = 2048
MERGE_ROWS = 256
MERGE_CHUNK = 256
PROJ_ROWS = 128
SAFE_LOGIT = 40.0

CONTRACT_LAST = (((1,), (1,)), ((), ()))


def _params(*sem):
    return pltpu.CompilerParams(dimension_semantics=sem, vmem_limit_bytes=VMEM_LIMIT)


def _block(n, target):
    if n <= target:
        return n
    b = target
    while n % b:
        b //= 2
    return b


def _rms_kernel(x_ref, g_ref, o_ref):
    x = x_ref[...]
    ms = jnp.mean(x * x, axis=-1, keepdims=True)
    o_ref[...] = (x * lax.rsqrt(ms + EPS) * g_ref[...]).astype(o_ref.dtype)


def rms_norm_bf16(x, g):
    m, d = x.shape
    bm = _block(m, 512)
    return pl.pallas_call(
        _rms_kernel,
        grid=(m // bm,),
        in_specs=[pl.BlockSpec((bm, d), lambda i: (i, 0)),
                  pl.BlockSpec((1, d), lambda i: (0, 0))],
        out_specs=pl.BlockSpec((bm, d), lambda i: (i, 0)),
        out_shape=jax.ShapeDtypeStruct((m, d), BF16),
        name="rms_norm",
        compiler_params=_params("parallel"),
    )(x, g.reshape(1, d))


def _mm_kernel(*refs, has_extra, tail, head_major):
    refs = list(refs)
    a_ref = refs.pop(0)
    a2_ref = refs.pop(0) if has_extra else None
    w_ref = refs.pop(0)
    o_ref = refs.pop(0)
    o2_ref = refs.pop(0) if has_extra else None
    t_ref = refs.pop(0) if tail else None
    hm_ref = refs.pop(0) if head_major else None
    wb_ref, = refs

    def product(x_ref, y_ref, z_ref=None, tail_ref=None):
        n_rows = x_ref.shape[0]
        rows = _block(n_rows, 1024)
        for r0 in range(0, n_rows, rows):
            acc = jnp.dot(x_ref[r0:r0 + rows, :], wb_ref[...], preferred_element_type=F32)
            y_ref[r0:r0 + rows, :] = acc.astype(y_ref.dtype)
            heads = acc.shape[1] // LANES
            if z_ref is not None:
                for h in range(heads):
                    slot = (h % head_major) * (heads // head_major) + h // head_major
                    z_ref[pl.ds(r0 * heads + slot, rows, stride=heads), :] = (
                        acc[:, h * LANES:(h + 1) * LANES].astype(z_ref.dtype))
            if tail_ref is not None:
                blocks, keep = tail
                first = max(r0, n_rows - keep)
                if first < r0 + rows:
                    @pl.when(pl.program_id(1) % blocks == blocks - 1)
                    def _():
                        for h in range(heads):
                            tail_ref[pl.ds((first - (n_rows - keep)) * heads + h, r0 + rows - first,
                                           stride=heads), :] = (
                                acc[first - r0:, h * LANES:(h + 1) * LANES].astype(tail_ref.dtype))

    @pl.when(pl.program_id(1) == 0)
    def _():
        wb_ref[...] = w_ref[...].astype(BF16)
        if has_extra:
            product(a2_ref, o2_ref)

    product(a_ref, o_ref, hm_ref, t_ref)


def matmul_cols(a, w, col0, n, out_dtype, extra=None, tail=None, head_major=0, cache_dtype=None):
    m, k = a.shape
    cache_dtype = cache_dtype or out_dtype
    side_bytes = jnp.dtype(cache_dtype).itemsize if (tail or head_major) else 0
    bm = _block(m, MM_ROWS * 2 // max(jnp.dtype(out_dtype).itemsize, side_bytes))
    bn = _block(n, 1024)
    assert col0 % bn == 0
    off = col0 // bn
    in_specs = [pl.BlockSpec((bm, k), lambda j, i: (i, 0))]
    args = [a]
    out_specs = [pl.BlockSpec((bm, bn), lambda j, i: (i, j))]
    out_shape = [jax.ShapeDtypeStruct((m, n), out_dtype)]
    if extra is not None:
        m2 = extra.shape[0]
        in_specs.append(pl.BlockSpec((m2, k), lambda j, i: (0, 0)))
        args.append(extra)
        out_specs.append(pl.BlockSpec((m2, bn), lambda j, i: (0, j)))
        out_shape.append(jax.ShapeDtypeStruct((m2, n), cache_dtype))
    if tail is not None:
        seq, keep = tail
        assert seq % bm == 0 and keep <= bm and m % seq == 0
        blocks = seq // bm
        assert bn == n
        out_specs.append(pl.BlockSpec((keep * (n // LANES), LANES), lambda j, i: (i // blocks, 0)))
        out_shape.append(jax.ShapeDtypeStruct((m // seq * keep * (n // LANES), LANES), cache_dtype))
        tail = (blocks, keep)
    if head_major:
        assert bn == n
        heads = n // LANES
        out_specs.append(pl.BlockSpec((bm * heads, LANES), lambda j, i: (i, 0)))
        out_shape.append(jax.ShapeDtypeStruct((m * heads, LANES), cache_dtype))
    in_specs.append(pl.BlockSpec((k, bn), lambda j, i: (0, j + off)))
    args.append(w)
    return pl.pallas_call(
        functools.partial(_mm_kernel, has_extra=extra is not None, tail=tail, head_major=head_major),
        grid=(n // bn, m // bm),
        in_specs=in_specs,
        out_specs=out_specs,
        out_shape=out_shape,
        scratch_shapes=[pltpu.VMEM((k, bn), BF16)],
        name="matmul_cols",
        compiler_params=_params("arbitrary", "arbitrary"),
    )(*args)


def _norm_mm_kernel(x_ref, g_ref, a2_ref, w_ref, h_ref, o_ref, o2_ref, wb_ref, *, rows):
    @pl.when(pl.program_id(0) == 0)
    def _():
        wb_ref[...] = w_ref[...].astype(BF16)
        o2_ref[...] = jnp.dot(a2_ref[...], wb_ref[...], preferred_element_type=F32).astype(o2_ref.dtype)

    for r0 in range(0, x_ref.shape[0], rows):
        x = x_ref[r0:r0 + rows, :]
        ms = jnp.mean(x * x, axis=-1, keepdims=True)
        h = (x * lax.rsqrt(ms + EPS) * g_ref[...]).astype(BF16)
        h_ref[r0:r0 + rows, :] = h
        o_ref[r0:r0 + rows, :] = jnp.dot(h, wb_ref[...], preferred_element_type=F32).astype(o_ref.dtype)


def norm_matmul_cols(x, g, w, col0, n, out_dtype, extra):
    m, k = x.shape
    m2 = extra.shape[0]
    bm = _block(m, 1024)
    assert col0 % n == 0
    off = col0 // n
    return pl.pallas_call(
        functools.partial(_norm_mm_kernel, rows=_block(bm, 128)),
        grid=(m // bm,),
        in_specs=[pl.BlockSpec((bm, k), lambda i: (i, 0)),
                  pl.BlockSpec((1, k), lambda i: (0, 0)),
                  pl.BlockSpec((m2, k), lambda i: (0, 0)),
                  pl.BlockSpec((k, n), lambda i: (0, off))],
        out_specs=[pl.BlockSpec((bm, k), lambda i: (i, 0)),
                   pl.BlockSpec((bm, n), lambda i: (i, 0)),
                   pl.BlockSpec((m2, n), lambda i: (0, 0))],
        out_shape=[jax.ShapeDtypeStruct((m, k), BF16), jax.ShapeDtypeStruct((m, n), out_dtype),
                   jax.ShapeDtypeStruct((m2, n), out_dtype)],
        scratch_shapes=[pltpu.VMEM((k, n), BF16)],
        name="norm_matmul_cols",
        compiler_params=_params("arbitrary"),
    )(x, g.reshape(1, k).astype(F32), extra, w)


def _diff_lambda(lq1_ref, lk1_ref, lq2_ref, lk2_ref, lam_init):
    return (jnp.exp(jnp.sum(lq1_ref[...] * lk1_ref[...], keepdims=True))
            - jnp.exp(jnp.sum(lq2_ref[...] * lk2_ref[...], keepdims=True)) + lam_init)


def _head_norm(o, sub_ref, lam_init):
    ms = jnp.mean(o * o, axis=-1, keepdims=True)
    return o * lax.rsqrt(ms + EPS) * sub_ref[...] * (1.0 - lam_init)


def _own_alibi(n, slope):
    r = lax.broadcasted_iota(jnp.int32, (n, n), 0)
    c = lax.broadcasted_iota(jnp.int32, (n, n), 1)
    bias = slope * (r - jnp.abs(r - c)).astype(F32)
    return jnp.where(c // CHUNK <= r // CHUNK, bias, NEG_INF)


def _max_sq_norms(x, lane_masks):
    sq = jnp.square(x.astype(F32))
    return [jnp.max(jnp.sum(jnp.where(mask, sq, 0.0), axis=-1, keepdims=True)) for mask in lane_masks]


def _attn_a_kernel(q_ref, k_ref, v_ref, slope_ref, kf_ref, qf_ref, lq1_ref, lk1_ref, lq2_ref, lk2_ref,
                   sub_ref, o_ref, ke_scr, ve_scr, *, tq, lam_init):
    first_half = lax.broadcasted_iota(jnp.int32, (1, DK_A), 1) < DH_A
    heads = []
    for hh in range(A_HEADS):
        cols = slice(hh * DK_A, (hh + 1) * DK_A)
        args = (q_ref.at[:, :, cols], k_ref.at[:, :, cols], v_ref.at[:, :, cols], slope_ref.at[hh:hh + 1],
                lq1_ref, lk1_ref, lq2_ref, lk2_ref, sub_ref, o_ref.at[:, :, cols])
        _attn_a_bounded(*args, kf_ref, qf_ref, ke_scr.at[hh], ve_scr.at[hh], tq=tq, lam_init=lam_init)
        heads.append(args)
    bounds = []
    for hh, args in enumerate(heads):
        q_sq = _max_sq_norms(args[0][0] * (DH_A ** -0.5), (first_half, ~first_half))
        k_sq = _max_sq_norms(ke_scr[hh, :, :DK_A], (first_half, ~first_half))
        bounds.append(jnp.maximum(q_sq[0] * k_sq[0], q_sq[1] * k_sq[1]))
    for args, bound_sq in zip(heads, bounds):
        @pl.when(bound_sq > SAFE_LOGIT ** 2 / 1.01)
        def _():
            _attn_a_general(*args, tq=tq, lam_init=lam_init)


def _position_features(t):
    pos = jnp.arange(t, dtype=jnp.int32)[:, None]
    lo = pos & 7
    hi = (pos - lo).astype(F32)
    lo = lo.astype(F32)
    one = jnp.ones((t, 1), F32)
    pad = jnp.zeros((t, LANES - 4), F32)
    k_side = jnp.concatenate([hi, lo, one, one, pad], axis=1).astype(BF16)
    q_side = jnp.concatenate([one, one, -hi, -lo, pad], axis=1)
    return k_side, q_side


def _attn_a_bounded(q_ref, k_ref, v_ref, slope_ref, lq1_ref, lk1_ref, lq2_ref, lk2_ref, sub_ref, o_ref,
                    kf_ref, qf_ref, ke_scr, ve_scr, *, tq, lam_init):
    t = q_ref.shape[1]
    slope = slope_ref[0][:, :1]
    lam = _diff_lambda(lq1_ref, lk1_ref, lq2_ref, lk2_ref, lam_init)
    first_half = lax.broadcasted_iota(jnp.int32, (1, DK_A), 1) < DH_A
    ke_scr[:, :DK_A] = k_ref[0].astype(BF16)
    ke_scr[:, DK_A:] = kf_ref[...]
    ve_scr[:, :DV_A] = v_ref[0].astype(BF16)
    ve_scr[:, DV_A:] = jnp.ones((t, DV_A), BF16)

    r = lax.broadcasted_iota(jnp.int32, (tq, tq), 0)
    c = lax.broadcasted_iota(jnp.int32, (tq, tq), 1)
    own_fix = jnp.where(c // CHUNK <= r // CHUNK, -2.0 * slope * jnp.maximum(c - r, 0).astype(F32), NEG_INF)
    own_fix = jnp.concatenate([own_fix, own_fix], axis=0)

    for i in reversed(range(t // tq)):
        nb = i * tq
        q = q_ref[0, nb:nb + tq, :] * (DH_A ** -0.5)
        q_feat = (qf_ref[nb:nb + tq, :] * slope).astype(BF16)
        zero = jnp.zeros_like(q)
        qe = jnp.concatenate([jnp.concatenate([jnp.where(first_half, q, zero), q_feat], axis=1),
                              jnp.concatenate([jnp.where(first_half, zero, q), q_feat], axis=1)], axis=0)
        s_own = lax.dot_general(qe, ke_scr[nb:nb + tq, :], CONTRACT_LAST, preferred_element_type=F32)
        acc = jnp.dot(jnp.exp(s_own + own_fix).astype(BF16), ve_scr[nb:nb + tq, :],
                      preferred_element_type=F32)
        if nb:
            s_bef = lax.dot_general(qe, ke_scr[:nb, :], CONTRACT_LAST, preferred_element_type=F32)
            acc = acc + jnp.dot(jnp.exp(s_bef).astype(BF16), ve_scr[:nb, :], preferred_element_type=F32)
        o = acc[:, :DV_A] / acc[:, DV_A:]
        o = o[:tq] - lam * o[tq:]
        o_ref[0, nb:nb + tq, :] = _head_norm(o, sub_ref, lam_init).astype(o_ref.dtype)


def _attn_a_general(q_ref, k_ref, v_ref, slope_ref, lq1_ref, lk1_ref, lq2_ref, lk2_ref, sub_ref, o_ref,
                    *, tq, lam_init):
    t = q_ref.shape[1]
    n_blocks = t // tq
    slope = slope_ref[0][:, :1]
    lam = _diff_lambda(lq1_ref, lk1_ref, lq2_ref, lk2_ref, lam_init)
    k = k_ref[0].astype(BF16)
    v = v_ref[0].astype(BF16)
    first_half = lax.broadcasted_iota(jnp.int32, (1, DK_A), 1) < DH_A
    own_bias = _own_alibi(tq, slope)
    n_before_max = (n_blocks - 1) * tq
    if n_before_max:
        j = lax.broadcasted_iota(jnp.int32, (1, n_before_max), 1)
        before_bias = slope * (j - n_before_max).astype(F32)

    for i in range(n_blocks):
        nb = i * tq
        q = q_ref[0, nb:nb + tq, :] * (DH_A ** -0.5)
        k_own, v_own = k[nb:nb + tq], v[nb:nb + tq]

        def softmax_pv(qh):
            s_own = lax.dot_general(qh, k_own, CONTRACT_LAST, preferred_element_type=F32) + own_bias
            m = jnp.max(s_own, axis=-1, keepdims=True)
            if nb:
                s_bef = lax.dot_general(qh, k[:nb], CONTRACT_LAST, preferred_element_type=F32)
                s_bef = s_bef + before_bias[:, n_before_max - nb:]
                m = jnp.maximum(m, jnp.max(s_bef, axis=-1, keepdims=True))
            p_own = jnp.exp(s_own - m)
            l = jnp.sum(p_own, axis=-1, keepdims=True)
            pv = jnp.dot(p_own.astype(BF16), v_own, preferred_element_type=F32)
            if nb:
                p_bef = jnp.exp(s_bef - m)
                l = l + jnp.sum(p_bef, axis=-1, keepdims=True)
                pv = pv + jnp.dot(p_bef.astype(BF16), v[:nb], preferred_element_type=F32)
            return pv / l

        o = (softmax_pv(jnp.where(first_half, q, jnp.zeros_like(q)))
             - lam * softmax_pv(jnp.where(first_half, jnp.zeros_like(q), q)))
        o_ref[0, nb:nb + tq, :] = _head_norm(o, sub_ref, lam_init).astype(o_ref.dtype)


def _row(a):
    return a.reshape(1, -1).astype(F32)


def _const_spec(shape):
    return pl.BlockSpec(shape, lambda *_: (0,) * len(shape))


def attention_a(q, k, v, lq1, lk1, lq2, lk2, subln, lam_init):
    b, t, _ = q.shape
    tq = _block(t, A_BLOCK)
    assert tq % CHUNK == 0
    slopes = jnp.asarray([[[2.0 ** (-8.0 * (hh + 1) / H_A)] * LANES] for hh in range(H_A)], dtype=F32)
    head = lambda width: pl.BlockSpec((1, t, A_HEADS * width), lambda bi, hi: (bi, 0, hi))
    k_feat, q_feat = _position_features(t)
    assert DK_A == DV_A == LANES and H_A % A_HEADS == 0
    return pl.pallas_call(
        functools.partial(_attn_a_kernel, tq=tq, lam_init=lam_init),
        grid=(b, H_A // A_HEADS),
        in_specs=[head(DK_A), head(DK_A), head(DV_A),
                  pl.BlockSpec((A_HEADS, 1, LANES), lambda bi, hi: (hi, 0, 0)),
                  _const_spec((t, LANES)), _const_spec((t, LANES)),
                  _const_spec((1, DH_A)), _const_spec((1, DH_A)), _const_spec((1, DH_A)),
                  _const_spec((1, DH_A)), _const_spec((1, DV_A))],
        out_specs=head(DV_A),
        out_shape=jax.ShapeDtypeStruct((b, t, H_A * DV_A), BF16),
        scratch_shapes=[pltpu.VMEM((A_HEADS, t, DK_A + LANES), BF16),
                        pltpu.VMEM((A_HEADS, t, 2 * DV_A), BF16)],
        name="attention_a",
        compiler_params=_params("parallel", "arbitrary"),
    )(q, k, v, slopes, k_feat, q_feat, _row(lq1), _row(lk1), _row(lq2), _row(lk2), _row(subln))


def _attn_a_decode_kernel(q_ref, kn_ref, vn_ref, ck_ref, cv_ref, lq1_ref, lk1_ref, lq2_ref, lk2_ref,
                          sub_ref, o_ref, *, lam_init):
    t = q_ref.shape[1]
    p_len = ck_ref.shape[1] // H_A
    lam = _diff_lambda(lq1_ref, lk1_ref, lq2_ref, lk2_ref, lam_init)
    first_half = lax.broadcasted_iota(jnp.int32, (1, DK_A), 1) < DH_A
    j = lax.broadcasted_iota(jnp.int32, (1, p_len), 1)
    before_dist = (j - p_len).astype(F32)
    for h in range(H_A):
        slope = 2.0 ** (-8.0 * (h + 1) / H_A)
        cols = slice(h * DK_A, (h + 1) * DK_A)
        q = q_ref[0, :, cols] * (DH_A ** -0.5)
        kc = ck_ref[0, pl.ds(h, p_len, stride=H_A), :].astype(BF16)
        vc = cv_ref[0, pl.ds(h, p_len, stride=H_A), :].astype(BF16)
        kn = kn_ref[0, :, cols].astype(BF16)
        vn = vn_ref[0, :, cols].astype(BF16)
        own_bias = _own_alibi(t, slope)
        before_bias = slope * before_dist

        def softmax_pv(qh):
            s_new = lax.dot_general(qh, kn, CONTRACT_LAST, preferred_element_type=F32) + own_bias
            s_old = lax.dot_general(qh, kc, CONTRACT_LAST, preferred_element_type=F32) + before_bias
            m = jnp.maximum(jnp.max(s_new, axis=-1, keepdims=True), jnp.max(s_old, axis=-1, keepdims=True))
            p_new = jnp.exp(s_new - m)
            p_old = jnp.exp(s_old - m)
            l = jnp.sum(p_new, axis=-1, keepdims=True) + jnp.sum(p_old, axis=-1, keepdims=True)
            pv = (jnp.dot(p_new.astype(BF16), vn, preferred_element_type=F32)
                  + jnp.dot(p_old.astype(BF16), vc, preferred_element_type=F32))
            return pv / l

        o = (softmax_pv(jnp.where(first_half, q, jnp.zeros_like(q)))
             - lam * softmax_pv(jnp.where(first_half, jnp.zeros_like(q), q)))
        o_ref[0, :, cols] = _head_norm(o, sub_ref, lam_init).astype(o_ref.dtype)


def attention_a_decode(q, k_new, v_new, cache_k, cache_v, lq1, lk1, lq2, lk2, subln, lam_init):
    b, t, w = q.shape
    p_len = cache_k.shape[1]
    assert p_len % CHUNK == 0 and t <= CHUNK
    rows = lambda a: a.reshape(b, p_len * H_A, a.shape[-1])
    new = pl.BlockSpec((1, t, w), lambda bi: (bi, 0, 0))
    old = pl.BlockSpec((1, p_len * H_A, DK_A), lambda bi: (bi, 0, 0))
    return pl.pallas_call(
        functools.partial(_attn_a_decode_kernel, lam_init=lam_init),
        grid=(b,),
        in_specs=[new, new, new, old, old,
                  _const_spec((1, DH_A)), _const_spec((1, DH_A)), _const_spec((1, DH_A)),
                  _const_spec((1, DH_A)), _const_spec((1, DV_A))],
        out_specs=new,
        out_shape=jax.ShapeDtypeStruct((b, t, w), BF16),
        name="attention_a_decode",
        compiler_params=_params("parallel"),
    )(q, k_new, v_new, rows(cache_k), rows(cache_v), _row(lq1), _row(lk1), _row(lq2), _row(lk2), _row(subln))


def _band_tile(tq):
    pad = -(-tq // LANES) * LANES
    return BAND_PAST + pad, pad


def _band_bias_mask(g_row, tq):
    tile_w, pad = _band_tile(tq)
    width = g_row.shape[1]
    assert width == tile_w + pad
    rolled = pltpu.roll(jnp.broadcast_to(g_row, (tq, width)), width - pad, 1, stride=1, stride_axis=0)
    r = lax.broadcasted_iota(jnp.int32, (tq, tile_w), 0)
    c = lax.broadcasted_iota(jnp.int32, (tq, tile_w), 1)
    qch = r // CHUNK
    kch = c // CHUNK - N_PREV_CHUNKS
    valid = (kch <= qch) & (kch >= qch - N_PREV_CHUNKS)
    return jnp.where(valid, rolled[:, :tile_w], NEG_INF)


def _toeplitz_rows(rel_bias, tq):
    tile_w, pad = _band_tile(tq)
    width = tile_w + pad
    n_lo = pad + BAND_PAST - MAX_REL
    n_hi = max(width - n_lo - (2 * MAX_REL + 1), 0)
    lo = jnp.broadcast_to(rel_bias[:, :1], (H_B, n_lo))
    hi = jnp.broadcast_to(rel_bias[:, -1:], (H_B, n_hi))
    g = jnp.concatenate([lo, rel_bias, hi], axis=1)[:, :width]
    return g.reshape(H_B, 1, width).astype(F32)


def _attn_b_kernel(q_ref, k_ref, v_ref, g_ref, o_ref, ve_scr, *, tq):
    t = q_ref.shape[1]
    tile_w, _ = _band_tile(tq)
    scale = DH_B ** -0.5
    bias_mask = _band_bias_mask(g_ref[0], tq)

    def groups():
        for gi in range(t // tq):
            qa = gi * tq
            lo = max(qa - BAND_PAST, 0)
            hi = qa + tq
            off = lo - qa + BAND_PAST
            assert off % LANES == 0 and off + hi - lo == tile_w
            yield slice(qa, hi), slice(lo, hi), off

    log2e = math.log2(math.e)
    k = k_ref[0].astype(BF16)
    ve_scr[:, :DH_B] = v_ref[0].astype(BF16)
    ve_scr[:, DH_B:] = jnp.ones((t, DH_B), BF16)
    bias2 = bias_mask * log2e
    for rows, keys, off in groups():
        s = lax.dot_general(q_ref[0, rows, :], k[keys], CONTRACT_LAST, preferred_element_type=F32)
        p = jnp.exp2(s * (scale * log2e) + bias2[:, off:])
        acc = jnp.dot(p.astype(BF16), ve_scr[keys, :], preferred_element_type=F32)
        o_ref[0, rows, :] = (acc[:, :DH_B] / acc[:, DH_B:]).astype(o_ref.dtype)

    (q_sq,), (k_sq,) = _max_sq_norms(q_ref[0], (True,)), _max_sq_norms(k, (True,))
    room = SAFE_LOGIT - jnp.max(jnp.abs(g_ref[0]))
    safe = jnp.logical_and(room > 0.0, q_sq * k_sq * (scale * scale * 1.01) <= room * room)

    @pl.when(jnp.logical_not(safe))
    def _():
        k = k_ref[0].astype(BF16)
        v = v_ref[0].astype(BF16)
        for rows, keys, off in groups():
            s = lax.dot_general(q_ref[0, rows, :], k[keys], CONTRACT_LAST, preferred_element_type=F32) * scale
            s = s + bias_mask[:, off:]
            m = jnp.max(s, axis=-1, keepdims=True)
            p = jnp.exp(s - m)
            l = jnp.sum(p, axis=-1, keepdims=True)
            o = jnp.dot(p.astype(BF16), v[keys], preferred_element_type=F32) / l
            o_ref[0, rows, :] = o.astype(o_ref.dtype)


def attention_b(q, k, v, rel_bias):
    b, t, _ = q.shape
    tq = _block(t, BAND_GROUP)
    assert tq % LANES == 0
    g = _toeplitz_rows(rel_bias, tq)
    head = pl.BlockSpec((1, t, DH_B), lambda bi, hi: (bi, 0, hi))
    return pl.pallas_call(
        functools.partial(_attn_b_kernel, tq=tq),
        grid=(b, H_B),
        in_specs=[head, head, head, pl.BlockSpec((1, 1, g.shape[2]), lambda bi, hi: (hi, 0, 0))],
        out_specs=head,
        out_shape=jax.ShapeDtypeStruct((b, t, H_B * DH_B), BF16),
        scratch_shapes=[pltpu.VMEM((t, 2 * DH_B), BF16)],
        name="attention_b",
        compiler_params=_params("parallel", "arbitrary"),
    )(q, k, v, g)


def _attn_b_decode_kernel(q_ref, kn_ref, vn_ref, ck_ref, cv_ref, g_ref, o_ref):
    t = q_ref.shape[1]
    scale = DH_B ** -0.5
    for h in range(H_B):
        cols = slice(h * DH_B, (h + 1) * DH_B)
        q = q_ref[0, :, cols]
        kc = ck_ref[0, pl.ds(h, BAND_PAST, stride=H_B), :].astype(BF16)
        vc = cv_ref[0, pl.ds(h, BAND_PAST, stride=H_B), :].astype(BF16)
        kn = kn_ref[0, :, cols].astype(BF16)
        vn = vn_ref[0, :, cols].astype(BF16)
        bias_mask = _band_bias_mask(g_ref[h], t)
        s_old = lax.dot_general(q, kc, CONTRACT_LAST, preferred_element_type=F32) * scale
        s_old = s_old + bias_mask[:, :BAND_PAST]
        s_new = lax.dot_general(q, kn, CONTRACT_LAST, preferred_element_type=F32) * scale
        s_new = s_new + bias_mask[:, BAND_PAST:BAND_PAST + t]
        m = jnp.maximum(jnp.max(s_new, axis=-1, keepdims=True), jnp.max(s_old, axis=-1, keepdims=True))
        p_new = jnp.exp(s_new - m)
        p_old = jnp.exp(s_old - m)
        l = jnp.sum(p_new, axis=-1, keepdims=True) + jnp.sum(p_old, axis=-1, keepdims=True)
        pv = (jnp.dot(p_new.astype(BF16), vn, preferred_element_type=F32)
              + jnp.dot(p_old.astype(BF16), vc, preferred_element_type=F32))
        o_ref[0, :, cols] = (pv / l).astype(o_ref.dtype)


def attention_b_decode(q, k_new, v_new, cache_k, cache_v, rel_bias, p_len):
    b, t, w = q.shape
    assert cache_k.shape[1] == BAND_PAST and p_len % CHUNK == 0 and p_len >= BAND_PAST and t <= CHUNK
    g = _toeplitz_rows(rel_bias, t)
    rows = lambda a: a.reshape(b, BAND_PAST * H_B, a.shape[-1])
    new = pl.BlockSpec((1, t, w), lambda bi: (bi, 0, 0))
    old = pl.BlockSpec((1, BAND_PAST * H_B, DH_B), lambda bi: (bi, 0, 0))
    return pl.pallas_call(
        _attn_b_decode_kernel,
        grid=(b,),
        in_specs=[new, new, new, old, old, _const_spec(g.shape)],
        out_specs=new,
        out_shape=jax.ShapeDtypeStruct((b, t, w), BF16),
        name="attention_b_decode",
        compiler_params=_params("parallel"),
    )(q, k_new, v_new, rows(cache_k), rows(cache_v), g)


def _attn_c_kernel(q_ref, k_ref, v_ref, o_ref, *, tq, cache_order):
    t = q_ref.shape[1]
    scale = DH_C ** -0.5
    blocks = DH_C // LANES

    def head_rows(ref, h):
        if not cache_order:
            return ref[0, :, h * DH_C:(h + 1) * DH_C]
        n_mem = ref.shape[1] // (blocks * H_C)
        return jnp.concatenate([ref[0, pl.ds(j * H_C + h, n_mem, stride=blocks * H_C), :]
                                for j in range(blocks)], axis=1)

    for h in range(H_C):
        cols = slice(h * DH_C, (h + 1) * DH_C)
        k = head_rows(k_ref, h).astype(BF16)
        v = head_rows(v_ref, h).astype(BF16)

        def attend(probabilities):
            for i in range(t // tq):
                rows = slice(i * tq, (i + 1) * tq)
                s = lax.dot_general(q_ref[0, rows, cols], k, CONTRACT_LAST, preferred_element_type=F32)
                p = probabilities(s)
                l = jnp.sum(p, axis=-1, keepdims=True)
                o = jnp.dot(p.astype(BF16), v, preferred_element_type=F32) / l
                o_ref[0, rows, cols] = o.astype(o_ref.dtype)

        attend(lambda s: jnp.exp2(s * (scale * math.log2(math.e))))
        (q_sq,), (k_sq,) = _max_sq_norms(q_ref[0, :, cols], (True,)), _max_sq_norms(k, (True,))

        @pl.when(q_sq * k_sq * (scale * scale * 1.01) > SAFE_LOGIT ** 2)
        def _():
            def shifted(s):
                s = s * scale
                return jnp.exp(s - jnp.max(s, axis=-1, keepdims=True))
            attend(shifted)


def attention_c(q, mk, mv):
    b, t, w = q.shape
    cache_order = mk.ndim == 4
    if cache_order:
        n_mem, blocks = mk.shape[1], DH_C // LANES
        tiles = lambda c: c.reshape(b, n_mem, H_C, blocks, LANES).transpose(0, 1, 3, 2, 4).reshape(
            b, n_mem * blocks * H_C, LANES)
        mk, mv = tiles(mk), tiles(mv)
    kv_spec = pl.BlockSpec((1,) + mk.shape[1:], lambda bi: (bi, 0, 0))
    tq = _block(t, 512)
    return pl.pallas_call(
        functools.partial(_attn_c_kernel, tq=tq, cache_order=cache_order),
        grid=(b,),
        in_specs=[pl.BlockSpec((1, t, w), lambda bi: (bi, 0, 0)), kv_spec, kv_spec],
        out_specs=pl.BlockSpec((1, t, w), lambda bi: (bi, 0, 0)),
        out_shape=jax.ShapeDtypeStruct((b, t, w), BF16),
        name="attention_c",
        compiler_params=_params("parallel"),
    )(q, mk, mv)


def _merge_kernel(oa_ref, ob_ref, oc_ref, ga_ref, gb_ref, gc_ref, ba_ref, bb_ref, bc_ref,
                  wa_f32, wb_f32, wc_f32, o_ref, wa_ref, wb_ref, wc_ref):
    @pl.when(pl.program_id(0) == 0)
    def _():
        wa_ref[...] = wa_f32[...].astype(BF16)
        wb_ref[...] = wb_f32[...].astype(BF16)
        wc_ref[...] = wc_f32[...].astype(BF16)

    rows = _block(o_ref.shape[0], MERGE_CHUNK)
    for r0 in range(0, o_ref.shape[0], rows):
        sl = slice(r0, r0 + rows)

        def branch(o_r, g_r, b_r, w_r):
            gate = jax.nn.sigmoid(g_r[sl, :].astype(F32) + b_r[...])
            return gate * jnp.dot(o_r[sl, :], w_r[...], preferred_element_type=F32)

        merged = (branch(oa_ref, ga_ref, ba_ref, wa_ref) + branch(ob_ref, gb_ref, bb_ref, wb_ref)
                  + branch(oc_ref, gc_ref, bc_ref, wc_ref))
        o_ref[sl, :] = merged.astype(o_ref.dtype)


def merge_branches(oa, ob, oc, gate_logits, b_gate, wa, wb, wc):
    m, w_in = oa.shape
    d = wa.shape[1]
    bm = _block(m, MERGE_ROWS)
    row = lambda width: pl.BlockSpec((bm, width), lambda i: (i, 0))
    gate = lambda j: pl.BlockSpec((bm, d), lambda i: (i, j))
    bias = lambda j: pl.BlockSpec((1, d), lambda i: (0, j))
    weight = pl.BlockSpec((w_in, d), lambda i: (0, 0), pipeline_mode=pl.Buffered(1))
    bg = b_gate.reshape(1, N_BRANCH * d).astype(F32)
    return pl.pallas_call(
        _merge_kernel,
        grid=(m // bm,),
        in_specs=[row(w_in), row(w_in), row(w_in), gate(0), gate(1), gate(2),
                  bias(0), bias(1), bias(2), weight, weight, weight],
        out_specs=pl.BlockSpec((bm, d), lambda i: (i, 0)),
        out_shape=jax.ShapeDtypeStruct((m, d), BF16),
        scratch_shapes=[pltpu.VMEM((w_in, d), BF16)] * N_BRANCH,
        name="merge_branches",
        compiler_params=_params("arbitrary"),
    )(oa, ob, oc, gate_logits, gate_logits, gate_logits, bg, bg, bg, wa, wb, wc)


def _proj_norm_res_kernel(*refs, next_norm, rows, cast_w):
    refs = list(refs)
    if cast_w:
        w_bf = refs.pop()
    if next_norm:
        a_ref, w_ref, x_ref, g_ref, g2_ref, o_ref, h_ref = refs
    else:
        a_ref, w_ref, x_ref, g_ref, o_ref = refs
    if cast_w:
        @pl.when(pl.program_id(0) == 0)
        def _():
            w_bf[...] = w_ref[...].astype(BF16)
        w_ref = w_bf
    bm = a_ref.shape[0]
    for r0 in range(0, bm, rows):
        sl = slice(r0, r0 + rows)
        y = jnp.dot(a_ref[sl, :], w_ref[...], preferred_element_type=F32)
        ms = jnp.mean(y * y, axis=-1, keepdims=True)
        o = x_ref[sl, :] + y * lax.rsqrt(ms + EPS) * g_ref[...]
        o_ref[sl, :] = o
        if next_norm:
            ms2 = jnp.mean(o * o, axis=-1, keepdims=True)
            h_ref[sl, :] = (o * lax.rsqrt(ms2 + EPS) * g2_ref[...]).astype(h_ref.dtype)


def proj_norm_residual(a, w, x, g, bm, rows, next_g=None):
    m, k = a.shape
    d = w.shape[1]
    bm = _block(m, bm)
    rows = _block(bm, rows)
    vec = pl.BlockSpec((1, d), lambda i: (0, 0))
    row = pl.BlockSpec((bm, d), lambda i: (i, 0))
    in_specs = [pl.BlockSpec((bm, k), lambda i: (i, 0)),
                pl.BlockSpec((k, d), lambda i: (0, 0), pipeline_mode=pl.Buffered(1)), row, vec]
    args = [a, w, x, g.reshape(1, d).astype(F32)]
    out_specs, out_shape = row, jax.ShapeDtypeStruct((m, d), F32)
    if next_g is not None:
        in_specs.append(vec)
        args.append(next_g.reshape(1, d).astype(F32))
        out_specs = (row, row)
        out_shape = (out_shape, jax.ShapeDtypeStruct((m, d), BF16))
    cast_w = w.dtype != BF16
    return pl.pallas_call(
        functools.partial(_proj_norm_res_kernel, next_norm=next_g is not None, rows=rows, cast_w=cast_w),
        grid=(m // bm,),
        in_specs=in_specs,
        out_specs=out_specs,
        out_shape=out_shape,
        scratch_shapes=[pltpu.VMEM((k, d), BF16)] if cast_w else [],
        name="proj_norm_residual",
        compiler_params=_params("arbitrary"),
    )(*args)


def _ffn_in_kernel(h_ref, h2_ref, wa_ref, wb_ref, o_ref, o2_ref, wa_bf, wb_bf):
    def swiglu(h):
        a = jnp.dot(h, wa_bf[...], preferred_element_type=F32)
        b = jnp.dot(h, wb_bf[...], preferred_element_type=F32)
        return (jax.nn.silu(a) * b).astype(o_ref.dtype)

    @pl.when(pl.program_id(1) == 0)
    def _():
        wa_bf[...] = wa_ref[...].astype(BF16)
        wb_bf[...] = wb_ref[...].astype(BF16)
        o2_ref[...] = swiglu(h2_ref[...])

    o_ref[...] = swiglu(h_ref[...])


def ffn_in(h, extra, w):
    m, k = h.shape
    m2 = extra.shape[0]
    f = w.shape[1] // 2
    bm = _block(m, 1024)
    bf = 512
    assert f % bf == 0
    nf = f // bf
    return pl.pallas_call(
        _ffn_in_kernel,
        grid=(nf, m // bm),
        in_specs=[pl.BlockSpec((bm, k), lambda j, i: (i, 0)),
                  pl.BlockSpec((m2, k), lambda j, i: (0, 0)),
                  pl.BlockSpec((k, bf), lambda j, i: (0, j)),
                  pl.BlockSpec((k, bf), lambda j, i: (0, j + nf))],
        out_specs=[pl.BlockSpec((bm, bf), lambda j, i: (i, j)),
                   pl.BlockSpec((m2, bf), lambda j, i: (0, j))],
        out_shape=[jax.ShapeDtypeStruct((m, f), BF16), jax.ShapeDtypeStruct((m2, f), BF16)],
        scratch_shapes=[pltpu.VMEM((k, bf), BF16), pltpu.VMEM((k, bf), BF16)],
        name="ffn_in",
        compiler_params=_params("arbitrary", "arbitrary"),
    )(h, extra, w, w)


def _layer(xp, xd, mk_p, mv_p, mk_d, mv_d, caches, lam_init, norm_mix_pre, norm_mix_post, w_in, b_gate,
           lq1, lk1, lq2, lk2, subln_a, rel_bias, w_br_a, w_br_b, w_br_c, w_out,
           norm_ffn_pre, norm_ffn_post, w_ffn_in, w_ffn_out):
    d = xp.shape[-1]
    w_head = H_A * DK_A
    shapes = [xp.shape[:2], xd.shape[:2]]
    xs = [xp.reshape(-1, d), xd.reshape(-1, d)]
    h_d = rms_norm_bf16(xs[1], norm_mix_pre)
    h_p, qa_p, qa_d = norm_matmul_cols(xs[0], norm_mix_pre, w_in, 0, w_head, BF16, h_d)
    hs = [h_p, h_d]
    split = lambda outs: [o.reshape(*shp, w_head) for o, shp in zip(outs, shapes)]
    proj = lambda idx, dt: split(matmul_cols(hs[0], w_in, idx * w_head, w_head, dt, extra=hs[1]))
    q_a, q_b, q_c = split([qa_p, qa_d]), proj(3, BF16), proj(6, BF16)
    kv = dict(extra=hs[1], cache_dtype=F32)
    *k_a, ka_cache = matmul_cols(hs[0], w_in, 1 * w_head, w_head, BF16, head_major=DK_A // LANES, **kv)
    *v_a, va_cache = matmul_cols(hs[0], w_in, 2 * w_head, w_head, BF16, head_major=DV_A // LANES, **kv)
    k_a, v_a = split(k_a), split(v_a)
    a_caches = [c.reshape(*shapes[0], H_A, DK_A) for c in (ka_cache, va_cache)]
    seq = shapes[0][1]
    keep = min(BAND_PAST, seq)
    *k_b, kb_tail = matmul_cols(hs[0], w_in, 4 * w_head, w_head, BF16, tail=(seq, keep), **kv)
    *v_b, vb_tail = matmul_cols(hs[0], w_in, 5 * w_head, w_head, BF16, tail=(seq, keep), **kv)
    k_b, v_b = split(k_b), split(v_b)
    tails = [t.reshape(shapes[0][0], keep, H_B, DH_B) for t in (kb_tail, vb_tail)]
    gates = matmul_cols(hs[0], w_in, 7 * w_head, N_BRANCH * d, BF16, extra=hs[1])

    ca_k, ca_v, cb_k, cb_v = caches
    o_a = [attention_a(q_a[0], k_a[0], v_a[0], lq1, lk1, lq2, lk2, subln_a, lam_init),
           attention_a_decode(q_a[1], k_a[1], v_a[1], ca_k, ca_v, lq1, lk1, lq2, lk2, subln_a, lam_init)]
    o_b = [attention_b(q_b[0], k_b[0], v_b[0], rel_bias),
           attention_b_decode(q_b[1], k_b[1], v_b[1], cb_k, cb_v, rel_bias, ca_k.shape[1])]
    o_c = [attention_c(q_c[0], mk_p, mv_p), attention_c(q_c[1], mk_d, mv_d)]

    x1, h2 = [], []
    for i in range(2):
        m = xs[i].shape[0]
        merged = merge_branches(o_a[i].reshape(m, -1), o_b[i].reshape(m, -1), o_c[i].reshape(m, -1),
                                gates[i], b_gate, w_br_a, w_br_b, w_br_c)
        a, b = proj_norm_residual(merged, w_out, xs[i], norm_mix_post, bm=512, rows=PROJ_ROWS,
                                  next_g=norm_ffn_pre)
        x1.append(a)
        h2.append(b)
    acts = ffn_in(h2[0], h2[1], w_ffn_in)
    ys = [proj_norm_residual(acts[i], w_ffn_out, x1[i], norm_ffn_post, bm=512, rows=256).reshape(*shapes[i], d)
          for i in range(2)]
    return (ys[0], *a_caches, *tails), (ys[1], k_a[1], v_a[1], k_b[1], v_b[1])


def kernel(x_prompt, x_sample, cache_a_k, cache_a_v, cache_b_k, cache_b_v, cache_mem_k, cache_mem_v, mem_prompt, norm_mix_pre, norm_mix_post, norm_mem, w_in, b_gate, lambda_q1, lambda_k1, lambda_q2, lambda_k2, subln_a, rel_bias_b, w_mem_kv, w_br_a, w_br_b, w_br_c, w_out, norm_ffn_pre, norm_ffn_post, w_ffn_in, w_ffn_out):
    depth = w_in.shape[0]
    bsz, s, d = x_prompt.shape
    n_mem = mem_prompt.shape[1]
    lb_prompt = min(BAND_PAST, s)
    yp, ys = x_prompt, x_sample
    outs = [[] for _ in range(10)]
    heads = lambda a, nh: a.reshape(a.shape[0], a.shape[1], nh, a.shape[2] // nh)
    flat = lambda a: a.reshape(a.shape[0], a.shape[1], -1)
    for l in range(depth):
        lam_init = 0.8 - 0.6 * math.exp(-0.3 * l)
        shared = (lam_init, norm_mix_pre[l], norm_mix_post[l], w_in[l], b_gate[l],
                  lambda_q1[l], lambda_k1[l], lambda_q2[l], lambda_k2[l], subln_a[l], rel_bias_b[l],
                  w_br_a[l], w_br_b[l], w_br_c[l], w_out[l],
                  norm_ffn_pre[l], norm_ffn_post[l], w_ffn_in[l], w_ffn_out[l].astype(BF16))

        mem_n = rms_norm_bf16(mem_prompt.reshape(bsz * n_mem, d), norm_mem[l])
        w_c = H_C * DH_C
        mk_p, mk_cache = matmul_cols(mem_n, w_mem_kv[l], 0, w_c, BF16, head_major=DH_C // LANES, cache_dtype=F32)
        mv_p, mv_cache = matmul_cols(mem_n, w_mem_kv[l], w_c, w_c, BF16, head_major=DH_C // LANES, cache_dtype=F32)
        mk_p, mv_p = mk_p.reshape(bsz, n_mem, w_c), mv_p.reshape(bsz, n_mem, w_c)
        caches = (cache_a_k[l], cache_a_v[l], cache_b_k[l], cache_b_v[l])
        (yp, ka, va, kb, vb), (ys, ka_d, va_d, kb_d, vb_d) = _layer(
            yp, ys, mk_p, mv_p, cache_mem_k[l], cache_mem_v[l], caches, *shared)
        mem_cache = lambda c: c.reshape(bsz, n_mem, DH_C // LANES, H_C, LANES).transpose(0, 1, 3, 2, 4).reshape(
            bsz, n_mem, H_C, DH_C)
        new = [ka, va, kb, vb, mem_cache(mk_cache), mem_cache(mv_cache),
               heads(ka_d, H_A), heads(va_d, H_A), heads(kb_d, H_B), heads(vb_d, H_B)]
        for o, a in zip(outs, new):
            o.append(a)
    return (yp, ys) + tuple(jnp.stack(o) for o in outs)
```

```python
import functools
import math

import jax
import jax.numpy as jnp
from jax import lax
from jax.experimental import pallas as pl
from jax.experimental.pallas import tpu as pltpu

F32 = jnp.float32
BF16 = jnp.bfloat16

CHUNK = 64
H_A = 8
DH_A = 64
DK_A = 2 * DH_A
DV_A = 128
H_B = 8
DH_B = 128
N_PREV_CHUNKS = 8
BAND_PAST = N_PREV_CHUNKS * CHUNK
MAX_REL = 128
H_C = 4
DH_C = 256
N_BRANCH = 3
EPS = 1e-6
NEG_INF = -1e30

LANES = 128
VMEM_LIMIT = 56 * 1024 * 1024
BAND_GROUP = 256
A_BLOCK = 256
A_HEADS = 1
MM_ROWS = 2048
MERGE_ROWS = 256
MERGE_CHUNK = 256
PROJ_ROWS = 128
SAFE_LOGIT = 40.0

CONTRACT_LAST = (((1,), (1,)), ((), ()))


def _params(*sem):
    return pltpu.CompilerParams(dimension_semantics=sem, vmem_limit_bytes=VMEM_LIMIT)


def _block(n, target):
    if n <= target:
        return n
    b = target
    while n % b:
        b //= 2
    return b


def _rms_kernel(x_ref, g_ref, o_ref):
    x = x_ref[...]
    ms = jnp.mean(x * x, axis=-1, keepdims=True)
    o_ref[...] = (x * lax.rsqrt(ms + EPS) * g_ref[...]).astype(o_ref.dtype)


def rms_norm_bf16(x, g):
    m, d = x.shape
    bm = _block(m, 512)
    return pl.pallas_call(
        _rms_kernel,
        grid=(m // bm,),
        in_specs=[pl.BlockSpec((bm, d), lambda i: (i, 0)),
                  pl.BlockSpec((1, d), lambda i: (0, 0))],
        out_specs=pl.BlockSpec((bm, d), lambda i: (i, 0)),
        out_shape=jax.ShapeDtypeStruct((m, d), BF16),
        name="rms_norm",
        compiler_params=_params("parallel"),
    )(x, g.reshape(1, d))


def _mm_kernel(*refs, has_extra, tail, head_major):
    refs = list(refs)
    a_ref = refs.pop(0)
    a2_ref = refs.pop(0) if has_extra else None
    w_ref = refs.pop(0)
    o_ref = refs.pop(0)
    o2_ref = refs.pop(0) if has_extra else None
    t_ref = refs.pop(0) if tail else None
    hm_ref = refs.pop(0) if head_major else None
    wb_ref, = refs

    def product(x_ref, y_ref, z_ref=None, tail_ref=None):
        n_rows = x_ref.shape[0]
        rows = _block(n_rows, 1024)
        for r0 in range(0, n_rows, rows):
            acc = jnp.dot(x_ref[r0:r0 + rows, :], wb_ref[...], preferred_element_type=F32)
            y_ref[r0:r0 + rows, :] = acc.astype(y_ref.dtype)
            heads = acc.shape[1] // LANES
            if z_ref is not None:
                for h in range(heads):
                    slot = (h % head_major) * (heads // head_major) + h // head_major
                    z_ref[pl.ds(r0 * heads + slot, rows, stride=heads), :] = (
                        acc[:, h * LANES:(h + 1) * LANES].astype(z_ref.dtype))
            if tail_ref is not None:
                blocks, keep = tail
                first = max(r0, n_rows - keep)
                if first < r0 + rows:
                    @pl.when(pl.program_id(1) % blocks == blocks - 1)
                    def _():
                        for h in range(heads):
                            tail_ref[pl.ds((first - (n_rows - keep)) * heads + h, r0 + rows - first,
                                           stride=heads), :] = (
                                acc[first - r0:, h * LANES:(h + 1) * LANES].astype(tail_ref.dtype))

    @pl.when(pl.program_id(1) == 0)
    def _():
        wb_ref[...] = w_ref[...].astype(BF16)
        if has_extra:
            product(a2_ref, o2_ref)

    product(a_ref, o_ref, hm_ref, t_ref)


def matmul_cols(a, w, col0, n, out_dtype, extra=None, tail=None, head_major=0, cache_dtype=None):
    m, k = a.shape
    cache_dtype = cache_dtype or out_dtype
    side_bytes = jnp.dtype(cache_dtype).itemsize if (tail or head_major) else 0
    bm = _block(m, MM_ROWS * 2 // max(jnp.dtype(out_dtype).itemsize, side_bytes))
    bn = _block(n, 1024)
    assert col0 % bn == 0
    off = col0 // bn
    in_specs = [pl.BlockSpec((bm, k), lambda j, i: (i, 0))]
    args = [a]
    out_specs = [pl.BlockSpec((bm, bn), lambda j, i: (i, j))]
    out_shape = [jax.ShapeDtypeStruct((m, n), out_dtype)]
    if extra is not None:
        m2 = extra.shape[0]
        in_specs.append(pl.BlockSpec((m2, k), lambda j, i: (0, 0)))
        args.append(extra)
        out_specs.append(pl.BlockSpec((m2, bn), lambda j, i: (0, j)))
        out_shape.append(jax.ShapeDtypeStruct((m2, n), cache_dtype))
    if tail is not None:
        seq, keep = tail
        assert seq % bm == 0 and keep <= bm and m % seq == 0
        blocks = seq // bm
        assert bn == n
        out_specs.append(pl.BlockSpec((keep * (n // LANES), LANES), lambda j, i: (i // blocks, 0)))
        out_shape.append(jax.ShapeDtypeStruct((m // seq * keep * (n // LANES), LANES), cache_dtype))
        tail = (blocks, keep)
    if head_major:
        assert bn == n
        heads = n // LANES
        out_specs.append(pl.BlockSpec((bm * heads, LANES), lambda j, i: (i, 0)))
        out_shape.append(jax.ShapeDtypeStruct((m * heads, LANES), cache_dtype))
    in_specs.append(pl.BlockSpec((k, bn), lambda j, i: (0, j + off)))
    args.append(w)
    return pl.pallas_call(
        functools.partial(_mm_kernel, has_extra=extra is not None, tail=tail, head_major=head_major),
        grid=(n // bn, m // bm),
        in_specs=in_specs,
        out_specs=out_specs,
        out_shape=out_shape,
        scratch_shapes=[pltpu.VMEM((k, bn), BF16)],
        name="matmul_cols",
        compiler_params=_params("arbitrary", "arbitrary"),
    )(*args)


def _norm_mm_kernel(x_ref, g_ref, a2_ref, w_ref, h_ref, o_ref, o2_ref, wb_ref, *, rows):
    @pl.when(pl.program_id(0) == 0)
    def _():
        wb_ref[...] = w_ref[...].astype(BF16)
        o2_ref[...] = jnp.dot(a2_ref[...], wb_ref[...], preferred_element_type=F32).astype(o2_ref.dtype)

    for r0 in range(0, x_ref.shape[0], rows):
        x = x_ref[r0:r0 + rows, :]
        ms = jnp.mean(x * x, axis=-1, keepdims=True)
        h = (x * lax.rsqrt(ms + EPS) * g_ref[...]).astype(BF16)
        h_ref[r0:r0 + rows, :] = h
        o_ref[r0:r0 + rows, :] = jnp.dot(h, wb_ref[...], preferred_element_type=F32).astype(o_ref.dtype)


def norm_matmul_cols(x, g, w, col0, n, out_dtype, extra):
    m, k = x.shape
    m2 = extra.shape[0]
    bm = _block(m, 1024)
    assert col0 % n == 0
    off = col0 // n
    return pl.pallas_call(
        functools.partial(_norm_mm_kernel, rows=_block(bm, 128)),
        grid=(m // bm,),
        in_specs=[pl.BlockSpec((bm, k), lambda i: (i, 0)),
                  pl.BlockSpec((1, k), lambda i: (0, 0)),
                  pl.BlockSpec((m2, k), lambda i: (0, 0)),
                  pl.BlockSpec((k, n), lambda i: (0, off))],
        out_specs=[pl.BlockSpec((bm, k), lambda i: (i, 0)),
                   pl.BlockSpec((bm, n), lambda i: (i, 0)),
                   pl.BlockSpec((m2, n), lambda i: (0, 0))],
        out_shape=[jax.ShapeDtypeStruct((m, k), BF16), jax.ShapeDtypeStruct((m, n), out_dtype),
                   jax.ShapeDtypeStruct((m2, n), out_dtype)],
        scratch_shapes=[pltpu.VMEM((k, n), BF16)],
        name="norm_matmul_cols",
        compiler_params=_params("arbitrary"),
    )(x, g.reshape(1, k).astype(F32), extra, w)


def _diff_lambda(lq1_ref, lk1_ref, lq2_ref, lk2_ref, lam_init):
    return (jnp.exp(jnp.sum(lq1_ref[...] * lk1_ref[...], keepdims=True))
            - jnp.exp(jnp.sum(lq2_ref[...] * lk2_ref[...], keepdims=True)) + lam_init)


def _head_norm(o, sub_ref, lam_init):
    ms = jnp.mean(o * o, axis=-1, keepdims=True)
    return o * lax.rsqrt(ms + EPS) * sub_ref[...] * (1.0 - lam_init)


def _own_alibi(n, slope):
    r = lax.broadcasted_iota(jnp.int32, (n, n), 0)
    c = lax.broadcasted_iota(jnp.int32, (n, n), 1)
    bias = slope * (r - jnp.abs(r - c)).astype(F32)
    return jnp.where(c // CHUNK <= r // CHUNK, bias, NEG_INF)


def _max_sq_norms(x, lane_masks):
    sq = jnp.square(x.astype(F32))
    return [jnp.max(jnp.sum(jnp.where(mask, sq, 0.0), axis=-1, keepdims=True)) for mask in lane_masks]


def _attn_a_kernel(q_ref, k_ref, v_ref, slope_ref, kf_ref, qf_ref, lq1_ref, lk1_ref, lq2_ref, lk2_ref,
                   sub_ref, o_ref, ke_scr, ve_scr, *, tq, lam_init):
    first_half = lax.broadcasted_iota(jnp.int32, (1, DK_A), 1) < DH_A
    heads = []
    for hh in range(A_HEADS):
        cols = slice(hh * DK_A, (hh + 1) * DK_A)
        args = (q_ref.at[:, :, cols], k_ref.at[:, :, cols], v_ref.at[:, :, cols], slope_ref.at[hh:hh + 1],
                lq1_ref, lk1_ref, lq2_ref, lk2_ref, sub_ref, o_ref.at[:, :, cols])
        _attn_a_bounded(*args, kf_ref, qf_ref, ke_scr.at[hh], ve_scr.at[hh], tq=tq, lam_init=lam_init)
        heads.append(args)
    bounds = []
    for hh, args in enumerate(heads):
        q_sq = _max_sq_norms(args[0][0] * (DH_A ** -0.5), (first_half, ~first_half))
        k_sq = _max_sq_norms(ke_scr[hh, :, :DK_A], (first_half, ~first_half))
        bounds.append(jnp.maximum(q_sq[0] * k_sq[0], q_sq[1] * k_sq[1]))
    for args, bound_sq in zip(heads, bounds):
        @pl.when(bound_sq > SAFE_LOGIT ** 2 / 1.01)
        def _():
            _attn_a_general(*args, tq=tq, lam_init=lam_init)


def _position_features(t):
    pos = jnp.arange(t, dtype=jnp.int32)[:, None]
    lo = pos & 7
    hi = (pos - lo).astype(F32)
    lo = lo.astype(F32)
    one = jnp.ones((t, 1), F32)
    pad = jnp.zeros((t, LANES - 4), F32)
    k_side = jnp.concatenate([hi, lo, one, one, pad], axis=1).astype(BF16)
    q_side = jnp.concatenate([one, one, -hi, -lo, pad], axis=1)
    return k_side, q_side


def _attn_a_bounded(q_ref, k_ref, v_ref, slope_ref, lq1_ref, lk1_ref, lq2_ref, lk2_ref, sub_ref, o_ref,
                    kf_ref, qf_ref, ke_scr, ve_scr, *, tq, lam_init):
    t = q_ref.shape[1]
    slope = slope_ref[0][:, :1]
    lam = _diff_lambda(lq1_ref, lk1_ref, lq2_ref, lk2_ref, lam_init)
    first_half = lax.broadcasted_iota(jnp.int32, (1, DK_A), 1) < DH_A
    ke_scr[:, :DK_A] = k_ref[0].astype(BF16)
    ke_scr[:, DK_A:] = kf_ref[...]
    ve_scr[:, :DV_A] = v_ref[0].astype(BF16)
    ve_scr[:, DV_A:] = jnp.ones((t, DV_A), BF16)

    r = lax.broadcasted_iota(jnp.int32, (tq, tq), 0)
    c = lax.broadcasted_iota(jnp.int32, (tq, tq), 1)
    own_fix = jnp.where(c // CHUNK <= r // CHUNK, -2.0 * slope * jnp.maximum(c - r, 0).astype(F32), NEG_INF)
    own_fix = jnp.concatenate([own_fix, own_fix], axis=0)

    for i in reversed(range(t // tq)):
        nb = i * tq
        q = q_ref[0, nb:nb + tq, :] * (DH_A ** -0.5)
        q_feat = (qf_ref[nb:nb + tq, :] * slope).astype(BF16)
        zero = jnp.zeros_like(q)
        qe = jnp.concatenate([jnp.concatenate([jnp.where(first_half, q, zero), q_feat], axis=1),
                              jnp.concatenate([jnp.where(first_half, zero, q), q_feat], axis=1)], axis=0)
        s_own = lax.dot_general(qe, ke_scr[nb:nb + tq, :], CONTRACT_LAST, preferred_element_type=F32)
        acc = jnp.dot(jnp.exp(s_own + own_fix).astype(BF16), ve_scr[nb:nb + tq, :],
                      preferred_element_type=F32)
        if nb:
            s_bef = lax.dot_general(qe, ke_scr[:nb, :], CONTRACT_LAST, preferred_element_type=F32)
            acc = acc + jnp.dot(jnp.exp(s_bef).astype(BF16), ve_scr[:nb, :], preferred_element_type=F32)
        o = acc[:, :DV_A] / acc[:, DV_A:]
        o = o[:tq] - lam * o[tq:]
        o_ref[0, nb:nb + tq, :] = _head_norm(o, sub_ref, lam_init).astype(o_ref.dtype)


def _attn_a_general(q_ref, k_ref, v_ref, slope_ref, lq1_ref, lk1_ref, lq2_ref, lk2_ref, sub_ref, o_ref,
                    *, tq, lam_init):
    t = q_ref.shape[1]
    n_blocks = t // tq
    slope = slope_ref[0][:, :1]
    lam = _diff_lambda(lq1_ref, lk1_ref, lq2_ref, lk2_ref, lam_init)
    k = k_ref[0].astype(BF16)
    v = v_ref[0].astype(BF16)
    first_half = lax.broadcasted_iota(jnp.int32, (1, DK_A), 1) < DH_A
    own_bias = _own_alibi(tq, slope)
    n_before_max = (n_blocks - 1) * tq
    if n_before_max:
        j = lax.broadcasted_iota(jnp.int32, (1, n_before_max), 1)
        before_bias = slope * (j - n_before_max).astype(F32)

    for i in range(n_blocks):
        nb = i * tq
        q = q_ref[0, nb:nb + tq, :] * (DH_A ** -0.5)
        k_own, v_own = k[nb:nb + tq], v[nb:nb + tq]

        def softmax_pv(qh):
            s_own = lax.dot_general(qh, k_own, CONTRACT_LAST, preferred_element_type=F32) + own_bias
            m = jnp.max(s_own, axis=-1, keepdims=True)
            if nb:
                s_bef = lax.dot_general(qh, k[:nb], CONTRACT_LAST, preferred_element_type=F32)
                s_bef = s_bef + before_bias[:, n_before_max - nb:]
                m = jnp.maximum(m, jnp.max(s_bef, axis=-1, keepdims=True))
            p_own = jnp.exp(s_own - m)
            l = jnp.sum(p_own, axis=-1, keepdims=True)
            pv = jnp.dot(p_own.astype(BF16), v_own, preferred_element_type=F32)
            if nb:
                p_bef = jnp.exp(s_bef - m)
                l = l + jnp.sum(p_bef, axis=-1, keepdims=True)
                pv = pv + jnp.dot(p_bef.astype(BF16), v[:nb], preferred_element_type=F32)
            return pv / l

        o = (softmax_pv(jnp.where(first_half, q, jnp.zeros_like(q)))
             - lam * softmax_pv(jnp.where(first_half, jnp.zeros_like(q), q)))
        o_ref[0, nb:nb + tq, :] = _head_norm(o, sub_ref, lam_init).astype(o_ref.dtype)


def _row(a):
    return a.reshape(1, -1).astype(F32)


def _const_spec(shape):
    return pl.BlockSpec(shape, lambda *_: (0,) * len(shape))


def attention_a(q, k, v, lq1, lk1, lq2, lk2, subln, lam_init):
    b, t, _ = q.shape
    tq = _block(t, A_BLOCK)
    assert tq % CHUNK == 0
    slopes = jnp.asarray([[[2.0 ** (-8.0 * (hh + 1) / H_A)] * LANES] for hh in range(H_A)], dtype=F32)
    head = lambda width: pl.BlockSpec((1, t, A_HEADS * width), lambda bi, hi: (bi, 0, hi))
    k_feat, q_feat = _position_features(t)
    assert DK_A == DV_A == LANES and H_A % A_HEADS == 0
    return pl.pallas_call(
        functools.partial(_attn_a_kernel, tq=tq, lam_init=lam_init),
        grid=(b, H_A // A_HEADS),
        in_specs=[head(DK_A), head(DK_A), head(DV_A),
                  pl.BlockSpec((A_HEADS, 1, LANES), lambda bi, hi: (hi, 0, 0)),
                  _const_spec((t, LANES)), _const_spec((t, LANES)),
                  _const_spec((1, DH_A)), _const_spec((1, DH_A)), _const_spec((1, DH_A)),
                  _const_spec((1, DH_A)), _const_spec((1, DV_A))],
        out_specs=head(DV_A),
        out_shape=jax.ShapeDtypeStruct((b, t, H_A * DV_A), BF16),
        scratch_shapes=[pltpu.VMEM((A_HEADS, t, DK_A + LANES), BF16),
                        pltpu.VMEM((A_HEADS, t, 2 * DV_A), BF16)],
        name="attention_a",
        compiler_params=_params("parallel", "arbitrary"),
    )(q, k, v, slopes, k_feat, q_feat, _row(lq1), _row(lk1), _row(lq2), _row(lk2), _row(subln))


def _attn_a_decode_kernel(q_ref, kn_ref, vn_ref, ck_ref, cv_ref, lq1_ref, lk1_ref, lq2_ref, lk2_ref,
                          sub_ref, o_ref, *, lam_init):
    t = q_ref.shape[1]
    p_len = ck_ref.shape[1] // H_A
    lam = _diff_lambda(lq1_ref, lk1_ref, lq2_ref, lk2_ref, lam_init)
    first_half = lax.broadcasted_iota(jnp.int32, (1, DK_A), 1) < DH_A
    j = lax.broadcasted_iota(jnp.int32, (1, p_len), 1)
    before_dist = (j - p_len).astype(F32)
    for h in range(H_A):
        slope = 2.0 ** (-8.0 * (h + 1) / H_A)
        cols = slice(h * DK_A, (h + 1) * DK_A)
        q = q_ref[0, :, cols] * (DH_A ** -0.5)
        kc = ck_ref[0, pl.ds(h, p_len, stride=H_A), :].astype(BF16)
        vc = cv_ref[0, pl.ds(h, p_len, stride=H_A), :].astype(BF16)
        kn = kn_ref[0, :, cols].astype(BF16)
        vn = vn_ref[0, :, cols].astype(BF16)
        own_bias = _own_alibi(t, slope)
        before_bias = slope * before_dist

        def softmax_pv(qh):
            s_new = lax.dot_general(qh, kn, CONTRACT_LAST, preferred_element_type=F32) + own_bias
            s_old = lax.dot_general(qh, kc, CONTRACT_LAST, preferred_element_type=F32) + before_bias
            m = jnp.maximum(jnp.max(s_new, axis=-1, keepdims=True), jnp.max(s_old, axis=-1, keepdims=True))
            p_new = jnp.exp(s_new - m)
            p_old = jnp.exp(s_old - m)
            l = jnp.sum(p_new, axis=-1, keepdims=True) + jnp.sum(p_old, axis=-1, keepdims=True)
            pv = (jnp.dot(p_new.astype(BF16), vn, preferred_element_type=F32)
                  + jnp.dot(p_old.astype(BF16), vc, preferred_element_type=F32))
            return pv / l

        o = (softmax_pv(jnp.where(first_half, q, jnp.zeros_like(q)))
             - lam * softmax_pv(jnp.where(first_half, jnp.zeros_like(q), q)))
        o_ref[0, :, cols] = _head_norm(o, sub_ref, lam_init).astype(o_ref.dtype)


def attention_a_decode(q, k_new, v_new, cache_k, cache_v, lq1, lk1, lq2, lk2, subln, lam_init):
    b, t, w = q.shape
    p_len = cache_k.shape[1]
    assert p_len % CHUNK == 0 and t <= CHUNK
    rows = lambda a: a.reshape(b, p_len * H_A, a.shape[-1])
    new = pl.BlockSpec((1, t, w), lambda bi: (bi, 0, 0))
    old = pl.BlockSpec((1, p_len * H_A, DK_A), lambda bi: (bi, 0, 0))
    return pl.pallas_call(
        functools.partial(_attn_a_decode_kernel, lam_init=lam_init),
        grid=(b,),
        in_specs=[new, new, new, old, old,
                  _const_spec((1, DH_A)), _const_spec((1, DH_A)), _const_spec((1, DH_A)),
                  _const_spec((1, DH_A)), _const_spec((1, DV_A))],
        out_specs=new,
        out_shape=jax.ShapeDtypeStruct((b, t, w), BF16),
        name="attention_a_decode",
        compiler_params=_params("parallel"),
    )(q, k_new, v_new, rows(cache_k), rows(cache_v), _row(lq1), _row(lk1), _row(lq2), _row(lk2), _row(subln))


def _band_tile(tq):
    pad = -(-tq // LANES) * LANES
    return BAND_PAST + pad, pad


def _band_bias_mask(g_row, tq):
    tile_w, pad = _band_tile(tq)
    width = g_row.shape[1]
    assert width == tile_w + pad
    rolled = pltpu.roll(jnp.broadcast_to(g_row, (tq, width)), width - pad, 1, stride=1, stride_axis=0)
    r = lax.broadcasted_iota(jnp.int32, (tq, tile_w), 0)
    c = lax.broadcasted_iota(jnp.int32, (tq, tile_w), 1)
    qch = r // CHUNK
    kch = c // CHUNK - N_PREV_CHUNKS
    valid = (kch <= qch) & (kch >= qch - N_PREV_CHUNKS)
    return jnp.where(valid, rolled[:, :tile_w], NEG_INF)


def _toeplitz_rows(rel_bias, tq):
    tile_w, pad = _band_tile(tq)
    width = tile_w + pad
    n_lo = pad + BAND_PAST - MAX_REL
    n_hi = max(width - n_lo - (2 * MAX_REL + 1), 0)
    lo = jnp.broadcast_to(rel_bias[:, :1], (H_B, n_lo))
    hi = jnp.broadcast_to(rel_bias[:, -1:], (H_B, n_hi))
    g = jnp.concatenate([lo, rel_bias, hi], axis=1)[:, :width]
    return g.reshape(H_B, 1, width).astype(F32)


def _attn_b_kernel(q_ref, k_ref, v_ref, g_ref, o_ref, ve_scr, *, tq):
    t = q_ref.shape[1]
    tile_w, _ = _band_tile(tq)
    scale = DH_B ** -0.5
    bias_mask = _band_bias_mask(g_ref[0], tq)

    def groups():
        for gi in range(t // tq):
            qa = gi * tq
            lo = max(qa - BAND_PAST, 0)
            hi = qa + tq
            off = lo - qa + BAND_PAST
            assert off % LANES == 0 and off + hi - lo == tile_w
            yield slice(qa, hi), slice(lo, hi), off

    log2e = math.log2(math.e)
    k = k_ref[0].astype(BF16)
    ve_scr[:, :DH_B] = v_ref[0].astype(BF16)
    ve_scr[:, DH_B:] = jnp.ones((t, DH_B), BF16)
    bias2 = bias_mask * log2e
    for rows, keys, off in groups():
        s = lax.dot_general(q_ref[0, rows, :], k[keys], CONTRACT_LAST, preferred_element_type=F32)
        p = jnp.exp2(s * (scale * log2e) + bias2[:, off:])
        acc = jnp.dot(p.astype(BF16), ve_scr[keys, :], preferred_element_type=F32)
        o_ref[0, rows, :] = (acc[:, :DH_B] / acc[:, DH_B:]).astype(o_ref.dtype)

    (q_sq,), (k_sq,) = _max_sq_norms(q_ref[0], (True,)), _max_sq_norms(k, (True,))
    room = SAFE_LOGIT - jnp.max(jnp.abs(g_ref[0]))
    safe = jnp.logical_and(room > 0.0, q_sq * k_sq * (scale * scale * 1.01) <= room * room)

    @pl.when(jnp.logical_not(safe))
    def _():
        k = k_ref[0].astype(BF16)
        v = v_ref[0].astype(BF16)
        for rows, keys, off in groups():
            s = lax.dot_general(q_ref[0, rows, :], k[keys], CONTRACT_LAST, preferred_element_type=F32) * scale
            s = s + bias_mask[:, off:]
            m = jnp.max(s, axis=-1, keepdims=True)
            p = jnp.exp(s - m)
            l = jnp.sum(p, axis=-1, keepdims=True)
            o = jnp.dot(p.astype(BF16), v[keys], preferred_element_type=F32) / l
            o_ref[0, rows, :] = o.astype(o_ref.dtype)


def attention_b(q, k, v, rel_bias):
    b, t, _ = q.shape
    tq = _block(t, BAND_GROUP)
    assert tq % LANES == 0
    g = _toeplitz_rows(rel_bias, tq)
    head = pl.BlockSpec((1, t, DH_B), lambda bi, hi: (bi, 0, hi))
    return pl.pallas_call(
        functools.partial(_attn_b_kernel, tq=tq),
        grid=(b, H_B),
        in_specs=[head, head, head, pl.BlockSpec((1, 1, g.shape[2]), lambda bi, hi: (hi, 0, 0))],
        out_specs=head,
        out_shape=jax.ShapeDtypeStruct((b, t, H_B * DH_B), BF16),
        scratch_shapes=[pltpu.VMEM((t, 2 * DH_B), BF16)],
        name="attention_b",
        compiler_params=_params("parallel", "arbitrary"),
    )(q, k, v, g)


def _attn_b_decode_kernel(q_ref, kn_ref, vn_ref, ck_ref, cv_ref, g_ref, o_ref):
    t = q_ref.shape[1]
    scale = DH_B ** -0.5
    for h in range(H_B):
        cols = slice(h * DH_B, (h + 1) * DH_B)
        q = q_ref[0, :, cols]
        kc = ck_ref[0, pl.ds(h, BAND_PAST, stride=H_B), :].astype(BF16)
        vc = cv_ref[0, pl.ds(h, BAND_PAST, stride=H_B), :].astype(BF16)
        kn = kn_ref[0, :, cols].astype(BF16)
        vn = vn_ref[0, :, cols].astype(BF16)
        bias_mask = _band_bias_mask(g_ref[h], t)
        s_old = lax.dot_general(q, kc, CONTRACT_LAST, preferred_element_type=F32) * scale
        s_old = s_old + bias_mask[:, :BAND_PAST]
        s_new = lax.dot_general(q, kn, CONTRACT_LAST, preferred_element_type=F32) * scale
        s_new = s_new + bias_mask[:, BAND_PAST:BAND_PAST + t]
        m = jnp.maximum(jnp.max(s_new, axis=-1, keepdims=True), jnp.max(s_old, axis=-1, keepdims=True))
        p_new = jnp.exp(s_new - m)
        p_old = jnp.exp(s_old - m)
        l = jnp.sum(p_new, axis=-1, keepdims=True) + jnp.sum(p_old, axis=-1, keepdims=True)
        pv = (jnp.dot(p_new.astype(BF16), vn, preferred_element_type=F32)
              + jnp.dot(p_old.astype(BF16), vc, preferred_element_type=F32))
        o_ref[0, :, cols] = (pv / l).astype(o_ref.dtype)


def attention_b_decode(q, k_new, v_new, cache_k, cache_v, rel_bias, p_len):
    b, t, w = q.shape
    assert cache_k.shape[1] == BAND_PAST and p_len % CHUNK == 0 and p_len >= BAND_PAST and t <= CHUNK
    g = _toeplitz_rows(rel_bias, t)
    rows = lambda a: a.reshape(b, BAND_PAST * H_B, a.shape[-1])
    new = pl.BlockSpec((1, t, w), lambda bi: (bi, 0, 0))
    old = pl.BlockSpec((1, BAND_PAST * H_B, DH_B), lambda bi: (bi, 0, 0))
    return pl.pallas_call(
        _attn_b_decode_kernel,
        grid=(b,),
        in_specs=[new, new, new, old, old, _const_spec(g.shape)],
        out_specs=new,
        out_shape=jax.ShapeDtypeStruct((b, t, w), BF16),
        name="attention_b_decode",
        compiler_params=_params("parallel"),
    )(q, k_new, v_new, rows(cache_k), rows(cache_v), g)


def _attn_c_kernel(q_ref, k_ref, v_ref, o_ref, *, tq, cache_order):
    t = q_ref.shape[1]
    scale = DH_C ** -0.5
    blocks = DH_C // LANES

    def head_rows(ref, h):
        if not cache_order:
            return ref[0, :, h * DH_C:(h + 1) * DH_C]
        n_mem = ref.shape[1] // (blocks * H_C)
        return jnp.concatenate([ref[0, pl.ds(j * H_C + h, n_mem, stride=blocks * H_C), :]
                                for j in range(blocks)], axis=1)

    for h in range(H_C):
        cols = slice(h * DH_C, (h + 1) * DH_C)
        k = head_rows(k_ref, h).astype(BF16)
        v = head_rows(v_ref, h).astype(BF16)

        def attend(probabilities):
            for i in range(t // tq):
                rows = slice(i * tq, (i + 1) * tq)
                s = lax.dot_general(q_ref[0, rows, cols], k, CONTRACT_LAST, preferred_element_type=F32)
                p = probabilities(s)
                l = jnp.sum(p, axis=-1, keepdims=True)
                o = jnp.dot(p.astype(BF16), v, preferred_element_type=F32) / l
                o_ref[0, rows, cols] = o.astype(o_ref.dtype)

        attend(lambda s: jnp.exp2(s * (scale * math.log2(math.e))))
        (q_sq,), (k_sq,) = _max_sq_norms(q_ref[0, :, cols], (True,)), _max_sq_norms(k, (True,))

        @pl.when(q_sq * k_sq * (scale * scale * 1.01) > SAFE_LOGIT ** 2)
        def _():
            def shifted(s):
                s = s * scale
                return jnp.exp(s - jnp.max(s, axis=-1, keepdims=True))
            attend(shifted)


def attention_c(q, mk, mv):
    b, t, w = q.shape
    cache_order = mk.ndim == 4
    if cache_order:
        n_mem, blocks = mk.shape[1], DH_C // LANES
        tiles = lambda c: c.reshape(b, n_mem, H_C, blocks, LANES).transpose(0, 1, 3, 2, 4).reshape(
            b, n_mem * blocks * H_C, LANES)
        mk, mv = tiles(mk), tiles(mv)
    kv_spec = pl.BlockSpec((1,) + mk.shape[1:], lambda bi: (bi, 0, 0))
    tq = _block(t, 512)
    return pl.pallas_call(
        functools.partial(_attn_c_kernel, tq=tq, cache_order=cache_order),
        grid=(b,),
        in_specs=[pl.BlockSpec((1, t, w), lambda bi: (bi, 0, 0)), kv_spec, kv_spec],
        out_specs=pl.BlockSpec((1, t, w), lambda bi: (bi, 0, 0)),
        out_shape=jax.ShapeDtypeStruct((b, t, w), BF16),
        name="attention_c",
        compiler_params=_params("parallel"),
    )(q, mk, mv)


def _merge_kernel(oa_ref, ob_ref, oc_ref, ga_ref, gb_ref, gc_ref, ba_ref, bb_ref, bc_ref,
                  wa_f32, wb_f32, wc_f32, o_ref, wa_ref, wb_ref, wc_ref):
    @pl.when(pl.program_id(0) == 0)
    def _():
        wa_ref[...] = wa_f32[...].astype(BF16)
        wb_ref[...] = wb_f32[...].astype(BF16)
        wc_ref[...] = wc_f32[...].astype(BF16)

    rows = _block(o_ref.shape[0], MERGE_CHUNK)
    for r0 in range(0, o_ref.shape[0], rows):
        sl = slice(r0, r0 + rows)

        def branch(o_r, g_r, b_r, w_r):
            gate = jax.nn.sigmoid(g_r[sl, :].astype(F32) + b_r[...])
            return gate * jnp.dot(o_r[sl, :], w_r[...], preferred_element_type=F32)

        merged = (branch(oa_ref, ga_ref, ba_ref, wa_ref) + branch(ob_ref, gb_ref, bb_ref, wb_ref)
                  + branch(oc_ref, gc_ref, bc_ref, wc_ref))
        o_ref[sl, :] = merged.astype(o_ref.dtype)


def merge_branches(oa, ob, oc, gate_logits, b_gate, wa, wb, wc):
    m, w_in = oa.shape
    d = wa.shape[1]
    bm = _block(m, MERGE_ROWS)
    row = lambda width: pl.BlockSpec((bm, width), lambda i: (i, 0))
    gate = lambda j: pl.BlockSpec((bm, d), lambda i: (i, j))
    bias = lambda j: pl.BlockSpec((1, d), lambda i: (0, j))
    weight = pl.BlockSpec((w_in, d), lambda i: (0, 0), pipeline_mode=pl.Buffered(1))
    bg = b_gate.reshape(1, N_BRANCH * d).astype(F32)
    return pl.pallas_call(
        _merge_kernel,
        grid=(m // bm,),
        in_specs=[row(w_in), row(w_in), row(w_in), gate(0), gate(1), gate(2),
                  bias(0), bias(1), bias(2), weight, weight, weight],
        out_specs=pl.BlockSpec((bm, d), lambda i: (i, 0)),
        out_shape=jax.ShapeDtypeStruct((m, d), BF16),
        scratch_shapes=[pltpu.VMEM((w_in, d), BF16)] * N_BRANCH,
        name="merge_branches",
        compiler_params=_params("arbitrary"),
    )(oa, ob, oc, gate_logits, gate_logits, gate_logits, bg, bg, bg, wa, wb, wc)


def _proj_norm_res_kernel(*refs, next_norm, rows, cast_w):
    refs = list(refs)
    if cast_w:
        w_bf = refs.pop()
    if next_norm:
        a_ref, w_ref, x_ref, g_ref, g2_ref, o_ref, h_ref = refs
    else:
        a_ref, w_ref, x_ref, g_ref, o_ref = refs
    if cast_w:
        @pl.when(pl.program_id(0) == 0)
        def _():
            w_bf[...] = w_ref[...].astype(BF16)
        w_ref = w_bf
    bm = a_ref.shape[0]
    for r0 in range(0, bm, rows):
        sl = slice(r0, r0 + rows)
        y = jnp.dot(a_ref[sl, :], w_ref[...], preferred_element_type=F32)
        ms = jnp.mean(y * y, axis=-1, keepdims=True)
        o = x_ref[sl, :] + y * lax.rsqrt(ms + EPS) * g_ref[...]
        o_ref[sl, :] = o
        if next_norm:
            ms2 = jnp.mean(o * o, axis=-1, keepdims=True)
            h_ref[sl, :] = (o * lax.rsqrt(ms2 + EPS) * g2_ref[...]).astype(h_ref.dtype)


def proj_norm_residual(a, w, x, g, bm, rows, next_g=None):
    m, k = a.shape
    d = w.shape[1]
    bm = _block(m, bm)
    rows = _block(bm, rows)
    vec = pl.BlockSpec((1, d), lambda i: (0, 0))
    row = pl.BlockSpec((bm, d), lambda i: (i, 0))
    in_specs = [pl.BlockSpec((bm, k), lambda i: (i, 0)),
                pl.BlockSpec((k, d), lambda i: (0, 0), pipeline_mode=pl.Buffered(1)), row, vec]
    args = [a, w, x, g.reshape(1, d).astype(F32)]
    out_specs, out_shape = row, jax.ShapeDtypeStruct((m, d), F32)
    if next_g is not None:
        in_specs.append(vec)
        args.append(next_g.reshape(1, d).astype(F32))
        out_specs = (row, row)
        out_shape = (out_shape, jax.ShapeDtypeStruct((m, d), BF16))
    cast_w = w.dtype != BF16
    return pl.pallas_call(
        functools.partial(_proj_norm_res_kernel, next_norm=next_g is not None, rows=rows, cast_w=cast_w),
        grid=(m // bm,),
        in_specs=in_specs,
        out_specs=out_specs,
        out_shape=out_shape,
        scratch_shapes=[pltpu.VMEM((k, d), BF16)] if cast_w else [],
        name="proj_norm_residual",
        compiler_params=_params("arbitrary"),
    )(*args)


def _ffn_in_kernel(h_ref, h2_ref, wa_ref, wb_ref, o_ref, o2_ref, wa_bf, wb_bf):
    def swiglu(h):
        a = jnp.dot(h, wa_bf[...], preferred_element_type=F32)
        b = jnp.dot(h, wb_bf[...], preferred_element_type=F32)
        return (jax.nn.silu(a) * b).astype(o_ref.dtype)

    @pl.when(pl.program_id(1) == 0)
    def _():
        wa_bf[...] = wa_ref[...].astype(BF16)
        wb_bf[...] = wb_ref[...].astype(BF16)
        o2_ref[...] = swiglu(h2_ref[...])

    o_ref[...] = swiglu(h_ref[...])


def ffn_in(h, extra, w):
    m, k = h.shape
    m2 = extra.shape[0]
    f = w.shape[1] // 2
    bm = _block(m, 1024)
    bf = 512
    assert f % bf == 0
    nf = f // bf
    return pl.pallas_call(
        _ffn_in_kernel,
        grid=(nf, m // bm),
        in_specs=[pl.BlockSpec((bm, k), lambda j, i: (i, 0)),
                  pl.BlockSpec((m2, k), lambda j, i: (0, 0)),
                  pl.BlockSpec((k, bf), lambda j, i: (0, j)),
                  pl.BlockSpec((k, bf), lambda j, i: (0, j + nf))],
        out_specs=[pl.BlockSpec((bm, bf), lambda j, i: (i, j)),
                   pl.BlockSpec((m2, bf), lambda j, i: (0, j))],
        out_shape=[jax.ShapeDtypeStruct((m, f), BF16), jax.ShapeDtypeStruct((m2, f), BF16)],
        scratch_shapes=[pltpu.VMEM((k, bf), BF16), pltpu.VMEM((k, bf), BF16)],
        name="ffn_in",
        compiler_params=_params("arbitrary", "arbitrary"),
    )(h, extra, w, w)


def _layer(xp, xd, mk_p, mv_p, mk_d, mv_d, caches, lam_init, norm_mix_pre, norm_mix_post, w_in, b_gate,
           lq1, lk1, lq2, lk2, subln_a, rel_bias, w_br_a, w_br_b, w_br_c, w_out,
           norm_ffn_pre, norm_ffn_post, w_ffn_in, w_ffn_out):
    d = xp.shape[-1]
    w_head = H_A * DK_A
    shapes = [xp.shape[:2], xd.shape[:2]]
    xs = [xp.reshape(-1, d), xd.reshape(-1, d)]
    h_d = rms_norm_bf16(xs[1], norm_mix_pre)
    h_p, qa_p, qa_d = norm_matmul_cols(xs[0], norm_mix_pre, w_in, 0, w_head, BF16, h_d)
    hs = [h_p, h_d]
    split = lambda outs: [o.reshape(*shp, w_head) for o, shp in zip(outs, shapes)]
    proj = lambda idx, dt: split(matmul_cols(hs[0], w_in, idx * w_head, w_head, dt, extra=hs[1]))
    q_a, q_b, q_c = split([qa_p, qa_d]), proj(3, BF16), proj(6, BF16)
    kv = dict(extra=hs[1], cache_dtype=F32)
    *k_a, ka_cache = matmul_cols(hs[0], w_in, 1 * w_head, w_head, BF16, head_major=DK_A // LANES, **kv)
    *v_a, va_cache = matmul_cols(hs[0], w_in, 2 * w_head, w_head, BF16, head_major=DV_A // LANES, **kv)
    k_a, v_a = split(k_a), split(v_a)
    a_caches = [c.reshape(*shapes[0], H_A, DK_A) for c in (ka_cache, va_cache)]
    seq = shapes[0][1]
    keep = min(BAND_PAST, seq)
    *k_b, kb_tail = matmul_cols(hs[0], w_in, 4 * w_head, w_head, BF16, tail=(seq, keep), **kv)
    *v_b, vb_tail = matmul_cols(hs[0], w_in, 5 * w_head, w_head, BF16, tail=(seq, keep), **kv)
    k_b, v_b = split(k_b), split(v_b)
    tails = [t.reshape(shapes[0][0], keep, H_B, DH_B) for t in (kb_tail, vb_tail)]
    gates = matmul_cols(hs[0], w_in, 7 * w_head, N_BRANCH * d, BF16, extra=hs[1])

    ca_k, ca_v, cb_k, cb_v = caches
    o_a = [attention_a(q_a[0], k_a[0], v_a[0], lq1, lk1, lq2, lk2, subln_a, lam_init),
           attention_a_decode(q_a[1], k_a[1], v_a[1], ca_k, ca_v, lq1, lk1, lq2, lk2, subln_a, lam_init)]
    o_b = [attention_b(q_b[0], k_b[0], v_b[0], rel_bias),
           attention_b_decode(q_b[1], k_b[1], v_b[1], cb_k, cb_v, rel_bias, ca_k.shape[1])]
    o_c = [attention_c(q_c[0], mk_p, mv_p), attention_c(q_c[1], mk_d, mv_d)]

    x1, h2 = [], []
    for i in range(2):
        m = xs[i].shape[0]
        merged = merge_branches(o_a[i].reshape(m, -1), o_b[i].reshape(m, -1), o_c[i].reshape(m, -1),
                                gates[i], b_gate, w_br_a, w_br_b, w_br_c)
        a, b = proj_norm_residual(merged, w_out, xs[i], norm_mix_post, bm=512, rows=PROJ_ROWS,
                                  next_g=norm_ffn_pre)
        x1.append(a)
        h2.append(b)
    acts = ffn_in(h2[0], h2[1], w_ffn_in)
    ys = [proj_norm_residual(acts[i], w_ffn_out, x1[i], norm_ffn_post, bm=512, rows=256).reshape(*shapes[i], d)
          for i in range(2)]
    return (ys[0], *a_caches, *tails), (ys[1], k_a[1], v_a[1], k_b[1], v_b[1])


def kernel(x_prompt, x_sample, cache_a_k, cache_a_v, cache_b_k, cache_b_v, cache_mem_k, cache_mem_v, mem_prompt, norm_mix_pre, norm_mix_post, norm_mem, w_in, b_gate, lambda_q1, lambda_k1, lambda_q2, lambda_k2, subln_a, rel_bias_b, w_mem_kv, w_br_a, w_br_b, w_br_c, w_out, norm_ffn_pre, norm_ffn_post, w_ffn_in, w_ffn_out):
    depth = w_in.shape[0]
    bsz, s, d = x_prompt.shape
    n_mem = mem_prompt.shape[1]
    yp, ys = x_prompt, x_sample
    outs = [[] for _ in range(10)]
    heads = lambda a, nh: a.reshape(a.shape[0], a.shape[1], nh, a.shape[2] // nh)
    for l in range(depth):
        lam_init = 0.8 - 0.6 * math.exp(-0.3 * l)
        shared = (lam_init, norm_mix_pre[l], norm_mix_post[l], w_in[l], b_gate[l],
                  lambda_q1[l], lambda_k1[l], lambda_q2[l], lambda_k2[l], subln_a[l], rel_bias_b[l],
                  w_br_a[l], w_br_b[l], w_br_c[l], w_out[l],
                  norm_ffn_pre[l], norm_ffn_post[l], w_ffn_in[l], w_ffn_out[l].astype(BF16))

        mem_n = rms_norm_bf16(mem_prompt.reshape(bsz * n_mem, d), norm_mem[l])
        w_c = H_C * DH_C
        mk_p, mk_cache = matmul_cols(mem_n, w_mem_kv[l], 0, w_c, BF16, head_major=DH_C // LANES, cache_dtype=F32)
        mv_p, mv_cache = matmul_cols(mem_n, w_mem_kv[l], w_c, w_c, BF16, head_major=DH_C // LANES, cache_dtype=F32)
        mk_p, mv_p = mk_p.reshape(bsz, n_mem, w_c), mv_p.reshape(bsz, n_mem, w_c)
        caches = (cache_a_k[l], cache_a_v[l], cache_b_k[l], cache_b_v[l])
        (yp, ka, va, kb, vb), (ys, ka_d, va_d, kb_d, vb_d) = _layer(
            yp, ys, mk_p, mv_p, cache_mem_k[l], cache_mem_v[l], caches, *shared)
        mem_cache = lambda c: c.reshape(bsz, n_mem, DH_C // LANES, H_C, LANES).transpose(0, 1, 3, 2, 4).reshape(
            bsz, n_mem, H_C, DH_C)
        new = [ka, va, kb, vb, mem_cache(mk_cache), mem_cache(mv_cache),
               heads(ka_d, H_A), heads(va_d, H_A), heads(kb_d, H_B), heads(vb_d, H_B)]
        for o, a in zip(outs, new):
            o.append(a)
    return (yp, ys) + tuple(jnp.stack(o) for o in outs)
```

```python
import functools
import math

import jax
import jax.numpy as jnp
from jax import lax
from jax.experimental import pallas as pl
from jax.experimental.pallas import tpu as pltpu

F32 = jnp.float32
BF16 = jnp.bfloat16

CHUNK = 64
H_A = 8
DH_A = 64
DK_A = 2 * DH_A
DV_A = 128
H_B = 8
DH_B = 128
N_PREV_CHUNKS = 8
BAND_PAST = N_PREV_CHUNKS * CHUNK
MAX_REL = 128
H_C = 4
DH_C = 256
N_BRANCH = 3
EPS = 1e-6
NEG_INF = -1e30

LANES = 128
VMEM_LIMIT = 56 * 1024 * 1024
BAND_GROUP = 256
A_BLOCK = 256
A_HEADS = 1
MM_ROWS = 2048
MERGE_ROWS = 256
MERGE_CHUNK = 256
PROJ_ROWS = 128
SAFE_LOGIT = 40.0

CONTRACT_LAST = (((1,), (1,)), ((), ()))


def _params(*sem):
    return pltpu.CompilerParams(dimension_semantics=sem, vmem_limit_bytes=VMEM_LIMIT)


def _block(n, target):
    if n <= target:
        return n
    b = target
    while n % b:
        b //= 2
    return b


def _rms_kernel(x_ref, g_ref, o_ref):
    x = x_ref[...]
    ms = jnp.mean(x * x, axis=-1, keepdims=True)
    o_ref[...] = (x * lax.rsqrt(ms + EPS) * g_ref[...]).astype(o_ref.dtype)


def rms_norm_bf16(x, g):
    m, d = x.shape
    bm = _block(m, 512)
    return pl.pallas_call(
        _rms_kernel,
        grid=(m // bm,),
        in_specs=[pl.BlockSpec((bm, d), lambda i: (i, 0)),
                  pl.BlockSpec((1, d), lambda i: (0, 0))],
        out_specs=pl.BlockSpec((bm, d), lambda i: (i, 0)),
        out_shape=jax.ShapeDtypeStruct((m, d), BF16),
        name="rms_norm",
        compiler_params=_params("parallel"),
    )(x, g.reshape(1, d))


def _mm_kernel(*refs, has_extra, tail, head_major):
    refs = list(refs)
    a_ref = refs.pop(0)
    a2_ref = refs.pop(0) if has_extra else None
    w_ref = refs.pop(0)
    o_ref = refs.pop(0)
    o2_ref = refs.pop(0) if has_extra else None
    t_ref = refs.pop(0) if tail else None
    hm_ref = refs.pop(0) if head_major else None
    wb_ref, = refs

    def product(x_ref, y_ref, z_ref=None, tail_ref=None):
        n_rows = x_ref.shape[0]
        rows = _block(n_rows, 1024)
        for r0 in range(0, n_rows, rows):
            acc = jnp.dot(x_ref[r0:r0 + rows, :], wb_ref[...], preferred_element_type=F32)
            y_ref[r0:r0 + rows, :] = acc.astype(y_ref.dtype)
            heads = acc.shape[1] // LANES
            if z_ref is not None:
                for h in range(heads):
                    slot = (h % head_major) * (heads // head_major) + h // head_major
                    z_ref[pl.ds(r0 * heads + slot, rows, stride=heads), :] = (
                        acc[:, h * LANES:(h + 1) * LANES].astype(z_ref.dtype))
            if tail_ref is not None:
                blocks, keep = tail
                first = max(r0, n_rows - keep)
                if first < r0 + rows:
                    @pl.when(pl.program_id(1) % blocks == blocks - 1)
                    def _():
                        for h in range(heads):
                            tail_ref[pl.ds((first - (n_rows - keep)) * heads + h, r0 + rows - first,
                                           stride=heads), :] = (
                                acc[first - r0:, h * LANES:(h + 1) * LANES].astype(tail_ref.dtype))

    @pl.when(pl.program_id(1) == 0)
    def _():
        wb_ref[...] = w_ref[...].astype(BF16)
        if has_extra:
            product(a2_ref, o2_ref)

    product(a_ref, o_ref, hm_ref, t_ref)


def matmul_cols(a, w, col0, n, out_dtype, extra=None, tail=None, head_major=0, cache_dtype=None):
    m, k = a.shape
    cache_dtype = cache_dtype or out_dtype
    side_bytes = jnp.dtype(cache_dtype).itemsize if head_major else 0
    bm = _block(m, MM_ROWS * 2 // max(jnp.dtype(out_dtype).itemsize, side_bytes))
    bn = _block(n, 1024)
    assert col0 % bn == 0
    off = col0 // bn
    in_specs = [pl.BlockSpec((bm, k), lambda j, i: (i, 0))]
    args = [a]
    out_specs = [pl.BlockSpec((bm, bn), lambda j, i: (i, j))]
    out_shape = [jax.ShapeDtypeStruct((m, n), out_dtype)]
    if extra is not None:
        m2 = extra.shape[0]
        in_specs.append(pl.BlockSpec((m2, k), lambda j, i: (0, 0)))
        args.append(extra)
        out_specs.append(pl.BlockSpec((m2, bn), lambda j, i: (0, j)))
        out_shape.append(jax.ShapeDtypeStruct((m2, n), cache_dtype))
    if tail is not None:
        seq, keep = tail
        assert seq % bm == 0 and keep <= bm and m % seq == 0
        blocks = seq // bm
        assert bn == n
        out_specs.append(pl.BlockSpec((keep * (n // LANES), LANES), lambda j, i: (i // blocks, 0)))
        out_shape.append(jax.ShapeDtypeStruct((m // seq * keep * (n // LANES), LANES), cache_dtype))
        tail = (blocks, keep)
    if head_major:
        assert bn == n
        heads = n // LANES
        out_specs.append(pl.BlockSpec((bm * heads, LANES), lambda j, i: (i, 0)))
        out_shape.append(jax.ShapeDtypeStruct((m * heads, LANES), cache_dtype))
    in_specs.append(pl.BlockSpec((k, bn), lambda j, i: (0, j + off)))
    args.append(w)
    return pl.pallas_call(
        functools.partial(_mm_kernel, has_extra=extra is not None, tail=tail, head_major=head_major),
        grid=(n // bn, m // bm),
        in_specs=in_specs,
        out_specs=out_specs,
        out_shape=out_shape,
        scratch_shapes=[pltpu.VMEM((k, bn), BF16)],
        name="matmul_cols",
        compiler_params=_params("arbitrary", "arbitrary"),
    )(*args)


def _norm_mm_kernel(x_ref, g_ref, a2_ref, w_ref, h_ref, o_ref, o2_ref, wb_ref, *, rows):
    @pl.when(pl.program_id(0) == 0)
    def _():
        wb_ref[...] = w_ref[...].astype(BF16)
        o2_ref[...] = jnp.dot(a2_ref[...], wb_ref[...], preferred_element_type=F32).astype(o2_ref.dtype)

    for r0 in range(0, x_ref.shape[0], rows):
        x = x_ref[r0:r0 + rows, :]
        ms = jnp.mean(x * x, axis=-1, keepdims=True)
        h = (x * lax.rsqrt(ms + EPS) * g_ref[...]).astype(BF16)
        h_ref[r0:r0 + rows, :] = h
        o_ref[r0:r0 + rows, :] = jnp.dot(h, wb_ref[...], preferred_element_type=F32).astype(o_ref.dtype)


def norm_matmul_cols(x, g, w, col0, n, out_dtype, extra):
    m, k = x.shape
    m2 = extra.shape[0]
    bm = _block(m, 1024)
    assert col0 % n == 0
    off = col0 // n
    return pl.pallas_call(
        functools.partial(_norm_mm_kernel, rows=_block(bm, 128)),
        grid=(m // bm,),
        in_specs=[pl.BlockSpec((bm, k), lambda i: (i, 0)),
                  pl.BlockSpec((1, k), lambda i: (0, 0)),
                  pl.BlockSpec((m2, k), lambda i: (0, 0)),
                  pl.BlockSpec((k, n), lambda i: (0, off))],
        out_specs=[pl.BlockSpec((bm, k), lambda i: (i, 0)),
                   pl.BlockSpec((bm, n), lambda i: (i, 0)),
                   pl.BlockSpec((m2, n), lambda i: (0, 0))],
        out_shape=[jax.ShapeDtypeStruct((m, k), BF16), jax.ShapeDtypeStruct((m, n), out_dtype),
                   jax.ShapeDtypeStruct((m2, n), out_dtype)],
        scratch_shapes=[pltpu.VMEM((k, n), BF16)],
        name="norm_matmul_cols",
        compiler_params=_params("arbitrary"),
    )(x, g.reshape(1, k).astype(F32), extra, w)


def _diff_lambda(lq1_ref, lk1_ref, lq2_ref, lk2_ref, lam_init):
    return (jnp.exp(jnp.sum(lq1_ref[...] * lk1_ref[...], keepdims=True))
            - jnp.exp(jnp.sum(lq2_ref[...] * lk2_ref[...], keepdims=True)) + lam_init)


def _head_norm(o, sub_ref, lam_init):
    ms = jnp.mean(o * o, axis=-1, keepdims=True)
    return o * lax.rsqrt(ms + EPS) * sub_ref[...] * (1.0 - lam_init)


def _own_alibi(n, slope):
    r = lax.broadcasted_iota(jnp.int32, (n, n), 0)
    c = lax.broadcasted_iota(jnp.int32, (n, n), 1)
    bias = slope * (r - jnp.abs(r - c)).astype(F32)
    return jnp.where(c // CHUNK <= r // CHUNK, bias, NEG_INF)


def _max_sq_norms(x, lane_masks):
    sq = jnp.square(x.astype(F32))
    return [jnp.max(jnp.sum(jnp.where(mask, sq, 0.0), axis=-1, keepdims=True)) for mask in lane_masks]


def _attn_a_kernel(q_ref, k_ref, v_ref, slope_ref, kf_ref, qf_ref, lq1_ref, lk1_ref, lq2_ref, lk2_ref,
                   sub_ref, o_ref, ke_scr, ve_scr, *, tq, lam_init):
    first_half = lax.broadcasted_iota(jnp.int32, (1, DK_A), 1) < DH_A
    heads = []
    for hh in range(A_HEADS):
        cols = slice(hh * DK_A, (hh + 1) * DK_A)
        args = (q_ref.at[:, :, cols], k_ref.at[:, :, cols], v_ref.at[:, :, cols], slope_ref.at[hh:hh + 1],
                lq1_ref, lk1_ref, lq2_ref, lk2_ref, sub_ref, o_ref.at[:, :, cols])
        _attn_a_bounded(*args, kf_ref, qf_ref, ke_scr.at[hh], ve_scr.at[hh], tq=tq, lam_init=lam_init)
        heads.append(args)
    bounds = []
    for hh, args in enumerate(heads):
        q_sq = _max_sq_norms(args[0][0] * (DH_A ** -0.5), (first_half, ~first_half))
        k_sq = _max_sq_norms(ke_scr[hh, :, :DK_A], (first_half, ~first_half))
        bounds.append(jnp.maximum(q_sq[0] * k_sq[0], q_sq[1] * k_sq[1]))
    for args, bound_sq in zip(heads, bounds):
        @pl.when(bound_sq > SAFE_LOGIT ** 2 / 1.01)
        def _():
            _attn_a_general(*args, tq=tq, lam_init=lam_init)


def _position_features(t):
    pos = jnp.arange(t, dtype=jnp.int32)[:, None]
    lo = pos & 7
    hi = (pos - lo).astype(F32)
    lo = lo.astype(F32)
    one = jnp.ones((t, 1), F32)
    pad = jnp.zeros((t, LANES - 4), F32)
    k_side = jnp.concatenate([hi, lo, one, one, pad], axis=1).astype(BF16)
    q_side = jnp.concatenate([one, one, -hi, -lo, pad], axis=1)
    return k_side, q_side


def _attn_a_bounded(q_ref, k_ref, v_ref, slope_ref, lq1_ref, lk1_ref, lq2_ref, lk2_ref, sub_ref, o_ref,
                    kf_ref, qf_ref, ke_scr, ve_scr, *, tq, lam_init):
    t = q_ref.shape[1]
    slope = slope_ref[0][:, :1]
    lam = _diff_lambda(lq1_ref, lk1_ref, lq2_ref, lk2_ref, lam_init)
    first_half = lax.broadcasted_iota(jnp.int32, (1, DK_A), 1) < DH_A
    ke_scr[:, :DK_A] = k_ref[0].astype(BF16)
    ke_scr[:, DK_A:] = kf_ref[...]
    ve_scr[:, :DV_A] = v_ref[0].astype(BF16)
    ve_scr[:, DV_A:] = jnp.ones((t, DV_A), BF16)

    r = lax.broadcasted_iota(jnp.int32, (tq, tq), 0)
    c = lax.broadcasted_iota(jnp.int32, (tq, tq), 1)
    own_fix = jnp.where(c // CHUNK <= r // CHUNK, -2.0 * slope * jnp.maximum(c - r, 0).astype(F32), NEG_INF)
    own_fix = jnp.concatenate([own_fix, own_fix], axis=0)

    for i in reversed(range(t // tq)):
        nb = i * tq
        q = q_ref[0, nb:nb + tq, :] * (DH_A ** -0.5)
        q_feat = (qf_ref[nb:nb + tq, :] * slope).astype(BF16)
        zero = jnp.zeros_like(q)
        qe = jnp.concatenate([jnp.concatenate([jnp.where(first_half, q, zero), q_feat], axis=1),
                              jnp.concatenate([jnp.where(first_half, zero, q), q_feat], axis=1)], axis=0)
        s_own = lax.dot_general(qe, ke_scr[nb:nb + tq, :], CONTRACT_LAST, preferred_element_type=F32)
        acc = jnp.dot(jnp.exp(s_own + own_fix).astype(BF16), ve_scr[nb:nb + tq, :],
                      preferred_element_type=F32)
        if nb:
            s_bef = lax.dot_general(qe, ke_scr[:nb, :], CONTRACT_LAST, preferred_element_type=F32)
            acc = acc + jnp.dot(jnp.exp(s_bef).astype(BF16), ve_scr[:nb, :], preferred_element_type=F32)
        o = acc[:, :DV_A] / acc[:, DV_A:]
        o = o[:tq] - lam * o[tq:]
        o_ref[0, nb:nb + tq, :] = _head_norm(o, sub_ref, lam_init).astype(o_ref.dtype)


def _attn_a_general(q_ref, k_ref, v_ref, slope_ref, lq1_ref, lk1_ref, lq2_ref, lk2_ref, sub_ref, o_ref,
                    *, tq, lam_init):
    t = q_ref.shape[1]
    n_blocks = t // tq
    slope = slope_ref[0][:, :1]
    lam = _diff_lambda(lq1_ref, lk1_ref, lq2_ref, lk2_ref, lam_init)
    k = k_ref[0].astype(BF16)
    v = v_ref[0].astype(BF16)
    first_half = lax.broadcasted_iota(jnp.int32, (1, DK_A), 1) < DH_A
    own_bias = _own_alibi(tq, slope)
    n_before_max = (n_blocks - 1) * tq
    if n_before_max:
        j = lax.broadcasted_iota(jnp.int32, (1, n_before_max), 1)
        before_bias = slope * (j - n_before_max).astype(F32)

    for i in range(n_blocks):
        nb = i * tq
        q = q_ref[0, nb:nb + tq, :] * (DH_A ** -0.5)
        k_own, v_own = k[nb:nb + tq], v[nb:nb + tq]

        def softmax_pv(qh):
            s_own = lax.dot_general(qh, k_own, CONTRACT_LAST, preferred_element_type=F32) + own_bias
            m = jnp.max(s_own, axis=-1, keepdims=True)
            if nb:
                s_bef = lax.dot_general(qh, k[:nb], CONTRACT_LAST, preferred_element_type=F32)
                s_bef = s_bef + before_bias[:, n_before_max - nb:]
                m = jnp.maximum(m, jnp.max(s_bef, axis=-1, keepdims=True))
            p_own = jnp.exp(s_own - m)
            l = jnp.sum(p_own, axis=-1, keepdims=True)
            pv = jnp.dot(p_own.astype(BF16), v_own, preferred_element_type=F32)
            if nb:
                p_bef = jnp.exp(s_bef - m)
                l = l + jnp.sum(p_bef, axis=-1, keepdims=True)
                pv = pv + jnp.dot(p_bef.astype(BF16), v[:nb], preferred_element_type=F32)
            return pv / l

        o = (softmax_pv(jnp.where(first_half, q, jnp.zeros_like(q)))
             - lam * softmax_pv(jnp.where(first_half, jnp.zeros_like(q), q)))
        o_ref[0, nb:nb + tq, :] = _head_norm(o, sub_ref, lam_init).astype(o_ref.dtype)


def _row(a):
    return a.reshape(1, -1).astype(F32)


def _const_spec(shape):
    return pl.BlockSpec(shape, lambda *_: (0,) * len(shape))


def attention_a(q, k, v, lq1, lk1, lq2, lk2, subln, lam_init):
    b, t, _ = q.shape
    tq = _block(t, A_BLOCK)
    assert tq % CHUNK == 0
    slopes = jnp.asarray([[[2.0 ** (-8.0 * (hh + 1) / H_A)] * LANES] for hh in range(H_A)], dtype=F32)
    head = lambda width: pl.BlockSpec((1, t, A_HEADS * width), lambda bi, hi: (bi, 0, hi))
    k_feat, q_feat = _position_features(t)
    assert DK_A == DV_A == LANES and H_A % A_HEADS == 0
    return pl.pallas_call(
        functools.partial(_attn_a_kernel, tq=tq, lam_init=lam_init),
        grid=(b, H_A // A_HEADS),
        in_specs=[head(DK_A), head(DK_A), head(DV_A),
                  pl.BlockSpec((A_HEADS, 1, LANES), lambda bi, hi: (hi, 0, 0)),
                  _const_spec((t, LANES)), _const_spec((t, LANES)),
                  _const_spec((1, DH_A)), _const_spec((1, DH_A)), _const_spec((1, DH_A)),
                  _const_spec((1, DH_A)), _const_spec((1, DV_A))],
        out_specs=head(DV_A),
        out_shape=jax.ShapeDtypeStruct((b, t, H_A * DV_A), BF16),
        scratch_shapes=[pltpu.VMEM((A_HEADS, t, DK_A + LANES), BF16),
                        pltpu.VMEM((A_HEADS, t, 2 * DV_A), BF16)],
        name="attention_a",
        compiler_params=_params("parallel", "arbitrary"),
    )(q, k, v, slopes, k_feat, q_feat, _row(lq1), _row(lk1), _row(lq2), _row(lk2), _row(subln))


def _attn_a_decode_kernel(q_ref, kn_ref, vn_ref, ck_ref, cv_ref, lq1_ref, lk1_ref, lq2_ref, lk2_ref,
                          sub_ref, o_ref, *, lam_init):
    t = q_ref.shape[1]
    p_len = ck_ref.shape[1] // H_A
    lam = _diff_lambda(lq1_ref, lk1_ref, lq2_ref, lk2_ref, lam_init)
    first_half = lax.broadcasted_iota(jnp.int32, (1, DK_A), 1) < DH_A
    j = lax.broadcasted_iota(jnp.int32, (1, p_len), 1)
    before_dist = (j - p_len).astype(F32)
    for h in range(H_A):
        slope = 2.0 ** (-8.0 * (h + 1) / H_A)
        cols = slice(h * DK_A, (h + 1) * DK_A)
        q = q_ref[0, :, cols] * (DH_A ** -0.5)
        kc = ck_ref[0, pl.ds(h, p_len, stride=H_A), :].astype(BF16)
        vc = cv_ref[0, pl.ds(h, p_len, stride=H_A), :].astype(BF16)
        kn = kn_ref[0, :, cols].astype(BF16)
        vn = vn_ref[0, :, cols].astype(BF16)
        own_bias = _own_alibi(t, slope)
        before_bias = slope * before_dist

        def softmax_pv(qh):
            s_new = lax.dot_general(qh, kn, CONTRACT_LAST, preferred_element_type=F32) + own_bias
            s_old = lax.dot_general(qh, kc, CONTRACT_LAST, preferred_element_type=F32) + before_bias
            m = jnp.maximum(jnp.max(s_new, axis=-1, keepdims=True), jnp.max(s_old, axis=-1, keepdims=True))
            p_new = jnp.exp(s_new - m)
            p_old = jnp.exp(s_old - m)
            l = jnp.sum(p_new, axis=-1, keepdims=True) + jnp.sum(p_old, axis=-1, keepdims=True)
            pv = (jnp.dot(p_new.astype(BF16), vn, preferred_element_type=F32)
                  + jnp.dot(p_old.astype(BF16), vc, preferred_element_type=F32))
            return pv / l

        o = (softmax_pv(jnp.where(first_half, q, jnp.zeros_like(q)))
             - lam * softmax_pv(jnp.where(first_half, jnp.zeros_like(q), q)))
        o_ref[0, :, cols] = _head_norm(o, sub_ref, lam_init).astype(o_ref.dtype)


def attention_a_decode(q, k_new, v_new, cache_k, cache_v, lq1, lk1, lq2, lk2, subln, lam_init):
    b, t, w = q.shape
    p_len = cache_k.shape[1]
    assert p_len % CHUNK == 0 and t <= CHUNK
    rows = lambda a: a.reshape(b, p_len * H_A, a.shape[-1])
    new = pl.BlockSpec((1, t, w), lambda bi: (bi, 0, 0))
    old = pl.BlockSpec((1, p_len * H_A, DK_A), lambda bi: (bi, 0, 0))
    return pl.pallas_call(
        functools.partial(_attn_a_decode_kernel, lam_init=lam_init),
        grid=(b,),
        in_specs=[new, new, new, old, old,
                  _const_spec((1, DH_A)), _const_spec((1, DH_A)), _const_spec((1, DH_A)),
                  _const_spec((1, DH_A)), _const_spec((1, DV_A))],
        out_specs=new,
        out_shape=jax.ShapeDtypeStruct((b, t, w), BF16),
        name="attention_a_decode",
        compiler_params=_params("parallel"),
    )(q, k_new, v_new, rows(cache_k), rows(cache_v), _row(lq1), _row(lk1), _row(lq2), _row(lk2), _row(subln))


def _band_tile(tq):
    pad = -(-tq // LANES) * LANES
    return BAND_PAST + pad, pad


def _band_bias_mask(g_row, tq):
    tile_w, pad = _band_tile(tq)
    width = g_row.shape[1]
    assert width == tile_w + pad
    rolled = pltpu.roll(jnp.broadcast_to(g_row, (tq, width)), width - pad, 1, stride=1, stride_axis=0)
    r = lax.broadcasted_iota(jnp.int32, (tq, tile_w), 0)
    c = lax.broadcasted_iota(jnp.int32, (tq, tile_w), 1)
    qch = r // CHUNK
    kch = c // CHUNK - N_PREV_CHUNKS
    valid = (kch <= qch) & (kch >= qch - N_PREV_CHUNKS)
    return jnp.where(valid, rolled[:, :tile_w], NEG_INF)


def _toeplitz_rows(rel_bias, tq):
    tile_w, pad = _band_tile(tq)
    width = tile_w + pad
    n_lo = pad + BAND_PAST - MAX_REL
    n_hi = max(width - n_lo - (2 * MAX_REL + 1), 0)
    lo = jnp.broadcast_to(rel_bias[:, :1], (H_B, n_lo))
    hi = jnp.broadcast_to(rel_bias[:, -1:], (H_B, n_hi))
    g = jnp.concatenate([lo, rel_bias, hi], axis=1)[:, :width]
    return g.reshape(H_B, 1, width).astype(F32)


def _attn_b_kernel(q_ref, k_ref, v_ref, g_ref, o_ref, ve_scr, *, tq):
    t = q_ref.shape[1]
    tile_w, _ = _band_tile(tq)
    scale = DH_B ** -0.5
    bias_mask = _band_bias_mask(g_ref[0], tq)

    def groups():
        for gi in range(t // tq):
            qa = gi * tq
            lo = max(qa - BAND_PAST, 0)
            hi = qa + tq
            off = lo - qa + BAND_PAST
            assert off % LANES == 0 and off + hi - lo == tile_w
            yield slice(qa, hi), slice(lo, hi), off

    log2e = math.log2(math.e)
    k = k_ref[0].astype(BF16)
    ve_scr[:, :DH_B] = v_ref[0].astype(BF16)
    ve_scr[:, DH_B:] = jnp.ones((t, DH_B), BF16)
    bias2 = bias_mask * log2e
    for rows, keys, off in groups():
        s = lax.dot_general(q_ref[0, rows, :], k[keys], CONTRACT_LAST, preferred_element_type=F32)
        p = jnp.exp2(s * (scale * log2e) + bias2[:, off:])
        acc = jnp.dot(p.astype(BF16), ve_scr[keys, :], preferred_element_type=F32)
        o_ref[0, rows, :] = (acc[:, :DH_B] / acc[:, DH_B:]).astype(o_ref.dtype)

    (q_sq,), (k_sq,) = _max_sq_norms(q_ref[0], (True,)), _max_sq_norms(k, (True,))
    room = SAFE_LOGIT - jnp.max(jnp.abs(g_ref[0]))
    safe = jnp.logical_and(room > 0.0, q_sq * k_sq * (scale * scale * 1.01) <= room * room)

    @pl.when(jnp.logical_not(safe))
    def _():
        k = k_ref[0].astype(BF16)
        v = v_ref[0].astype(BF16)
        for rows, keys, off in groups():
            s = lax.dot_general(q_ref[0, rows, :], k[keys], CONTRACT_LAST, preferred_element_type=F32) * scale
            s = s + bias_mask[:, off:]
            m = jnp.max(s, axis=-1, keepdims=True)
            p = jnp.exp(s - m)
            l = jnp.sum(p, axis=-1, keepdims=True)
            o = jnp.dot(p.astype(BF16), v[keys], preferred_element_type=F32) / l
            o_ref[0, rows, :] = o.astype(o_ref.dtype)


def attention_b(q, k, v, rel_bias):
    b, t, _ = q.shape
    tq = _block(t, BAND_GROUP)
    assert tq % LANES == 0
    g = _toeplitz_rows(rel_bias, tq)
    head = pl.BlockSpec((1, t, DH_B), lambda bi, hi: (bi, 0, hi))
    return pl.pallas_call(
        functools.partial(_attn_b_kernel, tq=tq),
        grid=(b, H_B),
        in_specs=[head, head, head, pl.BlockSpec((1, 1, g.shape[2]), lambda bi, hi: (hi, 0, 0))],
        out_specs=head,
        out_shape=jax.ShapeDtypeStruct((b, t, H_B * DH_B), BF16),
        scratch_shapes=[pltpu.VMEM((t, 2 * DH_B), BF16)],
        name="attention_b",
        compiler_params=_params("parallel", "arbitrary"),
    )(q, k, v, g)


def _attn_b_decode_kernel(q_ref, kn_ref, vn_ref, ck_ref, cv_ref, g_ref, o_ref):
    t = q_ref.shape[1]
    scale = DH_B ** -0.5
    for h in range(H_B):
        cols = slice(h * DH_B, (h + 1) * DH_B)
        q = q_ref[0, :, cols]
        kc = ck_ref[0, pl.ds(h, BAND_PAST, stride=H_B), :].astype(BF16)
        vc = cv_ref[0, pl.ds(h, BAND_PAST, stride=H_B), :].astype(BF16)
        kn = kn_ref[0, :, cols].astype(BF16)
        vn = vn_ref[0, :, cols].astype(BF16)
        bias_mask = _band_bias_mask(g_ref[h], t)
        s_old = lax.dot_general(q, kc, CONTRACT_LAST, preferred_element_type=F32) * scale
        s_old = s_old + bias_mask[:, :BAND_PAST]
        s_new = lax.dot_general(q, kn, CONTRACT_LAST, preferred_element_type=F32) * scale
        s_new = s_new + bias_mask[:, BAND_PAST:BAND_PAST + t]
        m = jnp.maximum(jnp.max(s_new, axis=-1, keepdims=True), jnp.max(s_old, axis=-1, keepdims=True))
        p_new = jnp.exp(s_new - m)
        p_old = jnp.exp(s_old - m)
        l = jnp.sum(p_new, axis=-1, keepdims=True) + jnp.sum(p_old, axis=-1, keepdims=True)
        pv = (jnp.dot(p_new.astype(BF16), vn, preferred_element_type=F32)
              + jnp.dot(p_old.astype(BF16), vc, preferred_element_type=F32))
        o_ref[0, :, cols] = (pv / l).astype(o_ref.dtype)


def attention_b_decode(q, k_new, v_new, cache_k, cache_v, rel_bias, p_len):
    b, t, w = q.shape
    assert cache_k.shape[1] == BAND_PAST and p_len % CHUNK == 0 and p_len >= BAND_PAST and t <= CHUNK
    g = _toeplitz_rows(rel_bias, t)
    rows = lambda a: a.reshape(b, BAND_PAST * H_B, a.shape[-1])
    new = pl.BlockSpec((1, t, w), lambda bi: (bi, 0, 0))
    old = pl.BlockSpec((1, BAND_PAST * H_B, DH_B), lambda bi: (bi, 0, 0))
    return pl.pallas_call(
        _attn_b_decode_kernel,
        grid=(b,),
        in_specs=[new, new, new, old, old, _const_spec(g.shape)],
        out_specs=new,
        out_shape=jax.ShapeDtypeStruct((b, t, w), BF16),
        name="attention_b_decode",
        compiler_params=_params("parallel"),
    )(q, k_new, v_new, rows(cache_k), rows(cache_v), g)


def _attn_c_kernel(q_ref, k_ref, v_ref, o_ref, *, tq, cache_order):
    t = q_ref.shape[1]
    scale = DH_C ** -0.5
    blocks = DH_C // LANES

    def head_rows(ref, h):
        if not cache_order:
            return ref[0, :, h * DH_C:(h + 1) * DH_C]
        n_mem = ref.shape[1] // (blocks * H_C)
        return jnp.concatenate([ref[0, pl.ds(j * H_C + h, n_mem, stride=blocks * H_C), :]
                                for j in range(blocks)], axis=1)

    for h in range(H_C):
        cols = slice(h * DH_C, (h + 1) * DH_C)
        k = head_rows(k_ref, h).astype(BF16)
        v = head_rows(v_ref, h).astype(BF16)

        def attend(probabilities):
            for i in range(t // tq):
                rows = slice(i * tq, (i + 1) * tq)
                s = lax.dot_general(q_ref[0, rows, cols], k, CONTRACT_LAST, preferred_element_type=F32)
                p = probabilities(s)
                l = jnp.sum(p, axis=-1, keepdims=True)
                o = jnp.dot(p.astype(BF16), v, preferred_element_type=F32) / l
                o_ref[0, rows, cols] = o.astype(o_ref.dtype)

        attend(lambda s: jnp.exp2(s * (scale * math.log2(math.e))))
        (q_sq,), (k_sq,) = _max_sq_norms(q_ref[0, :, cols], (True,)), _max_sq_norms(k, (True,))

        @pl.when(q_sq * k_sq * (scale * scale * 1.01) > SAFE_LOGIT ** 2)
        def _():
            def shifted(s):
                s = s * scale
                return jnp.exp(s - jnp.max(s, axis=-1, keepdims=True))
            attend(shifted)


def attention_c(q, mk, mv):
    b, t, w = q.shape
    cache_order = mk.ndim == 4
    if cache_order:
        n_mem, blocks = mk.shape[1], DH_C // LANES
        tiles = lambda c: c.reshape(b, n_mem, H_C, blocks, LANES).transpose(0, 1, 3, 2, 4).reshape(
            b, n_mem * blocks * H_C, LANES)
        mk, mv = tiles(mk), tiles(mv)
    kv_spec = pl.BlockSpec((1,) + mk.shape[1:], lambda bi: (bi, 0, 0))
    tq = _block(t, 512)
    return pl.pallas_call(
        functools.partial(_attn_c_kernel, tq=tq, cache_order=cache_order),
        grid=(b,),
        in_specs=[pl.BlockSpec((1, t, w), lambda bi: (bi, 0, 0)), kv_spec, kv_spec],
        out_specs=pl.BlockSpec((1, t, w), lambda bi: (bi, 0, 0)),
        out_shape=jax.ShapeDtypeStruct((b, t, w), BF16),
        name="attention_c",
        compiler_params=_params("parallel"),
    )(q, mk, mv)


def _merge_kernel(oa_ref, ob_ref, oc_ref, ga_ref, gb_ref, gc_ref, ba_ref, bb_ref, bc_ref,
                  wa_f32, wb_f32, wc_f32, o_ref, wa_ref, wb_ref, wc_ref):
    @pl.when(pl.program_id(0) == 0)
    def _():
        wa_ref[...] = wa_f32[...].astype(BF16)
        wb_ref[...] = wb_f32[...].astype(BF16)
        wc_ref[...] = wc_f32[...].astype(BF16)

    rows = _block(o_ref.shape[0], MERGE_CHUNK)
    for r0 in range(0, o_ref.shape[0], rows):
        sl = slice(r0, r0 + rows)

        def branch(o_r, g_r, b_r, w_r):
            gate = jax.nn.sigmoid(g_r[sl, :].astype(F32) + b_r[...])
            return gate * jnp.dot(o_r[sl, :], w_r[...], preferred_element_type=F32)

        merged = (branch(oa_ref, ga_ref, ba_ref, wa_ref) + branch(ob_ref, gb_ref, bb_ref, wb_ref)
                  + branch(oc_ref, gc_ref, bc_ref, wc_ref))
        o_ref[sl, :] = merged.astype(o_ref.dtype)


def merge_branches(oa, ob, oc, gate_logits, b_gate, wa, wb, wc):
    m, w_in = oa.shape
    d = wa.shape[1]
    bm = _block(m, MERGE_ROWS)
    row = lambda width: pl.BlockSpec((bm, width), lambda i: (i, 0))
    gate = lambda j: pl.BlockSpec((bm, d), lambda i: (i, j))
    bias = lambda j: pl.BlockSpec((1, d), lambda i: (0, j))
    weight = pl.BlockSpec((w_in, d), lambda i: (0, 0), pipeline_mode=pl.Buffered(1))
    bg = b_gate.reshape(1, N_BRANCH * d).astype(F32)
    return pl.pallas_call(
        _merge_kernel,
        grid=(m // bm,),
        in_specs=[row(w_in), row(w_in), row(w_in), gate(0), gate(1), gate(2),
                  bias(0), bias(1), bias(2), weight, weight, weight],
        out_specs=pl.BlockSpec((bm, d), lambda i: (i, 0)),
        out_shape=jax.ShapeDtypeStruct((m, d), BF16),
        scratch_shapes=[pltpu.VMEM((w_in, d), BF16)] * N_BRANCH,
        name="merge_branches",
        compiler_params=_params("arbitrary"),
    )(oa, ob, oc, gate_logits, gate_logits, gate_logits, bg, bg, bg, wa, wb, wc)


def _proj_norm_res_kernel(*refs, next_norm, rows, cast_w):
    refs = list(refs)
    if cast_w:
        w_bf = refs.pop()
    if next_norm:
        a_ref, w_ref, x_ref, g_ref, g2_ref, o_ref, h_ref = refs
    else:
        a_ref, w_ref, x_ref, g_ref, o_ref = refs
    if cast_w:
        @pl.when(pl.program_id(0) == 0)
        def _():
            w_bf[...] = w_ref[...].astype(BF16)
        w_ref = w_bf
    bm = a_ref.shape[0]
    for r0 in range(0, bm, rows):
        sl = slice(r0, r0 + rows)
        y = jnp.dot(a_ref[sl, :], w_ref[...], preferred_element_type=F32)
        ms = jnp.mean(y * y, axis=-1, keepdims=True)
        o = x_ref[sl, :] + y * lax.rsqrt(ms + EPS) * g_ref[...]
        o_ref[sl, :] = o
        if next_norm:
            ms2 = jnp.mean(o * o, axis=-1, keepdims=True)
            h_ref[sl, :] = (o * lax.rsqrt(ms2 + EPS) * g2_ref[...]).astype(h_ref.dtype)


def proj_norm_residual(a, w, x, g, bm, rows, next_g=None):
    m, k = a.shape
    d = w.shape[1]
    bm = _block(m, bm)
    rows = _block(bm, rows)
    vec = pl.BlockSpec((1, d), lambda i: (0, 0))
    row = pl.BlockSpec((bm, d), lambda i: (i, 0))
    in_specs = [pl.BlockSpec((bm, k), lambda i: (i, 0)),
                pl.BlockSpec((k, d), lambda i: (0, 0), pipeline_mode=pl.Buffered(1)), row, vec]
    args = [a, w, x, g.reshape(1, d).astype(F32)]
    out_specs, out_shape = row, jax.ShapeDtypeStruct((m, d), F32)
    if next_g is not None:
        in_specs.append(vec)
        args.append(next_g.reshape(1, d).astype(F32))
        out_specs = (row, row)
        out_shape = (out_shape, jax.ShapeDtypeStruct((m, d), BF16))
    cast_w = w.dtype != BF16
    return pl.pallas_call(
        functools.partial(_proj_norm_res_kernel, next_norm=next_g is not None, rows=rows, cast_w=cast_w),
        grid=(m // bm,),
        in_specs=in_specs,
        out_specs=out_specs,
        out_shape=out_shape,
        scratch_shapes=[pltpu.VMEM((k, d), BF16)] if cast_w else [],
        name="proj_norm_residual",
        compiler_params=_params("arbitrary"),
    )(*args)


def _ffn_in_kernel(h_ref, h2_ref, wa_ref, wb_ref, o_ref, o2_ref, wa_bf, wb_bf):
    def swiglu(h):
        a = jnp.dot(h, wa_bf[...], preferred_element_type=F32)
        b = jnp.dot(h, wb_bf[...], preferred_element_type=F32)
        return (jax.nn.silu(a) * b).astype(o_ref.dtype)

    @pl.when(pl.program_id(1) == 0)
    def _():
        wa_bf[...] = wa_ref[...].astype(BF16)
        wb_bf[...] = wb_ref[...].astype(BF16)
        o2_ref[...] = swiglu(h2_ref[...])

    o_ref[...] = swiglu(h_ref[...])


def ffn_in(h, extra, w):
    m, k = h.shape
    m2 = extra.shape[0]
    f = w.shape[1] // 2
    bm = _block(m, 1024)
    bf = 512
    assert f % bf == 0
    nf = f // bf
    return pl.pallas_call(
        _ffn_in_kernel,
        grid=(nf, m // bm),
        in_specs=[pl.BlockSpec((bm, k), lambda j, i: (i, 0)),
                  pl.BlockSpec((m2, k), lambda j, i: (0, 0)),
                  pl.BlockSpec((k, bf), lambda j, i: (0, j)),
                  pl.BlockSpec((k, bf), lambda j, i: (0, j + nf))],
        out_specs=[pl.BlockSpec((bm, bf), lambda j, i: (i, j)),
                   pl.BlockSpec((m2, bf), lambda j, i: (0, j))],
        out_shape=[jax.ShapeDtypeStruct((m, f), BF16), jax.ShapeDtypeStruct((m2, f), BF16)],
        scratch_shapes=[pltpu.VMEM((k, bf), BF16), pltpu.VMEM((k, bf), BF16)],
        name="ffn_in",
        compiler_params=_params("arbitrary", "arbitrary"),
    )(h, extra, w, w)


def _layer(xp, xd, mk_p, mv_p, mk_d, mv_d, caches, lam_init, norm_mix_pre, norm_mix_post, w_in, b_gate,
           lq1, lk1, lq2, lk2, subln_a, rel_bias, w_br_a, w_br_b, w_br_c, w_out,
           norm_ffn_pre, norm_ffn_post, w_ffn_in, w_ffn_out):
    d = xp.shape[-1]
    w_head = H_A * DK_A
    shapes = [xp.shape[:2], xd.shape[:2]]
    xs = [xp.reshape(-1, d), xd.reshape(-1, d)]
    h_d = rms_norm_bf16(xs[1], norm_mix_pre)
    h_p, qa_p, qa_d = norm_matmul_cols(xs[0], norm_mix_pre, w_in, 0, w_head, BF16, h_d)
    hs = [h_p, h_d]
    split = lambda outs: [o.reshape(*shp, w_head) for o, shp in zip(outs, shapes)]
    proj = lambda idx, dt: split(matmul_cols(hs[0], w_in, idx * w_head, w_head, dt, extra=hs[1]))
    q_a, q_b, q_c = split([qa_p, qa_d]), proj(3, BF16), proj(6, BF16)
    kv = dict(extra=hs[1], cache_dtype=F32)
    *k_a, ka_cache = matmul_cols(hs[0], w_in, 1 * w_head, w_head, BF16, head_major=DK_A // LANES, **kv)
    *v_a, va_cache = matmul_cols(hs[0], w_in, 2 * w_head, w_head, BF16, head_major=DV_A // LANES, **kv)
    k_a, v_a = split(k_a), split(v_a)
    a_caches = [c.reshape(*shapes[0], H_A, DK_A) for c in (ka_cache, va_cache)]
    seq = shapes[0][1]
    keep = min(BAND_PAST, seq)
    *k_b, kb_tail = matmul_cols(hs[0], w_in, 4 * w_head, w_head, BF16, tail=(seq, keep), **kv)
    *v_b, vb_tail = matmul_cols(hs[0], w_in, 5 * w_head, w_head, BF16, tail=(seq, keep), **kv)
    k_b, v_b = split(k_b), split(v_b)
    tails = [t.reshape(shapes[0][0], keep, H_B, DH_B) for t in (kb_tail, vb_tail)]
    gates = matmul_cols(hs[0], w_in, 7 * w_head, N_BRANCH * d, BF16, extra=hs[1])

    ca_k, ca_v, cb_k, cb_v = caches
    o_a = [attention_a(q_a[0], k_a[0], v_a[0], lq1, lk1, lq2, lk2, subln_a, lam_init),
           attention_a_decode(q_a[1], k_a[1], v_a[1], ca_k, ca_v, lq1, lk1, lq2, lk2, subln_a, lam_init)]
    o_b = [attention_b(q_b[0], k_b[0], v_b[0], rel_bias),
           attention_b_decode(q_b[1], k_b[1], v_b[1], cb_k, cb_v, rel_bias, ca_k.shape[1])]
    o_c = [attention_c(q_c[0], mk_p, mv_p), attention_c(q_c[1], mk_d, mv_d)]

    x1, h2 = [], []
    for i in range(2):
        m = xs[i].shape[0]
        merged = merge_branches(o_a[i].reshape(m, -1), o_b[i].reshape(m, -1), o_c[i].reshape(m, -1),
                                gates[i], b_gate, w_br_a, w_br_b, w_br_c)
        a, b = proj_norm_residual(merged, w_out, xs[i], norm_mix_post, bm=512, rows=PROJ_ROWS,
                                  next_g=norm_ffn_pre)
        x1.append(a)
        h2.append(b)
    acts = ffn_in(h2[0], h2[1], w_ffn_in)
    ys = [proj_norm_residual(acts[i], w_ffn_out, x1[i], norm_ffn_post, bm=512, rows=256).reshape(*shapes[i], d)
          for i in range(2)]
    return (ys[0], *a_caches, *tails), (ys[1], k_a[1], v_a[1], k_b[1], v_b[1])


def kernel(x_prompt, x_sample, cache_a_k, cache_a_v, cache_b_k, cache_b_v, cache_mem_k, cache_mem_v, mem_prompt, norm_mix_pre, norm_mix_post, norm_mem, w_in, b_gate, lambda_q1, lambda_k1, lambda_q2, lambda_k2, subln_a, rel_bias_b, w_mem_kv, w_br_a, w_br_b, w_br_c, w_out, norm_ffn_pre, norm_ffn_post, w_ffn_in, w_ffn_out):
    depth = w_in.shape[0]
    bsz, s, d = x_prompt.shape
    n_mem = mem_prompt.shape[1]
    yp, ys = x_prompt, x_sample
    outs = [[] for _ in range(10)]
    heads = lambda a, nh: a.reshape(a.shape[0], a.shape[1], nh, a.shape[2] // nh)
    for l in range(depth):
        lam_init = 0.8 - 0.6 * math.exp(-0.3 * l)
        shared = (lam_init, norm_mix_pre[l], norm_mix_post[l], w_in[l], b_gate[l],
                  lambda_q1[l], lambda_k1[l], lambda_q2[l], lambda_k2[l], subln_a[l], rel_bias_b[l],
                  w_br_a[l], w_br_b[l], w_br_c[l], w_out[l],
                  norm_ffn_pre[l], norm_ffn_post[l], w_ffn_in[l], w_ffn_out[l].astype(BF16))

        mem_n = rms_norm_bf16(mem_prompt.reshape(bsz * n_mem, d), norm_mem[l])
        w_c = H_C * DH_C
        mk_p, mk_cache = matmul_cols(mem_n, w_mem_kv[l], 0, w_c, BF16, head_major=DH_C // LANES, cache_dtype=F32)
        mv_p, mv_cache = matmul_cols(mem_n, w_mem_kv[l], w_c, w_c, BF16, head_major=DH_C // LANES, cache_dtype=F32)
        mk_p, mv_p = mk_p.reshape(bsz, n_mem, w_c), mv_p.reshape(bsz, n_mem, w_c)
        caches = (cache_a_k[l], cache_a_v[l], cache_b_k[l], cache_b_v[l])
        (yp, ka, va, kb, vb), (ys, ka_d, va_d, kb_d, vb_d) = _layer(
            yp, ys, mk_p, mv_p, cache_mem_k[l], cache_mem_v[l], caches, *shared)
        mem_cache = lambda c: c.reshape(bsz, n_mem, DH_C // LANES, H_C, LANES).transpose(0, 1, 3, 2, 4).reshape(
            bsz, n_mem, H_C, DH_C)
        new = [ka, va, kb, vb, mem_cache(mk_cache), mem_cache(mv_cache),
               heads(ka_d, H_A), heads(va_d, H_A), heads(kb_d, H_B), heads(vb_d, H_B)]
        for o, a in zip(outs, new):
            o.append(a)
    return (yp, ys) + tuple(jnp.stack(o) for o in outs)
```
